```python
import jax
import jax.numpy as jnp
from jax import lax

D_MODEL = 1024
BATCH = 4
SEQ = 8192
DEPTH = 2

GRID_W = 64
CTX_LEN = 256
HEAD_DIM = 64
A_Q_HEADS = 8
A_KV_HEADS = 2
A_GROUP = A_Q_HEADS // A_KV_HEADS
A_WINDOW = 128
A_BLOCK = 128
B_HEADS = 4
NA_ROWS = 8
NA_COLS = 16
C_CHANNELS = 256
C_CONV_WIDTH = 31
A_Q_W = A_Q_HEADS * HEAD_DIM
A_KV_W = A_KV_HEADS * HEAD_DIM
B_W = B_HEADS * HEAD_DIM
MIX_WIDTH = A_Q_W + B_W + C_CHANNELS
IN_WIDTH = A_Q_W + 2 * A_KV_W + 3 * B_W + 2 * C_CHANNELS
OFF_AQ = 0
OFF_AK = OFF_AQ + A_Q_W
OFF_AV = OFF_AK + A_KV_W
OFF_BQ = OFF_AV + A_KV_W
OFF_BK = OFF_BQ + B_W
OFF_BV = OFF_BK + B_W
OFF_C = OFF_BV + B_W
N_EXPERTS = 16
EXPERT_FF = 1024
EC_CAPACITY = 2
ROPE_BASE = 10000.0
LN_EPS = 1e-6
N_MOD = 6
MOD_INIT = 0.5
DEEPNORM_ALPHA = (2 * DEPTH) ** 0.25
DEEPNORM_BETA = (8 * DEPTH) ** -0.25
NEG_INF = -1e30

kernel_name = 'hybrid_diffusion_parallel_heads_ec_moe'


def layer_norm(x, g=None, b=None):
    xf = x.astype(jnp.float32)
    mu = jnp.mean(xf, axis=-1, keepdims=True)
    var = jnp.mean(jnp.square(xf - mu), axis=-1, keepdims=True)
    y = (xf - mu) * lax.rsqrt(var + LN_EPS)
    if g is not None:
        y = y * g.astype(jnp.float32) + b.astype(jnp.float32)
    return y.astype(x.dtype)


def modulate(x, shift, scale):
    return layer_norm(x) * (1 + scale) + shift


def adaln(cond, w_mod_l, b_mod_l, n_chunks):
    m = jax.nn.silu(cond) @ w_mod_l[:, :n_chunks * D_MODEL] + b_mod_l[:n_chunks * D_MODEL]
    return jnp.split(m, n_chunks, axis=-1)


def split_heads(t, n_heads):
    return t.reshape(t.shape[:-1] + (n_heads, HEAD_DIM))


def axial_rope(n_tokens, dtype):
    t = jnp.arange(n_tokens, dtype=jnp.int32)
    row = (t // GRID_W).astype(jnp.float32)[:, None]
    col = (t % GRID_W).astype(jnp.float32)[:, None]
    n_freq = HEAD_DIM // 4
    inv_freq = ROPE_BASE ** (-jnp.arange(n_freq, dtype=jnp.float32) / n_freq)
    ang_r = row * inv_freq
    ang_c = col * inv_freq
    return (jnp.cos(ang_r)[:, None, :].astype(dtype), jnp.sin(ang_r)[:, None, :].astype(dtype),
            jnp.cos(ang_c)[:, None, :].astype(dtype), jnp.sin(ang_c)[:, None, :].astype(dtype))


def rope_half(x, cos, sin):
    x1, x2 = jnp.split(x, 2, axis=-1)
    return jnp.concatenate([x1 * cos - x2 * sin, x2 * cos + x1 * sin], axis=-1)


def apply_axial_rope(x, cos_r, sin_r, cos_c, sin_c):
    xr, xc = jnp.split(x, 2, axis=-1)
    return jnp.concatenate([rope_half(xr, cos_r, sin_r), rope_half(xc, cos_c, sin_c)], axis=-1)


def window_gqa_latent(q, k, v, kc, vc, sink):
    bsz, n_lat = q.shape[:2]
    n_blk = n_lat // A_BLOCK
    span = A_BLOCK + 2 * A_WINDOW
    pad = ((0, 0), (A_WINDOW, A_WINDOW), (0, 0), (0, 0))
    kp = jnp.pad(k, pad)
    vp = jnp.pad(v, pad)
    scale = HEAD_DIM ** -0.5
    sink_l = sink.reshape(1, A_KV_HEADS, A_GROUP, 1, 1).astype(jnp.float32)
    q_off = jnp.arange(A_BLOCK)
    k_off = jnp.arange(span) - A_WINDOW
    band = jnp.abs(k_off[None, :] - q_off[:, None]) <= A_WINDOW

    def block(i):
        start = i * A_BLOCK
        qb = lax.dynamic_slice_in_dim(q, start, A_BLOCK, axis=1)
        kb = lax.dynamic_slice_in_dim(kp, start, span, axis=1)
        vb = lax.dynamic_slice_in_dim(vp, start, span, axis=1)
        kpos = start + k_off
        valid = band & ((kpos >= 0) & (kpos < n_lat))[None, :]
        s_win = jnp.einsum('bqhgd,bkhd->bhgqk', qb, kb).astype(jnp.float32) * scale
        s_win = jnp.where(valid, s_win, NEG_INF)
        s_ctx = jnp.einsum('bqhgd,bkhd->bhgqk', qb, kc).astype(jnp.float32) * scale
        s_sink = jnp.broadcast_to(sink_l, s_win.shape[:-1] + (1,))
        p = jax.nn.softmax(jnp.concatenate([s_win, s_ctx, s_sink], axis=-1), axis=-1).astype(q.dtype)
        return (jnp.einsum('bhgqk,bkhd->bqhgd', p[..., :span], vb)
                + jnp.einsum('bhgqk,bkhd->bqhgd', p[..., span:span + kc.shape[1]], vc))

    out = lax.map(block, jnp.arange(n_blk))
    return jnp.moveaxis(out, 0, 1).reshape(bsz, n_lat, A_Q_W)


def context_gqa(qc, kc, vc, sink):
    s = jnp.einsum('bqhgd,bkhd->bhgqk', qc, kc).astype(jnp.float32) * HEAD_DIM ** -0.5
    s_sink = jnp.broadcast_to(sink.reshape(1, A_KV_HEADS, A_GROUP, 1, 1).astype(jnp.float32), s.shape[:-1] + (1,))
    p = jax.nn.softmax(jnp.concatenate([s, s_sink], axis=-1), axis=-1).astype(qc.dtype)
    o = jnp.einsum('bhgqk,bkhd->bqhgd', p[..., :-1], vc)
    return o.reshape(qc.shape[0], qc.shape[1], A_Q_W)


def neighbourhood_latent(q, k, v, kc, vc, rel_bias):
    bsz, n_lat = q.shape[:2]
    rows = n_lat // GRID_W
    kr_n = min(NA_ROWS, rows)
    n_keys = kr_n * GRID_W
    scale = HEAD_DIM ** -0.5
    qg = q.reshape(bsz, rows, GRID_W, B_HEADS, HEAD_DIM)
    kg = k.reshape(bsz, rows, GRID_W, B_HEADS, HEAD_DIM)
    vg = v.reshape(bsz, rows, GRID_W, B_HEADS, HEAD_DIM)
    cols = jnp.arange(GRID_W)
    c_start = jnp.clip(cols - NA_COLS // 2, 0, GRID_W - NA_COLS)
    col_mask = (cols[None, :] >= c_start[:, None]) & (cols[None, :] < c_start[:, None] + NA_COLS)
    mask = jnp.broadcast_to(col_mask[:, None, :], (GRID_W, kr_n, GRID_W)).reshape(GRID_W, n_keys)
    dc = jnp.clip(cols[None, :] - cols[:, None], -(NA_COLS - 1), NA_COLS - 1) + NA_COLS - 1

    def row(r):
        r_start = jnp.clip(r - kr_n // 2, 0, rows - kr_n)
        qr = lax.dynamic_index_in_dim(qg, r, axis=1, keepdims=False)
        kr = lax.dynamic_slice_in_dim(kg, r_start, kr_n, axis=1).reshape(bsz, n_keys, B_HEADS, HEAD_DIM)
        vr = lax.dynamic_slice_in_dim(vg, r_start, kr_n, axis=1).reshape(bsz, n_keys, B_HEADS, HEAD_DIM)
        dr = r_start + jnp.arange(kr_n) - r + NA_ROWS - 1
        bias = rel_bias[:, dr[None, :, None], dc[:, None, :]].reshape(B_HEADS, GRID_W, n_keys)
        s_nb = jnp.einsum('bqhd,bkhd->bhqk', qr, kr).astype(jnp.float32) * scale + bias.astype(jnp.float32)
        s_nb = jnp.where(mask, s_nb, NEG_INF)
        s_ctx = jnp.einsum('bqhd,bkhd->bhqk', qr, kc).astype(jnp.float32) * scale
        p = jax.nn.softmax(jnp.concatenate([s_nb, s_ctx], axis=-1), axis=-1).astype(q.dtype)
        return (jnp.einsum('bhqk,bkhd->bqhd', p[..., :n_keys], vr)
                + jnp.einsum('bhqk,bkhd->bqhd', p[..., n_keys:], vc))

    out = lax.map(row, jnp.arange(rows))
    return jnp.moveaxis(out, 0, 1).reshape(bsz, n_lat, B_W)


def context_mha(qc, kc, vc):
    s = jnp.einsum('bqhd,bkhd->bhqk', qc, kc).astype(jnp.float32) * HEAD_DIM ** -0.5
    p = jax.nn.softmax(s, axis=-1).astype(qc.dtype)
    o = jnp.einsum('bhqk,bkhd->bqhd', p, vc)
    return o.reshape(qc.shape[0], qc.shape[1], B_W)


def conformer_conv(u, conv_w, conv_b, ln_g, ln_b):
    a, gate = jnp.split(u, 2, axis=-1)
    h = a * jax.nn.sigmoid(gate)
    half = C_CONV_WIDTH // 2
    h = lax.conv_general_dilated(h, conv_w[:, None, :], window_strides=(1,), padding=((half, half),),
                                 dimension_numbers=('NWC', 'WIO', 'NWC'),
                                 feature_group_count=C_CHANNELS) + conv_b
    return jax.nn.silu(layer_norm(h, ln_g, ln_b))


def ec_moe(h, w_router, w_gate, w_up, w_down):
    bsz, n_tok, d = h.shape
    cap = EC_CAPACITY * n_tok // N_EXPERTS
    aff = jax.nn.softmax(jnp.einsum('bnd,de->bne', h, w_router).astype(jnp.float32), axis=-1)
    g, idx = lax.top_k(jnp.swapaxes(aff, 1, 2), cap)
    xs = jax.vmap(lambda hb, ib: hb[ib])(h, idx)
    hid = jax.nn.silu(jnp.einsum('becd,edf->becf', xs, w_gate)) * jnp.einsum('becd,edf->becf', xs, w_up)
    ye = jnp.einsum('becf,efd->becd', hid, w_down) * g[..., None].astype(h.dtype)
    return jax.vmap(lambda yb, ib: jnp.zeros((n_tok, d), h.dtype).at[ib.reshape(-1)].add(yb.reshape(-1, d)))(ye, idx)


def context_kv(hc, w_in_l):
    ka, va = jnp.split(hc @ w_in_l[:, OFF_AK:OFF_BQ], 2, axis=-1)
    kb, vb = jnp.split(hc @ w_in_l[:, OFF_BK:OFF_C], 2, axis=-1)
    return (split_heads(ka, A_KV_HEADS), split_heads(va, A_KV_HEADS),
            split_heads(kb, B_HEADS), split_heads(vb, B_HEADS))


def setup_inputs(seed: int = 0) -> dict:
    key = jax.random.key(seed)
    ks = jax.random.split(key, 22)
    L, D = DEPTH, D_MODEL

    def nrm(k, shape, s):
        return s * jax.random.normal(k, shape, jnp.float32)

    return {
        'x': nrm(ks[0], (BATCH, SEQ, D), 1.0),
        'c': nrm(ks[1], (BATCH, D), 1.0),
        'ctx': nrm(ks[2], (BATCH, CTX_LEN, D), 1.0),
        'c_ctx': nrm(ks[3], (D,), 1.0),
        'w_mod': nrm(ks[4], (L, D, N_MOD * D), MOD_INIT * D ** -0.5),
        'b_mod': nrm(ks[5], (L, N_MOD * D), 0.02),
        'w_in': nrm(ks[6], (L, D, IN_WIDTH), D ** -0.5),
        'a_sink': nrm(ks[7], (L, A_Q_HEADS), 0.5),
        'nat_bias': nrm(ks[8], (L, B_HEADS, 2 * NA_ROWS - 1, 2 * NA_COLS - 1), 0.1),
        'conv_w': nrm(ks[9], (L, C_CONV_WIDTH, C_CHANNELS), C_CONV_WIDTH ** -0.5),
        'conv_b': nrm(ks[10], (L, C_CHANNELS), 0.02),
        'conv_ln_g': 1.0 + nrm(ks[11], (L, C_CHANNELS), 0.05),
        'conv_ln_b': nrm(ks[12], (L, C_CHANNELS), 0.02),
        'w_out': nrm(ks[13], (L, MIX_WIDTH, D), DEEPNORM_BETA * MIX_WIDTH ** -0.5),
        'ln1_g': 1.0 + nrm(ks[14], (L, D), 0.05),
        'ln1_b': nrm(ks[15], (L, D), 0.02),
        'w_router': nrm(ks[16], (L, D, N_EXPERTS), D ** -0.5),
        'w_gate': nrm(ks[17], (L, N_EXPERTS, D, EXPERT_FF), D ** -0.5),
        'w_up': nrm(ks[18], (L, N_EXPERTS, D, EXPERT_FF), D ** -0.5),
        'w_down': nrm(ks[19], (L, N_EXPERTS, EXPERT_FF, D), DEEPNORM_BETA * EXPERT_FF ** -0.5),
        'ln2_g': 1.0 + nrm(ks[20], (L, D), 0.05),
        'ln2_b': nrm(ks[21], (L, D), 0.02),
    }


def reference(x, c, ctx, c_ctx, w_mod, b_mod, w_in, a_sink, nat_bias, conv_w, conv_b, conv_ln_g, conv_ln_b,
              w_out, ln1_g, ln1_b, w_router, w_gate, w_up, w_down, ln2_g, ln2_b):
    bsz, n_lat, _ = x.shape
    n_ctx = ctx.shape[1]
    rope = axial_rope(n_lat, x.dtype)
    for l in range(DEPTH):
        last = l == DEPTH - 1
        w_in_l = w_in[l]
        sh1, sc1, g1, sh2, sc2, g2 = [m[:, None, :] for m in adaln(c, w_mod[l], b_mod[l], N_MOD)]

        ctx_mod = adaln(c_ctx, w_mod[l], b_mod[l], 2 if last else N_MOD)
        hc = modulate(ctx, ctx_mod[0], ctx_mod[1])
        kc_a, vc_a, kc_b, vc_b = context_kv(hc, w_in_l)
        if not last:
            qc_a = split_heads(hc @ w_in_l[:, OFF_AQ:OFF_AK], A_Q_HEADS).reshape(bsz, n_ctx, A_KV_HEADS, A_GROUP, HEAD_DIM)
            qc_b = split_heads(hc @ w_in_l[:, OFF_BQ:OFF_BK], B_HEADS)
            oc = jnp.concatenate([
                context_gqa(qc_a, kc_a, vc_a, a_sink[l]),
                context_mha(qc_b, kc_b, vc_b),
                conformer_conv(hc @ w_in_l[:, OFF_C:], conv_w[l], conv_b[l], conv_ln_g[l], conv_ln_b[l]),
            ], axis=-1) @ w_out[l]
            ctx_new = layer_norm(DEEPNORM_ALPHA * ctx + ctx_mod[2] * oc, ln1_g[l], ln1_b[l])
            hc2 = modulate(ctx_new, ctx_mod[3], ctx_mod[4])
            ctx_new = layer_norm(DEEPNORM_ALPHA * ctx_new + ctx_mod[5] * ec_moe(hc2, w_router[l], w_gate[l], w_up[l], w_down[l]),
                                 ln2_g[l], ln2_b[l])

        h = modulate(x, sh1, sc1)
        u = h @ w_in_l
        q_a = apply_axial_rope(split_heads(u[..., OFF_AQ:OFF_AK], A_Q_HEADS), *rope)
        k_a = apply_axial_rope(split_heads(u[..., OFF_AK:OFF_AV], A_KV_HEADS), *rope)
        v_a = split_heads(u[..., OFF_AV:OFF_BQ], A_KV_HEADS)
        q_b = split_heads(u[..., OFF_BQ:OFF_BK], B_HEADS)
        k_b = split_heads(u[..., OFF_BK:OFF_BV], B_HEADS)
        v_b = split_heads(u[..., OFF_BV:OFF_C], B_HEADS)
        o_a = window_gqa_latent(q_a.reshape(bsz, n_lat, A_KV_HEADS, A_GROUP, HEAD_DIM), k_a, v_a, kc_a, vc_a, a_sink[l])
        o_b = neighbourhood_latent(q_b, k_b, v_b, kc_b, vc_b, nat_bias[l])
        o_c = conformer_conv(u[..., OFF_C:], conv_w[l], conv_b[l], conv_ln_g[l], conv_ln_b[l])
        o = jnp.concatenate([o_a, o_b, o_c], axis=-1) @ w_out[l]
        x = layer_norm(DEEPNORM_ALPHA * x + g1 * o, ln1_g[l], ln1_b[l])
        h2 = modulate(x, sh2, sc2)
        x = layer_norm(DEEPNORM_ALPHA * x + g2 * ec_moe(h2, w_router[l], w_gate[l], w_up[l], w_down[l]),
                       ln2_g[l], ln2_b[l])
        if not last:
            ctx = ctx_new
    return x
```

```python
import functools

import numpy as np
import jax
import jax.numpy as jnp
from jax import lax
from jax.experimental import pallas as pl
from jax.experimental.pallas import tpu as pltpu

HEAD_DIM = 64
GRID_W = 64
A_Q_HEADS = 8
A_KV_HEADS = 2
A_GROUP = A_Q_HEADS // A_KV_HEADS
A_WINDOW = 128
B_HEADS = 4
NA_ROWS = 8
NA_COLS = 16
C_CHANNELS = 256
C_CONV_WIDTH = 31
A_Q_W = A_Q_HEADS * HEAD_DIM
A_KV_W = A_KV_HEADS * HEAD_DIM
B_W = B_HEADS * HEAD_DIM
OFF_AK = A_Q_W
OFF_AV = OFF_AK + A_KV_W
OFF_BQ = OFF_AV + A_KV_W
OFF_BK = OFF_BQ + B_W
OFF_BV = OFF_BK + B_W
OFF_C = OFF_BV + B_W
IN_WIDTH = OFF_C + 2 * C_CHANNELS
ROPE_WIDTH = A_Q_W + A_KV_W
N_EXPERTS = 16
EC_CAPACITY = 2
ROPE_BASE = 10000.0
LN_EPS = 1e-6
N_MOD = 6
NEG_INF = -1e30
QK_SCALE = HEAD_DIM ** -0.5

LANES = 128
BF16_SUBLANES = 16
TOKEN_BLOCK = 256
GATHER_WINDOW = TOKEN_BLOCK + BF16_SUBLANES
SCATTER_WINDOW = 3 * LANES
VMEM_LIMIT = 56 * 1024 * 1024

F32 = jnp.float32
BF16 = jnp.bfloat16


def _dot(a, b):
    return jnp.dot(a, b, preferred_element_type=F32)


def _dot_t(a, b):
    return lax.dot_general(a, b, (((1,), (1,)), ((), ())), preferred_element_type=F32)


def _layer_norm(x):
    mu = jnp.mean(x, axis=-1, keepdims=True)
    xc = x - mu
    var = jnp.mean(xc * xc, axis=-1, keepdims=True)
    return xc * lax.rsqrt(var + LN_EPS)


def _params(*sem):
    return pltpu.CompilerParams(dimension_semantics=sem, vmem_limit_bytes=VMEM_LIMIT)


def _mod_kernel(cond_ref, w_ref, b_ref, out_ref):
    cnd = cond_ref[...]
    act = cnd * jax.nn.sigmoid(cnd)
    out_ref[0] = jnp.dot(act, w_ref[0], preferred_element_type=F32,
                         precision=lax.Precision.HIGHEST) + b_ref[0]


def adaln_all(cond, w_mod, b_mod):
    n_layers, d, width = w_mod.shape
    rows = cond.shape[0]
    tn = 1536
    return pl.pallas_call(
        _mod_kernel,
        grid=(n_layers, width // tn),
        in_specs=[
            pl.BlockSpec((rows, d), lambda l, j: (0, 0)),
            pl.BlockSpec((1, d, tn), lambda l, j: (l, 0, j)),
            pl.BlockSpec((1, 1, tn), lambda l, j: (l, 0, j)),
        ],
        out_specs=pl.BlockSpec((1, rows, tn), lambda l, j: (l, 0, j)),
        out_shape=jax.ShapeDtypeStruct((n_layers, rows, width), F32),
        compiler_params=_params("parallel", "parallel"),
        name="adaln",
    )(cond, w_mod, b_mod.reshape(n_layers, 1, width))


def _inproj_kernel(x_ref, sh_ref, sc_ref, w_ref, cos_ref, sin_ref,
                   qa_ref, ka_ref, va_ref, qb_ref, kb_ref, vb_ref, hc_ref, *, rope):
    x = x_ref[0]
    h = _layer_norm(x) * (1.0 + sc_ref[0]) + sh_ref[0]
    u = _dot(h.astype(BF16), w_ref[...])

    def rotated(col):
        xq = u[:, col:col + LANES]
        if not rope:
            return xq
        lane = lax.broadcasted_iota(jnp.int32, xq.shape, 1)
        first = (lane & (HEAD_DIM // 2 - 1)) < (HEAD_DIM // 4)
        partner = jnp.where(first, pltpu.roll(xq, LANES - HEAD_DIM // 4, 1),
                            pltpu.roll(xq, HEAD_DIM // 4, 1))
        return xq * cos_ref[...] + partner * sin_ref[...]

    rot = [rotated(col) for col in range(0, ROPE_WIDTH, LANES)]
    n_q = A_Q_W // LANES
    qa_ref[0] = (jnp.concatenate(rot[:n_q], axis=1) * QK_SCALE).astype(BF16)
    ka_ref[0] = jnp.concatenate(rot[n_q:], axis=1).astype(BF16)
    va_ref[0] = u[:, OFF_AV:OFF_BQ].astype(BF16)
    qb_ref[0] = (u[:, OFF_BQ:OFF_BK] * QK_SCALE).astype(BF16)
    kb_ref[0] = u[:, OFF_BK:OFF_BV].astype(BF16)
    vb_ref[0] = u[:, OFF_BV:OFF_C].astype(BF16)
    a = u[:, OFF_C:OFF_C + C_CHANNELS]
    gate = u[:, OFF_C + C_CHANNELS:]
    hc_ref[0] = a * jax.nn.sigmoid(gate)


def in_projection(x, shift, scale, w_in_bf16, cos_t, sin_t, *, rope):
    bx, t, d = x.shape
    tm = min(512, t)
    widths = (A_Q_W, A_KV_W, A_KV_W, B_W, B_W, B_W, C_CHANNELS)
    dtypes = (BF16,) * 6 + (F32,)
    tok = lambda b, i: (b, i, 0)
    per_b = lambda b, i: (b, 0, 0)
    return pl.pallas_call(
        functools.partial(_inproj_kernel, rope=rope),
        grid=(bx, t // tm),
        in_specs=[
            pl.BlockSpec((1, tm, d), tok),
            pl.BlockSpec((1, 1, d), per_b),
            pl.BlockSpec((1, 1, d), per_b),
            pl.BlockSpec((d, IN_WIDTH), lambda b, i: (0, 0)),
            pl.BlockSpec((tm, LANES), lambda b, i: (i, 0)),
            pl.BlockSpec((tm, LANES), lambda b, i: (i, 0)),
        ],
        out_specs=[pl.BlockSpec((1, tm, w), tok) for w in widths],
        out_shape=[jax.ShapeDtypeStruct((bx, t, w), dt) for w, dt in zip(widths, dtypes)],
        compiler_params=_params("parallel", "parallel"),
        name="in_projection",
    )(x, shift, scale, w_in_bf16, cos_t, sin_t)


def rope_tables(n_tokens):
    t = jnp.arange(n_tokens, dtype=jnp.int32)
    row = (t // GRID_W).astype(F32)[:, None]
    col = (t % GRID_W).astype(F32)[:, None]
    n_freq = HEAD_DIM // 4
    inv_freq = ROPE_BASE ** (-jnp.arange(n_freq, dtype=F32) / n_freq)
    ang_r = row * inv_freq
    ang_c = col * inv_freq
    cos_h = jnp.concatenate([jnp.cos(ang_r), jnp.cos(ang_r), jnp.cos(ang_c), jnp.cos(ang_c)], axis=1)
    sin_h = jnp.concatenate([-jnp.sin(ang_r), jnp.sin(ang_r), -jnp.sin(ang_c), jnp.sin(ang_c)], axis=1)
    reps = LANES // HEAD_DIM
    return jnp.tile(cos_h, (1, reps)), jnp.tile(sin_h, (1, reps))


def _attend(q, score_parts, values, sink=None):
    m = score_parts[0].max(axis=-1, keepdims=True)
    for s in score_parts[1:]:
        m = jnp.maximum(m, s.max(axis=-1, keepdims=True))
    if sink is not None:
        m = jnp.maximum(m, sink)
    den = jnp.zeros_like(m) if sink is None else jnp.exp(sink - m)
    out = jnp.zeros((q.shape[0], HEAD_DIM), F32)
    for s, v in zip(score_parts, values):
        p = jnp.exp(s - m)
        den = den + p.sum(axis=-1, keepdims=True)
        out = out + _dot(p.astype(BF16), v)
    return out / den


def _attn_a_kernel(sink_ref, q_ref, kp_ref, kc_ref, kn_ref, vp_ref, vc_ref, vn_ref,
                   kctx_ref, vctx_ref, out_ref, *, n_lat, tq):
    i = pl.program_id(1)
    k_win = jnp.concatenate([kp_ref[0], kc_ref[0], kn_ref[0]], axis=0)
    v_win = jnp.concatenate([vp_ref[0], vc_ref[0], vn_ref[0]], axis=0)
    span = tq + 2 * A_WINDOW
    qpos = i * tq + lax.broadcasted_iota(jnp.int32, (tq, span), 0)
    kpos = i * tq - A_WINDOW + lax.broadcasted_iota(jnp.int32, (tq, span), 1)
    rel = kpos - qpos
    valid = (rel <= A_WINDOW) & (rel >= -A_WINDOW) & (kpos >= 0) & (kpos < n_lat)
    q_all = q_ref[0]
    kctx = kctx_ref[0]
    vctx = vctx_ref[0]
    for hk in range(A_KV_HEADS):
        sl = slice(hk * HEAD_DIM, (hk + 1) * HEAD_DIM)
        kh, vh, kch, vch = k_win[:, sl], v_win[:, sl], kctx[:, sl], vctx[:, sl]
        for g in range(A_GROUP):
            hq = hk * A_GROUP + g
            q = q_all[:, hq * HEAD_DIM:(hq + 1) * HEAD_DIM]
            s_win = jnp.where(valid, _dot_t(q, kh), NEG_INF)
            s_ctx = _dot_t(q, kch)
            o = _attend(q, [s_win, s_ctx], [vh, vch], sink=sink_ref[hq])
            out_ref[0, :, hq * HEAD_DIM:(hq + 1) * HEAD_DIM] = o.astype(BF16)


def window_attention(qa, ka, va, kc_a, vc_a, sink):
    bsz, n_lat, _ = qa.shape
    n_ctx = kc_a.shape[1]
    tq = min(256, n_lat)
    w = A_WINDOW
    per = tq // w
    last = n_lat // w - 1
    prev = lambda b, i, s: (b, jnp.maximum(i * per - 1, 0), 0)
    cur = lambda b, i, s: (b, i, 0)
    nxt = lambda b, i, s: (b, jnp.minimum((i + 1) * per, last), 0)
    ctx = lambda b, i, s: (b, 0, 0)
    kv_specs = [pl.BlockSpec((1, w, A_KV_W), prev), pl.BlockSpec((1, tq, A_KV_W), cur),
                pl.BlockSpec((1, w, A_KV_W), nxt)]
    grid_spec = pltpu.PrefetchScalarGridSpec(
        num_scalar_prefetch=1,
        grid=(bsz, n_lat // tq),
        in_specs=[pl.BlockSpec((1, tq, A_Q_W), cur)] + kv_specs + kv_specs + [
            pl.BlockSpec((1, n_ctx, A_KV_W), ctx), pl.BlockSpec((1, n_ctx, A_KV_W), ctx)],
        out_specs=pl.BlockSpec((1, tq, A_Q_W), cur),
    )
    return pl.pallas_call(
        functools.partial(_attn_a_kernel, n_lat=n_lat, tq=tq),
        grid_spec=grid_spec,
        out_shape=jax.ShapeDtypeStruct((bsz, n_lat, A_Q_W), BF16),
        compiler_params=_params("parallel", "parallel"),
        name="window_attention",
    )(sink, qa, ka, ka, ka, va, va, va, kc_a, vc_a)


NB_Q_ROWS = 4


def _attn_b_kernel(q_ref, kp_ref, kc_ref, kn_ref, vp_ref, vc_ref, vn_ref,
                   kctx_ref, vctx_ref, bias_ref, out_ref):
    k_win = jnp.concatenate([kp_ref[0], kc_ref[0], kn_ref[0]], axis=0)
    v_win = jnp.concatenate([vp_ref[0], vc_ref[0], vn_ref[0]], axis=0)
    q_all = q_ref[0]
    kctx = kctx_ref[0]
    vctx = vctx_ref[0]
    for h in range(B_HEADS):
        sl = slice(h * HEAD_DIM, (h + 1) * HEAD_DIM)
        q = q_all[:, sl]
        s_nb = _dot_t(q, k_win[:, sl]) + bias_ref[0, h]
        s_ctx = _dot_t(q, kctx[:, sl])
        o = _attend(q, [s_nb, s_ctx], [v_win[:, sl], vctx[:, sl]])
        out_ref[0, :, sl] = o.astype(BF16)


def neighbourhood_bias(rel_bias, n_lat):
    rows = n_lat // GRID_W
    kr_n = min(NA_ROWS, rows)
    n_blocks = rows // NB_Q_ROWS
    n_heads, n_dr, n_dc = rel_bias.shape
    cols = np.arange(GRID_W)
    c_start = np.clip(cols - NA_COLS // 2, 0, GRID_W - NA_COLS)
    col_ok = (cols[None, :] >= c_start[:, None]) & (cols[None, :] < c_start[:, None] + NA_COLS)
    dc = np.clip(cols[None, :] - cols[:, None], -(NA_COLS - 1), NA_COLS - 1) + NA_COLS - 1
    pick_dc = (dc.reshape(-1)[None, :] == np.arange(n_dc)[:, None]).astype(np.float32)
    toeplitz = jnp.dot(rel_bias.reshape(n_heads * n_dr, n_dc), pick_dc, precision=lax.Precision.HIGHEST)
    toeplitz = jnp.where(col_ok.reshape(-1), toeplitz, NEG_INF).reshape(n_heads, n_dr, GRID_W, GRID_W)
    q_rl = np.arange(NB_Q_ROWS)
    k_rl = np.arange(3 * NB_Q_ROWS)
    row_ok, dr = [], []
    for j in sorted({0, min(1, n_blocks - 1), n_blocks - 1}):
        r = NB_Q_ROWS * j + q_rl
        kr = NB_Q_ROWS * (j - 1) + k_rl
        r_start = np.clip(r - kr_n // 2, 0, rows - kr_n)
        ok = (kr[None, :] >= r_start[:, None]) & (kr[None, :] < r_start[:, None] + kr_n)
        row_ok.append(ok & (kr[None, :] >= 0) & (kr[None, :] < rows))
        dr.append(np.clip(kr[None, :] - r[:, None] + NA_ROWS - 1, 0, n_dr - 1))
    row_ok = np.stack(row_ok)
    dr = np.stack(dr)
    tiles = jnp.stack([toeplitz[:, int(i)] for i in dr.reshape(-1)], axis=1)
    tiles = tiles.reshape((n_heads,) + dr.shape + (GRID_W, GRID_W))
    tiles = jnp.where(row_ok[None, :, :, :, None, None], tiles, NEG_INF)
    table = tiles.transpose(1, 0, 2, 4, 3, 5).reshape(
        dr.shape[0], n_heads, NB_Q_ROWS * GRID_W, 3 * NB_Q_ROWS * GRID_W)
    return table, n_blocks


def neighbourhood_attention(qb, kb, vb, kc_b, vc_b, rel_bias):
    bsz, n_lat, _ = qb.shape
    n_ctx = kc_b.shape[1]
    table, n_blocks = neighbourhood_bias(rel_bias, n_lat)
    n_var = table.shape[0]
    tq = NB_Q_ROWS * GRID_W
    prev = lambda b, j: (b, jnp.maximum(j - 1, 0), 0)
    cur = lambda b, j: (b, j, 0)
    nxt = lambda b, j: (b, jnp.minimum(j + 1, n_blocks - 1), 0)
    ctx = lambda b, j: (b, 0, 0)

    def variant(b, j):
        v = jnp.where(j == 0, 0, jnp.where(j == n_blocks - 1, n_var - 1, min(1, n_var - 1)))
        return (v, 0, 0, 0)

    kv_specs = [pl.BlockSpec((1, tq, B_W), prev), pl.BlockSpec((1, tq, B_W), cur),
                pl.BlockSpec((1, tq, B_W), nxt)]
    return pl.pallas_call(
        _attn_b_kernel,
        grid=(bsz, n_blocks),
        in_specs=[pl.BlockSpec((1, tq, B_W), cur)] + kv_specs + kv_specs + [
            pl.BlockSpec((1, n_ctx, B_W), ctx), pl.BlockSpec((1, n_ctx, B_W), ctx),
            pl.BlockSpec((1, B_HEADS, tq, 3 * tq), variant)],
        out_specs=pl.BlockSpec((1, tq, B_W), cur),
        out_shape=jax.ShapeDtypeStruct((bsz, n_lat, B_W), BF16),
        compiler_params=_params("parallel", "parallel"),
        name="neighbourhood_attention",
    )(qb, kb, kb, kb, vb, vb, vb, kc_b, vc_b, table)


def _ctx_attn_kernel(sink_ref, qa_ref, qb_ref, ka_ref, va_ref, kb_ref, vb_ref, oa_ref, ob_ref):
    qa, qb = qa_ref[0], qb_ref[0]
    ka, va, kb, vb = ka_ref[0], va_ref[0], kb_ref[0], vb_ref[0]
    for hq in range(A_Q_HEADS):
        sl = slice(hq * HEAD_DIM, (hq + 1) * HEAD_DIM)
        hk = hq // A_GROUP
        kv = slice(hk * HEAD_DIM, (hk + 1) * HEAD_DIM)
        q = qa[:, sl]
        o = _attend(q, [_dot_t(q, ka[:, kv])], [va[:, kv]], sink=sink_ref[hq])
        oa_ref[0, :, sl] = o.astype(BF16)
    for h in range(B_HEADS):
        sl = slice(h * HEAD_DIM, (h + 1) * HEAD_DIM)
        q = qb[:, sl]
        o = _attend(q, [_dot_t(q, kb[:, sl])], [vb[:, sl]])
        ob_ref[0, :, sl] = o.astype(BF16)


def context_attention(qa, qb, ka, va, kb, vb, sink):
    bsz, n_ctx, _ = qa.shape
    blk = lambda w: pl.BlockSpec((1, n_ctx, w), lambda b, s: (b, 0, 0))
    grid_spec = pltpu.PrefetchScalarGridSpec(
        num_scalar_prefetch=1,
        grid=(bsz,),
        in_specs=[blk(A_Q_W), blk(B_W), blk(A_KV_W), blk(A_KV_W), blk(B_W), blk(B_W)],
        out_specs=[blk(A_Q_W), blk(B_W)],
    )
    return pl.pallas_call(
        _ctx_attn_kernel,
        grid_spec=grid_spec,
        out_shape=[jax.ShapeDtypeStruct((bsz, n_ctx, A_Q_W), BF16),
                   jax.ShapeDtypeStruct((bsz, n_ctx, B_W), BF16)],
        compiler_params=_params("parallel"),
        name="context_attention",
    )(sink, qa, qb, ka, va, kb, vb)


CONV_HALO = 16


def _conv_kernel(prev_ref, cur_ref, next_ref, w_ref, b_ref, g_ref, beta_ref, out_ref, ext_ref, *, ts):
    i = pl.program_id(1)
    n_i = pl.num_programs(1)
    ext_ref[0:CONV_HALO] = jnp.where(i > 0, prev_ref[0], 0.0)
    ext_ref[CONV_HALO:CONV_HALO + ts] = cur_ref[0]
    ext_ref[CONV_HALO + ts:] = jnp.where(i < n_i - 1, next_ref[0], 0.0)
    half = C_CONV_WIDTH // 2
    acc = jnp.zeros((ts, C_CHANNELS), F32)
    for k in range(C_CONV_WIDTH):
        acc = acc + ext_ref[pl.ds(CONV_HALO - half + k, ts)] * w_ref[k:k + 1]
    y = _layer_norm(acc + b_ref[...]) * g_ref[...] + beta_ref[...]
    out_ref[0] = (y * jax.nn.sigmoid(y)).astype(BF16)


def conformer_conv(hc, conv_w, conv_b, ln_g, ln_b):
    bx, t, ch = hc.shape
    ts = min(512, t)
    per = ts // CONV_HALO
    last = t // CONV_HALO - 1
    row = lambda v: v.reshape(1, ch)
    const = lambda b, i: (0, 0)
    return pl.pallas_call(
        functools.partial(_conv_kernel, ts=ts),
        grid=(bx, t // ts),
        in_specs=[
            pl.BlockSpec((1, CONV_HALO, ch), lambda b, i: (b, jnp.maximum(i * per - 1, 0), 0)),
            pl.BlockSpec((1, ts, ch), lambda b, i: (b, i, 0)),
            pl.BlockSpec((1, CONV_HALO, ch), lambda b, i: (b, jnp.minimum((i + 1) * per, last), 0)),
            pl.BlockSpec((C_CONV_WIDTH, ch), const),
            pl.BlockSpec((1, ch), const), pl.BlockSpec((1, ch), const), pl.BlockSpec((1, ch), const),
        ],
        out_specs=pl.BlockSpec((1, ts, ch), lambda b, i: (b, i, 0)),
        out_shape=jax.ShapeDtypeStruct((bx, t, ch), BF16),
        scratch_shapes=[pltpu.VMEM((ts + 2 * CONV_HALO, ch), F32)],
        compiler_params=_params("parallel", "parallel"),
        name="conformer_conv",
    )(hc, hc, hc, conv_w, row(conv_b), row(ln_g), row(ln_b))


def _outproj_kernel(oa_ref, ob_ref, oc_ref, x_ref, g1_ref, sh_ref, sc_ref, w_ref, lng_ref, lnb_ref,
                    wr_hi_ref, wr_lo_ref, x1_ref, h2_ref, afft_ref, aff_ref, *, alpha):
    o = (_dot(oa_ref[0], w_ref[0:A_Q_W])
         + _dot(ob_ref[0], w_ref[A_Q_W:A_Q_W + B_W])
         + _dot(oc_ref[0], w_ref[A_Q_W + B_W:]))
    y = _layer_norm(alpha * x_ref[0] + g1_ref[0] * o) * lng_ref[...] + lnb_ref[...]
    x1_ref[0] = y
    h2 = _layer_norm(y) * (1.0 + sc_ref[0]) + sh_ref[0]
    h_hi = h2.astype(BF16)
    h2_ref[0] = h_hi
    h_lo = (h2 - h_hi.astype(F32)).astype(BF16)
    w_hi, w_lo = wr_hi_ref[...], wr_lo_ref[...]
    logits_t = _dot_t(w_hi, h_hi) + _dot_t(w_hi, h_lo) + _dot_t(w_lo, h_hi)
    e_t = jnp.exp(logits_t - logits_t.max(axis=0, keepdims=True))
    afft_ref[0] = e_t / e_t.sum(axis=0, keepdims=True)
    logits = _dot_t(h_hi, w_hi) + _dot_t(h_lo, w_hi) + _dot_t(h_hi, w_lo)
    e_n = jnp.exp(logits - logits.max(axis=1, keepdims=True))
    aff_ref[0] = e_n / e_n.sum(axis=1, keepdims=True)


def out_projection(oa, ob, oc, x, g1, sh2, sc2, w_out_bf16, ln_g, ln_b, wr_hi, wr_lo, alpha):
    bx, t, d = x.shape
    tm = min(512, t)
    tok = lambda b, i: (b, i, 0)
    per_b = lambda b, i: (b, 0, 0)
    const = lambda b, i: (0, 0)
    vec = pl.BlockSpec((1, d), const)
    return pl.pallas_call(
        functools.partial(_outproj_kernel, alpha=alpha),
        grid=(bx, t // tm),
        in_specs=[
            pl.BlockSpec((1, tm, A_Q_W), tok), pl.BlockSpec((1, tm, B_W), tok),
            pl.BlockSpec((1, tm, C_CHANNELS), tok), pl.BlockSpec((1, tm, d), tok),
            pl.BlockSpec((1, 1, d), per_b), pl.BlockSpec((1, 1, d), per_b), pl.BlockSpec((1, 1, d), per_b),
            pl.BlockSpec(w_out_bf16.shape, const), vec, vec,
            pl.BlockSpec((N_EXPERTS, d), const), pl.BlockSpec((N_EXPERTS, d), const),
        ],
        out_specs=[pl.BlockSpec((1, tm, d), tok), pl.BlockSpec((1, tm, d), tok),
                   pl.BlockSpec((1, N_EXPERTS, tm), lambda b, i: (b, 0, i)),
                   pl.BlockSpec((1, tm, N_EXPERTS), tok)],
        out_shape=[jax.ShapeDtypeStruct((bx, t, d), F32), jax.ShapeDtypeStruct((bx, t, d), BF16),
                   jax.ShapeDtypeStruct((bx, N_EXPERTS, t), F32),
                   jax.ShapeDtypeStruct((bx, t, N_EXPERTS), F32)],
        compiler_params=_params("parallel", "parallel"),
        name="out_projection",
    )(oa, ob, oc, x, g1, sh2, sc2, w_out_bf16, ln_g.reshape(1, d), ln_b.reshape(1, d), wr_hi, wr_lo)


def _select_kernel(afft_ref, pos_ref, post_ref, off_ref, *, cap, n_tok):
    aff = afft_ref[0]

    def bit_step(j, bits):
        cand = bits | (jnp.int32(1) << (30 - j))
        cnt = jnp.sum((aff >= pltpu.bitcast(cand, F32)).astype(jnp.int32), axis=1, keepdims=True)
        return jnp.where(cnt >= cap, cand, bits)

    thr = pltpu.bitcast(lax.fori_loop(0, 31, bit_step, jnp.zeros((N_EXPERTS, 1), jnp.int32)), F32)
    above = (aff > thr).astype(F32)
    tied = (aff == thr).astype(F32)
    need = cap - jnp.sum(above, axis=1, keepdims=True)

    blk = LANES
    n_blk = n_tok // blk
    r_i = lax.broadcasted_iota(jnp.int32, (blk, blk), 0)
    c_i = lax.broadcasted_iota(jnp.int32, (blk, blk), 1)
    strict_upper = (r_i < c_i).astype(BF16)
    eye = (r_i == c_i).astype(BF16)

    run_tied = jnp.zeros((N_EXPERTS, 1), F32)
    run_sel = jnp.zeros((N_EXPERTS, 1), F32)
    offs = []
    for kb in range(n_blk):
        sl = slice(kb * blk, (kb + 1) * blk)
        if kb % (TOKEN_BLOCK // blk) == 0:
            offs.append(run_sel)
        t_f = tied[:, sl]
        rank_tied = run_tied + _dot(t_f.astype(BF16), strict_upper)
        run_tied = run_tied + t_f.sum(axis=1, keepdims=True)
        s_f = above[:, sl] + t_f * (rank_tied < need).astype(F32)
        rank = run_sel + _dot(s_f.astype(BF16), strict_upper)
        run_sel = run_sel + s_f.sum(axis=1, keepdims=True)
        pos = jnp.where(s_f > 0.5, rank.astype(jnp.int32), -1)
        pos_ref[0, :, sl] = pos
        p1 = pos + 1
        hi = (p1 >> 5).astype(F32).astype(BF16)
        lo = (p1 & 31).astype(F32).astype(BF16)
        p1_t = _dot_t(eye, hi) * 32.0 + _dot_t(eye, lo)
        post_ref[0, sl, :] = p1_t.astype(jnp.int32) - 1
    off_ref[0] = jnp.concatenate(offs, axis=1).astype(jnp.int32)


def expert_choice_select(aff_t, cap):
    bx, n_e, t = aff_t.shape
    n_tb = t // TOKEN_BLOCK
    return pl.pallas_call(
        functools.partial(_select_kernel, cap=cap, n_tok=t),
        grid=(bx,),
        in_specs=[pl.BlockSpec((1, n_e, t), lambda b: (b, 0, 0))],
        out_specs=[pl.BlockSpec((1, n_e, t), lambda b: (b, 0, 0)),
                   pl.BlockSpec((1, t, n_e), lambda b: (b, 0, 0)),
                   pl.BlockSpec((1, n_e, n_tb), lambda b: (b, 0, 0))],
        out_shape=[jax.ShapeDtypeStruct((bx, n_e, t), jnp.int32),
                   jax.ShapeDtypeStruct((bx, t, n_e), jnp.int32),
                   jax.ShapeDtypeStruct((bx, n_e, n_tb), jnp.int32)],
        compiler_params=_params("parallel"),
        name="expert_choice_select",
    )(aff_t)


def _gather_kernel(off_ref, pos_ref, h_ref, xs_ref, *, n_tb, blocks_per_step):
    b, e, kc = pl.program_id(0), pl.program_id(1), pl.program_id(2)

    @pl.when(kc == 0)
    def _():
        xs_ref[...] = jnp.zeros_like(xs_ref)

    slot = lax.broadcasted_iota(jnp.int32, (GATHER_WINDOW, TOKEN_BLOCK), 0)
    for kk in range(blocks_per_step):
        kb = kc * blocks_per_step + kk
        off = off_ref[(b * N_EXPERTS + e) * n_tb + kb]
        base = pl.multiple_of((off >> 4) << 4, BF16_SUBLANES)
        onehot = (pos_ref[0, 0, 0, kk:kk + 1, :] - base == slot).astype(BF16)
        rows = _dot(onehot, h_ref[0, kk * TOKEN_BLOCK:(kk + 1) * TOKEN_BLOCK]).astype(BF16)
        old = xs_ref[0, 0, pl.ds(base, GATHER_WINDOW), :]
        xs_ref[0, 0, pl.ds(base, GATHER_WINDOW), :] = jnp.where(slot[:, 0:1] >= off - base, rows, old)


def compact_tokens(h2, pos, block_off, cap_pad):
    bx, t, d = h2.shape
    n_tb = t // TOKEN_BLOCK
    blocks_per_step = min(4, n_tb)
    n_steps = n_tb // blocks_per_step
    pos5 = pos.reshape(bx, N_EXPERTS, n_steps, blocks_per_step, TOKEN_BLOCK)
    grid_spec = pltpu.PrefetchScalarGridSpec(
        num_scalar_prefetch=1,
        grid=(bx, N_EXPERTS, n_steps),
        in_specs=[
            pl.BlockSpec((1, 1, 1, blocks_per_step, TOKEN_BLOCK), lambda b, e, k, s: (b, e, k, 0, 0)),
            pl.BlockSpec((1, blocks_per_step * TOKEN_BLOCK, d), lambda b, e, k, s: (b, k, 0)),
        ],
        out_specs=pl.BlockSpec((1, 1, cap_pad, d), lambda b, e, k, s: (b, e, 0, 0)),
    )
    return pl.pallas_call(
        functools.partial(_gather_kernel, n_tb=n_tb, blocks_per_step=blocks_per_step),
        grid_spec=grid_spec,
        out_shape=jax.ShapeDtypeStruct((bx, N_EXPERTS, cap_pad, d), BF16),
        compiler_params=_params("parallel", "parallel", "arbitrary"),
        name="compact_tokens",
    )(block_off.reshape(-1), pos5, h2)


def _ffn_kernel(xs_ref, wg_ref, wu_ref, wd_ref, ye_ref, *, cap):
    x = xs_ref[0, 0]
    gate = _dot(x, wg_ref[0])
    up = _dot(x, wu_ref[0])
    hid = (gate * jax.nn.sigmoid(gate) * up).astype(BF16)
    ye_ref[0, 0, 0:cap] = _dot(hid, wd_ref[0]).astype(BF16)
    ye_ref[0, 0, cap:] = jnp.zeros((ye_ref.shape[2] - cap, ye_ref.shape[3]), BF16)


def expert_ffn(xs, wg, wu, wd, cap):
    bx, n_e, cap_pad, d = xs.shape
    ff = wg.shape[-1]
    return pl.pallas_call(
        functools.partial(_ffn_kernel, cap=cap),
        grid=(n_e, bx),
        in_specs=[
            pl.BlockSpec((1, 1, cap, d), lambda e, b: (b, e, 0, 0)),
            pl.BlockSpec((1, d, ff), lambda e, b: (e, 0, 0)),
            pl.BlockSpec((1, d, ff), lambda e, b: (e, 0, 0)),
            pl.BlockSpec((1, ff, d), lambda e, b: (e, 0, 0)),
        ],
        out_specs=pl.BlockSpec((1, 1, cap_pad, d), lambda e, b: (b, e, 0, 0)),
        out_shape=jax.ShapeDtypeStruct((bx, n_e, cap_pad, d), BF16),
        compiler_params=_params("parallel", "parallel"),
        name="expert_ffn",
    )(xs, wg, wu, wd)


def _combine_kernel(off_ref, post_ref, aff_ref, ye_ref, x1_ref, g2_ref, lng_ref, lnb_ref,
                    out_ref, acc_ref, *, n_tb, blocks_per_step, alpha):
    b, tq, e = pl.program_id(0), pl.program_id(1), pl.program_id(2)

    @pl.when(e == 0)
    def _():
        acc_ref[...] = jnp.zeros_like(acc_ref)

    lane_e = lax.broadcasted_iota(jnp.int32, (TOKEN_BLOCK, N_EXPERTS), 1) == e
    slot = lax.broadcasted_iota(jnp.int32, (TOKEN_BLOCK, SCATTER_WINDOW), 1)
    for kk in range(blocks_per_step):
        kb = tq * blocks_per_step + kk
        rows = slice(kk * TOKEN_BLOCK, (kk + 1) * TOKEN_BLOCK)
        off = off_ref[(b * N_EXPERTS + e) * n_tb + kb]
        base = pl.multiple_of((off >> 4) << 4, BF16_SUBLANES)
        pcol = jnp.sum(jnp.where(lane_e, post_ref[0, rows, :], 0), axis=1, keepdims=True)
        gcol = jnp.sum(jnp.where(lane_e, aff_ref[0, rows, :], 0.0), axis=1, keepdims=True)
        onehot = (pcol - base == slot).astype(BF16)
        contrib = _dot(onehot, ye_ref[0, 0, pl.ds(base, SCATTER_WINDOW), :])
        acc_ref[rows, :] += contrib * gcol

    @pl.when(e == N_EXPERTS - 1)
    def _():
        y = alpha * x1_ref[0] + g2_ref[0] * acc_ref[...]
        out_ref[0] = _layer_norm(y) * lng_ref[...] + lnb_ref[...]


def combine_and_norm(ye, pos_t, aff, block_off, x1, g2, ln_g, ln_b, alpha):
    bx, t, d = x1.shape
    cap_pad = ye.shape[2]
    n_tb = t // TOKEN_BLOCK
    blocks_per_step = min(4, n_tb)
    tq = blocks_per_step * TOKEN_BLOCK
    tok = lambda b, q, e, s: (b, q, 0)
    const = lambda b, q, e, s: (0, 0)
    grid_spec = pltpu.PrefetchScalarGridSpec(
        num_scalar_prefetch=1,
        grid=(bx, t // tq, N_EXPERTS),
        in_specs=[
            pl.BlockSpec((1, tq, N_EXPERTS), tok), pl.BlockSpec((1, tq, N_EXPERTS), tok),
            pl.BlockSpec((1, 1, cap_pad, d), lambda b, q, e, s: (b, e, 0, 0)),
            pl.BlockSpec((1, tq, d), tok),
            pl.BlockSpec((1, 1, d), lambda b, q, e, s: (b, 0, 0)),
            pl.BlockSpec((1, d), const), pl.BlockSpec((1, d), const),
        ],
        out_specs=pl.BlockSpec((1, tq, d), tok),
        scratch_shapes=[pltpu.VMEM((tq, d), F32)],
    )
    return pl.pallas_call(
        functools.partial(_combine_kernel, n_tb=n_tb, blocks_per_step=blocks_per_step, alpha=alpha),
        grid_spec=grid_spec,
        out_shape=jax.ShapeDtypeStruct((bx, t, d), F32),
        compiler_params=_params("parallel", "parallel", "arbitrary"),
        name="combine_and_norm",
    )(block_off.reshape(-1), pos_t, aff, ye, x1, g2, ln_g.reshape(1, d), ln_b.reshape(1, d))


def _split_bf16(w):
    hi = w.astype(BF16)
    return hi, (w - hi.astype(F32)).astype(BF16)


def _mixer_tail(oa, ob, oc, x, mod, lw, alpha):
    g1, sh2, sc2, g2 = mod
    t = x.shape[1]
    cap = EC_CAPACITY * t // N_EXPERTS
    cap_pad = cap + SCATTER_WINDOW
    x1, h2, aff_t, aff = out_projection(oa, ob, oc, x, g1, sh2, sc2, lw["w_out"], lw["ln1_g"], lw["ln1_b"],
                                        lw["wr_hi"], lw["wr_lo"], alpha)
    pos, pos_t, block_off = expert_choice_select(aff_t, cap)
    xs = compact_tokens(h2, pos, block_off, cap_pad)
    ye = expert_ffn(xs, lw["w_gate"], lw["w_up"], lw["w_down"], cap)
    return combine_and_norm(ye, pos_t, aff, block_off, x1, g2, lw["ln2_g"], lw["ln2_b"], alpha)


def kernel(x, c, ctx, c_ctx, w_mod, b_mod, w_in, a_sink, nat_bias, conv_w, conv_b, conv_ln_g, conv_ln_b,
           w_out, ln1_g, ln1_b, w_router, w_gate, w_up, w_down, ln2_g, ln2_b):
    bsz, n_lat, d = x.shape
    depth = w_mod.shape[0]
    alpha = (2 * depth) ** 0.25
    cos_t, sin_t = rope_tables(n_lat)

    cond = jnp.concatenate([c, c_ctx[None, :], jnp.zeros((8 - bsz - 1, d), F32)], axis=0)
    mods = adaln_all(cond, w_mod, b_mod)

    for l in range(depth):
        last = l == depth - 1
        wr_hi, wr_lo = _split_bf16(w_router[l].T)
        lw = dict(w_out=w_out[l].astype(BF16), ln1_g=ln1_g[l], ln1_b=ln1_b[l], wr_hi=wr_hi, wr_lo=wr_lo,
                  w_gate=w_gate[l].astype(BF16), w_up=w_up[l].astype(BF16), w_down=w_down[l].astype(BF16),
                  ln2_g=ln2_g[l], ln2_b=ln2_b[l])
        w_in_l = w_in[l].astype(BF16)
        lat = [mods[l, :bsz, k * d:(k + 1) * d][:, None, :] for k in range(N_MOD)]
        cm = [jnp.broadcast_to(mods[l, bsz, k * d:(k + 1) * d][None, None, :], (bsz, 1, d))
              for k in range(N_MOD)]
        conv_args = (conv_w[l], conv_b[l], conv_ln_g[l], conv_ln_b[l])

        qa_c, ka_c, va_c, qb_c, kb_c, vb_c, hc_c = in_projection(ctx, cm[0], cm[1], w_in_l, cos_t, sin_t, rope=False)
        if not last:
            oa_c, ob_c = context_attention(qa_c, qb_c, ka_c, va_c, kb_c, vb_c, a_sink[l])
            oc_c = conformer_conv(hc_c, *conv_args)
            ctx_new = _mixer_tail(oa_c, ob_c, oc_c, ctx, (cm[2], cm[3], cm[4], cm[5]), lw, alpha)

        qa, ka, va, qb, kb, vb, hc = in_projection(x, lat[0], lat[1], w_in_l, cos_t, sin_t, rope=True)
        oa = window_attention(qa, ka, va, ka_c, va_c, a_sink[l])
        ob = neighbourhood_attention(qb, kb, vb, kb_c, vb_c, nat_bias[l])
        oc = conformer_conv(hc, *conv_args)
        x = _mixer_tail(oa, ob, oc, x, (lat[2], lat[3], lat[4], lat[5]), lw, alpha)
        if not last:
            ctx = ctx_new
    return x
```

```python
import functools

import numpy as np
import jax
import jax.numpy as jnp
from jax import lax
from jax.experimental import pallas as pl
from jax.experimental.pallas import tpu as pltpu

HEAD_DIM = 64
GRID_W = 64
A_Q_HEADS = 8
A_KV_HEADS = 2
A_GROUP = A_Q_HEADS // A_KV_HEADS
A_WINDOW = 128
B_HEADS = 4
NA_ROWS = 8
NA_COLS = 16
C_CHANNELS = 256
C_CONV_WIDTH = 31
A_Q_W = A_Q_HEADS * HEAD_DIM
A_KV_W = A_KV_HEADS * HEAD_DIM
B_W = B_HEADS * HEAD_DIM
OFF_AK = A_Q_W
OFF_AV = OFF_AK + A_KV_W
OFF_BQ = OFF_AV + A_KV_W
OFF_BK = OFF_BQ + B_W
OFF_BV = OFF_BK + B_W
OFF_C = OFF_BV + B_W
IN_WIDTH = OFF_C + 2 * C_CHANNELS
ROPE_WIDTH = A_Q_W + A_KV_W
N_EXPERTS = 16
EC_CAPACITY = 2
ROPE_BASE = 10000.0
LN_EPS = 1e-6
N_MOD = 6
NEG_INF = -1e30
QK_SCALE = HEAD_DIM ** -0.5

LANES = 128
BF16_SUBLANES = 16
MXU_DEPTH = 256
OFFSET_BLOCK = 128
GATHER_BLOCK = MXU_DEPTH
GATHER_WINDOW = GATHER_BLOCK + BF16_SUBLANES
SCATTER_BLOCK = OFFSET_BLOCK
SCATTER_WINDOW = MXU_DEPTH
SLOT_PAD = 3 * LANES
EXPERTS_PER_GATHER_STEP = 4
VMEM_LIMIT = 56 * 1024 * 1024

F32 = jnp.float32
BF16 = jnp.bfloat16


def _dot(a, b):
    return jnp.dot(a, b, preferred_element_type=F32)


def _dot_t(a, b):
    return lax.dot_general(a, b, (((1,), (1,)), ((), ())), preferred_element_type=F32)


def _layer_norm(x):
    mu = jnp.mean(x, axis=-1, keepdims=True)
    xc = x - mu
    var = jnp.mean(xc * xc, axis=-1, keepdims=True)
    return xc * lax.rsqrt(var + LN_EPS)


def _params(*sem):
    return pltpu.CompilerParams(dimension_semantics=sem, vmem_limit_bytes=VMEM_LIMIT)


def _mod_kernel(cond_ref, w_ref, b_ref, out_ref):
    cnd = cond_ref[...]
    act = cnd * jax.nn.sigmoid(cnd)
    out_ref[0] = jnp.dot(act, w_ref[0], preferred_element_type=F32,
                         precision=lax.Precision.HIGHEST) + b_ref[0]


def adaln_all(cond, w_mod, b_mod):
    n_layers, d, width = w_mod.shape
    rows = cond.shape[0]
    tn = 1536
    return pl.pallas_call(
        _mod_kernel,
        grid=(n_layers, width // tn),
        in_specs=[
            pl.BlockSpec((rows, d), lambda l, j: (0, 0)),
            pl.BlockSpec((1, d, tn), lambda l, j: (l, 0, j)),
            pl.BlockSpec((1, 1, tn), lambda l, j: (l, 0, j)),
        ],
        out_specs=pl.BlockSpec((1, rows, tn), lambda l, j: (l, 0, j)),
        out_shape=jax.ShapeDtypeStruct((n_layers, rows, width), F32),
        compiler_params=_params("parallel", "parallel"),
        name="adaln",
    )(cond, w_mod, b_mod.reshape(n_layers, 1, width))


def _inproj_kernel(x_ref, sh_ref, sc_ref, w_ref, cos_ref, sin_ref,
                   qa_ref, ka_ref, va_ref, qb_ref, kb_ref, vb_ref, hc_ref, *, rope):
    x = x_ref[0]
    h = _layer_norm(x) * (1.0 + sc_ref[0]) + sh_ref[0]
    u = _dot(h.astype(BF16), w_ref[...])

    def rotated(col):
        xq = u[:, col:col + LANES]
        if not rope:
            return xq
        lane = lax.broadcasted_iota(jnp.int32, xq.shape, 1)
        first = (lane & (HEAD_DIM // 2 - 1)) < (HEAD_DIM // 4)
        partner = jnp.where(first, pltpu.roll(xq, LANES - HEAD_DIM // 4, 1),
                            pltpu.roll(xq, HEAD_DIM // 4, 1))
        return xq * cos_ref[...] + partner * sin_ref[...]

    rot = [rotated(col) for col in range(0, ROPE_WIDTH, LANES)]
    n_q = A_Q_W // LANES
    qa_ref[0] = (jnp.concatenate(rot[:n_q], axis=1) * QK_SCALE).astype(BF16)
    ka_ref[0] = jnp.concatenate(rot[n_q:], axis=1).astype(BF16)
    va_ref[0] = u[:, OFF_AV:OFF_BQ].astype(BF16)
    qb_ref[0] = (u[:, OFF_BQ:OFF_BK] * QK_SCALE).astype(BF16)
    kb_ref[0] = u[:, OFF_BK:OFF_BV].astype(BF16)
    vb_ref[0] = u[:, OFF_BV:OFF_C].astype(BF16)
    a = u[:, OFF_C:OFF_C + C_CHANNELS]
    gate = u[:, OFF_C + C_CHANNELS:]
    hc_ref[0] = a * jax.nn.sigmoid(gate)


def in_projection(x, shift, scale, w_in_bf16, cos_t, sin_t, *, rope):
    bx, t, d = x.shape
    tm = min(512, t)
    widths = (A_Q_W, A_KV_W, A_KV_W, B_W, B_W, B_W, C_CHANNELS)
    dtypes = (BF16,) * 6 + (F32,)
    tok = lambda b, i: (b, i, 0)
    per_b = lambda b, i: (b, 0, 0)
    return pl.pallas_call(
        functools.partial(_inproj_kernel, rope=rope),
        grid=(bx, t // tm),
        in_specs=[
            pl.BlockSpec((1, tm, d), tok),
            pl.BlockSpec((1, 1, d), per_b),
            pl.BlockSpec((1, 1, d), per_b),
            pl.BlockSpec((d, IN_WIDTH), lambda b, i: (0, 0)),
            pl.BlockSpec((tm, LANES), lambda b, i: (i, 0)),
            pl.BlockSpec((tm, LANES), lambda b, i: (i, 0)),
        ],
        out_specs=[pl.BlockSpec((1, tm, w), tok) for w in widths],
        out_shape=[jax.ShapeDtypeStruct((bx, t, w), dt) for w, dt in zip(widths, dtypes)],
        compiler_params=_params("parallel", "parallel"),
        name="in_projection",
    )(x, shift, scale, w_in_bf16, cos_t, sin_t)


def rope_tables(n_tokens):
    t = jnp.arange(n_tokens, dtype=jnp.int32)
    row = (t // GRID_W).astype(F32)[:, None]
    col = (t % GRID_W).astype(F32)[:, None]
    n_freq = HEAD_DIM // 4
    inv_freq = ROPE_BASE ** (-jnp.arange(n_freq, dtype=F32) / n_freq)
    ang_r = row * inv_freq
    ang_c = col * inv_freq
    cos_h = jnp.concatenate([jnp.cos(ang_r), jnp.cos(ang_r), jnp.cos(ang_c), jnp.cos(ang_c)], axis=1)
    sin_h = jnp.concatenate([-jnp.sin(ang_r), jnp.sin(ang_r), -jnp.sin(ang_c), jnp.sin(ang_c)], axis=1)
    reps = LANES // HEAD_DIM
    return jnp.tile(cos_h, (1, reps)), jnp.tile(sin_h, (1, reps))


def _attend(q, score_parts, values, sink=None):
    m = score_parts[0].max(axis=-1, keepdims=True)
    for s in score_parts[1:]:
        m = jnp.maximum(m, s.max(axis=-1, keepdims=True))
    if sink is not None:
        m = jnp.maximum(m, sink)
    den = jnp.zeros_like(m) if sink is None else jnp.exp(sink - m)
    out = jnp.zeros((q.shape[0], HEAD_DIM), F32)
    for s, v in zip(score_parts, values):
        p = jnp.exp(s - m)
        den = den + p.sum(axis=-1, keepdims=True)
        out = out + _dot(p.astype(BF16), v)
    return out / den


def _attn_a_kernel(sink_ref, q_ref, kp_ref, kc_ref, kn_ref, vp_ref, vc_ref, vn_ref,
                   kctx_ref, vctx_ref, out_ref, *, n_lat, tq):
    i = pl.program_id(1)
    k_win = jnp.concatenate([kp_ref[0], kc_ref[0], kn_ref[0]], axis=0)
    v_win = jnp.concatenate([vp_ref[0], vc_ref[0], vn_ref[0]], axis=0)
    kctx = kctx_ref[0]
    vctx = vctx_ref[0]
    sub = A_WINDOW
    span = 3 * A_WINDOW
    rows = A_GROUP * sub
    row_i = lax.broadcasted_iota(jnp.int32, (rows, span), 0)
    col_i = lax.broadcasted_iota(jnp.int32, (rows, span), 1)
    rel = col_i - A_WINDOW - (row_i & (sub - 1))
    in_band = (rel <= A_WINDOW) & (rel >= -A_WINDOW)
    group_of_row = lax.broadcasted_iota(jnp.int32, (rows, 1), 0) >> (sub.bit_length() - 1)
    sinks = []
    for hk in range(A_KV_HEADS):
        sink = jnp.zeros((rows, 1), F32)
        for g in range(A_GROUP):
            sink = jnp.where(group_of_row == g, sink_ref[hk * A_GROUP + g], sink)
        sinks.append(sink)
    for j in range(tq // sub):
        kpos = i * tq + j * sub - A_WINDOW + col_i
        valid = in_band & (kpos >= 0) & (kpos < n_lat)
        q_rows = q_ref[0, j * sub:(j + 1) * sub]
        k_sub = k_win[j * sub:j * sub + span]
        v_sub = v_win[j * sub:j * sub + span]
        for hk in range(A_KV_HEADS):
            sl = slice(hk * HEAD_DIM, (hk + 1) * HEAD_DIM)
            heads = [hk * A_GROUP + g for g in range(A_GROUP)]
            q = jnp.concatenate([q_rows[:, h * HEAD_DIM:(h + 1) * HEAD_DIM] for h in heads], axis=0)
            s_win = jnp.where(valid, _dot_t(q, k_sub[:, sl]), NEG_INF)
            s_ctx = _dot_t(q, kctx[:, sl])
            o = _attend(q, [s_win, s_ctx], [v_sub[:, sl], vctx[:, sl]], sink=sinks[hk]).astype(BF16)
            for g, h in enumerate(heads):
                out_ref[0, j * sub:(j + 1) * sub, h * HEAD_DIM:(h + 1) * HEAD_DIM] = o[g * sub:(g + 1) * sub]


def window_attention(qa, ka, va, kc_a, vc_a, sink):
    bsz, n_lat, _ = qa.shape
    n_ctx = kc_a.shape[1]
    tq = min(256, n_lat)
    w = A_WINDOW
    per = tq // w
    last = n_lat // w - 1
    prev = lambda b, i, s: (b, jnp.maximum(i * per - 1, 0), 0)
    cur = lambda b, i, s: (b, i, 0)
    nxt = lambda b, i, s: (b, jnp.minimum((i + 1) * per, last), 0)
    ctx = lambda b, i, s: (b, 0, 0)
    kv_specs = [pl.BlockSpec((1, w, A_KV_W), prev), pl.BlockSpec((1, tq, A_KV_W), cur),
                pl.BlockSpec((1, w, A_KV_W), nxt)]
    grid_spec = pltpu.PrefetchScalarGridSpec(
        num_scalar_prefetch=1,
        grid=(bsz, n_lat // tq),
        in_specs=[pl.BlockSpec((1, tq, A_Q_W), cur)] + kv_specs + kv_specs + [
            pl.BlockSpec((1, n_ctx, A_KV_W), ctx), pl.BlockSpec((1, n_ctx, A_KV_W), ctx)],
        out_specs=pl.BlockSpec((1, tq, A_Q_W), cur),
    )
    return pl.pallas_call(
        functools.partial(_attn_a_kernel, n_lat=n_lat, tq=tq),
        grid_spec=grid_spec,
        out_shape=jax.ShapeDtypeStruct((bsz, n_lat, A_Q_W), BF16),
        compiler_params=_params("parallel", "parallel"),
        name="window_attention",
    )(sink, qa, ka, ka, ka, va, va, va, kc_a, vc_a)


NB_Q_ROWS = 4


def _attn_b_kernel(q_ref, kp_ref, kc_ref, kn_ref, vp_ref, vc_ref, vn_ref,
                   kctx_ref, vctx_ref, bias_ref, out_ref):
    k_win = jnp.concatenate([kp_ref[0], kc_ref[0], kn_ref[0]], axis=0)
    v_win = jnp.concatenate([vp_ref[0], vc_ref[0], vn_ref[0]], axis=0)
    q_all = q_ref[0]
    kctx = kctx_ref[0]
    vctx = vctx_ref[0]
    for h in range(B_HEADS):
        sl = slice(h * HEAD_DIM, (h + 1) * HEAD_DIM)
        q = q_all[:, sl]
        s_nb = _dot_t(q, k_win[:, sl]) + bias_ref[0, h]
        s_ctx = _dot_t(q, kctx[:, sl])
        o = _attend(q, [s_nb, s_ctx], [v_win[:, sl], vctx[:, sl]])
        out_ref[0, :, sl] = o.astype(BF16)


def neighbourhood_bias(rel_bias, n_lat):
    rows = n_lat // GRID_W
    kr_n = min(NA_ROWS, rows)
    n_blocks = rows // NB_Q_ROWS
    n_heads, n_dr, n_dc = rel_bias.shape
    cols = np.arange(GRID_W)
    c_start = np.clip(cols - NA_COLS // 2, 0, GRID_W - NA_COLS)
    col_ok = (cols[None, :] >= c_start[:, None]) & (cols[None, :] < c_start[:, None] + NA_COLS)
    dc = np.clip(cols[None, :] - cols[:, None], -(NA_COLS - 1), NA_COLS - 1) + NA_COLS - 1
    pick_dc = (dc.reshape(-1)[None, :] == np.arange(n_dc)[:, None]).astype(np.float32)
    toeplitz = jnp.dot(rel_bias.reshape(n_heads * n_dr, n_dc), pick_dc, precision=lax.Precision.HIGHEST)
    toeplitz = jnp.where(col_ok.reshape(-1), toeplitz, NEG_INF).reshape(n_heads, n_dr, GRID_W, GRID_W)
    q_rl = np.arange(NB_Q_ROWS)
    k_rl = np.arange(3 * NB_Q_ROWS)
    row_ok, dr = [], []
    for j in sorted({0, min(1, n_blocks - 1), n_blocks - 1}):
        r = NB_Q_ROWS * j + q_rl
        kr = NB_Q_ROWS * (j - 1) + k_rl
        r_start = np.clip(r - kr_n // 2, 0, rows - kr_n)
        ok = (kr[None, :] >= r_start[:, None]) & (kr[None, :] < r_start[:, None] + kr_n)
        row_ok.append(ok & (kr[None, :] >= 0) & (kr[None, :] < rows))
        dr.append(np.clip(kr[None, :] - r[:, None] + NA_ROWS - 1, 0, n_dr - 1))
    row_ok = np.stack(row_ok)
    dr = np.stack(dr)
    tiles = jnp.stack([toeplitz[:, int(i)] for i in dr.reshape(-1)], axis=1)
    tiles = tiles.reshape((n_heads,) + dr.shape + (GRID_W, GRID_W))
    tiles = jnp.where(row_ok[None, :, :, :, None, None], tiles, NEG_INF)
    table = tiles.transpose(1, 0, 2, 4, 3, 5).reshape(
        dr.shape[0], n_heads, NB_Q_ROWS * GRID_W, 3 * NB_Q_ROWS * GRID_W)
    return table, n_blocks


def neighbourhood_attention(qb, kb, vb, kc_b, vc_b, rel_bias):
    bsz, n_lat, _ = qb.shape
    n_ctx = kc_b.shape[1]
    table, n_blocks = neighbourhood_bias(rel_bias, n_lat)
    n_var = table.shape[0]
    tq = NB_Q_ROWS * GRID_W
    prev = lambda b, j: (b, jnp.maximum(j - 1, 0), 0)
    cur = lambda b, j: (b, j, 0)
    nxt = lambda b, j: (b, jnp.minimum(j + 1, n_blocks - 1), 0)
    ctx = lambda b, j: (b, 0, 0)

    def variant(b, j):
        v = jnp.where(j == 0, 0, jnp.where(j == n_blocks - 1, n_var - 1, min(1, n_var - 1)))
        return (v, 0, 0, 0)

    kv_specs = [pl.BlockSpec((1, tq, B_W), prev), pl.BlockSpec((1, tq, B_W), cur),
                pl.BlockSpec((1, tq, B_W), nxt)]
    return pl.pallas_call(
        _attn_b_kernel,
        grid=(bsz, n_blocks),
        in_specs=[pl.BlockSpec((1, tq, B_W), cur)] + kv_specs + kv_specs + [
            pl.BlockSpec((1, n_ctx, B_W), ctx), pl.BlockSpec((1, n_ctx, B_W), ctx),
            pl.BlockSpec((1, B_HEADS, tq, 3 * tq), variant)],
        out_specs=pl.BlockSpec((1, tq, B_W), cur),
        out_shape=jax.ShapeDtypeStruct((bsz, n_lat, B_W), BF16),
        compiler_params=_params("parallel", "parallel"),
        name="neighbourhood_attention",
    )(qb, kb, kb, kb, vb, vb, vb, kc_b, vc_b, table)


def _ctx_attn_kernel(sink_ref, qa_ref, qb_ref, ka_ref, va_ref, kb_ref, vb_ref, oa_ref, ob_ref):
    qa, qb = qa_ref[0], qb_ref[0]
    ka, va, kb, vb = ka_ref[0], va_ref[0], kb_ref[0], vb_ref[0]
    for hq in range(A_Q_HEADS):
        sl = slice(hq * HEAD_DIM, (hq + 1) * HEAD_DIM)
        hk = hq // A_GROUP
        kv = slice(hk * HEAD_DIM, (hk + 1) * HEAD_DIM)
        q = qa[:, sl]
        o = _attend(q, [_dot_t(q, ka[:, kv])], [va[:, kv]], sink=sink_ref[hq])
        oa_ref[0, :, sl] = o.astype(BF16)
    for h in range(B_HEADS):
        sl = slice(h * HEAD_DIM, (h + 1) * HEAD_DIM)
        q = qb[:, sl]
        o = _attend(q, [_dot_t(q, kb[:, sl])], [vb[:, sl]])
        ob_ref[0, :, sl] = o.astype(BF16)


def context_attention(qa, qb, ka, va, kb, vb, sink):
    bsz, n_ctx, _ = qa.shape
    blk = lambda w: pl.BlockSpec((1, n_ctx, w), lambda b, s: (b, 0, 0))
    grid_spec = pltpu.PrefetchScalarGridSpec(
        num_scalar_prefetch=1,
        grid=(bsz,),
        in_specs=[blk(A_Q_W), blk(B_W), blk(A_KV_W), blk(A_KV_W), blk(B_W), blk(B_W)],
        out_specs=[blk(A_Q_W), blk(B_W)],
    )
    return pl.pallas_call(
        _ctx_attn_kernel,
        grid_spec=grid_spec,
        out_shape=[jax.ShapeDtypeStruct((bsz, n_ctx, A_Q_W), BF16),
                   jax.ShapeDtypeStruct((bsz, n_ctx, B_W), BF16)],
        compiler_params=_params("parallel"),
        name="context_attention",
    )(sink, qa, qb, ka, va, kb, vb)


CONV_HALO = 16


def _conv_kernel(prev_ref, cur_ref, next_ref, w_ref, b_ref, g_ref, beta_ref, out_ref, ext_ref, *, ts):
    i = pl.program_id(1)
    n_i = pl.num_programs(1)
    ext_ref[0:CONV_HALO] = jnp.where(i > 0, prev_ref[0], 0.0)
    ext_ref[CONV_HALO:CONV_HALO + ts] = cur_ref[0]
    ext_ref[CONV_HALO + ts:] = jnp.where(i < n_i - 1, next_ref[0], 0.0)
    half = C_CONV_WIDTH // 2
    acc = jnp.zeros((ts, C_CHANNELS), F32)
    for k in range(C_CONV_WIDTH):
        acc = acc + ext_ref[pl.ds(CONV_HALO - half + k, ts)] * w_ref[k:k + 1]
    y = _layer_norm(acc + b_ref[...]) * g_ref[...] + beta_ref[...]
    out_ref[0] = (y * jax.nn.sigmoid(y)).astype(BF16)


def conformer_conv(hc, conv_w, conv_b, ln_g, ln_b):
    bx, t, ch = hc.shape
    ts = min(512, t)
    per = ts // CONV_HALO
    last = t // CONV_HALO - 1
    row = lambda v: v.reshape(1, ch)
    const = lambda b, i: (0, 0)
    return pl.pallas_call(
        functools.partial(_conv_kernel, ts=ts),
        grid=(bx, t // ts),
        in_specs=[
            pl.BlockSpec((1, CONV_HALO, ch), lambda b, i: (b, jnp.maximum(i * per - 1, 0), 0)),
            pl.BlockSpec((1, ts, ch), lambda b, i: (b, i, 0)),
            pl.BlockSpec((1, CONV_HALO, ch), lambda b, i: (b, jnp.minimum((i + 1) * per, last), 0)),
            pl.BlockSpec((C_CONV_WIDTH, ch), const),
            pl.BlockSpec((1, ch), const), pl.BlockSpec((1, ch), const), pl.BlockSpec((1, ch), const),
        ],
        out_specs=pl.BlockSpec((1, ts, ch), lambda b, i: (b, i, 0)),
        out_shape=jax.ShapeDtypeStruct((bx, t, ch), BF16),
        scratch_shapes=[pltpu.VMEM((ts + 2 * CONV_HALO, ch), F32)],
        compiler_params=_params("parallel", "parallel"),
        name="conformer_conv",
    )(hc, hc, hc, conv_w, row(conv_b), row(ln_g), row(ln_b))


def _outproj_kernel(oa_ref, ob_ref, oc_ref, x_ref, g1_ref, sh_ref, sc_ref, w_ref, lng_ref, lnb_ref,
                    wr_hi_ref, wr_lo_ref, x1_ref, h2_ref, afft_ref, aff_ref, *, alpha):
    o = (_dot(oa_ref[0], w_ref[0:A_Q_W])
         + _dot(ob_ref[0], w_ref[A_Q_W:A_Q_W + B_W])
         + _dot(oc_ref[0], w_ref[A_Q_W + B_W:]))
    y = _layer_norm(alpha * x_ref[0] + g1_ref[0] * o) * lng_ref[...] + lnb_ref[...]
    x1_ref[0] = y
    h2 = _layer_norm(y) * (1.0 + sc_ref[0]) + sh_ref[0]
    h_hi = h2.astype(BF16)
    h2_ref[0] = h_hi
    h_lo = (h2 - h_hi.astype(F32)).astype(BF16)
    w_hi, w_lo = wr_hi_ref[...], wr_lo_ref[...]
    logits_t = _dot_t(w_hi, h_hi) + _dot_t(w_hi, h_lo) + _dot_t(w_lo, h_hi)
    e_t = jnp.exp(logits_t - logits_t.max(axis=0, keepdims=True))
    afft_ref[0] = e_t / e_t.sum(axis=0, keepdims=True)
    logits = _dot_t(h_hi, w_hi) + _dot_t(h_lo, w_hi) + _dot_t(h_hi, w_lo)
    e_n = jnp.exp(logits - logits.max(axis=1, keepdims=True))
    aff_ref[0] = e_n / e_n.sum(axis=1, keepdims=True)


def out_projection(oa, ob, oc, x, g1, sh2, sc2, w_out_bf16, ln_g, ln_b, wr_hi, wr_lo, alpha):
    bx, t, d = x.shape
    tm = min(512, t)
    tok = lambda b, i: (b, i, 0)
    per_b = lambda b, i: (b, 0, 0)
    const = lambda b, i: (0, 0)
    vec = pl.BlockSpec((1, d), const)
    return pl.pallas_call(
        functools.partial(_outproj_kernel, alpha=alpha),
        grid=(bx, t // tm),
        in_specs=[
            pl.BlockSpec((1, tm, A_Q_W), tok), pl.BlockSpec((1, tm, B_W), tok),
            pl.BlockSpec((1, tm, C_CHANNELS), tok), pl.BlockSpec((1, tm, d), tok),
            pl.BlockSpec((1, 1, d), per_b), pl.BlockSpec((1, 1, d), per_b), pl.BlockSpec((1, 1, d), per_b),
            pl.BlockSpec(w_out_bf16.shape, const), vec, vec,
            pl.BlockSpec((N_EXPERTS, d), const), pl.BlockSpec((N_EXPERTS, d), const),
        ],
        out_specs=[pl.BlockSpec((1, tm, d), tok), pl.BlockSpec((1, tm, d), tok),
                   pl.BlockSpec((1, N_EXPERTS, tm), lambda b, i: (b, 0, i)),
                   pl.BlockSpec((1, tm, N_EXPERTS), tok)],
        out_shape=[jax.ShapeDtypeStruct((bx, t, d), F32), jax.ShapeDtypeStruct((bx, t, d), BF16),
                   jax.ShapeDtypeStruct((bx, N_EXPERTS, t), F32),
                   jax.ShapeDtypeStruct((bx, t, N_EXPERTS), F32)],
        compiler_params=_params("parallel", "parallel"),
        name="out_projection",
    )(oa, ob, oc, x, g1, sh2, sc2, w_out_bf16, ln_g.reshape(1, d), ln_b.reshape(1, d), wr_hi, wr_lo)


def _select_kernel(afft_ref, pos_ref, post_ref, off_ref, *, cap, n_tok):
    aff = afft_ref[0]

    def bit_step(j, bits):
        cand = bits | (jnp.int32(1) << (30 - j))
        cnt = jnp.sum((aff >= pltpu.bitcast(cand, F32)).astype(jnp.int32), axis=1, keepdims=True)
        return jnp.where(cnt >= cap, cand, bits)

    thr = pltpu.bitcast(lax.fori_loop(0, 31, bit_step, jnp.zeros((N_EXPERTS, 1), jnp.int32)), F32)
    above = (aff > thr).astype(F32)
    tied = (aff == thr).astype(F32)
    need = cap - jnp.sum(above, axis=1, keepdims=True)

    blk = LANES
    n_blk = n_tok // blk
    r_i = lax.broadcasted_iota(jnp.int32, (blk, blk), 0)
    c_i = lax.broadcasted_iota(jnp.int32, (blk, blk), 1)
    strict_upper = (r_i < c_i).astype(BF16)
    eye = (r_i == c_i).astype(BF16)

    run_tied = jnp.zeros((N_EXPERTS, 1), F32)
    run_sel = jnp.zeros((N_EXPERTS, 1), F32)
    offs = []
    for kb in range(n_blk):
        sl = slice(kb * blk, (kb + 1) * blk)
        offs.append(run_sel)
        t_f = tied[:, sl]
        rank_tied = run_tied + _dot(t_f.astype(BF16), strict_upper)
        run_tied = run_tied + t_f.sum(axis=1, keepdims=True)
        s_f = above[:, sl] + t_f * (rank_tied < need).astype(F32)
        rank = run_sel + _dot(s_f.astype(BF16), strict_upper)
        run_sel = run_sel + s_f.sum(axis=1, keepdims=True)
        pos = jnp.where(s_f > 0.5, rank.astype(jnp.int32), -1)
        pos_ref[0, :, sl] = pos
        p1 = pos + 1
        hi = (p1 >> 5).astype(F32).astype(BF16)
        lo = (p1 & 31).astype(F32).astype(BF16)
        p1_t = _dot_t(eye, hi) * 32.0 + _dot_t(eye, lo)
        post_ref[0, sl, :] = p1_t.astype(jnp.int32) - 1
    off_ref[0] = jnp.concatenate(offs, axis=1).astype(jnp.int32)


def expert_choice_select(aff_t, cap):
    bx, n_e, t = aff_t.shape
    n_tb = t // OFFSET_BLOCK
    return pl.pallas_call(
        functools.partial(_select_kernel, cap=cap, n_tok=t),
        grid=(bx,),
        in_specs=[pl.BlockSpec((1, n_e, t), lambda b: (b, 0, 0))],
        out_specs=[pl.BlockSpec((1, n_e, t), lambda b: (b, 0, 0)),
                   pl.BlockSpec((1, t, n_e), lambda b: (b, 0, 0)),
                   pl.BlockSpec((1, n_e, n_tb), lambda b: (b, 0, 0))],
        out_shape=[jax.ShapeDtypeStruct((bx, n_e, t), jnp.int32),
                   jax.ShapeDtypeStruct((bx, t, n_e), jnp.int32),
                   jax.ShapeDtypeStruct((bx, n_e, n_tb), jnp.int32)],
        compiler_params=_params("parallel"),
        name="expert_choice_select",
    )(aff_t)


def _gather_kernel(off_ref, pos_ref, h_ref, xs_ref, *, n_off, blocks_per_step):
    b, eg, kc = pl.program_id(0), pl.program_id(1), pl.program_id(2)

    @pl.when(kc == 0)
    def _():
        xs_ref[...] = jnp.zeros_like(xs_ref)

    slot = lax.broadcasted_iota(jnp.int32, (GATHER_WINDOW, GATHER_BLOCK), 0)
    for ee in range(EXPERTS_PER_GATHER_STEP):
        e = eg * EXPERTS_PER_GATHER_STEP + ee
        for kk in range(blocks_per_step):
            kb = kc * blocks_per_step + kk
            off = off_ref[(b * N_EXPERTS + e) * n_off + kb * (GATHER_BLOCK // OFFSET_BLOCK)]
            base = pl.multiple_of((off >> 4) << 4, BF16_SUBLANES)
            onehot = (pos_ref[0, ee, 0, kk:kk + 1, :] - base == slot).astype(BF16)
            rows = _dot(onehot, h_ref[0, kk * GATHER_BLOCK:(kk + 1) * GATHER_BLOCK]).astype(BF16)
            old = xs_ref[0, ee, pl.ds(base, GATHER_WINDOW), :]
            xs_ref[0, ee, pl.ds(base, GATHER_WINDOW), :] = jnp.where(slot[:, 0:1] >= off - base, rows, old)


def compact_tokens(h2, pos, block_off, cap_pad):
    bx, t, d = h2.shape
    n_tb = t // GATHER_BLOCK
    blocks_per_step = min(4, n_tb)
    n_steps = n_tb // blocks_per_step
    epg = EXPERTS_PER_GATHER_STEP
    pos5 = pos.reshape(bx, N_EXPERTS, n_steps, blocks_per_step, GATHER_BLOCK)
    grid_spec = pltpu.PrefetchScalarGridSpec(
        num_scalar_prefetch=1,
        grid=(bx, N_EXPERTS // epg, n_steps),
        in_specs=[
            pl.BlockSpec((1, epg, 1, blocks_per_step, GATHER_BLOCK), lambda b, g, k, s: (b, g, k, 0, 0)),
            pl.BlockSpec((1, blocks_per_step * GATHER_BLOCK, d), lambda b, g, k, s: (b, k, 0)),
        ],
        out_specs=pl.BlockSpec((1, epg, cap_pad, d), lambda b, g, k, s: (b, g, 0, 0)),
    )
    return pl.pallas_call(
        functools.partial(_gather_kernel, n_off=t // OFFSET_BLOCK, blocks_per_step=blocks_per_step),
        grid_spec=grid_spec,
        out_shape=jax.ShapeDtypeStruct((bx, N_EXPERTS, cap_pad, d), BF16),
        compiler_params=_params("parallel", "parallel", "arbitrary"),
        name="compact_tokens",
    )(block_off.reshape(-1), pos5, h2)


def _ffn_kernel(xs_ref, wg_ref, wu_ref, wd_ref, ye_ref, wg_bf, wu_bf, wd_bf, *, cap, row_tile):
    @pl.when(pl.program_id(1) == 0)
    def _():
        wg_bf[...] = wg_ref[0].astype(BF16)
        wu_bf[...] = wu_ref[0].astype(BF16)
        wd_bf[...] = wd_ref[0].astype(BF16)

    for r0 in range(0, cap, row_tile):
        x = xs_ref[0, 0, r0:r0 + row_tile]
        gate = _dot(x, wg_bf[...])
        up = _dot(x, wu_bf[...])
        hid = (gate * jax.nn.sigmoid(gate) * up).astype(BF16)
        ye_ref[0, 0, r0:r0 + row_tile] = _dot(hid, wd_bf[...]).astype(BF16)
    ye_ref[0, 0, cap:] = jnp.zeros((ye_ref.shape[2] - cap, ye_ref.shape[3]), BF16)


def expert_ffn(xs, wg, wu, wd, cap):
    bx, n_e, cap_pad, d = xs.shape
    ff = wg.shape[-1]
    return pl.pallas_call(
        functools.partial(_ffn_kernel, cap=cap, row_tile=min(512, cap)),
        grid=(n_e, bx),
        in_specs=[
            pl.BlockSpec((1, 1, cap, d), lambda e, b: (b, e, 0, 0)),
            pl.BlockSpec((1, d, ff), lambda e, b: (e, 0, 0)),
            pl.BlockSpec((1, d, ff), lambda e, b: (e, 0, 0)),
            pl.BlockSpec((1, ff, d), lambda e, b: (e, 0, 0)),
        ],
        out_specs=pl.BlockSpec((1, 1, cap_pad, d), lambda e, b: (b, e, 0, 0)),
        out_shape=jax.ShapeDtypeStruct((bx, n_e, cap_pad, d), BF16),
        scratch_shapes=[pltpu.VMEM((d, ff), BF16), pltpu.VMEM((d, ff), BF16), pltpu.VMEM((ff, d), BF16)],
        compiler_params=_params("parallel", "arbitrary"),
        name="expert_ffn",
    )(xs, wg, wu, wd)


def _combine_kernel(off_ref, post_ref, aff_ref, ye_ref, x1_ref, g2_ref, lng_ref, lnb_ref,
                    out_ref, acc_ref, *, n_tb, blocks_per_step, alpha):
    b, tq, e = pl.program_id(0), pl.program_id(1), pl.program_id(2)

    @pl.when(e == 0)
    def _():
        acc_ref[...] = jnp.zeros_like(acc_ref)

    lane_e = lax.broadcasted_iota(jnp.int32, (SCATTER_BLOCK, N_EXPERTS), 1) == e
    slot = lax.broadcasted_iota(jnp.int32, (SCATTER_BLOCK, SCATTER_WINDOW), 1)
    for kk in range(blocks_per_step):
        kb = tq * blocks_per_step + kk
        rows = slice(kk * SCATTER_BLOCK, (kk + 1) * SCATTER_BLOCK)
        off = off_ref[(b * N_EXPERTS + e) * n_tb + kb * (SCATTER_BLOCK // OFFSET_BLOCK)]
        base = pl.multiple_of((off >> 4) << 4, BF16_SUBLANES)
        pcol = jnp.sum(jnp.where(lane_e, post_ref[0, rows, :], 0), axis=1, keepdims=True)
        gcol = jnp.sum(jnp.where(lane_e, aff_ref[0, rows, :], 0.0), axis=1, keepdims=True)
        onehot = (pcol - base == slot).astype(BF16)
        contrib = _dot(onehot, ye_ref[0, 0, pl.ds(base, SCATTER_WINDOW), :])
        acc_ref[rows, :] += contrib * gcol

    @pl.when(e == N_EXPERTS - 1)
    def _():
        y = alpha * x1_ref[0] + g2_ref[0] * acc_ref[...]
        out_ref[0] = _layer_norm(y) * lng_ref[...] + lnb_ref[...]


def combine_and_norm(ye, pos_t, aff, block_off, x1, g2, ln_g, ln_b, alpha):
    bx, t, d = x1.shape
    cap_pad = ye.shape[2]
    n_tb = t // OFFSET_BLOCK
    blocks_per_step = min(8, t // SCATTER_BLOCK)
    tq = blocks_per_step * SCATTER_BLOCK
    tok = lambda b, q, e, s: (b, q, 0)
    const = lambda b, q, e, s: (0, 0)
    grid_spec = pltpu.PrefetchScalarGridSpec(
        num_scalar_prefetch=1,
        grid=(bx, t // tq, N_EXPERTS),
        in_specs=[
            pl.BlockSpec((1, tq, N_EXPERTS), tok), pl.BlockSpec((1, tq, N_EXPERTS), tok),
            pl.BlockSpec((1, 1, cap_pad, d), lambda b, q, e, s: (b, e, 0, 0)),
            pl.BlockSpec((1, tq, d), tok),
            pl.BlockSpec((1, 1, d), lambda b, q, e, s: (b, 0, 0)),
            pl.BlockSpec((1, d), const), pl.BlockSpec((1, d), const),
        ],
        out_specs=pl.BlockSpec((1, tq, d), tok),
        scratch_shapes=[pltpu.VMEM((tq, d), F32)],
    )
    return pl.pallas_call(
        functools.partial(_combine_kernel, n_tb=n_tb, blocks_per_step=blocks_per_step, alpha=alpha),
        grid_spec=grid_spec,
        out_shape=jax.ShapeDtypeStruct((bx, t, d), F32),
        compiler_params=_params("parallel", "parallel", "arbitrary"),
        name="combine_and_norm",
    )(block_off.reshape(-1), pos_t, aff, ye, x1, g2, ln_g.reshape(1, d), ln_b.reshape(1, d))


def _split_bf16(w):
    hi = w.astype(BF16)
    return hi, (w - hi.astype(F32)).astype(BF16)


def _mixer_tail(oa, ob, oc, x, mod, lw, alpha):
    g1, sh2, sc2, g2 = mod
    t = x.shape[1]
    cap = EC_CAPACITY * t // N_EXPERTS
    cap_pad = cap + SLOT_PAD
    x1, h2, aff_t, aff = out_projection(oa, ob, oc, x, g1, sh2, sc2, lw["w_out"], lw["ln1_g"], lw["ln1_b"],
                                        lw["wr_hi"], lw["wr_lo"], alpha)
    pos, pos_t, block_off = expert_choice_select(aff_t, cap)
    xs = compact_tokens(h2, pos, block_off, cap_pad)
    ye = expert_ffn(xs, lw["w_gate"], lw["w_up"], lw["w_down"], cap)
    return combine_and_norm(ye, pos_t, aff, block_off, x1, g2, lw["ln2_g"], lw["ln2_b"], alpha)


def kernel(x, c, ctx, c_ctx, w_mod, b_mod, w_in, a_sink, nat_bias, conv_w, conv_b, conv_ln_g, conv_ln_b,
           w_out, ln1_g, ln1_b, w_router, w_gate, w_up, w_down, ln2_g, ln2_b):
    bsz, n_lat, d = x.shape
    depth = w_mod.shape[0]
    alpha = (2 * depth) ** 0.25
    cos_t, sin_t = rope_tables(n_lat)

    cond = jnp.concatenate([c, c_ctx[None, :], jnp.zeros((8 - bsz - 1, d), F32)], axis=0)
    mods = adaln_all(cond, w_mod, b_mod)

    for l in range(depth):
        last = l == depth - 1
        wr_hi, wr_lo = _split_bf16(w_router[l].T)
        lw = dict(w_out=w_out[l].astype(BF16), ln1_g=ln1_g[l], ln1_b=ln1_b[l], wr_hi=wr_hi, wr_lo=wr_lo,
                  w_gate=w_gate[l], w_up=w_up[l], w_down=w_down[l],
                  ln2_g=ln2_g[l], ln2_b=ln2_b[l])
        w_in_l = w_in[l].astype(BF16)
        lat = [mods[l, :bsz, k * d:(k + 1) * d][:, None, :] for k in range(N_MOD)]
        cm = [jnp.broadcast_to(mods[l, bsz, k * d:(k + 1) * d][None, None, :], (bsz, 1, d))
              for k in range(N_MOD)]
        conv_args = (conv_w[l], conv_b[l], conv_ln_g[l], conv_ln_b[l])

        qa_c, ka_c, va_c, qb_c, kb_c, vb_c, hc_c = in_projection(ctx, cm[0], cm[1], w_in_l, cos_t, sin_t, rope=False)
        if not last:
            oa_c, ob_c = context_attention(qa_c, qb_c, ka_c, va_c, kb_c, vb_c, a_sink[l])
            oc_c = conformer_conv(hc_c, *conv_args)
            ctx_new = _mixer_tail(oa_c, ob_c, oc_c, ctx, (cm[2], cm[3], cm[4], cm[5]), lw, alpha)

        qa, ka, va, qb, kb, vb, hc = in_projection(x, lat[0], lat[1], w_in_l, cos_t, sin_t, rope=True)
        oa = window_attention(qa, ka, va, ka_c, va_c, a_sink[l])
        ob = neighbourhood_attention(qb, kb, vb, kb_c, vb_c, nat_bias[l])
        oc = conformer_conv(hc, *conv_args)
        x = _mixer_tail(oa, ob, oc, x, (lat[2], lat[3], lat[4], lat[5]), lw, alpha)
        if not last:
            ctx = ctx_new
    return x
```

```python
import functools

import numpy as np
import jax
import jax.numpy as jnp
from jax import lax
from jax.experimental import pallas as pl
from jax.experimental.pallas import tpu as pltpu

HEAD_DIM = 64
GRID_W = 64
A_Q_HEADS = 8
A_KV_HEADS = 2
A_GROUP = A_Q_HEADS // A_KV_HEADS
A_WINDOW = 128
B_HEADS = 4
NA_ROWS = 8
NA_COLS = 16
C_CHANNELS = 256
C_CONV_WIDTH = 31
A_Q_W = A_Q_HEADS * HEAD_DIM
A_KV_W = A_KV_HEADS * HEAD_DIM
B_W = B_HEADS * HEAD_DIM
OFF_AK = A_Q_W
OFF_AV = OFF_AK + A_KV_W
OFF_BQ = OFF_AV + A_KV_W
OFF_BK = OFF_BQ + B_W
OFF_BV = OFF_BK + B_W
OFF_C = OFF_BV + B_W
IN_WIDTH = OFF_C + 2 * C_CHANNELS
ROPE_WIDTH = A_Q_W + A_KV_W
N_EXPERTS = 16
EC_CAPACITY = 2
ROPE_BASE = 10000.0
LN_EPS = 1e-6
N_MOD = 6
NEG_INF = -1e30
QK_SCALE = HEAD_DIM ** -0.5

LANES = 128
BF16_SUBLANES = 16
MXU_DEPTH = 256
OFFSET_BLOCK = 128
GATHER_BLOCK = MXU_DEPTH
GATHER_WINDOW = GATHER_BLOCK + BF16_SUBLANES
SCATTER_BLOCK = OFFSET_BLOCK
SCATTER_WINDOW = MXU_DEPTH
SLOT_PAD = 3 * LANES
EXPERTS_PER_GATHER_STEP = 4
VMEM_LIMIT = 56 * 1024 * 1024

F32 = jnp.float32
BF16 = jnp.bfloat16


def _dot(a, b):
    return jnp.dot(a, b, preferred_element_type=F32)


def _dot_t(a, b):
    return lax.dot_general(a, b, (((1,), (1,)), ((), ())), preferred_element_type=F32)


def _layer_norm(x):
    mu = jnp.mean(x, axis=-1, keepdims=True)
    xc = x - mu
    var = jnp.mean(xc * xc, axis=-1, keepdims=True)
    return xc * lax.rsqrt(var + LN_EPS)


def _params(*sem):
    return pltpu.CompilerParams(dimension_semantics=sem, vmem_limit_bytes=VMEM_LIMIT)


def _mod_kernel(cond_ref, w_ref, b_ref, out_ref):
    cnd = cond_ref[...]
    act = cnd * jax.nn.sigmoid(cnd)
    out_ref[0] = jnp.dot(act, w_ref[0], preferred_element_type=F32,
                         precision=lax.Precision.HIGHEST) + b_ref[0]


def adaln_all(cond, w_mod, b_mod):
    n_layers, d, width = w_mod.shape
    rows = cond.shape[0]
    tn = 1536
    return pl.pallas_call(
        _mod_kernel,
        grid=(n_layers, width // tn),
        in_specs=[
            pl.BlockSpec((rows, d), lambda l, j: (0, 0)),
            pl.BlockSpec((1, d, tn), lambda l, j: (l, 0, j)),
            pl.BlockSpec((1, 1, tn), lambda l, j: (l, 0, j)),
        ],
        out_specs=pl.BlockSpec((1, rows, tn), lambda l, j: (l, 0, j)),
        out_shape=jax.ShapeDtypeStruct((n_layers, rows, width), F32),
        compiler_params=_params("parallel", "parallel"),
        name="adaln",
    )(cond, w_mod, b_mod.reshape(n_layers, 1, width))


def _inproj_kernel(x_ref, sh_ref, sc_ref, w_ref, cos_ref, sin_ref,
                   qa_ref, ka_ref, va_ref, qb_ref, kb_ref, vb_ref, hc_ref, *, rope):
    x = x_ref[0]
    h = _layer_norm(x) * (1.0 + sc_ref[0]) + sh_ref[0]
    u = _dot(h.astype(BF16), w_ref[...])

    def rotated(col):
        xq = u[:, col:col + LANES]
        if not rope:
            return xq
        lane = lax.broadcasted_iota(jnp.int32, xq.shape, 1)
        first = (lane & (HEAD_DIM // 2 - 1)) < (HEAD_DIM // 4)
        partner = jnp.where(first, pltpu.roll(xq, LANES - HEAD_DIM // 4, 1),
                            pltpu.roll(xq, HEAD_DIM // 4, 1))
        return xq * cos_ref[...] + partner * sin_ref[...]

    rot = [rotated(col) for col in range(0, ROPE_WIDTH, LANES)]
    n_q = A_Q_W // LANES
    qa_ref[0] = (jnp.concatenate(rot[:n_q], axis=1) * QK_SCALE).astype(BF16)
    ka_ref[0] = jnp.concatenate(rot[n_q:], axis=1).astype(BF16)
    va_ref[0] = u[:, OFF_AV:OFF_BQ].astype(BF16)
    qb_ref[0] = (u[:, OFF_BQ:OFF_BK] * QK_SCALE).astype(BF16)
    kb_ref[0] = u[:, OFF_BK:OFF_BV].astype(BF16)
    vb_ref[0] = u[:, OFF_BV:OFF_C].astype(BF16)
    a = u[:, OFF_C:OFF_C + C_CHANNELS]
    gate = u[:, OFF_C + C_CHANNELS:]
    hc_ref[0] = a * jax.nn.sigmoid(gate)


def in_projection(x, shift, scale, w_in_bf16, cos_t, sin_t, *, rope):
    bx, t, d = x.shape
    tm = min(512, t)
    widths = (A_Q_W, A_KV_W, A_KV_W, B_W, B_W, B_W, C_CHANNELS)
    dtypes = (BF16,) * 6 + (F32,)
    tok = lambda b, i: (b, i, 0)
    per_b = lambda b, i: (b, 0, 0)
    return pl.pallas_call(
        functools.partial(_inproj_kernel, rope=rope),
        grid=(bx, t // tm),
        in_specs=[
            pl.BlockSpec((1, tm, d), tok),
            pl.BlockSpec((1, 1, d), per_b),
            pl.BlockSpec((1, 1, d), per_b),
            pl.BlockSpec((d, IN_WIDTH), lambda b, i: (0, 0)),
            pl.BlockSpec((tm, LANES), lambda b, i: (i, 0)),
            pl.BlockSpec((tm, LANES), lambda b, i: (i, 0)),
        ],
        out_specs=[pl.BlockSpec((1, tm, w), tok) for w in widths],
        out_shape=[jax.ShapeDtypeStruct((bx, t, w), dt) for w, dt in zip(widths, dtypes)],
        compiler_params=_params("parallel", "parallel"),
        name="in_projection",
    )(x, shift, scale, w_in_bf16, cos_t, sin_t)


def rope_tables(n_tokens):
    t = jnp.arange(n_tokens, dtype=jnp.int32)
    row = (t // GRID_W).astype(F32)[:, None]
    col = (t % GRID_W).astype(F32)[:, None]
    n_freq = HEAD_DIM // 4
    inv_freq = ROPE_BASE ** (-jnp.arange(n_freq, dtype=F32) / n_freq)
    ang_r = row * inv_freq
    ang_c = col * inv_freq
    cos_h = jnp.concatenate([jnp.cos(ang_r), jnp.cos(ang_r), jnp.cos(ang_c), jnp.cos(ang_c)], axis=1)
    sin_h = jnp.concatenate([-jnp.sin(ang_r), jnp.sin(ang_r), -jnp.sin(ang_c), jnp.sin(ang_c)], axis=1)
    reps = LANES // HEAD_DIM
    return jnp.tile(cos_h, (1, reps)), jnp.tile(sin_h, (1, reps))


def _with_ones(v):
    return jnp.concatenate([v, jnp.ones_like(v)], axis=1)


def _attend(score_parts, values, sink=None):
    m = score_parts[0].max(axis=-1, keepdims=True)
    for s in score_parts[1:]:
        m = jnp.maximum(m, s.max(axis=-1, keepdims=True))
    if sink is not None:
        m = jnp.maximum(m, sink)
    acc = None
    for s, v in zip(score_parts, values):
        term = _dot(jnp.exp((s - m).astype(BF16)), v)
        acc = term if acc is None else acc + term
    den = acc[:, HEAD_DIM:HEAD_DIM + 1]
    if sink is not None:
        den = den + jnp.exp(sink - m)
    return acc[:, :HEAD_DIM] / den


def _attn_a_kernel(sink_ref, q_ref, kp_ref, kc_ref, kn_ref, vp_ref, vc_ref, vn_ref,
                   kctx_ref, vctx_ref, out_ref, *, n_lat, tq):
    i = pl.program_id(1)
    k_win = jnp.concatenate([kp_ref[0], kc_ref[0], kn_ref[0]], axis=0)
    v_win = jnp.concatenate([vp_ref[0], vc_ref[0], vn_ref[0]], axis=0)
    kctx = kctx_ref[0]
    vctx = vctx_ref[0]
    sub = A_WINDOW
    span = 3 * A_WINDOW
    rows = A_GROUP * sub
    row_i = lax.broadcasted_iota(jnp.int32, (rows, span), 0)
    col_i = lax.broadcasted_iota(jnp.int32, (rows, span), 1)
    rel = col_i - A_WINDOW - (row_i & (sub - 1))
    in_band = (rel <= A_WINDOW) & (rel >= -A_WINDOW)
    group_of_row = lax.broadcasted_iota(jnp.int32, (rows, 1), 0) >> (sub.bit_length() - 1)
    sinks = []
    for hk in range(A_KV_HEADS):
        sink = jnp.zeros((rows, 1), F32)
        for g in range(A_GROUP):
            sink = jnp.where(group_of_row == g, sink_ref[hk * A_GROUP + g], sink)
        sinks.append(sink)
    vctx_ext = [_with_ones(vctx[:, hk * HEAD_DIM:(hk + 1) * HEAD_DIM]) for hk in range(A_KV_HEADS)]
    v_ext = [_with_ones(v_win[:, hk * HEAD_DIM:(hk + 1) * HEAD_DIM]) for hk in range(A_KV_HEADS)]
    for j in range(tq // sub):
        kpos = i * tq + j * sub - A_WINDOW + col_i
        valid = in_band & (kpos >= 0) & (kpos < n_lat)
        q_rows = q_ref[0, j * sub:(j + 1) * sub]
        k_sub = k_win[j * sub:j * sub + span]
        for hk in range(A_KV_HEADS):
            sl = slice(hk * HEAD_DIM, (hk + 1) * HEAD_DIM)
            heads = [hk * A_GROUP + g for g in range(A_GROUP)]
            q = jnp.concatenate([q_rows[:, h * HEAD_DIM:(h + 1) * HEAD_DIM] for h in heads], axis=0)
            s_win = jnp.where(valid, _dot_t(q, k_sub[:, sl]), NEG_INF)
            s_ctx = _dot_t(q, kctx[:, sl])
            v_sub = v_ext[hk][j * sub:j * sub + span]
            o = _attend([s_win, s_ctx], [v_sub, vctx_ext[hk]], sink=sinks[hk]).astype(BF16)
            for g, h in enumerate(heads):
                out_ref[0, j * sub:(j + 1) * sub, h * HEAD_DIM:(h + 1) * HEAD_DIM] = o[g * sub:(g + 1) * sub]


def window_attention(qa, ka, va, kc_a, vc_a, sink):
    bsz, n_lat, _ = qa.shape
    n_ctx = kc_a.shape[1]
    tq = min(256, n_lat)
    w = A_WINDOW
    per = tq // w
    last = n_lat // w - 1
    prev = lambda b, i, s: (b, jnp.maximum(i * per - 1, 0), 0)
    cur = lambda b, i, s: (b, i, 0)
    nxt = lambda b, i, s: (b, jnp.minimum((i + 1) * per, last), 0)
    ctx = lambda b, i, s: (b, 0, 0)
    kv_specs = [pl.BlockSpec((1, w, A_KV_W), prev), pl.BlockSpec((1, tq, A_KV_W), cur),
                pl.BlockSpec((1, w, A_KV_W), nxt)]
    grid_spec = pltpu.PrefetchScalarGridSpec(
        num_scalar_prefetch=1,
        grid=(bsz, n_lat // tq),
        in_specs=[pl.BlockSpec((1, tq, A_Q_W), cur)] + kv_specs + kv_specs + [
            pl.BlockSpec((1, n_ctx, A_KV_W), ctx), pl.BlockSpec((1, n_ctx, A_KV_W), ctx)],
        out_specs=pl.BlockSpec((1, tq, A_Q_W), cur),
    )
    return pl.pallas_call(
        functools.partial(_attn_a_kernel, n_lat=n_lat, tq=tq),
        grid_spec=grid_spec,
        out_shape=jax.ShapeDtypeStruct((bsz, n_lat, A_Q_W), BF16),
        compiler_params=_params("parallel", "parallel"),
        name="window_attention",
    )(sink, qa, ka, ka, ka, va, va, va, kc_a, vc_a)


NB_Q_ROWS = 4


def _attn_b_kernel(q_ref, kp_ref, kc_ref, kn_ref, vp_ref, vc_ref, vn_ref,
                   kctx_ref, vctx_ref, bias_ref, out_ref):
    k_win = jnp.concatenate([kp_ref[0], kc_ref[0], kn_ref[0]], axis=0)
    v_win = jnp.concatenate([vp_ref[0], vc_ref[0], vn_ref[0]], axis=0)
    q_all = q_ref[0]
    kctx = kctx_ref[0]
    vctx = vctx_ref[0]
    for h in range(B_HEADS):
        sl = slice(h * HEAD_DIM, (h + 1) * HEAD_DIM)
        q = q_all[:, sl]
        s_nb = _dot_t(q, k_win[:, sl]) + bias_ref[0, h]
        s_ctx = _dot_t(q, kctx[:, sl])
        o = _attend([s_nb, s_ctx], [_with_ones(v_win[:, sl]), _with_ones(vctx[:, sl])])
        out_ref[0, :, sl] = o.astype(BF16)


def neighbourhood_bias(rel_bias, n_lat):
    rows = n_lat // GRID_W
    kr_n = min(NA_ROWS, rows)
    n_blocks = rows // NB_Q_ROWS
    n_heads, n_dr, n_dc = rel_bias.shape
    cols = np.arange(GRID_W)
    c_start = np.clip(cols - NA_COLS // 2, 0, GRID_W - NA_COLS)
    col_ok = (cols[None, :] >= c_start[:, None]) & (cols[None, :] < c_start[:, None] + NA_COLS)
    dc = np.clip(cols[None, :] - cols[:, None], -(NA_COLS - 1), NA_COLS - 1) + NA_COLS - 1
    pick_dc = (dc.reshape(-1)[None, :] == np.arange(n_dc)[:, None]).astype(np.float32)
    toeplitz = jnp.dot(rel_bias.reshape(n_heads * n_dr, n_dc), pick_dc, precision=lax.Precision.HIGHEST)
    toeplitz = jnp.where(col_ok.reshape(-1), toeplitz, NEG_INF).reshape(n_heads, n_dr, GRID_W, GRID_W)
    q_rl = np.arange(NB_Q_ROWS)
    k_rl = np.arange(3 * NB_Q_ROWS)
    row_ok, dr = [], []
    for j in sorted({0, min(1, n_blocks - 1), n_blocks - 1}):
        r = NB_Q_ROWS * j + q_rl
        kr = NB_Q_ROWS * (j - 1) + k_rl
        r_start = np.clip(r - kr_n // 2, 0, rows - kr_n)
        ok = (kr[None, :] >= r_start[:, None]) & (kr[None, :] < r_start[:, None] + kr_n)
        row_ok.append(ok & (kr[None, :] >= 0) & (kr[None, :] < rows))
        dr.append(np.clip(kr[None, :] - r[:, None] + NA_ROWS - 1, 0, n_dr - 1))
    row_ok = np.stack(row_ok)
    dr = np.stack(dr)
    tiles = jnp.stack([toeplitz[:, int(i)] for i in dr.reshape(-1)], axis=1)
    tiles = tiles.reshape((n_heads,) + dr.shape + (GRID_W, GRID_W))
    tiles = jnp.where(row_ok[None, :, :, :, None, None], tiles, NEG_INF)
    table = tiles.transpose(1, 0, 2, 4, 3, 5).reshape(
        dr.shape[0], n_heads, NB_Q_ROWS * GRID_W, 3 * NB_Q_ROWS * GRID_W)
    return table, n_blocks


def neighbourhood_attention(qb, kb, vb, kc_b, vc_b, rel_bias):
    bsz, n_lat, _ = qb.shape
    n_ctx = kc_b.shape[1]
    table, n_blocks = neighbourhood_bias(rel_bias, n_lat)
    n_var = table.shape[0]
    tq = NB_Q_ROWS * GRID_W
    prev = lambda b, j: (b, jnp.maximum(j - 1, 0), 0)
    cur = lambda b, j: (b, j, 0)
    nxt = lambda b, j: (b, jnp.minimum(j + 1, n_blocks - 1), 0)
    ctx = lambda b, j: (b, 0, 0)

    def variant(b, j):
        v = jnp.where(j == 0, 0, jnp.where(j == n_blocks - 1, n_var - 1, min(1, n_var - 1)))
        return (v, 0, 0, 0)

    kv_specs = [pl.BlockSpec((1, tq, B_W), prev), pl.BlockSpec((1, tq, B_W), cur),
                pl.BlockSpec((1, tq, B_W), nxt)]
    return pl.pallas_call(
        _attn_b_kernel,
        grid=(bsz, n_blocks),
        in_specs=[pl.BlockSpec((1, tq, B_W), cur)] + kv_specs + kv_specs + [
            pl.BlockSpec((1, n_ctx, B_W), ctx), pl.BlockSpec((1, n_ctx, B_W), ctx),
            pl.BlockSpec((1, B_HEADS, tq, 3 * tq), variant)],
        out_specs=pl.BlockSpec((1, tq, B_W), cur),
        out_shape=jax.ShapeDtypeStruct((bsz, n_lat, B_W), BF16),
        compiler_params=_params("parallel", "parallel"),
        name="neighbourhood_attention",
    )(qb, kb, kb, kb, vb, vb, vb, kc_b, vc_b, table)


def _ctx_attn_kernel(sink_ref, qa_ref, qb_ref, ka_ref, va_ref, kb_ref, vb_ref, oa_ref, ob_ref):
    qa, qb = qa_ref[0], qb_ref[0]
    ka, va, kb, vb = ka_ref[0], va_ref[0], kb_ref[0], vb_ref[0]
    for hq in range(A_Q_HEADS):
        sl = slice(hq * HEAD_DIM, (hq + 1) * HEAD_DIM)
        hk = hq // A_GROUP
        kv = slice(hk * HEAD_DIM, (hk + 1) * HEAD_DIM)
        q = qa[:, sl]
        o = _attend([_dot_t(q, ka[:, kv])], [_with_ones(va[:, kv])], sink=sink_ref[hq])
        oa_ref[0, :, sl] = o.astype(BF16)
    for h in range(B_HEADS):
        sl = slice(h * HEAD_DIM, (h + 1) * HEAD_DIM)
        q = qb[:, sl]
        o = _attend([_dot_t(q, kb[:, sl])], [_with_ones(vb[:, sl])])
        ob_ref[0, :, sl] = o.astype(BF16)


def context_attention(qa, qb, ka, va, kb, vb, sink):
    bsz, n_ctx, _ = qa.shape
    blk = lambda w: pl.BlockSpec((1, n_ctx, w), lambda b, s: (b, 0, 0))
    grid_spec = pltpu.PrefetchScalarGridSpec(
        num_scalar_prefetch=1,
        grid=(bsz,),
        in_specs=[blk(A_Q_W), blk(B_W), blk(A_KV_W), blk(A_KV_W), blk(B_W), blk(B_W)],
        out_specs=[blk(A_Q_W), blk(B_W)],
    )
    return pl.pallas_call(
        _ctx_attn_kernel,
        grid_spec=grid_spec,
        out_shape=[jax.ShapeDtypeStruct((bsz, n_ctx, A_Q_W), BF16),
                   jax.ShapeDtypeStruct((bsz, n_ctx, B_W), BF16)],
        compiler_params=_params("parallel"),
        name="context_attention",
    )(sink, qa, qb, ka, va, kb, vb)


CONV_HALO = 16


def _conv_kernel(prev_ref, cur_ref, next_ref, w_ref, b_ref, g_ref, beta_ref, out_ref, ext_ref, *, ts):
    i = pl.program_id(1)
    n_i = pl.num_programs(1)
    ext_ref[0:CONV_HALO] = jnp.where(i > 0, prev_ref[0], 0.0)
    ext_ref[CONV_HALO:CONV_HALO + ts] = cur_ref[0]
    ext_ref[CONV_HALO + ts:] = jnp.where(i < n_i - 1, next_ref[0], 0.0)
    half = C_CONV_WIDTH // 2
    acc = jnp.zeros((ts, C_CHANNELS), F32)
    for k in range(C_CONV_WIDTH):
        acc = acc + ext_ref[pl.ds(CONV_HALO - half + k, ts)] * w_ref[k:k + 1]
    y = _layer_norm(acc + b_ref[...]) * g_ref[...] + beta_ref[...]
    out_ref[0] = (y * jax.nn.sigmoid(y)).astype(BF16)


def conformer_conv(hc, conv_w, conv_b, ln_g, ln_b):
    bx, t, ch = hc.shape
    ts = min(512, t)
    per = ts // CONV_HALO
    last = t // CONV_HALO - 1
    row = lambda v: v.reshape(1, ch)
    const = lambda b, i: (0, 0)
    return pl.pallas_call(
        functools.partial(_conv_kernel, ts=ts),
        grid=(bx, t // ts),
        in_specs=[
            pl.BlockSpec((1, CONV_HALO, ch), lambda b, i: (b, jnp.maximum(i * per - 1, 0), 0)),
            pl.BlockSpec((1, ts, ch), lambda b, i: (b, i, 0)),
            pl.BlockSpec((1, CONV_HALO, ch), lambda b, i: (b, jnp.minimum((i + 1) * per, last), 0)),
            pl.BlockSpec((C_CONV_WIDTH, ch), const),
            pl.BlockSpec((1, ch), const), pl.BlockSpec((1, ch), const), pl.BlockSpec((1, ch), const),
        ],
        out_specs=pl.BlockSpec((1, ts, ch), lambda b, i: (b, i, 0)),
        out_shape=jax.ShapeDtypeStruct((bx, t, ch), BF16),
        scratch_shapes=[pltpu.VMEM((ts + 2 * CONV_HALO, ch), F32)],
        compiler_params=_params("parallel", "parallel"),
        name="conformer_conv",
    )(hc, hc, hc, conv_w, row(conv_b), row(ln_g), row(ln_b))


def _outproj_kernel(oa_ref, ob_ref, oc_ref, x_ref, g1_ref, sh_ref, sc_ref, w_ref, lng_ref, lnb_ref,
                    wr_hi_ref, wr_lo_ref, x1_ref, h2_ref, afft_ref, aff_ref, *, alpha):
    o = (_dot(oa_ref[0], w_ref[0:A_Q_W])
         + _dot(ob_ref[0], w_ref[A_Q_W:A_Q_W + B_W])
         + _dot(oc_ref[0], w_ref[A_Q_W + B_W:]))
    y = _layer_norm(alpha * x_ref[0] + g1_ref[0] * o) * lng_ref[...] + lnb_ref[...]
    x1_ref[0] = y
    h2 = _layer_norm(y) * (1.0 + sc_ref[0]) + sh_ref[0]
    h_hi = h2.astype(BF16)
    h2_ref[0] = h_hi
    h_lo = (h2 - h_hi.astype(F32)).astype(BF16)
    w_hi, w_lo = wr_hi_ref[...], wr_lo_ref[...]
    logits_t = _dot_t(w_hi, h_hi) + _dot_t(w_hi, h_lo) + _dot_t(w_lo, h_hi)
    e_t = jnp.exp(logits_t - logits_t.max(axis=0, keepdims=True))
    afft_ref[0] = e_t / e_t.sum(axis=0, keepdims=True)
    logits = _dot_t(h_hi, w_hi) + _dot_t(h_lo, w_hi) + _dot_t(h_hi, w_lo)
    e_n = jnp.exp(logits - logits.max(axis=1, keepdims=True))
    aff_ref[0] = e_n / e_n.sum(axis=1, keepdims=True)


def out_projection(oa, ob, oc, x, g1, sh2, sc2, w_out_bf16, ln_g, ln_b, wr_hi, wr_lo, alpha):
    bx, t, d = x.shape
    tm = min(512, t)
    tok = lambda b, i: (b, i, 0)
    per_b = lambda b, i: (b, 0, 0)
    const = lambda b, i: (0, 0)
    vec = pl.BlockSpec((1, d), const)
    return pl.pallas_call(
        functools.partial(_outproj_kernel, alpha=alpha),
        grid=(bx, t // tm),
        in_specs=[
            pl.BlockSpec((1, tm, A_Q_W), tok), pl.BlockSpec((1, tm, B_W), tok),
            pl.BlockSpec((1, tm, C_CHANNELS), tok), pl.BlockSpec((1, tm, d), tok),
            pl.BlockSpec((1, 1, d), per_b), pl.BlockSpec((1, 1, d), per_b), pl.BlockSpec((1, 1, d), per_b),
            pl.BlockSpec(w_out_bf16.shape, const), vec, vec,
            pl.BlockSpec((N_EXPERTS, d), const), pl.BlockSpec((N_EXPERTS, d), const),
        ],
        out_specs=[pl.BlockSpec((1, tm, d), tok), pl.BlockSpec((1, tm, d), tok),
                   pl.BlockSpec((1, N_EXPERTS, tm), lambda b, i: (b, 0, i)),
                   pl.BlockSpec((1, tm, N_EXPERTS), tok)],
        out_shape=[jax.ShapeDtypeStruct((bx, t, d), F32), jax.ShapeDtypeStruct((bx, t, d), BF16),
                   jax.ShapeDtypeStruct((bx, N_EXPERTS, t), F32),
                   jax.ShapeDtypeStruct((bx, t, N_EXPERTS), F32)],
        compiler_params=_params("parallel", "parallel"),
        name="out_projection",
    )(oa, ob, oc, x, g1, sh2, sc2, w_out_bf16, ln_g.reshape(1, d), ln_b.reshape(1, d), wr_hi, wr_lo)


def _select_kernel(afft_ref, pos_ref, post_ref, off_ref, *, cap, n_tok):
    aff = afft_ref[0]

    def bit_step(j, bits):
        cand = bits | (jnp.int32(1) << (30 - j))
        cnt = jnp.sum((aff >= pltpu.bitcast(cand, F32)).astype(jnp.int32), axis=1, keepdims=True)
        return jnp.where(cnt >= cap, cand, bits)

    thr = pltpu.bitcast(lax.fori_loop(0, 31, bit_step, jnp.zeros((N_EXPERTS, 1), jnp.int32)), F32)
    above = (aff > thr).astype(F32)
    tied = (aff == thr).astype(F32)
    need = cap - jnp.sum(above, axis=1, keepdims=True)

    blk = LANES
    n_blk = n_tok // blk
    r_i = lax.broadcasted_iota(jnp.int32, (blk, blk), 0)
    c_i = lax.broadcasted_iota(jnp.int32, (blk, blk), 1)
    strict_upper = (r_i < c_i).astype(BF16)
    eye = (r_i == c_i).astype(BF16)

    run_tied = jnp.zeros((N_EXPERTS, 1), F32)
    run_sel = jnp.zeros((N_EXPERTS, 1), F32)
    offs = []
    for kb in range(n_blk):
        sl = slice(kb * blk, (kb + 1) * blk)
        offs.append(run_sel)
        t_f = tied[:, sl]
        rank_tied = run_tied + _dot(t_f.astype(BF16), strict_upper)
        run_tied = run_tied + t_f.sum(axis=1, keepdims=True)
        s_f = above[:, sl] + t_f * (rank_tied < need).astype(F32)
        rank = run_sel + _dot(s_f.astype(BF16), strict_upper)
        run_sel = run_sel + s_f.sum(axis=1, keepdims=True)
        pos = jnp.where(s_f > 0.5, rank.astype(jnp.int32), -1)
        pos_ref[0, :, sl] = pos
        p1 = pos + 1
        hi = (p1 >> 5).astype(F32).astype(BF16)
        lo = (p1 & 31).astype(F32).astype(BF16)
        p1_t = _dot_t(eye, hi) * 32.0 + _dot_t(eye, lo)
        post_ref[0, sl, :] = p1_t.astype(jnp.int32) - 1
    off_ref[0] = jnp.concatenate(offs, axis=1).astype(jnp.int32)


def expert_choice_select(aff_t, cap):
    bx, n_e, t = aff_t.shape
    n_tb = t // OFFSET_BLOCK
    return pl.pallas_call(
        functools.partial(_select_kernel, cap=cap, n_tok=t),
        grid=(bx,),
        in_specs=[pl.BlockSpec((1, n_e, t), lambda b: (b, 0, 0))],
        out_specs=[pl.BlockSpec((1, n_e, t), lambda b: (b, 0, 0)),
                   pl.BlockSpec((1, t, n_e), lambda b: (b, 0, 0)),
                   pl.BlockSpec((1, n_e, n_tb), lambda b: (b, 0, 0))],
        out_shape=[jax.ShapeDtypeStruct((bx, n_e, t), jnp.int32),
                   jax.ShapeDtypeStruct((bx, t, n_e), jnp.int32),
                   jax.ShapeDtypeStruct((bx, n_e, n_tb), jnp.int32)],
        compiler_params=_params("parallel"),
        name="expert_choice_select",
    )(aff_t)


GATE_PARTS = 3


def _gather_kernel(off_ref, pos_ref, h_ref, aff_ref, xs_ref, gs_ref, *, n_off, blocks_per_step):
    b, eg, kc = pl.program_id(0), pl.program_id(1), pl.program_id(2)
    epg = EXPERTS_PER_GATHER_STEP

    @pl.when(kc == 0)
    def _():
        xs_ref[...] = jnp.zeros_like(xs_ref)
        gs_ref[...] = jnp.zeros_like(gs_ref)

    slot = lax.broadcasted_iota(jnp.int32, (GATHER_WINDOW, GATHER_BLOCK), 0)
    lane = lax.broadcasted_iota(jnp.int32, (GATHER_BLOCK, LANES), 1)
    expert_lane = lax.broadcasted_iota(jnp.int32, (GATHER_BLOCK, N_EXPERTS), 1)
    for kk in range(blocks_per_step):
        kb = kc * blocks_per_step + kk
        tok = slice(kk * GATHER_BLOCK, (kk + 1) * GATHER_BLOCK)
        offs, bases, onehots = [], [], []
        for ee in range(epg):
            off = off_ref[(b * N_EXPERTS + eg * epg + ee) * n_off + kb * (GATHER_BLOCK // OFFSET_BLOCK)]
            base = pl.multiple_of((off >> 4) << 4, BF16_SUBLANES)
            offs.append(off)
            bases.append(base)
            onehots.append((pos_ref[0, ee, 0, kk:kk + 1, :] - base == slot).astype(BF16))
        rows = _dot(jnp.concatenate(onehots, axis=0), h_ref[0, tok]).astype(BF16)
        aff_blk = aff_ref[0, tok]
        for ee in range(epg):
            win = pl.ds(bases[ee], GATHER_WINDOW)
            keep_new = slot[:, 0:1] >= offs[ee] - bases[ee]
            xs_ref[0, ee, win, :] = jnp.where(keep_new, rows[ee * GATHER_WINDOW:(ee + 1) * GATHER_WINDOW],
                                              xs_ref[0, ee, win, :])
            gate = jnp.sum(jnp.where(expert_lane == eg * epg + ee, aff_blk, 0.0), axis=1, keepdims=True)
            gate_cols, rest = jnp.zeros((GATHER_BLOCK, LANES), F32), gate
            for k in range(GATE_PARTS):
                part = rest.astype(BF16).astype(F32)
                gate_cols = jnp.where(lane == k, part, gate_cols)
                rest = rest - part
            gathered = _dot(onehots[ee], gate_cols.astype(BF16))
            gs_ref[0, ee, win, :] = jnp.where(keep_new, gathered, gs_ref[0, ee, win, :])


def compact_tokens(h2, aff, pos, block_off, cap_pad):
    bx, t, d = h2.shape
    n_tb = t // GATHER_BLOCK
    blocks_per_step = min(4, n_tb)
    n_steps = n_tb // blocks_per_step
    epg = EXPERTS_PER_GATHER_STEP
    tokens = blocks_per_step * GATHER_BLOCK
    pos5 = pos.reshape(bx, N_EXPERTS, n_steps, blocks_per_step, GATHER_BLOCK)
    grid_spec = pltpu.PrefetchScalarGridSpec(
        num_scalar_prefetch=1,
        grid=(bx, N_EXPERTS // epg, n_steps),
        in_specs=[
            pl.BlockSpec((1, epg, 1, blocks_per_step, GATHER_BLOCK), lambda b, g, k, s: (b, g, k, 0, 0)),
            pl.BlockSpec((1, tokens, d), lambda b, g, k, s: (b, k, 0)),
            pl.BlockSpec((1, tokens, N_EXPERTS), lambda b, g, k, s: (b, k, 0)),
        ],
        out_specs=[pl.BlockSpec((1, epg, cap_pad, d), lambda b, g, k, s: (b, g, 0, 0)),
                   pl.BlockSpec((1, epg, cap_pad, LANES), lambda b, g, k, s: (b, g, 0, 0))],
    )
    return pl.pallas_call(
        functools.partial(_gather_kernel, n_off=t // OFFSET_BLOCK, blocks_per_step=blocks_per_step),
        grid_spec=grid_spec,
        out_shape=[jax.ShapeDtypeStruct((bx, N_EXPERTS, cap_pad, d), BF16),
                   jax.ShapeDtypeStruct((bx, N_EXPERTS, cap_pad, LANES), F32)],
        compiler_params=_params("parallel", "parallel", "arbitrary"),
        name="compact_tokens",
    )(block_off.reshape(-1), pos5, h2, aff)


def _ffn_kernel(xs_ref, gs_ref, wg_ref, wu_ref, wd_ref, ye_ref, wg_bf, wu_bf, wd_bf, *, cap, row_tile):
    @pl.when(pl.program_id(1) == 0)
    def _():
        wg_bf[...] = wg_ref[0, 0].astype(BF16)
        wu_bf[...] = wu_ref[0, 0].astype(BF16)
        wd_bf[...] = wd_ref[0, 0].astype(BF16)

    for r0 in range(0, cap, row_tile):
        rows = slice(r0, r0 + row_tile)
        x = xs_ref[0, 0, rows]
        gate = _dot(x, wg_bf[...])
        up = _dot(x, wu_bf[...])
        hid = (gate * jax.nn.sigmoid(gate) * up).astype(BF16)
        g = jnp.sum(gs_ref[0, 0, rows, 0:GATE_PARTS], axis=1, keepdims=True)
        ye_ref[0, 0, rows] = (_dot(hid, wd_bf[...]) * g).astype(BF16)
    ye_ref[0, 0, cap:] = jnp.zeros((ye_ref.shape[2] - cap, ye_ref.shape[3]), BF16)


def expert_ffn(xs, gs, wg, wu, wd, layer, cap):
    bx, n_e, cap_pad, d = xs.shape
    ff = wg.shape[-1]
    return pl.pallas_call(
        functools.partial(_ffn_kernel, cap=cap, row_tile=min(512, cap)),
        grid=(n_e, bx),
        in_specs=[
            pl.BlockSpec((1, 1, cap, d), lambda e, b: (b, e, 0, 0)),
            pl.BlockSpec((1, 1, cap, LANES), lambda e, b: (b, e, 0, 0)),
            pl.BlockSpec((1, 1, d, ff), lambda e, b: (layer, e, 0, 0)),
            pl.BlockSpec((1, 1, d, ff), lambda e, b: (layer, e, 0, 0)),
            pl.BlockSpec((1, 1, ff, d), lambda e, b: (layer, e, 0, 0)),
        ],
        out_specs=pl.BlockSpec((1, 1, cap_pad, d), lambda e, b: (b, e, 0, 0)),
        out_shape=jax.ShapeDtypeStruct((bx, n_e, cap_pad, d), BF16),
        scratch_shapes=[pltpu.VMEM((d, ff), BF16), pltpu.VMEM((d, ff), BF16), pltpu.VMEM((ff, d), BF16)],
        compiler_params=_params("parallel", "arbitrary"),
        name="expert_ffn",
    )(xs, gs, wg, wu, wd)


EXPERTS_PER_SCATTER_STEP = 4


def _combine_kernel(off_ref, post_ref, ye_ref, x1_ref, g2_ref, lng_ref, lnb_ref,
                    out_ref, acc_ref, *, n_tb, blocks_per_step, alpha):
    b, tq, eg = pl.program_id(0), pl.program_id(1), pl.program_id(2)
    eps = EXPERTS_PER_SCATTER_STEP

    @pl.when(eg == 0)
    def _():
        acc_ref[...] = jnp.zeros_like(acc_ref)

    expert_lane = lax.broadcasted_iota(jnp.int32, (SCATTER_BLOCK, N_EXPERTS), 1)
    slot = lax.broadcasted_iota(jnp.int32, (SCATTER_BLOCK, SCATTER_WINDOW), 1)
    for kk in range(blocks_per_step):
        kb = tq * blocks_per_step + kk
        rows = slice(kk * SCATTER_BLOCK, (kk + 1) * SCATTER_BLOCK)
        post_blk = post_ref[0, rows, :]
        contrib = None
        for ee in range(eps):
            e = eg * eps + ee
            off = off_ref[(b * N_EXPERTS + e) * n_tb + kb * (SCATTER_BLOCK // OFFSET_BLOCK)]
            base = pl.multiple_of((off >> 4) << 4, BF16_SUBLANES)
            pcol = jnp.sum(jnp.where(expert_lane == e, post_blk, 0), axis=1, keepdims=True)
            onehot = (pcol - base == slot).astype(BF16)
            term = _dot(onehot, ye_ref[0, ee, pl.ds(base, SCATTER_WINDOW), :])
            contrib = term if contrib is None else contrib + term
        acc_ref[rows, :] += contrib

    @pl.when(eg == N_EXPERTS // eps - 1)
    def _():
        y = alpha * x1_ref[0] + g2_ref[0] * acc_ref[...]
        out_ref[0] = _layer_norm(y) * lng_ref[...] + lnb_ref[...]


def combine_and_norm(ye, pos_t, block_off, x1, g2, ln_g, ln_b, alpha):
    bx, t, d = x1.shape
    cap_pad = ye.shape[2]
    n_tb = t // OFFSET_BLOCK
    blocks_per_step = min(8, t // SCATTER_BLOCK)
    tq = blocks_per_step * SCATTER_BLOCK
    eps = EXPERTS_PER_SCATTER_STEP
    tok = lambda b, q, e, s: (b, q, 0)
    const = lambda b, q, e, s: (0, 0)
    grid_spec = pltpu.PrefetchScalarGridSpec(
        num_scalar_prefetch=1,
        grid=(bx, t // tq, N_EXPERTS // eps),
        in_specs=[
            pl.BlockSpec((1, tq, N_EXPERTS), tok),
            pl.BlockSpec((1, eps, cap_pad, d), lambda b, q, e, s: (b, e, 0, 0)),
            pl.BlockSpec((1, tq, d), tok),
            pl.BlockSpec((1, 1, d), lambda b, q, e, s: (b, 0, 0)),
            pl.BlockSpec((1, d), const), pl.BlockSpec((1, d), const),
        ],
        out_specs=pl.BlockSpec((1, tq, d), tok),
        scratch_shapes=[pltpu.VMEM((tq, d), F32)],
    )
    return pl.pallas_call(
        functools.partial(_combine_kernel, n_tb=n_tb, blocks_per_step=blocks_per_step, alpha=alpha),
        grid_spec=grid_spec,
        out_shape=jax.ShapeDtypeStruct((bx, t, d), F32),
        compiler_params=_params("parallel", "parallel", "arbitrary"),
        name="combine_and_norm",
    )(block_off.reshape(-1), pos_t, ye, x1, g2, ln_g.reshape(1, d), ln_b.reshape(1, d))


def _split_bf16(w):
    hi = w.astype(BF16)
    return hi, (w - hi.astype(F32)).astype(BF16)


def _mixer_tail(oa, ob, oc, x, mod, lw, alpha):
    g1, sh2, sc2, g2 = mod
    t = x.shape[1]
    cap = EC_CAPACITY * t // N_EXPERTS
    cap_pad = cap + SLOT_PAD
    x1, h2, aff_t, aff = out_projection(oa, ob, oc, x, g1, sh2, sc2, lw["w_out"], lw["ln1_g"], lw["ln1_b"],
                                        lw["wr_hi"], lw["wr_lo"], alpha)
    pos, pos_t, block_off = expert_choice_select(aff_t, cap)
    xs, gs = compact_tokens(h2, aff, pos, block_off, cap_pad)
    ye = expert_ffn(xs, gs, lw["w_gate"], lw["w_up"], lw["w_down"], lw["layer"], cap)
    return combine_and_norm(ye, pos_t, block_off, x1, g2, lw["ln2_g"], lw["ln2_b"], alpha)


def kernel(x, c, ctx, c_ctx, w_mod, b_mod, w_in, a_sink, nat_bias, conv_w, conv_b, conv_ln_g, conv_ln_b,
           w_out, ln1_g, ln1_b, w_router, w_gate, w_up, w_down, ln2_g, ln2_b):
    bsz, n_lat, d = x.shape
    depth = w_mod.shape[0]
    alpha = (2 * depth) ** 0.25
    cos_t, sin_t = rope_tables(n_lat)

    cond = jnp.concatenate([c, c_ctx[None, :], jnp.zeros((8 - bsz - 1, d), F32)], axis=0)
    mods = adaln_all(cond, w_mod, b_mod)

    for l in range(depth):
        last = l == depth - 1
        wr_hi, wr_lo = _split_bf16(w_router[l].T)
        lw = dict(w_out=w_out[l].astype(BF16), ln1_g=ln1_g[l], ln1_b=ln1_b[l], wr_hi=wr_hi, wr_lo=wr_lo,
                  w_gate=w_gate, w_up=w_up, w_down=w_down, layer=l,
                  ln2_g=ln2_g[l], ln2_b=ln2_b[l])
        w_in_l = w_in[l].astype(BF16)
        lat = [mods[l, :bsz, k * d:(k + 1) * d][:, None, :] for k in range(N_MOD)]
        cm = [jnp.broadcast_to(mods[l, bsz, k * d:(k + 1) * d][None, None, :], (bsz, 1, d))
              for k in range(N_MOD)]
        conv_args = (conv_w[l], conv_b[l], conv_ln_g[l], conv_ln_b[l])

        qa_c, ka_c, va_c, qb_c, kb_c, vb_c, hc_c = in_projection(ctx, cm[0], cm[1], w_in_l, cos_t, sin_t, rope=False)
        if not last:
            oa_c, ob_c = context_attention(qa_c, qb_c, ka_c, va_c, kb_c, vb_c, a_sink[l])
            oc_c = conformer_conv(hc_c, *conv_args)
            ctx_new = _mixer_tail(oa_c, ob_c, oc_c, ctx, (cm[2], cm[3], cm[4], cm[5]), lw, alpha)

        qa, ka, va, qb, kb, vb, hc = in_projection(x, lat[0], lat[1], w_in_l, cos_t, sin_t, rope=True)
        oa = window_attention(qa, ka, va, ka_c, va_c, a_sink[l])
        ob = neighbourhood_attention(qb, kb, vb, kb_c, vb_c, nat_bias[l])
        oc = conformer_conv(hc, *conv_args)
        x = _mixer_tail(oa, ob, oc, x, (lat[2], lat[3], lat[4], lat[5]), lw, alpha)
        if not last:
            ctx = ctx_new
    return x
```

```python
import functools

import numpy as np
import jax
import jax.numpy as jnp
from jax import lax
from jax.experimental import pallas as pl
from jax.experimental.pallas import tpu as pltpu

HEAD_DIM = 64
GRID_W = 64
A_Q_HEADS = 8
A_KV_HEADS = 2
A_GROUP = A_Q_HEADS // A_KV_HEADS
A_WINDOW = 128
B_HEADS = 4
NA_ROWS = 8
NA_COLS = 16
C_CHANNELS = 256
C_CONV_WIDTH = 31
A_Q_W = A_Q_HEADS * HEAD_DIM
A_KV_W = A_KV_HEADS * HEAD_DIM
B_W = B_HEADS * HEAD_DIM
OFF_AK = A_Q_W
OFF_AV = OFF_AK + A_KV_W
OFF_BQ = OFF_AV + A_KV_W
OFF_BK = OFF_BQ + B_W
OFF_BV = OFF_BK + B_W
OFF_C = OFF_BV + B_W
IN_WIDTH = OFF_C + 2 * C_CHANNELS
ROPE_WIDTH = A_Q_W + A_KV_W
N_EXPERTS = 16
EC_CAPACITY = 2
ROPE_BASE = 10000.0
LN_EPS = 1e-6
N_MOD = 6
NEG_INF = -1e30
QK_SCALE = HEAD_DIM ** -0.5

LANES = 128
BF16_SUBLANES = 16
MXU_DEPTH = 256
OFFSET_BLOCK = 128
GATHER_BLOCK = MXU_DEPTH
GATHER_WINDOW = GATHER_BLOCK + BF16_SUBLANES
SCATTER_BLOCK = OFFSET_BLOCK
SCATTER_WINDOW = MXU_DEPTH
SLOT_PAD = 3 * LANES
EXPERTS_PER_GATHER_STEP = 4
VMEM_LIMIT = 56 * 1024 * 1024

F32 = jnp.float32
BF16 = jnp.bfloat16


def _dot(a, b):
    return jnp.dot(a, b, preferred_element_type=F32)


def _dot_t(a, b):
    return lax.dot_general(a, b, (((1,), (1,)), ((), ())), preferred_element_type=F32)


def _layer_norm(x):
    mu = jnp.mean(x, axis=-1, keepdims=True)
    xc = x - mu
    var = jnp.mean(xc * xc, axis=-1, keepdims=True)
    return xc * lax.rsqrt(var + LN_EPS)


def _params(*sem):
    return pltpu.CompilerParams(dimension_semantics=sem, vmem_limit_bytes=VMEM_LIMIT)


def _mod_kernel(cond_ref, w_ref, b_ref, out_ref):
    cnd = cond_ref[...]
    act = cnd * jax.nn.sigmoid(cnd)
    out_ref[0] = jnp.dot(act, w_ref[0], preferred_element_type=F32,
                         precision=lax.Precision.HIGHEST) + b_ref[0]


def adaln_all(cond, w_mod, b_mod):
    n_layers, d, width = w_mod.shape
    rows = cond.shape[0]
    tn = 1536
    return pl.pallas_call(
        _mod_kernel,
        grid=(n_layers, width // tn),
        in_specs=[
            pl.BlockSpec((rows, d), lambda l, j: (0, 0)),
            pl.BlockSpec((1, d, tn), lambda l, j: (l, 0, j)),
            pl.BlockSpec((1, 1, tn), lambda l, j: (l, 0, j)),
        ],
        out_specs=pl.BlockSpec((1, rows, tn), lambda l, j: (l, 0, j)),
        out_shape=jax.ShapeDtypeStruct((n_layers, rows, width), F32),
        compiler_params=_params("parallel", "parallel"),
        name="adaln",
    )(cond, w_mod, b_mod.reshape(n_layers, 1, width))


def _inproj_kernel(x_ref, sh_ref, sc_ref, w_ref, cos_ref, sin_ref,
                   qa_ref, ka_ref, va_ref, qb_ref, kb_ref, vb_ref, hc_ref, *, rope):
    x = x_ref[0]
    h = _layer_norm(x) * (1.0 + sc_ref[0]) + sh_ref[0]
    u = _dot(h.astype(BF16), w_ref[...])

    def rotated(col):
        xq = u[:, col:col + LANES]
        if not rope:
            return xq
        lane = lax.broadcasted_iota(jnp.int32, xq.shape, 1)
        first = (lane & (HEAD_DIM // 2 - 1)) < (HEAD_DIM // 4)
        partner = jnp.where(first, pltpu.roll(xq, LANES - HEAD_DIM // 4, 1),
                            pltpu.roll(xq, HEAD_DIM // 4, 1))
        return xq * cos_ref[...] + partner * sin_ref[...]

    rot = [rotated(col) for col in range(0, ROPE_WIDTH, LANES)]
    n_q = A_Q_W // LANES
    qa_ref[0] = (jnp.concatenate(rot[:n_q], axis=1) * QK_SCALE).astype(BF16)
    ka_ref[0] = jnp.concatenate(rot[n_q:], axis=1).astype(BF16)
    va_ref[0] = u[:, OFF_AV:OFF_BQ].astype(BF16)
    qb_ref[0] = (u[:, OFF_BQ:OFF_BK] * QK_SCALE).astype(BF16)
    kb_ref[0] = u[:, OFF_BK:OFF_BV].astype(BF16)
    vb_ref[0] = u[:, OFF_BV:OFF_C].astype(BF16)
    a = u[:, OFF_C:OFF_C + C_CHANNELS]
    gate = u[:, OFF_C + C_CHANNELS:]
    hc_ref[0] = a * jax.nn.sigmoid(gate)


def in_projection(x, shift, scale, w_in_bf16, cos_t, sin_t, *, rope):
    bx, t, d = x.shape
    tm = min(512, t)
    widths = (A_Q_W, A_KV_W, A_KV_W, B_W, B_W, B_W, C_CHANNELS)
    dtypes = (BF16,) * 6 + (F32,)
    tok = lambda b, i: (b, i, 0)
    per_b = lambda b, i: (b, 0, 0)
    return pl.pallas_call(
        functools.partial(_inproj_kernel, rope=rope),
        grid=(bx, t // tm),
        in_specs=[
            pl.BlockSpec((1, tm, d), tok),
            pl.BlockSpec((1, 1, d), per_b),
            pl.BlockSpec((1, 1, d), per_b),
            pl.BlockSpec((d, IN_WIDTH), lambda b, i: (0, 0)),
            pl.BlockSpec((tm, LANES), lambda b, i: (i, 0)),
            pl.BlockSpec((tm, LANES), lambda b, i: (i, 0)),
        ],
        out_specs=[pl.BlockSpec((1, tm, w), tok) for w in widths],
        out_shape=[jax.ShapeDtypeStruct((bx, t, w), dt) for w, dt in zip(widths, dtypes)],
        compiler_params=_params("parallel", "parallel"),
        name="in_projection",
    )(x, shift, scale, w_in_bf16, cos_t, sin_t)


def rope_tables(n_tokens):
    t = jnp.arange(n_tokens, dtype=jnp.int32)
    row = (t // GRID_W).astype(F32)[:, None]
    col = (t % GRID_W).astype(F32)[:, None]
    n_freq = HEAD_DIM // 4
    inv_freq = ROPE_BASE ** (-jnp.arange(n_freq, dtype=F32) / n_freq)
    ang_r = row * inv_freq
    ang_c = col * inv_freq
    cos_h = jnp.concatenate([jnp.cos(ang_r), jnp.cos(ang_r), jnp.cos(ang_c), jnp.cos(ang_c)], axis=1)
    sin_h = jnp.concatenate([-jnp.sin(ang_r), jnp.sin(ang_r), -jnp.sin(ang_c), jnp.sin(ang_c)], axis=1)
    reps = LANES // HEAD_DIM
    return jnp.tile(cos_h, (1, reps)), jnp.tile(sin_h, (1, reps))


def _with_ones(v):
    return jnp.concatenate([v, jnp.ones_like(v)], axis=1)


def _attend(score_parts, values, sink=None):
    m = score_parts[0].max(axis=-1, keepdims=True)
    for s in score_parts[1:]:
        m = jnp.maximum(m, s.max(axis=-1, keepdims=True))
    if sink is not None:
        m = jnp.maximum(m, sink)
    acc = None
    for s, v in zip(score_parts, values):
        term = _dot(jnp.exp((s - m).astype(BF16)), v)
        acc = term if acc is None else acc + term
    den = acc[:, HEAD_DIM:HEAD_DIM + 1]
    if sink is not None:
        den = den + jnp.exp(sink - m)
    return acc[:, :HEAD_DIM] / den


def _attn_a_kernel(sink_ref, q_ref, kp_ref, kc_ref, kn_ref, vp_ref, vc_ref, vn_ref,
                   kctx_ref, vctx_ref, out_ref, *, n_lat, tq):
    i = pl.program_id(1)
    k_win = jnp.concatenate([kp_ref[0], kc_ref[0], kn_ref[0]], axis=0)
    v_win = jnp.concatenate([vp_ref[0], vc_ref[0], vn_ref[0]], axis=0)
    kctx = kctx_ref[0]
    vctx = vctx_ref[0]
    sub = A_WINDOW
    span = 3 * A_WINDOW
    rows = A_GROUP * sub
    row_i = lax.broadcasted_iota(jnp.int32, (rows, span), 0)
    col_i = lax.broadcasted_iota(jnp.int32, (rows, span), 1)
    rel = col_i - A_WINDOW - (row_i & (sub - 1))
    in_band = (rel <= A_WINDOW) & (rel >= -A_WINDOW)
    group_of_row = lax.broadcasted_iota(jnp.int32, (rows, 1), 0) >> (sub.bit_length() - 1)
    sinks = []
    for hk in range(A_KV_HEADS):
        sink = jnp.zeros((rows, 1), F32)
        for g in range(A_GROUP):
            sink = jnp.where(group_of_row == g, sink_ref[hk * A_GROUP + g], sink)
        sinks.append(sink)
    vctx_ext = [_with_ones(vctx[:, hk * HEAD_DIM:(hk + 1) * HEAD_DIM]) for hk in range(A_KV_HEADS)]
    v_ext = [_with_ones(v_win[:, hk * HEAD_DIM:(hk + 1) * HEAD_DIM]) for hk in range(A_KV_HEADS)]
    for j in range(tq // sub):
        kpos = i * tq + j * sub - A_WINDOW + col_i
        valid = in_band & (kpos >= 0) & (kpos < n_lat)
        q_rows = q_ref[0, j * sub:(j + 1) * sub]
        k_sub = k_win[j * sub:j * sub + span]
        for hk in range(A_KV_HEADS):
            sl = slice(hk * HEAD_DIM, (hk + 1) * HEAD_DIM)
            heads = [hk * A_GROUP + g for g in range(A_GROUP)]
            q = jnp.concatenate([q_rows[:, h * HEAD_DIM:(h + 1) * HEAD_DIM] for h in heads], axis=0)
            s_win = jnp.where(valid, _dot_t(q, k_sub[:, sl]), NEG_INF)
            s_ctx = _dot_t(q, kctx[:, sl])
            v_sub = v_ext[hk][j * sub:j * sub + span]
            o = _attend([s_win, s_ctx], [v_sub, vctx_ext[hk]], sink=sinks[hk]).astype(BF16)
            for g, h in enumerate(heads):
                out_ref[0, j * sub:(j + 1) * sub, h * HEAD_DIM:(h + 1) * HEAD_DIM] = o[g * sub:(g + 1) * sub]


def window_attention(qa, ka, va, kc_a, vc_a, sink):
    bsz, n_lat, _ = qa.shape
    n_ctx = kc_a.shape[1]
    tq = min(256, n_lat)
    w = A_WINDOW
    per = tq // w
    last = n_lat // w - 1
    prev = lambda b, i, s: (b, jnp.maximum(i * per - 1, 0), 0)
    cur = lambda b, i, s: (b, i, 0)
    nxt = lambda b, i, s: (b, jnp.minimum((i + 1) * per, last), 0)
    ctx = lambda b, i, s: (b, 0, 0)
    kv_specs = [pl.BlockSpec((1, w, A_KV_W), prev), pl.BlockSpec((1, tq, A_KV_W), cur),
                pl.BlockSpec((1, w, A_KV_W), nxt)]
    grid_spec = pltpu.PrefetchScalarGridSpec(
        num_scalar_prefetch=1,
        grid=(bsz, n_lat // tq),
        in_specs=[pl.BlockSpec((1, tq, A_Q_W), cur)] + kv_specs + kv_specs + [
            pl.BlockSpec((1, n_ctx, A_KV_W), ctx), pl.BlockSpec((1, n_ctx, A_KV_W), ctx)],
        out_specs=pl.BlockSpec((1, tq, A_Q_W), cur),
    )
    return pl.pallas_call(
        functools.partial(_attn_a_kernel, n_lat=n_lat, tq=tq),
        grid_spec=grid_spec,
        out_shape=jax.ShapeDtypeStruct((bsz, n_lat, A_Q_W), BF16),
        compiler_params=_params("parallel", "parallel"),
        name="window_attention",
    )(sink, qa, ka, ka, ka, va, va, va, kc_a, vc_a)


NB_Q_ROWS = 4


def _attn_b_kernel(q_ref, kp_ref, kc_ref, kn_ref, vp_ref, vc_ref, vn_ref,
                   kctx_ref, vctx_ref, bias_ref, out_ref):
    k_win = jnp.concatenate([kp_ref[0], kc_ref[0], kn_ref[0]], axis=0)
    v_win = jnp.concatenate([vp_ref[0], vc_ref[0], vn_ref[0]], axis=0)
    q_all = q_ref[0]
    kctx = kctx_ref[0]
    vctx = vctx_ref[0]
    for h in range(B_HEADS):
        sl = slice(h * HEAD_DIM, (h + 1) * HEAD_DIM)
        q = q_all[:, sl]
        s_nb = _dot_t(q, k_win[:, sl]) + bias_ref[0, h]
        s_ctx = _dot_t(q, kctx[:, sl])
        o = _attend([s_nb, s_ctx], [_with_ones(v_win[:, sl]), _with_ones(vctx[:, sl])])
        out_ref[0, :, sl] = o.astype(BF16)


def neighbourhood_bias(rel_bias, n_lat):
    rows = n_lat // GRID_W
    kr_n = min(NA_ROWS, rows)
    n_blocks = rows // NB_Q_ROWS
    n_heads, n_dr, n_dc = rel_bias.shape
    cols = np.arange(GRID_W)
    c_start = np.clip(cols - NA_COLS // 2, 0, GRID_W - NA_COLS)
    col_ok = (cols[None, :] >= c_start[:, None]) & (cols[None, :] < c_start[:, None] + NA_COLS)
    dc = np.clip(cols[None, :] - cols[:, None], -(NA_COLS - 1), NA_COLS - 1) + NA_COLS - 1
    pick_dc = (dc.reshape(-1)[None, :] == np.arange(n_dc)[:, None]).astype(np.float32)
    toeplitz = jnp.dot(rel_bias.reshape(n_heads * n_dr, n_dc), pick_dc, precision=lax.Precision.HIGHEST)
    toeplitz = jnp.where(col_ok.reshape(-1), toeplitz, NEG_INF).reshape(n_heads, n_dr, GRID_W, GRID_W)
    q_rl = np.arange(NB_Q_ROWS)
    k_rl = np.arange(3 * NB_Q_ROWS)
    row_ok, dr = [], []
    for j in sorted({0, min(1, n_blocks - 1), n_blocks - 1}):
        r = NB_Q_ROWS * j + q_rl
        kr = NB_Q_ROWS * (j - 1) + k_rl
        r_start = np.clip(r - kr_n // 2, 0, rows - kr_n)
        ok = (kr[None, :] >= r_start[:, None]) & (kr[None, :] < r_start[:, None] + kr_n)
        row_ok.append(ok & (kr[None, :] >= 0) & (kr[None, :] < rows))
        dr.append(np.clip(kr[None, :] - r[:, None] + NA_ROWS - 1, 0, n_dr - 1))
    row_ok = np.stack(row_ok)
    dr = np.stack(dr)
    tiles = jnp.stack([toeplitz[:, int(i)] for i in dr.reshape(-1)], axis=1)
    tiles = tiles.reshape((n_heads,) + dr.shape + (GRID_W, GRID_W))
    tiles = jnp.where(row_ok[None, :, :, :, None, None], tiles, NEG_INF)
    table = tiles.transpose(1, 0, 2, 4, 3, 5).reshape(
        dr.shape[0], n_heads, NB_Q_ROWS * GRID_W, 3 * NB_Q_ROWS * GRID_W)
    return table, n_blocks


def neighbourhood_attention(qb, kb, vb, kc_b, vc_b, rel_bias):
    bsz, n_lat, _ = qb.shape
    n_ctx = kc_b.shape[1]
    table, n_blocks = neighbourhood_bias(rel_bias, n_lat)
    n_var = table.shape[0]
    tq = NB_Q_ROWS * GRID_W
    prev = lambda b, j: (b, jnp.maximum(j - 1, 0), 0)
    cur = lambda b, j: (b, j, 0)
    nxt = lambda b, j: (b, jnp.minimum(j + 1, n_blocks - 1), 0)
    ctx = lambda b, j: (b, 0, 0)

    def variant(b, j):
        v = jnp.where(j == 0, 0, jnp.where(j == n_blocks - 1, n_var - 1, min(1, n_var - 1)))
        return (v, 0, 0, 0)

    kv_specs = [pl.BlockSpec((1, tq, B_W), prev), pl.BlockSpec((1, tq, B_W), cur),
                pl.BlockSpec((1, tq, B_W), nxt)]
    return pl.pallas_call(
        _attn_b_kernel,
        grid=(bsz, n_blocks),
        in_specs=[pl.BlockSpec((1, tq, B_W), cur)] + kv_specs + kv_specs + [
            pl.BlockSpec((1, n_ctx, B_W), ctx), pl.BlockSpec((1, n_ctx, B_W), ctx),
            pl.BlockSpec((1, B_HEADS, tq, 3 * tq), variant)],
        out_specs=pl.BlockSpec((1, tq, B_W), cur),
        out_shape=jax.ShapeDtypeStruct((bsz, n_lat, B_W), BF16),
        compiler_params=_params("parallel", "parallel"),
        name="neighbourhood_attention",
    )(qb, kb, kb, kb, vb, vb, vb, kc_b, vc_b, table)


def _ctx_attn_kernel(sink_ref, qa_ref, qb_ref, ka_ref, va_ref, kb_ref, vb_ref, oa_ref, ob_ref):
    qa, qb = qa_ref[0], qb_ref[0]
    ka, va, kb, vb = ka_ref[0], va_ref[0], kb_ref[0], vb_ref[0]
    for hq in range(A_Q_HEADS):
        sl = slice(hq * HEAD_DIM, (hq + 1) * HEAD_DIM)
        hk = hq // A_GROUP
        kv = slice(hk * HEAD_DIM, (hk + 1) * HEAD_DIM)
        q = qa[:, sl]
        o = _attend([_dot_t(q, ka[:, kv])], [_with_ones(va[:, kv])], sink=sink_ref[hq])
        oa_ref[0, :, sl] = o.astype(BF16)
    for h in range(B_HEADS):
        sl = slice(h * HEAD_DIM, (h + 1) * HEAD_DIM)
        q = qb[:, sl]
        o = _attend([_dot_t(q, kb[:, sl])], [_with_ones(vb[:, sl])])
        ob_ref[0, :, sl] = o.astype(BF16)


def context_attention(qa, qb, ka, va, kb, vb, sink):
    bsz, n_ctx, _ = qa.shape
    blk = lambda w: pl.BlockSpec((1, n_ctx, w), lambda b, s: (b, 0, 0))
    grid_spec = pltpu.PrefetchScalarGridSpec(
        num_scalar_prefetch=1,
        grid=(bsz,),
        in_specs=[blk(A_Q_W), blk(B_W), blk(A_KV_W), blk(A_KV_W), blk(B_W), blk(B_W)],
        out_specs=[blk(A_Q_W), blk(B_W)],
    )
    return pl.pallas_call(
        _ctx_attn_kernel,
        grid_spec=grid_spec,
        out_shape=[jax.ShapeDtypeStruct((bsz, n_ctx, A_Q_W), BF16),
                   jax.ShapeDtypeStruct((bsz, n_ctx, B_W), BF16)],
        compiler_params=_params("parallel"),
        name="context_attention",
    )(sink, qa, qb, ka, va, kb, vb)


CONV_HALO = 16


def _conv_kernel(prev_ref, cur_ref, next_ref, w_ref, b_ref, g_ref, beta_ref, out_ref, ext_ref, *, ts):
    i = pl.program_id(1)
    n_i = pl.num_programs(1)
    ext_ref[0:CONV_HALO] = jnp.where(i > 0, prev_ref[0], 0.0)
    ext_ref[CONV_HALO:CONV_HALO + ts] = cur_ref[0]
    ext_ref[CONV_HALO + ts:] = jnp.where(i < n_i - 1, next_ref[0], 0.0)
    half = C_CONV_WIDTH // 2
    acc = jnp.zeros((ts, C_CHANNELS), F32)
    for k in range(C_CONV_WIDTH):
        acc = acc + ext_ref[pl.ds(CONV_HALO - half + k, ts)] * w_ref[k:k + 1]
    y = _layer_norm(acc + b_ref[...]) * g_ref[...] + beta_ref[...]
    out_ref[0] = (y * jax.nn.sigmoid(y)).astype(BF16)


def conformer_conv(hc, conv_w, conv_b, ln_g, ln_b):
    bx, t, ch = hc.shape
    ts = min(512, t)
    per = ts // CONV_HALO
    last = t // CONV_HALO - 1
    row = lambda v: v.reshape(1, ch)
    const = lambda b, i: (0, 0)
    return pl.pallas_call(
        functools.partial(_conv_kernel, ts=ts),
        grid=(bx, t // ts),
        in_specs=[
            pl.BlockSpec((1, CONV_HALO, ch), lambda b, i: (b, jnp.maximum(i * per - 1, 0), 0)),
            pl.BlockSpec((1, ts, ch), lambda b, i: (b, i, 0)),
            pl.BlockSpec((1, CONV_HALO, ch), lambda b, i: (b, jnp.minimum((i + 1) * per, last), 0)),
            pl.BlockSpec((C_CONV_WIDTH, ch), const),
            pl.BlockSpec((1, ch), const), pl.BlockSpec((1, ch), const), pl.BlockSpec((1, ch), const),
        ],
        out_specs=pl.BlockSpec((1, ts, ch), lambda b, i: (b, i, 0)),
        out_shape=jax.ShapeDtypeStruct((bx, t, ch), BF16),
        scratch_shapes=[pltpu.VMEM((ts + 2 * CONV_HALO, ch), F32)],
        compiler_params=_params("parallel", "parallel"),
        name="conformer_conv",
    )(hc, hc, hc, conv_w, row(conv_b), row(ln_g), row(ln_b))


def _outproj_kernel(oa_ref, ob_ref, oc_ref, x_ref, g1_ref, sh_ref, sc_ref, w_ref, lng_ref, lnb_ref,
                    wr_hi_ref, wr_lo_ref, x1_ref, h2_ref, afft_ref, aff_ref, *, alpha):
    o = (_dot(oa_ref[0], w_ref[0:A_Q_W])
         + _dot(ob_ref[0], w_ref[A_Q_W:A_Q_W + B_W])
         + _dot(oc_ref[0], w_ref[A_Q_W + B_W:]))
    y = _layer_norm(alpha * x_ref[0] + g1_ref[0] * o) * lng_ref[...] + lnb_ref[...]
    x1_ref[0] = y
    h2 = _layer_norm(y) * (1.0 + sc_ref[0]) + sh_ref[0]
    h_hi = h2.astype(BF16)
    rows_per_token = h2.shape[1] // LANES
    for j in range(rows_per_token):
        h2_ref[0, pl.ds(j, h2.shape[0], stride=rows_per_token), :] = h2[:, j * LANES:(j + 1) * LANES]
    h_lo = (h2 - h_hi.astype(F32)).astype(BF16)
    w_hi, w_lo = wr_hi_ref[...], wr_lo_ref[...]
    logits_t = _dot_t(w_hi, h_hi) + _dot_t(w_hi, h_lo) + _dot_t(w_lo, h_hi)
    e_t = jnp.exp(logits_t - logits_t.max(axis=0, keepdims=True))
    afft_ref[0] = e_t / e_t.sum(axis=0, keepdims=True)
    logits = _dot_t(h_hi, w_hi) + _dot_t(h_lo, w_hi) + _dot_t(h_hi, w_lo)
    e_n = jnp.exp(logits - logits.max(axis=1, keepdims=True))
    aff_ref[0] = e_n / e_n.sum(axis=1, keepdims=True)


def out_projection(oa, ob, oc, x, g1, sh2, sc2, w_out_bf16, ln_g, ln_b, wr_hi, wr_lo, alpha):
    bx, t, d = x.shape
    tm = min(512, t)
    tok = lambda b, i: (b, i, 0)
    per_b = lambda b, i: (b, 0, 0)
    const = lambda b, i: (0, 0)
    vec = pl.BlockSpec((1, d), const)
    return pl.pallas_call(
        functools.partial(_outproj_kernel, alpha=alpha),
        grid=(bx, t // tm),
        in_specs=[
            pl.BlockSpec((1, tm, A_Q_W), tok), pl.BlockSpec((1, tm, B_W), tok),
            pl.BlockSpec((1, tm, C_CHANNELS), tok), pl.BlockSpec((1, tm, d), tok),
            pl.BlockSpec((1, 1, d), per_b), pl.BlockSpec((1, 1, d), per_b), pl.BlockSpec((1, 1, d), per_b),
            pl.BlockSpec(w_out_bf16.shape, const), vec, vec,
            pl.BlockSpec((N_EXPERTS, d), const), pl.BlockSpec((N_EXPERTS, d), const),
        ],
        out_specs=[pl.BlockSpec((1, tm, d), tok), pl.BlockSpec((1, tm * (d // LANES), LANES), tok),
                   pl.BlockSpec((1, N_EXPERTS, tm), lambda b, i: (b, 0, i)),
                   pl.BlockSpec((1, tm, N_EXPERTS), tok)],
        out_shape=[jax.ShapeDtypeStruct((bx, t, d), F32),
                   jax.ShapeDtypeStruct((bx, t * (d // LANES), LANES), F32),
                   jax.ShapeDtypeStruct((bx, N_EXPERTS, t), F32),
                   jax.ShapeDtypeStruct((bx, t, N_EXPERTS), F32)],
        compiler_params=_params("parallel", "parallel"),
        name="out_projection",
    )(oa, ob, oc, x, g1, sh2, sc2, w_out_bf16, ln_g.reshape(1, d), ln_b.reshape(1, d), wr_hi, wr_lo)


def _select_kernel(afft_ref, pos_ref, post_ref, off_ref, *, cap, n_tok):
    aff = afft_ref[0]

    def bit_step(j, bits):
        cand = bits | (jnp.int32(1) << (30 - j))
        cnt = jnp.sum((aff >= pltpu.bitcast(cand, F32)).astype(jnp.int32), axis=1, keepdims=True)
        return jnp.where(cnt >= cap, cand, bits)

    thr = pltpu.bitcast(lax.fori_loop(0, 31, bit_step, jnp.zeros((N_EXPERTS, 1), jnp.int32)), F32)
    above = (aff > thr).astype(F32)
    tied = (aff == thr).astype(F32)
    need = cap - jnp.sum(above, axis=1, keepdims=True)

    blk = LANES
    n_blk = n_tok // blk
    r_i = lax.broadcasted_iota(jnp.int32, (blk, blk), 0)
    c_i = lax.broadcasted_iota(jnp.int32, (blk, blk), 1)
    strict_upper = (r_i < c_i).astype(BF16)
    eye = (r_i == c_i).astype(BF16)

    run_tied = jnp.zeros((N_EXPERTS, 1), F32)
    run_sel = jnp.zeros((N_EXPERTS, 1), F32)
    offs = []
    for kb in range(n_blk):
        sl = slice(kb * blk, (kb + 1) * blk)
        offs.append(run_sel)
        t_f = tied[:, sl]
        rank_tied = run_tied + _dot(t_f.astype(BF16), strict_upper)
        run_tied = run_tied + t_f.sum(axis=1, keepdims=True)
        s_f = above[:, sl] + t_f * (rank_tied < need).astype(F32)
        rank = run_sel + _dot(s_f.astype(BF16), strict_upper)
        run_sel = run_sel + s_f.sum(axis=1, keepdims=True)
        pos = jnp.where(s_f > 0.5, rank.astype(jnp.int32), -1)
        pos_ref[0, :, sl] = pos
        p1 = pos + 1
        hi = (p1 >> 5).astype(F32).astype(BF16)
        lo = (p1 & 31).astype(F32).astype(BF16)
        p1_t = _dot_t(eye, hi) * 32.0 + _dot_t(eye, lo)
        post_ref[0, sl, :] = p1_t.astype(jnp.int32) - 1
    off_ref[0] = jnp.concatenate(offs, axis=1).astype(jnp.int32)


def expert_choice_select(aff_t, cap):
    bx, n_e, t = aff_t.shape
    n_tb = t // OFFSET_BLOCK
    return pl.pallas_call(
        functools.partial(_select_kernel, cap=cap, n_tok=t),
        grid=(bx,),
        in_specs=[pl.BlockSpec((1, n_e, t), lambda b: (b, 0, 0))],
        out_specs=[pl.BlockSpec((1, n_e, t), lambda b: (b, 0, 0)),
                   pl.BlockSpec((1, t, n_e), lambda b: (b, 0, 0)),
                   pl.BlockSpec((1, n_e, n_tb), lambda b: (b, 0, 0))],
        out_shape=[jax.ShapeDtypeStruct((bx, n_e, t), jnp.int32),
                   jax.ShapeDtypeStruct((bx, t, n_e), jnp.int32),
                   jax.ShapeDtypeStruct((bx, n_e, n_tb), jnp.int32)],
        compiler_params=_params("parallel"),
        name="expert_choice_select",
    )(aff_t)


GATE_PARTS = 3
TOKEN_LANE = GATE_PARTS


def _slot_table_kernel(off_ref, pos_ref, aff_ref, tbl_ref, *, n_off, blocks_per_step):
    b, eg, kc = pl.program_id(0), pl.program_id(1), pl.program_id(2)
    epg = EXPERTS_PER_GATHER_STEP

    @pl.when(kc == 0)
    def _():
        tbl_ref[...] = jnp.zeros_like(tbl_ref)

    slot = lax.broadcasted_iota(jnp.int32, (GATHER_WINDOW, GATHER_BLOCK), 0)
    lane = lax.broadcasted_iota(jnp.int32, (GATHER_BLOCK, LANES), 1)
    out_lane = lax.broadcasted_iota(jnp.int32, (GATHER_WINDOW, LANES), 1)
    local_token = lax.broadcasted_iota(jnp.int32, (GATHER_BLOCK, LANES), 0).astype(F32)
    expert_lane = lax.broadcasted_iota(jnp.int32, (GATHER_BLOCK, N_EXPERTS), 1)
    for kk in range(blocks_per_step):
        kb = kc * blocks_per_step + kk
        aff_blk = aff_ref[0, kk * GATHER_BLOCK:(kk + 1) * GATHER_BLOCK]
        first_token = (kb * GATHER_BLOCK).astype(F32)
        for ee in range(epg):
            e = eg * epg + ee
            off = off_ref[(b * N_EXPERTS + e) * n_off + kb * (GATHER_BLOCK // OFFSET_BLOCK)]
            base = pl.multiple_of((off >> 4) << 4, BF16_SUBLANES)
            onehot = (pos_ref[0, ee, 0, kk:kk + 1, :] - base == slot).astype(BF16)
            gate = jnp.sum(jnp.where(expert_lane == e, aff_blk, 0.0), axis=1, keepdims=True)
            payload, rest = jnp.where(lane == TOKEN_LANE, local_token, 0.0), gate
            for k in range(GATE_PARTS):
                part = rest.astype(BF16).astype(F32)
                payload = jnp.where(lane == k, part, payload)
                rest = rest - part
            gathered = _dot(onehot, payload.astype(BF16)) + jnp.where(out_lane == TOKEN_LANE, first_token, 0.0)
            win = pl.ds(base, GATHER_WINDOW)
            tbl_ref[0, ee, win, :] = jnp.where(slot[:, 0:1] >= off - base, gathered, tbl_ref[0, ee, win, :])


def slot_table(aff, pos, block_off, cap_pad):
    bx, t, _ = aff.shape
    n_tb = t // GATHER_BLOCK
    blocks_per_step = min(4, n_tb)
    n_steps = n_tb // blocks_per_step
    epg = EXPERTS_PER_GATHER_STEP
    tokens = blocks_per_step * GATHER_BLOCK
    pos5 = pos.reshape(bx, N_EXPERTS, n_steps, blocks_per_step, GATHER_BLOCK)
    grid_spec = pltpu.PrefetchScalarGridSpec(
        num_scalar_prefetch=1,
        grid=(bx, N_EXPERTS // epg, n_steps),
        in_specs=[
            pl.BlockSpec((1, epg, 1, blocks_per_step, GATHER_BLOCK), lambda b, g, k, s: (b, g, k, 0, 0)),
            pl.BlockSpec((1, tokens, N_EXPERTS), lambda b, g, k, s: (b, k, 0)),
        ],
        out_specs=pl.BlockSpec((1, epg, cap_pad, LANES), lambda b, g, k, s: (b, g, 0, 0)),
    )
    return pl.pallas_call(
        functools.partial(_slot_table_kernel, n_off=t // OFFSET_BLOCK, blocks_per_step=blocks_per_step),
        grid_spec=grid_spec,
        out_shape=jax.ShapeDtypeStruct((bx, N_EXPERTS, cap_pad, LANES), F32),
        compiler_params=_params("parallel", "parallel", "arbitrary"),
        name="slot_table",
    )(block_off.reshape(-1), pos5, aff)


GATHER_UNROLL = 8


def _row_gather_kernel(idx_ref, src_ref, xs_ref, tile_ref, *, cap, rows_per_token, chunk_stride):
    b, e = pl.program_id(0), pl.program_id(1)
    first = (b * N_EXPERTS + e) * cap

    def group(g, carry):
        for u in range(GATHER_UNROLL):
            s = g * GATHER_UNROLL + u
            row = pl.multiple_of(idx_ref[first + s] * rows_per_token, rows_per_token)
            tile_ref[pl.ds(s, rows_per_token, stride=chunk_stride), :] = src_ref[0, pl.ds(row, rows_per_token), :]
        return carry

    lax.fori_loop(0, cap // GATHER_UNROLL, group, 0)
    for j in range(rows_per_token):
        xs_ref[0, 0, :, j * LANES:(j + 1) * LANES] = tile_ref[j * chunk_stride:j * chunk_stride + cap, :].astype(BF16)


def gather_rows(h2_rows, token_idx, cap, d):
    bx, n_rows, _ = h2_rows.shape
    rows_per_token = d // LANES
    chunk_stride = cap + 8
    grid_spec = pltpu.PrefetchScalarGridSpec(
        num_scalar_prefetch=1,
        grid=(bx, N_EXPERTS),
        in_specs=[pl.BlockSpec((1, n_rows, LANES), lambda b, e, s: (b, 0, 0), pipeline_mode=pl.Buffered(1))],
        out_specs=pl.BlockSpec((1, 1, cap, d), lambda b, e, s: (b, e, 0, 0)),
        scratch_shapes=[pltpu.VMEM((rows_per_token * chunk_stride, LANES), F32)],
    )
    return pl.pallas_call(
        functools.partial(_row_gather_kernel, cap=cap, rows_per_token=rows_per_token, chunk_stride=chunk_stride),
        grid_spec=grid_spec,
        out_shape=jax.ShapeDtypeStruct((bx, N_EXPERTS, cap, d), BF16),
        compiler_params=_params("parallel", "arbitrary"),
        name="gather_rows",
    )(token_idx, h2_rows)


def _ffn_kernel(xs_ref, gs_ref, wg_ref, wu_ref, wd_ref, ye_ref, wg_bf, wu_bf, wd_bf, *, cap, row_tile):
    @pl.when(pl.program_id(1) == 0)
    def _():
        wg_bf[...] = wg_ref[0, 0].astype(BF16)
        wu_bf[...] = wu_ref[0, 0].astype(BF16)
        wd_bf[...] = wd_ref[0, 0].astype(BF16)

    for r0 in range(0, cap, row_tile):
        rows = slice(r0, r0 + row_tile)
        x = xs_ref[0, 0, rows]
        gate = _dot(x, wg_bf[...])
        up = _dot(x, wu_bf[...])
        hid = (gate * jax.nn.sigmoid(gate) * up).astype(BF16)
        g = jnp.sum(gs_ref[0, 0, rows, 0:GATE_PARTS], axis=1, keepdims=True)
        ye_ref[0, 0, rows] = (_dot(hid, wd_bf[...]) * g).astype(BF16)
    ye_ref[0, 0, cap:] = jnp.zeros((ye_ref.shape[2] - cap, ye_ref.shape[3]), BF16)


def expert_ffn(xs, gs, wg, wu, wd, layer, cap_pad):
    bx, n_e, cap, d = xs.shape
    ff = wg.shape[-1]
    return pl.pallas_call(
        functools.partial(_ffn_kernel, cap=cap, row_tile=min(512, cap)),
        grid=(n_e, bx),
        in_specs=[
            pl.BlockSpec((1, 1, cap, d), lambda e, b: (b, e, 0, 0)),
            pl.BlockSpec((1, 1, cap, LANES), lambda e, b: (b, e, 0, 0)),
            pl.BlockSpec((1, 1, d, ff), lambda e, b: (layer, e, 0, 0)),
            pl.BlockSpec((1, 1, d, ff), lambda e, b: (layer, e, 0, 0)),
            pl.BlockSpec((1, 1, ff, d), lambda e, b: (layer, e, 0, 0)),
        ],
        out_specs=pl.BlockSpec((1, 1, cap_pad, d), lambda e, b: (b, e, 0, 0)),
        out_shape=jax.ShapeDtypeStruct((bx, n_e, cap_pad, d), BF16),
        scratch_shapes=[pltpu.VMEM((d, ff), BF16), pltpu.VMEM((d, ff), BF16), pltpu.VMEM((ff, d), BF16)],
        compiler_params=_params("parallel", "arbitrary"),
        name="expert_ffn",
    )(xs, gs, wg, wu, wd)


EXPERTS_PER_SCATTER_STEP = 4


def _combine_kernel(off_ref, post_ref, ye_ref, x1_ref, g2_ref, lng_ref, lnb_ref,
                    out_ref, acc_ref, *, n_tb, blocks_per_step, alpha):
    b, tq, eg = pl.program_id(0), pl.program_id(1), pl.program_id(2)
    eps = EXPERTS_PER_SCATTER_STEP

    @pl.when(eg == 0)
    def _():
        acc_ref[...] = jnp.zeros_like(acc_ref)

    expert_lane = lax.broadcasted_iota(jnp.int32, (SCATTER_BLOCK, N_EXPERTS), 1)
    slot = lax.broadcasted_iota(jnp.int32, (SCATTER_BLOCK, SCATTER_WINDOW), 1)
    for kk in range(blocks_per_step):
        kb = tq * blocks_per_step + kk
        rows = slice(kk * SCATTER_BLOCK, (kk + 1) * SCATTER_BLOCK)
        post_blk = post_ref[0, rows, :]
        contrib = None
        for ee in range(eps):
            e = eg * eps + ee
            off = off_ref[(b * N_EXPERTS + e) * n_tb + kb * (SCATTER_BLOCK // OFFSET_BLOCK)]
            base = pl.multiple_of((off >> 4) << 4, BF16_SUBLANES)
            pcol = jnp.sum(jnp.where(expert_lane == e, post_blk, 0), axis=1, keepdims=True)
            onehot = (pcol - base == slot).astype(BF16)
            term = _dot(onehot, ye_ref[0, ee, pl.ds(base, SCATTER_WINDOW), :])
            contrib = term if contrib is None else contrib + term
        acc_ref[rows, :] += contrib

    @pl.when(eg == N_EXPERTS // eps - 1)
    def _():
        y = alpha * x1_ref[0] + g2_ref[0] * acc_ref[...]
        out_ref[0] = _layer_norm(y) * lng_ref[...] + lnb_ref[...]


def combine_and_norm(ye, pos_t, block_off, x1, g2, ln_g, ln_b, alpha):
    bx, t, d = x1.shape
    cap_pad = ye.shape[2]
    n_tb = t // OFFSET_BLOCK
    blocks_per_step = min(8, t // SCATTER_BLOCK)
    tq = blocks_per_step * SCATTER_BLOCK
    eps = EXPERTS_PER_SCATTER_STEP
    tok = lambda b, q, e, s: (b, q, 0)
    const = lambda b, q, e, s: (0, 0)
    grid_spec = pltpu.PrefetchScalarGridSpec(
        num_scalar_prefetch=1,
        grid=(bx, t // tq, N_EXPERTS // eps),
        in_specs=[
            pl.BlockSpec((1, tq, N_EXPERTS), tok),
            pl.BlockSpec((1, eps, cap_pad, d), lambda b, q, e, s: (b, e, 0, 0)),
            pl.BlockSpec((1, tq, d), tok),
            pl.BlockSpec((1, 1, d), lambda b, q, e, s: (b, 0, 0)),
            pl.BlockSpec((1, d), const), pl.BlockSpec((1, d), const),
        ],
        out_specs=pl.BlockSpec((1, tq, d), tok),
        scratch_shapes=[pltpu.VMEM((tq, d), F32)],
    )
    return pl.pallas_call(
        functools.partial(_combine_kernel, n_tb=n_tb, blocks_per_step=blocks_per_step, alpha=alpha),
        grid_spec=grid_spec,
        out_shape=jax.ShapeDtypeStruct((bx, t, d), F32),
        compiler_params=_params("parallel", "parallel", "arbitrary"),
        name="combine_and_norm",
    )(block_off.reshape(-1), pos_t, ye, x1, g2, ln_g.reshape(1, d), ln_b.reshape(1, d))


def _split_bf16(w):
    hi = w.astype(BF16)
    return hi, (w - hi.astype(F32)).astype(BF16)


def _mixer_tail(oa, ob, oc, x, mod, lw, alpha):
    g1, sh2, sc2, g2 = mod
    t = x.shape[1]
    cap = EC_CAPACITY * t // N_EXPERTS
    cap_pad = cap + SLOT_PAD
    x1, h2_rows, aff_t, aff = out_projection(oa, ob, oc, x, g1, sh2, sc2, lw["w_out"], lw["ln1_g"], lw["ln1_b"],
                                        lw["wr_hi"], lw["wr_lo"], alpha)
    pos, pos_t, block_off = expert_choice_select(aff_t, cap)
    table = slot_table(aff, pos, block_off, cap_pad)
    token_idx = table[:, :, :cap, TOKEN_LANE].astype(jnp.int32).reshape(-1)
    xs = gather_rows(h2_rows, token_idx, cap, x.shape[2])
    ye = expert_ffn(xs, table, lw["w_gate"], lw["w_up"], lw["w_down"], lw["layer"], cap_pad)
    return combine_and_norm(ye, pos_t, block_off, x1, g2, lw["ln2_g"], lw["ln2_b"], alpha)


def kernel(x, c, ctx, c_ctx, w_mod, b_mod, w_in, a_sink, nat_bias, conv_w, conv_b, conv_ln_g, conv_ln_b,
           w_out, ln1_g, ln1_b, w_router, w_gate, w_up, w_down, ln2_g, ln2_b):
    bsz, n_lat, d = x.shape
    depth = w_mod.shape[0]
    alpha = (2 * depth) ** 0.25
    cos_t, sin_t = rope_tables(n_lat)

    cond = jnp.concatenate([c, c_ctx[None, :], jnp.zeros((8 - bsz - 1, d), F32)], axis=0)
    mods = adaln_all(cond, w_mod, b_mod)

    for l in range(depth):
        last = l == depth - 1
        wr_hi, wr_lo = _split_bf16(w_router[l].T)
        lw = dict(w_out=w_out[l].astype(BF16), ln1_g=ln1_g[l], ln1_b=ln1_b[l], wr_hi=wr_hi, wr_lo=wr_lo,
                  w_gate=w_gate, w_up=w_up, w_down=w_down, layer=l,
                  ln2_g=ln2_g[l], ln2_b=ln2_b[l])
        w_in_l = w_in[l].astype(BF16)
        lat = [mods[l, :bsz, k * d:(k + 1) * d][:, None, :] for k in range(N_MOD)]
        cm = [jnp.broadcast_to(mods[l, bsz, k * d:(k + 1) * d][None, None, :], (bsz, 1, d))
              for k in range(N_MOD)]
        conv_args = (conv_w[l], conv_b[l], conv_ln_g[l], conv_ln_b[l])

        qa_c, ka_c, va_c, qb_c, kb_c, vb_c, hc_c = in_projection(ctx, cm[0], cm[1], w_in_l, cos_t, sin_t, rope=False)
        if not last:
            oa_c, ob_c = context_attention(qa_c, qb_c, ka_c, va_c, kb_c, vb_c, a_sink[l])
            oc_c = conformer_conv(hc_c, *conv_args)
            ctx_new = _mixer_tail(oa_c, ob_c, oc_c, ctx, (cm[2], cm[3], cm[4], cm[5]), lw, alpha)

        qa, ka, va, qb, kb, vb, hc = in_projection(x, lat[0], lat[1], w_in_l, cos_t, sin_t, rope=True)
        oa = window_attention(qa, ka, va, ka_c, va_c, a_sink[l])
        ob = neighbourhood_attention(qb, kb, vb, kb_c, vb_c, nat_bias[l])
        oc = conformer_conv(hc, *conv_args)
        x = _mixer_tail(oa, ob, oc, x, (lat[2], lat[3], lat[4], lat[5]), lw, alpha)
        if not last:
            ctx = ctx_new
    return x
```

```python
import functools

import numpy as np
import jax
import jax.numpy as jnp
from jax import lax
from jax.experimental import pallas as pl
from jax.experimental.pallas import tpu as pltpu

HEAD_DIM = 64
GRID_W = 64
A_Q_HEADS = 8
A_KV_HEADS = 2
A_GROUP = A_Q_HEADS // A_KV_HEADS
A_WINDOW = 128
B_HEADS = 4
NA_ROWS = 8
NA_COLS = 16
C_CHANNELS = 256
C_CONV_WIDTH = 31
A_Q_W = A_Q_HEADS * HEAD_DIM
A_KV_W = A_KV_HEADS * HEAD_DIM
B_W = B_HEADS * HEAD_DIM
OFF_AK = A_Q_W
OFF_AV = OFF_AK + A_KV_W
OFF_BQ = OFF_AV + A_KV_W
OFF_BK = OFF_BQ + B_W
OFF_BV = OFF_BK + B_W
OFF_C = OFF_BV + B_W
IN_WIDTH = OFF_C + 2 * C_CHANNELS
ROPE_WIDTH = A_Q_W + A_KV_W
N_EXPERTS = 16
EC_CAPACITY = 2
ROPE_BASE = 10000.0
LN_EPS = 1e-6
N_MOD = 6
NEG_INF = -1e30
QK_SCALE = HEAD_DIM ** -0.5

LANES = 128
BF16_SUBLANES = 16
MXU_DEPTH = 256
OFFSET_BLOCK = 128
GATHER_BLOCK = MXU_DEPTH
GATHER_WINDOW = GATHER_BLOCK + BF16_SUBLANES
SCATTER_BLOCK = OFFSET_BLOCK
SCATTER_WINDOW = MXU_DEPTH
SLOT_PAD = 3 * LANES
EXPERTS_PER_GATHER_STEP = 4
VMEM_LIMIT = 56 * 1024 * 1024

F32 = jnp.float32
BF16 = jnp.bfloat16


def _dot(a, b):
    return jnp.dot(a, b, preferred_element_type=F32)


def _dot_t(a, b):
    return lax.dot_general(a, b, (((1,), (1,)), ((), ())), preferred_element_type=F32)


def _layer_norm(x):
    mu = jnp.mean(x, axis=-1, keepdims=True)
    xc = x - mu
    var = jnp.mean(xc * xc, axis=-1, keepdims=True)
    return xc * lax.rsqrt(var + LN_EPS)


def _params(*sem):
    return pltpu.CompilerParams(dimension_semantics=sem, vmem_limit_bytes=VMEM_LIMIT)


def _mod_kernel(cond_ref, w_ref, b_ref, out_ref):
    cnd = cond_ref[...]
    act = cnd * jax.nn.sigmoid(cnd)
    out_ref[0] = jnp.dot(act, w_ref[0], preferred_element_type=F32,
                         precision=lax.Precision.HIGHEST) + b_ref[0]


def adaln_all(cond, w_mod, b_mod):
    n_layers, d, width = w_mod.shape
    rows = cond.shape[0]
    tn = 1536
    return pl.pallas_call(
        _mod_kernel,
        grid=(n_layers, width // tn),
        in_specs=[
            pl.BlockSpec((rows, d), lambda l, j: (0, 0)),
            pl.BlockSpec((1, d, tn), lambda l, j: (l, 0, j)),
            pl.BlockSpec((1, 1, tn), lambda l, j: (l, 0, j)),
        ],
        out_specs=pl.BlockSpec((1, rows, tn), lambda l, j: (l, 0, j)),
        out_shape=jax.ShapeDtypeStruct((n_layers, rows, width), F32),
        compiler_params=_params("parallel", "parallel"),
        name="adaln",
    )(cond, w_mod, b_mod.reshape(n_layers, 1, width))


def _inproj_kernel(x_ref, sh_ref, sc_ref, w_ref, cos_ref, sin_ref,
                   qa_ref, ka_ref, va_ref, qb_ref, kb_ref, vb_ref, hc_ref, *, rope):
    x = x_ref[0]
    h = _layer_norm(x) * (1.0 + sc_ref[0]) + sh_ref[0]
    u = _dot(h.astype(BF16), w_ref[...])

    def rotated(col):
        xq = u[:, col:col + LANES]
        if not rope:
            return xq
        lane = lax.broadcasted_iota(jnp.int32, xq.shape, 1)
        first = (lane & (HEAD_DIM // 2 - 1)) < (HEAD_DIM // 4)
        partner = jnp.where(first, pltpu.roll(xq, LANES - HEAD_DIM // 4, 1),
                            pltpu.roll(xq, HEAD_DIM // 4, 1))
        return xq * cos_ref[...] + partner * sin_ref[...]

    rot = [rotated(col) for col in range(0, ROPE_WIDTH, LANES)]
    n_q = A_Q_W // LANES
    qa_ref[0] = (jnp.concatenate(rot[:n_q], axis=1) * QK_SCALE).astype(BF16)
    ka_ref[0] = jnp.concatenate(rot[n_q:], axis=1).astype(BF16)
    va_ref[0] = u[:, OFF_AV:OFF_BQ].astype(BF16)
    qb_ref[0] = (u[:, OFF_BQ:OFF_BK] * QK_SCALE).astype(BF16)
    kb_ref[0] = u[:, OFF_BK:OFF_BV].astype(BF16)
    vb_ref[0] = u[:, OFF_BV:OFF_C].astype(BF16)
    a = u[:, OFF_C:OFF_C + C_CHANNELS]
    gate = u[:, OFF_C + C_CHANNELS:]
    hc_ref[0] = a * jax.nn.sigmoid(gate)


def in_projection(x, shift, scale, w_in_bf16, cos_t, sin_t, *, rope):
    bx, t, d = x.shape
    tm = min(512, t)
    widths = (A_Q_W, A_KV_W, A_KV_W, B_W, B_W, B_W, C_CHANNELS)
    dtypes = (BF16,) * 6 + (F32,)
    tok = lambda b, i: (b, i, 0)
    per_b = lambda b, i: (b, 0, 0)
    return pl.pallas_call(
        functools.partial(_inproj_kernel, rope=rope),
        grid=(bx, t // tm),
        in_specs=[
            pl.BlockSpec((1, tm, d), tok),
            pl.BlockSpec((1, 1, d), per_b),
            pl.BlockSpec((1, 1, d), per_b),
            pl.BlockSpec((d, IN_WIDTH), lambda b, i: (0, 0)),
            pl.BlockSpec((tm, LANES), lambda b, i: (i, 0)),
            pl.BlockSpec((tm, LANES), lambda b, i: (i, 0)),
        ],
        out_specs=[pl.BlockSpec((1, tm, w), tok) for w in widths],
        out_shape=[jax.ShapeDtypeStruct((bx, t, w), dt) for w, dt in zip(widths, dtypes)],
        compiler_params=_params("parallel", "parallel"),
        name="in_projection",
    )(x, shift, scale, w_in_bf16, cos_t, sin_t)


def rope_tables(n_tokens):
    t = jnp.arange(n_tokens, dtype=jnp.int32)
    row = (t // GRID_W).astype(F32)[:, None]
    col = (t % GRID_W).astype(F32)[:, None]
    n_freq = HEAD_DIM // 4
    inv_freq = ROPE_BASE ** (-jnp.arange(n_freq, dtype=F32) / n_freq)
    ang_r = row * inv_freq
    ang_c = col * inv_freq
    cos_h = jnp.concatenate([jnp.cos(ang_r), jnp.cos(ang_r), jnp.cos(ang_c), jnp.cos(ang_c)], axis=1)
    sin_h = jnp.concatenate([-jnp.sin(ang_r), jnp.sin(ang_r), -jnp.sin(ang_c), jnp.sin(ang_c)], axis=1)
    reps = LANES // HEAD_DIM
    return jnp.tile(cos_h, (1, reps)), jnp.tile(sin_h, (1, reps))


def _with_ones(v):
    return jnp.concatenate([v, jnp.ones_like(v)], axis=1)


def _attend(score_parts, values, sink=None):
    m = score_parts[0].max(axis=-1, keepdims=True)
    for s in score_parts[1:]:
        m = jnp.maximum(m, s.max(axis=-1, keepdims=True))
    if sink is not None:
        m = jnp.maximum(m, sink)
    acc = None
    for s, v in zip(score_parts, values):
        term = _dot(jnp.exp((s - m).astype(BF16)), v)
        acc = term if acc is None else acc + term
    den = acc[:, HEAD_DIM:HEAD_DIM + 1]
    if sink is not None:
        den = den + jnp.exp(sink - m)
    return acc[:, :HEAD_DIM] / den


def _attn_a_kernel(sink_ref, q_ref, kp_ref, kc_ref, kn_ref, vp_ref, vc_ref, vn_ref,
                   kctx_ref, vctx_ref, out_ref, *, n_lat, tq):
    i = pl.program_id(1)
    k_win = jnp.concatenate([kp_ref[0], kc_ref[0], kn_ref[0]], axis=0)
    v_win = jnp.concatenate([vp_ref[0], vc_ref[0], vn_ref[0]], axis=0)
    kctx = kctx_ref[0]
    vctx = vctx_ref[0]
    sub = A_WINDOW
    span = 3 * A_WINDOW
    rows = A_GROUP * sub
    row_i = lax.broadcasted_iota(jnp.int32, (rows, span), 0)
    col_i = lax.broadcasted_iota(jnp.int32, (rows, span), 1)
    rel = col_i - A_WINDOW - (row_i & (sub - 1))
    in_band = (rel <= A_WINDOW) & (rel >= -A_WINDOW)
    group_of_row = lax.broadcasted_iota(jnp.int32, (rows, 1), 0) >> (sub.bit_length() - 1)
    sinks = []
    for hk in range(A_KV_HEADS):
        sink = jnp.zeros((rows, 1), F32)
        for g in range(A_GROUP):
            sink = jnp.where(group_of_row == g, sink_ref[hk * A_GROUP + g], sink)
        sinks.append(sink)
    vctx_ext = [_with_ones(vctx[:, hk * HEAD_DIM:(hk + 1) * HEAD_DIM]) for hk in range(A_KV_HEADS)]
    v_ext = [_with_ones(v_win[:, hk * HEAD_DIM:(hk + 1) * HEAD_DIM]) for hk in range(A_KV_HEADS)]
    for j in range(tq // sub):
        kpos = i * tq + j * sub - A_WINDOW + col_i
        valid = in_band & (kpos >= 0) & (kpos < n_lat)
        q_rows = q_ref[0, j * sub:(j + 1) * sub]
        k_sub = k_win[j * sub:j * sub + span]
        for hk in range(A_KV_HEADS):
            sl = slice(hk * HEAD_DIM, (hk + 1) * HEAD_DIM)
            heads = [hk * A_GROUP + g for g in range(A_GROUP)]
            q = jnp.concatenate([q_rows[:, h * HEAD_DIM:(h + 1) * HEAD_DIM] for h in heads], axis=0)
            s_win = jnp.where(valid, _dot_t(q, k_sub[:, sl]), NEG_INF)
            s_ctx = _dot_t(q, kctx[:, sl])
            v_sub = v_ext[hk][j * sub:j * sub + span]
            o = _attend([s_win, s_ctx], [v_sub, vctx_ext[hk]], sink=sinks[hk]).astype(BF16)
            for g, h in enumerate(heads):
                out_ref[0, j * sub:(j + 1) * sub, h * HEAD_DIM:(h + 1) * HEAD_DIM] = o[g * sub:(g + 1) * sub]


def window_attention(qa, ka, va, kc_a, vc_a, sink):
    bsz, n_lat, _ = qa.shape
    n_ctx = kc_a.shape[1]
    tq = min(512, n_lat)
    w = A_WINDOW
    per = tq // w
    last = n_lat // w - 1
    prev = lambda b, i, s: (b, jnp.maximum(i * per - 1, 0), 0)
    cur = lambda b, i, s: (b, i, 0)
    nxt = lambda b, i, s: (b, jnp.minimum((i + 1) * per, last), 0)
    ctx = lambda b, i, s: (b, 0, 0)
    kv_specs = [pl.BlockSpec((1, w, A_KV_W), prev), pl.BlockSpec((1, tq, A_KV_W), cur),
                pl.BlockSpec((1, w, A_KV_W), nxt)]
    grid_spec = pltpu.PrefetchScalarGridSpec(
        num_scalar_prefetch=1,
        grid=(bsz, n_lat // tq),
        in_specs=[pl.BlockSpec((1, tq, A_Q_W), cur)] + kv_specs + kv_specs + [
            pl.BlockSpec((1, n_ctx, A_KV_W), ctx), pl.BlockSpec((1, n_ctx, A_KV_W), ctx)],
        out_specs=pl.BlockSpec((1, tq, A_Q_W), cur),
    )
    return pl.pallas_call(
        functools.partial(_attn_a_kernel, n_lat=n_lat, tq=tq),
        grid_spec=grid_spec,
        out_shape=jax.ShapeDtypeStruct((bsz, n_lat, A_Q_W), BF16),
        compiler_params=_params("parallel", "parallel"),
        name="window_attention",
    )(sink, qa, ka, ka, ka, va, va, va, kc_a, vc_a)


NB_Q_ROWS = 4


def _attn_b_kernel(q_ref, kp_ref, kc_ref, kn_ref, vp_ref, vc_ref, vn_ref,
                   kctx_ref, vctx_ref, bias_ref, out_ref):
    k_win = jnp.concatenate([kp_ref[0], kc_ref[0], kn_ref[0]], axis=0)
    v_win = jnp.concatenate([vp_ref[0], vc_ref[0], vn_ref[0]], axis=0)
    q_all = q_ref[0]
    kctx = kctx_ref[0]
    vctx = vctx_ref[0]
    for h in range(B_HEADS):
        sl = slice(h * HEAD_DIM, (h + 1) * HEAD_DIM)
        q = q_all[:, sl]
        s_nb = _dot_t(q, k_win[:, sl]) + bias_ref[0, h]
        s_ctx = _dot_t(q, kctx[:, sl])
        o = _attend([s_nb, s_ctx], [_with_ones(v_win[:, sl]), _with_ones(vctx[:, sl])])
        out_ref[0, :, sl] = o.astype(BF16)


def neighbourhood_bias(rel_bias, n_lat):
    rows = n_lat // GRID_W
    kr_n = min(NA_ROWS, rows)
    n_blocks = rows // NB_Q_ROWS
    n_heads, n_dr, n_dc = rel_bias.shape
    cols = np.arange(GRID_W)
    c_start = np.clip(cols - NA_COLS // 2, 0, GRID_W - NA_COLS)
    col_ok = (cols[None, :] >= c_start[:, None]) & (cols[None, :] < c_start[:, None] + NA_COLS)
    dc = np.clip(cols[None, :] - cols[:, None], -(NA_COLS - 1), NA_COLS - 1) + NA_COLS - 1
    pick_dc = (dc.reshape(-1)[None, :] == np.arange(n_dc)[:, None]).astype(np.float32)
    toeplitz = jnp.dot(rel_bias.reshape(n_heads * n_dr, n_dc), pick_dc, precision=lax.Precision.HIGHEST)
    toeplitz = jnp.where(col_ok.reshape(-1), toeplitz, NEG_INF).reshape(n_heads, n_dr, GRID_W, GRID_W)
    q_rl = np.arange(NB_Q_ROWS)
    k_rl = np.arange(3 * NB_Q_ROWS)
    row_ok, dr = [], []
    for j in sorted({0, min(1, n_blocks - 1), n_blocks - 1}):
        r = NB_Q_ROWS * j + q_rl
        kr = NB_Q_ROWS * (j - 1) + k_rl
        r_start = np.clip(r - kr_n // 2, 0, rows - kr_n)
        ok = (kr[None, :] >= r_start[:, None]) & (kr[None, :] < r_start[:, None] + kr_n)
        row_ok.append(ok & (kr[None, :] >= 0) & (kr[None, :] < rows))
        dr.append(np.clip(kr[None, :] - r[:, None] + NA_ROWS - 1, 0, n_dr - 1))
    row_ok = np.stack(row_ok)
    dr = np.stack(dr)
    tiles = jnp.stack([toeplitz[:, int(i)] for i in dr.reshape(-1)], axis=1)
    tiles = tiles.reshape((n_heads,) + dr.shape + (GRID_W, GRID_W))
    tiles = jnp.where(row_ok[None, :, :, :, None, None], tiles, NEG_INF)
    table = tiles.transpose(1, 0, 2, 4, 3, 5).reshape(
        dr.shape[0], n_heads, NB_Q_ROWS * GRID_W, 3 * NB_Q_ROWS * GRID_W)
    return table, n_blocks


def neighbourhood_attention(qb, kb, vb, kc_b, vc_b, rel_bias):
    bsz, n_lat, _ = qb.shape
    n_ctx = kc_b.shape[1]
    table, n_blocks = neighbourhood_bias(rel_bias, n_lat)
    n_var = table.shape[0]
    tq = NB_Q_ROWS * GRID_W
    prev = lambda b, j: (b, jnp.maximum(j - 1, 0), 0)
    cur = lambda b, j: (b, j, 0)
    nxt = lambda b, j: (b, jnp.minimum(j + 1, n_blocks - 1), 0)
    ctx = lambda b, j: (b, 0, 0)

    def variant(b, j):
        v = jnp.where(j == 0, 0, jnp.where(j == n_blocks - 1, n_var - 1, min(1, n_var - 1)))
        return (v, 0, 0, 0)

    kv_specs = [pl.BlockSpec((1, tq, B_W), prev), pl.BlockSpec((1, tq, B_W), cur),
                pl.BlockSpec((1, tq, B_W), nxt)]
    return pl.pallas_call(
        _attn_b_kernel,
        grid=(bsz, n_blocks),
        in_specs=[pl.BlockSpec((1, tq, B_W), cur)] + kv_specs + kv_specs + [
            pl.BlockSpec((1, n_ctx, B_W), ctx), pl.BlockSpec((1, n_ctx, B_W), ctx),
            pl.BlockSpec((1, B_HEADS, tq, 3 * tq), variant)],
        out_specs=pl.BlockSpec((1, tq, B_W), cur),
        out_shape=jax.ShapeDtypeStruct((bsz, n_lat, B_W), BF16),
        compiler_params=_params("parallel", "parallel"),
        name="neighbourhood_attention",
    )(qb, kb, kb, kb, vb, vb, vb, kc_b, vc_b, table)


def _ctx_attn_kernel(sink_ref, qa_ref, qb_ref, ka_ref, va_ref, kb_ref, vb_ref, oa_ref, ob_ref):
    qa, qb = qa_ref[0], qb_ref[0]
    ka, va, kb, vb = ka_ref[0], va_ref[0], kb_ref[0], vb_ref[0]
    for hq in range(A_Q_HEADS):
        sl = slice(hq * HEAD_DIM, (hq + 1) * HEAD_DIM)
        hk = hq // A_GROUP
        kv = slice(hk * HEAD_DIM, (hk + 1) * HEAD_DIM)
        q = qa[:, sl]
        o = _attend([_dot_t(q, ka[:, kv])], [_with_ones(va[:, kv])], sink=sink_ref[hq])
        oa_ref[0, :, sl] = o.astype(BF16)
    for h in range(B_HEADS):
        sl = slice(h * HEAD_DIM, (h + 1) * HEAD_DIM)
        q = qb[:, sl]
        o = _attend([_dot_t(q, kb[:, sl])], [_with_ones(vb[:, sl])])
        ob_ref[0, :, sl] = o.astype(BF16)


def context_attention(qa, qb, ka, va, kb, vb, sink):
    bsz, n_ctx, _ = qa.shape
    blk = lambda w: pl.BlockSpec((1, n_ctx, w), lambda b, s: (b, 0, 0))
    grid_spec = pltpu.PrefetchScalarGridSpec(
        num_scalar_prefetch=1,
        grid=(bsz,),
        in_specs=[blk(A_Q_W), blk(B_W), blk(A_KV_W), blk(A_KV_W), blk(B_W), blk(B_W)],
        out_specs=[blk(A_Q_W), blk(B_W)],
    )
    return pl.pallas_call(
        _ctx_attn_kernel,
        grid_spec=grid_spec,
        out_shape=[jax.ShapeDtypeStruct((bsz, n_ctx, A_Q_W), BF16),
                   jax.ShapeDtypeStruct((bsz, n_ctx, B_W), BF16)],
        compiler_params=_params("parallel"),
        name="context_attention",
    )(sink, qa, qb, ka, va, kb, vb)


CONV_HALO = 16


def _conv_kernel(prev_ref, cur_ref, next_ref, w_ref, b_ref, g_ref, beta_ref, out_ref, ext_ref, *, ts):
    i = pl.program_id(1)
    n_i = pl.num_programs(1)
    ext_ref[0:CONV_HALO] = jnp.where(i > 0, prev_ref[0], 0.0)
    ext_ref[CONV_HALO:CONV_HALO + ts] = cur_ref[0]
    ext_ref[CONV_HALO + ts:] = jnp.where(i < n_i - 1, next_ref[0], 0.0)
    half = C_CONV_WIDTH // 2
    acc = jnp.zeros((ts, C_CHANNELS), F32)
    for k in range(C_CONV_WIDTH):
        acc = acc + ext_ref[pl.ds(CONV_HALO - half + k, ts)] * w_ref[k:k + 1]
    y = _layer_norm(acc + b_ref[...]) * g_ref[...] + beta_ref[...]
    out_ref[0] = (y * jax.nn.sigmoid(y)).astype(BF16)


def conformer_conv(hc, conv_w, conv_b, ln_g, ln_b):
    bx, t, ch = hc.shape
    ts = min(512, t)
    per = ts // CONV_HALO
    last = t // CONV_HALO - 1
    row = lambda v: v.reshape(1, ch)
    const = lambda b, i: (0, 0)
    return pl.pallas_call(
        functools.partial(_conv_kernel, ts=ts),
        grid=(bx, t // ts),
        in_specs=[
            pl.BlockSpec((1, CONV_HALO, ch), lambda b, i: (b, jnp.maximum(i * per - 1, 0), 0)),
            pl.BlockSpec((1, ts, ch), lambda b, i: (b, i, 0)),
            pl.BlockSpec((1, CONV_HALO, ch), lambda b, i: (b, jnp.minimum((i + 1) * per, last), 0)),
            pl.BlockSpec((C_CONV_WIDTH, ch), const),
            pl.BlockSpec((1, ch), const), pl.BlockSpec((1, ch), const), pl.BlockSpec((1, ch), const),
        ],
        out_specs=pl.BlockSpec((1, ts, ch), lambda b, i: (b, i, 0)),
        out_shape=jax.ShapeDtypeStruct((bx, t, ch), BF16),
        scratch_shapes=[pltpu.VMEM((ts + 2 * CONV_HALO, ch), F32)],
        compiler_params=_params("parallel", "parallel"),
        name="conformer_conv",
    )(hc, hc, hc, conv_w, row(conv_b), row(ln_g), row(ln_b))


def _outproj_kernel(oa_ref, ob_ref, oc_ref, x_ref, g1_ref, sh_ref, sc_ref, w_ref, lng_ref, lnb_ref,
                    wr_hi_ref, wr_lo_ref, x1_ref, h2_ref, afft_ref, aff_ref, *, alpha):
    o = (_dot(oa_ref[0], w_ref[0:A_Q_W])
         + _dot(ob_ref[0], w_ref[A_Q_W:A_Q_W + B_W])
         + _dot(oc_ref[0], w_ref[A_Q_W + B_W:]))
    y = _layer_norm(alpha * x_ref[0] + g1_ref[0] * o) * lng_ref[...] + lnb_ref[...]
    x1_ref[0] = y
    h2 = _layer_norm(y) * (1.0 + sc_ref[0]) + sh_ref[0]
    h_hi = h2.astype(BF16)
    rows_per_token = h2.shape[1] // LANES
    for j in range(rows_per_token):
        h2_ref[0, pl.ds(j, h2.shape[0], stride=rows_per_token), :] = h2[:, j * LANES:(j + 1) * LANES]
    h_lo = (h2 - h_hi.astype(F32)).astype(BF16)
    w_hi, w_lo = wr_hi_ref[...], wr_lo_ref[...]
    logits_t = _dot_t(w_hi, h_hi) + _dot_t(w_hi, h_lo) + _dot_t(w_lo, h_hi)
    e_t = jnp.exp(logits_t - logits_t.max(axis=0, keepdims=True))
    afft_ref[0] = e_t / e_t.sum(axis=0, keepdims=True)
    logits = _dot_t(h_hi, w_hi) + _dot_t(h_lo, w_hi) + _dot_t(h_hi, w_lo)
    e_n = jnp.exp(logits - logits.max(axis=1, keepdims=True))
    aff_ref[0] = e_n / e_n.sum(axis=1, keepdims=True)


def out_projection(oa, ob, oc, x, g1, sh2, sc2, w_out_bf16, ln_g, ln_b, wr_hi, wr_lo, alpha):
    bx, t, d = x.shape
    tm = min(512, t)
    tok = lambda b, i: (b, i, 0)
    per_b = lambda b, i: (b, 0, 0)
    const = lambda b, i: (0, 0)
    vec = pl.BlockSpec((1, d), const)
    return pl.pallas_call(
        functools.partial(_outproj_kernel, alpha=alpha),
        grid=(bx, t // tm),
        in_specs=[
            pl.BlockSpec((1, tm, A_Q_W), tok), pl.BlockSpec((1, tm, B_W), tok),
            pl.BlockSpec((1, tm, C_CHANNELS), tok), pl.BlockSpec((1, tm, d), tok),
            pl.BlockSpec((1, 1, d), per_b), pl.BlockSpec((1, 1, d), per_b), pl.BlockSpec((1, 1, d), per_b),
            pl.BlockSpec(w_out_bf16.shape, const), vec, vec,
            pl.BlockSpec((N_EXPERTS, d), const), pl.BlockSpec((N_EXPERTS, d), const),
        ],
        out_specs=[pl.BlockSpec((1, tm, d), tok), pl.BlockSpec((1, tm * (d // LANES), LANES), tok),
                   pl.BlockSpec((1, N_EXPERTS, tm), lambda b, i: (b, 0, i)),
                   pl.BlockSpec((1, tm, N_EXPERTS), tok)],
        out_shape=[jax.ShapeDtypeStruct((bx, t, d), F32),
                   jax.ShapeDtypeStruct((bx, t * (d // LANES), LANES), F32),
                   jax.ShapeDtypeStruct((bx, N_EXPERTS, t), F32),
                   jax.ShapeDtypeStruct((bx, t, N_EXPERTS), F32)],
        compiler_params=_params("parallel", "parallel"),
        name="out_projection",
    )(oa, ob, oc, x, g1, sh2, sc2, w_out_bf16, ln_g.reshape(1, d), ln_b.reshape(1, d), wr_hi, wr_lo)


def _select_kernel(afft_ref, pos_ref, post_ref, off_ref, *, cap, n_tok):
    aff = afft_ref[0]

    def bit_step(j, bits):
        cand = bits | (jnp.int32(1) << (30 - j))
        cnt = jnp.sum((aff >= pltpu.bitcast(cand, F32)).astype(jnp.int32), axis=1, keepdims=True)
        return jnp.where(cnt >= cap, cand, bits)

    thr = pltpu.bitcast(lax.fori_loop(0, 31, bit_step, jnp.zeros((N_EXPERTS, 1), jnp.int32)), F32)
    above = (aff > thr).astype(F32)
    tied = (aff == thr).astype(F32)
    need = cap - jnp.sum(above, axis=1, keepdims=True)

    blk = LANES
    n_blk = n_tok // blk
    r_i = lax.broadcasted_iota(jnp.int32, (blk, blk), 0)
    c_i = lax.broadcasted_iota(jnp.int32, (blk, blk), 1)
    strict_upper = (r_i < c_i).astype(BF16)
    eye = (r_i == c_i).astype(BF16)

    run_tied = jnp.zeros((N_EXPERTS, 1), F32)
    run_sel = jnp.zeros((N_EXPERTS, 1), F32)
    offs = []
    for kb in range(n_blk):
        sl = slice(kb * blk, (kb + 1) * blk)
        offs.append(run_sel)
        t_f = tied[:, sl]
        rank_tied = run_tied + _dot(t_f.astype(BF16), strict_upper)
        run_tied = run_tied + t_f.sum(axis=1, keepdims=True)
        s_f = above[:, sl] + t_f * (rank_tied < need).astype(F32)
        rank = run_sel + _dot(s_f.astype(BF16), strict_upper)
        run_sel = run_sel + s_f.sum(axis=1, keepdims=True)
        pos = jnp.where(s_f > 0.5, rank.astype(jnp.int32), -1)
        pos_ref[0, :, sl] = pos
        p1 = pos + 1
        hi = (p1 >> 5).astype(F32).astype(BF16)
        lo = (p1 & 31).astype(F32).astype(BF16)
        p1_t = _dot_t(eye, hi) * 32.0 + _dot_t(eye, lo)
        post_ref[0, sl, :] = p1_t - 1.0
    off_ref[0] = jnp.concatenate(offs, axis=1).astype(jnp.int32)


def expert_choice_select(aff_t, cap):
    bx, n_e, t = aff_t.shape
    n_tb = t // OFFSET_BLOCK
    return pl.pallas_call(
        functools.partial(_select_kernel, cap=cap, n_tok=t),
        grid=(bx,),
        in_specs=[pl.BlockSpec((1, n_e, t), lambda b: (b, 0, 0))],
        out_specs=[pl.BlockSpec((1, n_e, t), lambda b: (b, 0, 0)),
                   pl.BlockSpec((1, t, n_e), lambda b: (b, 0, 0)),
                   pl.BlockSpec((1, n_e, n_tb), lambda b: (b, 0, 0))],
        out_shape=[jax.ShapeDtypeStruct((bx, n_e, t), jnp.int32),
                   jax.ShapeDtypeStruct((bx, t, n_e), F32),
                   jax.ShapeDtypeStruct((bx, n_e, n_tb), jnp.int32)],
        compiler_params=_params("parallel"),
        name="expert_choice_select",
    )(aff_t)


GATE_PARTS = 3
TOKEN_LANE = GATE_PARTS


def _slot_table_kernel(off_ref, pos_ref, aff_ref, tbl_ref, *, n_off, blocks_per_step):
    b, eg, kc = pl.program_id(0), pl.program_id(1), pl.program_id(2)
    epg = EXPERTS_PER_GATHER_STEP

    @pl.when(kc == 0)
    def _():
        tbl_ref[...] = jnp.zeros_like(tbl_ref)

    slot = lax.broadcasted_iota(jnp.int32, (GATHER_WINDOW, 1), 0)
    slot_2d = lax.broadcasted_iota(jnp.int32, (GATHER_WINDOW, GATHER_BLOCK), 0)
    lane = lax.broadcasted_iota(jnp.int32, (GATHER_BLOCK, LANES), 1)
    out_lane = lax.broadcasted_iota(jnp.int32, (GATHER_WINDOW, LANES), 1)
    local_token = lax.broadcasted_iota(jnp.int32, (GATHER_BLOCK, LANES), 0).astype(F32)
    expert_lane = lax.broadcasted_iota(jnp.int32, (GATHER_BLOCK, N_EXPERTS), 1)
    for kk in range(blocks_per_step):
        kb = kc * blocks_per_step + kk
        aff_blk = aff_ref[0, kk * GATHER_BLOCK:(kk + 1) * GATHER_BLOCK]
        first_token = (kb * GATHER_BLOCK).astype(F32)
        for ee in range(epg):
            e = eg * epg + ee
            off = off_ref[(b * N_EXPERTS + e) * n_off + kb * (GATHER_BLOCK // OFFSET_BLOCK)]
            base = pl.multiple_of((off >> 4) << 4, BF16_SUBLANES)
            onehot = (pos_ref[0, ee, 0, kk:kk + 1, :] - base == slot_2d).astype(BF16)
            gate = jnp.sum(jnp.where(expert_lane == e, aff_blk, 0.0), axis=1, keepdims=True)
            payload, rest = jnp.where(lane == TOKEN_LANE, local_token, 0.0), gate
            for k in range(GATE_PARTS):
                part = rest.astype(BF16).astype(F32)
                payload = jnp.where(lane == k, part, payload)
                rest = rest - part
            gathered = _dot(onehot, payload.astype(BF16)) + jnp.where(out_lane == TOKEN_LANE, first_token, 0.0)
            win = pl.ds(base, GATHER_WINDOW)
            tbl_ref[0, ee, win, :] = jnp.where(slot >= off - base, gathered, tbl_ref[0, ee, win, :])


def slot_table(aff, pos, block_off, cap_pad):
    bx, t, _ = aff.shape
    n_tb = t // GATHER_BLOCK
    blocks_per_step = min(4, n_tb)
    n_steps = n_tb // blocks_per_step
    epg = EXPERTS_PER_GATHER_STEP
    tokens = blocks_per_step * GATHER_BLOCK
    pos5 = pos.reshape(bx, N_EXPERTS, n_steps, blocks_per_step, GATHER_BLOCK)
    grid_spec = pltpu.PrefetchScalarGridSpec(
        num_scalar_prefetch=1,
        grid=(bx, N_EXPERTS // epg, n_steps),
        in_specs=[
            pl.BlockSpec((1, epg, 1, blocks_per_step, GATHER_BLOCK), lambda b, g, k, s: (b, g, k, 0, 0)),
            pl.BlockSpec((1, tokens, N_EXPERTS), lambda b, g, k, s: (b, k, 0)),
        ],
        out_specs=pl.BlockSpec((1, epg, cap_pad, LANES), lambda b, g, k, s: (b, g, 0, 0)),
    )
    return pl.pallas_call(
        functools.partial(_slot_table_kernel, n_off=t // OFFSET_BLOCK, blocks_per_step=blocks_per_step),
        grid_spec=grid_spec,
        out_shape=jax.ShapeDtypeStruct((bx, N_EXPERTS, cap_pad, LANES), F32),
        compiler_params=_params("parallel", "parallel", "arbitrary"),
        name="slot_table",
    )(block_off.reshape(-1), pos5, aff)


GATHER_UNROLL = 8


def _row_gather_kernel(idx_ref, src_ref, xs_ref, tile_ref, *, cap, rows_per_token, chunk_stride):
    b, e = pl.program_id(0), pl.program_id(1)
    first = (b * N_EXPERTS + e) * cap

    def group(g, carry):
        for u in range(GATHER_UNROLL):
            s = g * GATHER_UNROLL + u
            row = pl.multiple_of(idx_ref[first + s] * rows_per_token, rows_per_token)
            tile_ref[pl.ds(s, rows_per_token, stride=chunk_stride), :] = src_ref[0, pl.ds(row, rows_per_token), :]
        return carry

    lax.fori_loop(0, cap // GATHER_UNROLL, group, 0)
    for j in range(rows_per_token):
        xs_ref[0, 0, :, j * LANES:(j + 1) * LANES] = tile_ref[j * chunk_stride:j * chunk_stride + cap, :].astype(BF16)


def gather_rows(h2_rows, token_idx, cap, d):
    bx, n_rows, _ = h2_rows.shape
    rows_per_token = d // LANES
    chunk_stride = cap + 8
    grid_spec = pltpu.PrefetchScalarGridSpec(
        num_scalar_prefetch=1,
        grid=(bx, N_EXPERTS),
        in_specs=[pl.BlockSpec((1, n_rows, LANES), lambda b, e, s: (b, 0, 0), pipeline_mode=pl.Buffered(1))],
        out_specs=pl.BlockSpec((1, 1, cap, d), lambda b, e, s: (b, e, 0, 0)),
        scratch_shapes=[pltpu.VMEM((rows_per_token * chunk_stride, LANES), F32)],
    )
    return pl.pallas_call(
        functools.partial(_row_gather_kernel, cap=cap, rows_per_token=rows_per_token, chunk_stride=chunk_stride),
        grid_spec=grid_spec,
        out_shape=jax.ShapeDtypeStruct((bx, N_EXPERTS, cap, d), BF16),
        compiler_params=_params("parallel", "arbitrary"),
        name="gather_rows",
    )(token_idx, h2_rows)


def _ffn_kernel(xs_ref, gs_ref, wg_ref, wu_ref, wd_ref, ye_ref, wg_bf, wu_bf, wd_bf, *, cap, row_tile):
    @pl.when(pl.program_id(1) == 0)
    def _():
        wg_bf[...] = wg_ref[0, 0].astype(BF16)
        wu_bf[...] = wu_ref[0, 0].astype(BF16)
        wd_bf[...] = wd_ref[0, 0].astype(BF16)

    n_b = xs_ref.shape[0]
    if n_b == 1:
        tiles = [[(0, r0, row_tile)] for r0 in range(0, cap, row_tile)]
    else:
        tiles = [[(bb, 0, cap) for bb in range(n_b)]]
    for tile in tiles:
        x = jnp.concatenate([xs_ref[bb, 0, r0:r0 + n] for bb, r0, n in tile], axis=0)
        terms = jnp.concatenate([gs_ref[bb, 0, r0:r0 + n, 0:GATE_PARTS] for bb, r0, n in tile], axis=0)
        gate = _dot(x, wg_bf[...])
        up = _dot(x, wu_bf[...])
        hid = (gate * jax.nn.sigmoid(gate) * up).astype(BF16)
        g = jnp.sum(terms, axis=1, keepdims=True)
        ye = (_dot(hid, wd_bf[...]) * g).astype(BF16)
        row = 0
        for bb, r0, n in tile:
            ye_ref[bb, 0, r0:r0 + n] = ye[row:row + n]
            row += n
    for bb in range(n_b):
        ye_ref[bb, 0, cap:] = jnp.zeros((ye_ref.shape[2] - cap, ye_ref.shape[3]), BF16)


def expert_ffn(xs, gs, wg, wu, wd, layer, cap_pad):
    bx, n_e, cap, d = xs.shape
    ff = wg.shape[-1]
    max_rows = 512
    row_tile = min(max_rows, cap)
    n_b = bx if bx * cap <= max_rows else 1
    return pl.pallas_call(
        functools.partial(_ffn_kernel, cap=cap, row_tile=row_tile),
        grid=(n_e, bx // n_b),
        in_specs=[
            pl.BlockSpec((n_b, 1, cap, d), lambda e, b: (b, e, 0, 0)),
            pl.BlockSpec((n_b, 1, cap, LANES), lambda e, b: (b, e, 0, 0)),
            pl.BlockSpec((1, 1, d, ff), lambda e, b: (layer, e, 0, 0)),
            pl.BlockSpec((1, 1, d, ff), lambda e, b: (layer, e, 0, 0)),
            pl.BlockSpec((1, 1, ff, d), lambda e, b: (layer, e, 0, 0)),
        ],
        out_specs=pl.BlockSpec((n_b, 1, cap_pad, d), lambda e, b: (b, e, 0, 0)),
        out_shape=jax.ShapeDtypeStruct((bx, n_e, cap_pad, d), BF16),
        scratch_shapes=[pltpu.VMEM((d, ff), BF16), pltpu.VMEM((d, ff), BF16), pltpu.VMEM((ff, d), BF16)],
        compiler_params=_params("parallel", "arbitrary"),
        name="expert_ffn",
    )(xs, gs, wg, wu, wd)


EXPERTS_PER_SCATTER_STEP = 2


def _combine_kernel(off_ref, post_ref, ye_ref, x1_ref, g2_ref, lng_ref, lnb_ref,
                    out_ref, acc_ref, *, n_tb, blocks_per_step, alpha):
    b, tq, eg = pl.program_id(0), pl.program_id(1), pl.program_id(2)
    eps = EXPERTS_PER_SCATTER_STEP

    @pl.when(eg == 0)
    def _():
        acc_ref[...] = jnp.zeros_like(acc_ref)

    expert_lane = lax.broadcasted_iota(jnp.int32, (SCATTER_BLOCK, N_EXPERTS), 1)
    slot = lax.broadcasted_iota(jnp.int32, (SCATTER_BLOCK, SCATTER_WINDOW), 1).astype(F32)
    for kk in range(blocks_per_step):
        kb = tq * blocks_per_step + kk
        rows = slice(kk * SCATTER_BLOCK, (kk + 1) * SCATTER_BLOCK)
        post_blk = post_ref[0, rows, :]
        onehots, windows = [], []
        for ee in range(eps):
            e = eg * eps + ee
            off = off_ref[(b * N_EXPERTS + e) * n_tb + kb * (SCATTER_BLOCK // OFFSET_BLOCK)]
            base = pl.multiple_of((off >> 4) << 4, BF16_SUBLANES)
            pcol = jnp.sum(jnp.where(expert_lane == e, post_blk, 0.0), axis=1, keepdims=True)
            onehots.append((pcol - base.astype(F32) == slot).astype(BF16))
            windows.append(ye_ref[0, ee, pl.ds(base, SCATTER_WINDOW), :])
        acc_ref[rows, :] += _dot(jnp.concatenate(onehots, axis=1), jnp.concatenate(windows, axis=0))

    @pl.when(eg == N_EXPERTS // eps - 1)
    def _():
        y = alpha * x1_ref[0] + g2_ref[0] * acc_ref[...]
        out_ref[0] = _layer_norm(y) * lng_ref[...] + lnb_ref[...]


def combine_and_norm(ye, pos_t, block_off, x1, g2, ln_g, ln_b, alpha):
    bx, t, d = x1.shape
    cap_pad = ye.shape[2]
    n_tb = t // OFFSET_BLOCK
    blocks_per_step = min(16, t // SCATTER_BLOCK)
    tq = blocks_per_step * SCATTER_BLOCK
    eps = EXPERTS_PER_SCATTER_STEP
    tok = lambda b, q, e, s: (b, q, 0)
    const = lambda b, q, e, s: (0, 0)
    grid_spec = pltpu.PrefetchScalarGridSpec(
        num_scalar_prefetch=1,
        grid=(bx, t // tq, N_EXPERTS // eps),
        in_specs=[
            pl.BlockSpec((1, tq, N_EXPERTS), tok),
            pl.BlockSpec((1, eps, cap_pad, d), lambda b, q, e, s: (b, e, 0, 0)),
            pl.BlockSpec((1, tq, d), tok, pipeline_mode=pl.Buffered(1)),
            pl.BlockSpec((1, 1, d), lambda b, q, e, s: (b, 0, 0)),
            pl.BlockSpec((1, d), const), pl.BlockSpec((1, d), const),
        ],
        out_specs=pl.BlockSpec((1, tq, d), tok),
        scratch_shapes=[pltpu.VMEM((tq, d), F32)],
    )
    return pl.pallas_call(
        functools.partial(_combine_kernel, n_tb=n_tb, blocks_per_step=blocks_per_step, alpha=alpha),
        grid_spec=grid_spec,
        out_shape=jax.ShapeDtypeStruct((bx, t, d), F32),
        compiler_params=_params("parallel", "parallel", "arbitrary"),
        name="combine_and_norm",
    )(block_off.reshape(-1), pos_t, ye, x1, g2, ln_g.reshape(1, d), ln_b.reshape(1, d))


def _split_bf16(w):
    hi = w.astype(BF16)
    return hi, (w - hi.astype(F32)).astype(BF16)


def _mixer_tail(oa, ob, oc, x, mod, lw, alpha):
    g1, sh2, sc2, g2 = mod
    t = x.shape[1]
    cap = EC_CAPACITY * t // N_EXPERTS
    cap_pad = cap + SLOT_PAD
    x1, h2_rows, aff_t, aff = out_projection(oa, ob, oc, x, g1, sh2, sc2, lw["w_out"], lw["ln1_g"], lw["ln1_b"],
                                        lw["wr_hi"], lw["wr_lo"], alpha)
    pos, pos_t, block_off = expert_choice_select(aff_t, cap)
    table = slot_table(aff, pos, block_off, cap_pad)
    token_idx = table[:, :, :cap, TOKEN_LANE].astype(jnp.int32).reshape(-1)
    xs = gather_rows(h2_rows, token_idx, cap, x.shape[2])
    ye = expert_ffn(xs, table, lw["w_gate"], lw["w_up"], lw["w_down"], lw["layer"], cap_pad)
    return combine_and_norm(ye, pos_t, block_off, x1, g2, lw["ln2_g"], lw["ln2_b"], alpha)


def kernel(x, c, ctx, c_ctx, w_mod, b_mod, w_in, a_sink, nat_bias, conv_w, conv_b, conv_ln_g, conv_ln_b,
           w_out, ln1_g, ln1_b, w_router, w_gate, w_up, w_down, ln2_g, ln2_b):
    bsz, n_lat, d = x.shape
    depth = w_mod.shape[0]
    alpha = (2 * depth) ** 0.25
    cos_t, sin_t = rope_tables(n_lat)

    cond = jnp.concatenate([c, c_ctx[None, :], jnp.zeros((8 - bsz - 1, d), F32)], axis=0)
    mods = adaln_all(cond, w_mod, b_mod)

    for l in range(depth):
        last = l == depth - 1
        wr_hi, wr_lo = _split_bf16(w_router[l].T)
        lw = dict(w_out=w_out[l].astype(BF16), ln1_g=ln1_g[l], ln1_b=ln1_b[l], wr_hi=wr_hi, wr_lo=wr_lo,
                  w_gate=w_gate, w_up=w_up, w_down=w_down, layer=l,
                  ln2_g=ln2_g[l], ln2_b=ln2_b[l])
        w_in_l = w_in[l].astype(BF16)
        lat = [mods[l, :bsz, k * d:(k + 1) * d][:, None, :] for k in range(N_MOD)]
        cm = [jnp.broadcast_to(mods[l, bsz, k * d:(k + 1) * d][None, None, :], (bsz, 1, d))
              for k in range(N_MOD)]
        conv_args = (conv_w[l], conv_b[l], conv_ln_g[l], conv_ln_b[l])

        qa_c, ka_c, va_c, qb_c, kb_c, vb_c, hc_c = in_projection(ctx, cm[0], cm[1], w_in_l, cos_t, sin_t, rope=False)
        if not last:
            oa_c, ob_c = context_attention(qa_c, qb_c, ka_c, va_c, kb_c, vb_c, a_sink[l])
            oc_c = conformer_conv(hc_c, *conv_args)
            ctx_new = _mixer_tail(oa_c, ob_c, oc_c, ctx, (cm[2], cm[3], cm[4], cm[5]), lw, alpha)

        qa, ka, va, qb, kb, vb, hc = in_projection(x, lat[0], lat[1], w_in_l, cos_t, sin_t, rope=True)
        oa = window_attention(qa, ka, va, ka_c, va_c, a_sink[l])
        ob = neighbourhood_attention(qb, kb, vb, kb_c, vb_c, nat_bias[l])
        oc = conformer_conv(hc, *conv_args)
        x = _mixer_tail(oa, ob, oc, x, (lat[2], lat[3], lat[4], lat[5]), lw, alpha)
        if not last:
            ctx = ctx_new
    return x
```

```python
import functools

import numpy as np
import jax
import jax.numpy as jnp
from jax import lax
from jax.experimental import pallas as pl
from jax.experimental.pallas import tpu as pltpu

HEAD_DIM = 64
GRID_W = 64
A_Q_HEADS = 8
A_KV_HEADS = 2
A_GROUP = A_Q_HEADS // A_KV_HEADS
A_WINDOW = 128
B_HEADS = 4
NA_ROWS = 8
NA_COLS = 16
C_CHANNELS = 256
C_CONV_WIDTH = 31
A_Q_W = A_Q_HEADS * HEAD_DIM
A_KV_W = A_KV_HEADS * HEAD_DIM
B_W = B_HEADS * HEAD_DIM
OFF_AK = A_Q_W
OFF_AV = OFF_AK + A_KV_W
OFF_BQ = OFF_AV + A_KV_W
OFF_BK = OFF_BQ + B_W
OFF_BV = OFF_BK + B_W
OFF_C = OFF_BV + B_W
IN_WIDTH = OFF_C + 2 * C_CHANNELS
ROPE_WIDTH = A_Q_W + A_KV_W
N_EXPERTS = 16
EC_CAPACITY = 2
ROPE_BASE = 10000.0
LN_EPS = 1e-6
N_MOD = 6
NEG_INF = -1e30
QK_SCALE = HEAD_DIM ** -0.5

LANES = 128
BF16_SUBLANES = 16
MXU_DEPTH = 256
OFFSET_BLOCK = 128
GATHER_BLOCK = MXU_DEPTH
GATHER_WINDOW = GATHER_BLOCK + BF16_SUBLANES
SCATTER_BLOCK = OFFSET_BLOCK
SCATTER_WINDOW = MXU_DEPTH
SLOT_PAD = 3 * LANES
EXPERTS_PER_GATHER_STEP = 4
VMEM_LIMIT = 56 * 1024 * 1024

F32 = jnp.float32
BF16 = jnp.bfloat16


def _dot(a, b):
    return jnp.dot(a, b, preferred_element_type=F32)


def _dot_t(a, b):
    return lax.dot_general(a, b, (((1,), (1,)), ((), ())), preferred_element_type=F32)


def _layer_norm(x):
    mu = jnp.mean(x, axis=-1, keepdims=True)
    xc = x - mu
    var = jnp.mean(xc * xc, axis=-1, keepdims=True)
    return xc * lax.rsqrt(var + LN_EPS)


def _params(*sem):
    return pltpu.CompilerParams(dimension_semantics=sem, vmem_limit_bytes=VMEM_LIMIT)


def _mod_kernel(cond_ref, w_ref, b_ref, out_ref):
    cnd = cond_ref[...]
    act = cnd * jax.nn.sigmoid(cnd)
    out_ref[0] = jnp.dot(act, w_ref[0], preferred_element_type=F32,
                         precision=lax.Precision.HIGHEST) + b_ref[0]


def adaln_all(cond, w_mod, b_mod):
    n_layers, d, width = w_mod.shape
    rows = cond.shape[0]
    tn = 1536
    return pl.pallas_call(
        _mod_kernel,
        grid=(n_layers, width // tn),
        in_specs=[
            pl.BlockSpec((rows, d), lambda l, j: (0, 0)),
            pl.BlockSpec((1, d, tn), lambda l, j: (l, 0, j)),
            pl.BlockSpec((1, 1, tn), lambda l, j: (l, 0, j)),
        ],
        out_specs=pl.BlockSpec((1, rows, tn), lambda l, j: (l, 0, j)),
        out_shape=jax.ShapeDtypeStruct((n_layers, rows, width), F32),
        compiler_params=_params("parallel", "parallel"),
        name="adaln",
    )(cond, w_mod, b_mod.reshape(n_layers, 1, width))


def _inproj_kernel(x_ref, sh_ref, sc_ref, w_ref, cos_ref, sin_ref,
                   qa_ref, ka_ref, va_ref, qb_ref, kb_ref, vb_ref, hc_ref, *, rope):
    x = x_ref[0]
    h = _layer_norm(x) * (1.0 + sc_ref[0]) + sh_ref[0]
    u = _dot(h.astype(BF16), w_ref[...])

    def rotated(col):
        xq = u[:, col:col + LANES]
        if not rope:
            return xq
        lane = lax.broadcasted_iota(jnp.int32, xq.shape, 1)
        first = (lane & (HEAD_DIM // 2 - 1)) < (HEAD_DIM // 4)
        partner = jnp.where(first, pltpu.roll(xq, LANES - HEAD_DIM // 4, 1),
                            pltpu.roll(xq, HEAD_DIM // 4, 1))
        return xq * cos_ref[...] + partner * sin_ref[...]

    rot = [rotated(col) for col in range(0, ROPE_WIDTH, LANES)]
    n_q = A_Q_W // LANES
    qa_ref[0] = (jnp.concatenate(rot[:n_q], axis=1) * QK_SCALE).astype(BF16)
    ka_ref[0] = jnp.concatenate(rot[n_q:], axis=1).astype(BF16)
    va_ref[0] = u[:, OFF_AV:OFF_BQ].astype(BF16)
    qb_ref[0] = (u[:, OFF_BQ:OFF_BK] * QK_SCALE).astype(BF16)
    kb_ref[0] = u[:, OFF_BK:OFF_BV].astype(BF16)
    vb_ref[0] = u[:, OFF_BV:OFF_C].astype(BF16)
    a = u[:, OFF_C:OFF_C + C_CHANNELS]
    gate = u[:, OFF_C + C_CHANNELS:]
    hc_ref[0] = a * jax.nn.sigmoid(gate)


def in_projection(x, shift, scale, w_in_bf16, cos_t, sin_t, *, rope):
    bx, t, d = x.shape
    tm = min(512, t)
    widths = (A_Q_W, A_KV_W, A_KV_W, B_W, B_W, B_W, C_CHANNELS)
    dtypes = (BF16,) * 6 + (F32,)
    tok = lambda b, i: (b, i, 0)
    per_b = lambda b, i: (b, 0, 0)
    return pl.pallas_call(
        functools.partial(_inproj_kernel, rope=rope),
        grid=(bx, t // tm),
        in_specs=[
            pl.BlockSpec((1, tm, d), tok),
            pl.BlockSpec((1, 1, d), per_b),
            pl.BlockSpec((1, 1, d), per_b),
            pl.BlockSpec((d, IN_WIDTH), lambda b, i: (0, 0)),
            pl.BlockSpec((tm, LANES), lambda b, i: (i, 0)),
            pl.BlockSpec((tm, LANES), lambda b, i: (i, 0)),
        ],
        out_specs=[pl.BlockSpec((1, tm, w), tok) for w in widths],
        out_shape=[jax.ShapeDtypeStruct((bx, t, w), dt) for w, dt in zip(widths, dtypes)],
        compiler_params=_params("parallel", "parallel"),
        name="in_projection",
    )(x, shift, scale, w_in_bf16, cos_t, sin_t)


def rope_tables(n_tokens):
    t = jnp.arange(n_tokens, dtype=jnp.int32)
    row = (t // GRID_W).astype(F32)[:, None]
    col = (t % GRID_W).astype(F32)[:, None]
    n_freq = HEAD_DIM // 4
    inv_freq = ROPE_BASE ** (-jnp.arange(n_freq, dtype=F32) / n_freq)
    ang_r = row * inv_freq
    ang_c = col * inv_freq
    cos_h = jnp.concatenate([jnp.cos(ang_r), jnp.cos(ang_r), jnp.cos(ang_c), jnp.cos(ang_c)], axis=1)
    sin_h = jnp.concatenate([-jnp.sin(ang_r), jnp.sin(ang_r), -jnp.sin(ang_c), jnp.sin(ang_c)], axis=1)
    reps = LANES // HEAD_DIM
    return jnp.tile(cos_h, (1, reps)), jnp.tile(sin_h, (1, reps))


def _with_ones(v):
    return jnp.concatenate([v, jnp.ones_like(v)], axis=1)


def _attend(score_parts, values, sink=None):
    m = score_parts[0].max(axis=-1, keepdims=True)
    for s in score_parts[1:]:
        m = jnp.maximum(m, s.max(axis=-1, keepdims=True))
    if sink is not None:
        m = jnp.maximum(m, sink)
    acc = None
    for s, v in zip(score_parts, values):
        term = _dot(jnp.exp((s - m).astype(BF16)), v)
        acc = term if acc is None else acc + term
    den = acc[:, HEAD_DIM:HEAD_DIM + 1]
    if sink is not None:
        den = den + jnp.exp(sink - m)
    return acc[:, :HEAD_DIM] / den


def _attn_a_kernel(sink_ref, q_ref, kp_ref, kc_ref, kn_ref, vp_ref, vc_ref, vn_ref,
                   kctx_ref, vctx_ref, out_ref, *, n_lat, tq):
    i = pl.program_id(1)
    k_win = jnp.concatenate([kp_ref[0], kc_ref[0], kn_ref[0]], axis=0)
    v_win = jnp.concatenate([vp_ref[0], vc_ref[0], vn_ref[0]], axis=0)
    kctx = kctx_ref[0]
    vctx = vctx_ref[0]
    sub = A_WINDOW
    span = 3 * A_WINDOW
    rows = A_GROUP * sub
    row_i = lax.broadcasted_iota(jnp.int32, (rows, span), 0)
    col_i = lax.broadcasted_iota(jnp.int32, (rows, span), 1)
    rel = col_i - A_WINDOW - (row_i & (sub - 1))
    in_band = (rel <= A_WINDOW) & (rel >= -A_WINDOW)
    group_of_row = lax.broadcasted_iota(jnp.int32, (rows, 1), 0) >> (sub.bit_length() - 1)
    sinks = []
    for hk in range(A_KV_HEADS):
        sink = jnp.zeros((rows, 1), F32)
        for g in range(A_GROUP):
            sink = jnp.where(group_of_row == g, sink_ref[hk * A_GROUP + g], sink)
        sinks.append(sink)
    vctx_ext = [_with_ones(vctx[:, hk * HEAD_DIM:(hk + 1) * HEAD_DIM]) for hk in range(A_KV_HEADS)]
    v_ext = [_with_ones(v_win[:, hk * HEAD_DIM:(hk + 1) * HEAD_DIM]) for hk in range(A_KV_HEADS)]
    for j in range(tq // sub):
        kpos = i * tq + j * sub - A_WINDOW + col_i
        valid = in_band & (kpos >= 0) & (kpos < n_lat)
        q_rows = q_ref[0, j * sub:(j + 1) * sub]
        k_sub = k_win[j * sub:j * sub + span]
        for hk in range(A_KV_HEADS):
            sl = slice(hk * HEAD_DIM, (hk + 1) * HEAD_DIM)
            heads = [hk * A_GROUP + g for g in range(A_GROUP)]
            q = jnp.concatenate([q_rows[:, h * HEAD_DIM:(h + 1) * HEAD_DIM] for h in heads], axis=0)
            s_win = jnp.where(valid, _dot_t(q, k_sub[:, sl]), NEG_INF)
            s_ctx = _dot_t(q, kctx[:, sl])
            v_sub = v_ext[hk][j * sub:j * sub + span]
            o = _attend([s_win, s_ctx], [v_sub, vctx_ext[hk]], sink=sinks[hk]).astype(BF16)
            for g, h in enumerate(heads):
                out_ref[0, j * sub:(j + 1) * sub, h * HEAD_DIM:(h + 1) * HEAD_DIM] = o[g * sub:(g + 1) * sub]


def window_attention(qa, ka, va, kc_a, vc_a, sink):
    bsz, n_lat, _ = qa.shape
    n_ctx = kc_a.shape[1]
    tq = min(512, n_lat)
    w = A_WINDOW
    per = tq // w
    last = n_lat // w - 1
    prev = lambda b, i, s: (b, jnp.maximum(i * per - 1, 0), 0)
    cur = lambda b, i, s: (b, i, 0)
    nxt = lambda b, i, s: (b, jnp.minimum((i + 1) * per, last), 0)
    ctx = lambda b, i, s: (b, 0, 0)
    kv_specs = [pl.BlockSpec((1, w, A_KV_W), prev), pl.BlockSpec((1, tq, A_KV_W), cur),
                pl.BlockSpec((1, w, A_KV_W), nxt)]
    grid_spec = pltpu.PrefetchScalarGridSpec(
        num_scalar_prefetch=1,
        grid=(bsz, n_lat // tq),
        in_specs=[pl.BlockSpec((1, tq, A_Q_W), cur)] + kv_specs + kv_specs + [
            pl.BlockSpec((1, n_ctx, A_KV_W), ctx), pl.BlockSpec((1, n_ctx, A_KV_W), ctx)],
        out_specs=pl.BlockSpec((1, tq, A_Q_W), cur),
    )
    return pl.pallas_call(
        functools.partial(_attn_a_kernel, n_lat=n_lat, tq=tq),
        grid_spec=grid_spec,
        out_shape=jax.ShapeDtypeStruct((bsz, n_lat, A_Q_W), BF16),
        compiler_params=_params("parallel", "parallel"),
        name="window_attention",
    )(sink, qa, ka, ka, ka, va, va, va, kc_a, vc_a)


NB_Q_ROWS = 4


def _attn_b_kernel(q_ref, kp_ref, kc_ref, kn_ref, vp_ref, vc_ref, vn_ref,
                   kctx_ref, vctx_ref, bias_ref, out_ref):
    k_win = jnp.concatenate([kp_ref[0], kc_ref[0], kn_ref[0]], axis=0)
    v_win = jnp.concatenate([vp_ref[0], vc_ref[0], vn_ref[0]], axis=0)
    q_all = q_ref[0]
    kctx = kctx_ref[0]
    vctx = vctx_ref[0]
    for h in range(B_HEADS):
        sl = slice(h * HEAD_DIM, (h + 1) * HEAD_DIM)
        q = q_all[:, sl]
        s_nb = _dot_t(q, k_win[:, sl]) + bias_ref[0, h]
        s_ctx = _dot_t(q, kctx[:, sl])
        o = _attend([s_nb, s_ctx], [_with_ones(v_win[:, sl]), _with_ones(vctx[:, sl])])
        out_ref[0, :, sl] = o.astype(BF16)


def neighbourhood_bias(rel_bias, n_lat):
    rows = n_lat // GRID_W
    kr_n = min(NA_ROWS, rows)
    n_blocks = rows // NB_Q_ROWS
    n_heads, n_dr, n_dc = rel_bias.shape
    cols = np.arange(GRID_W)
    c_start = np.clip(cols - NA_COLS // 2, 0, GRID_W - NA_COLS)
    col_ok = (cols[None, :] >= c_start[:, None]) & (cols[None, :] < c_start[:, None] + NA_COLS)
    dc = np.clip(cols[None, :] - cols[:, None], -(NA_COLS - 1), NA_COLS - 1) + NA_COLS - 1
    pick_dc = (dc.reshape(-1)[None, :] == np.arange(n_dc)[:, None]).astype(np.float32)
    toeplitz = jnp.dot(rel_bias.reshape(n_heads * n_dr, n_dc), pick_dc, precision=lax.Precision.HIGHEST)
    toeplitz = jnp.where(col_ok.reshape(-1), toeplitz, NEG_INF).reshape(n_heads, n_dr, GRID_W, GRID_W)
    q_rl = np.arange(NB_Q_ROWS)
    k_rl = np.arange(3 * NB_Q_ROWS)
    row_ok, dr = [], []
    for j in sorted({0, min(1, n_blocks - 1), n_blocks - 1}):
        r = NB_Q_ROWS * j + q_rl
        kr = NB_Q_ROWS * (j - 1) + k_rl
        r_start = np.clip(r - kr_n // 2, 0, rows - kr_n)
        ok = (kr[None, :] >= r_start[:, None]) & (kr[None, :] < r_start[:, None] + kr_n)
        row_ok.append(ok & (kr[None, :] >= 0) & (kr[None, :] < rows))
        dr.append(np.clip(kr[None, :] - r[:, None] + NA_ROWS - 1, 0, n_dr - 1))
    row_ok = np.stack(row_ok)
    dr = np.stack(dr)
    tiles = jnp.stack([toeplitz[:, int(i)] for i in dr.reshape(-1)], axis=1)
    tiles = tiles.reshape((n_heads,) + dr.shape + (GRID_W, GRID_W))
    tiles = jnp.where(row_ok[None, :, :, :, None, None], tiles, NEG_INF)
    table = tiles.transpose(1, 0, 2, 4, 3, 5).reshape(
        dr.shape[0], n_heads, NB_Q_ROWS * GRID_W, 3 * NB_Q_ROWS * GRID_W)
    return table, n_blocks


def neighbourhood_attention(qb, kb, vb, kc_b, vc_b, rel_bias):
    bsz, n_lat, _ = qb.shape
    n_ctx = kc_b.shape[1]
    table, n_blocks = neighbourhood_bias(rel_bias, n_lat)
    n_var = table.shape[0]
    tq = NB_Q_ROWS * GRID_W
    prev = lambda b, j: (b, jnp.maximum(j - 1, 0), 0)
    cur = lambda b, j: (b, j, 0)
    nxt = lambda b, j: (b, jnp.minimum(j + 1, n_blocks - 1), 0)
    ctx = lambda b, j: (b, 0, 0)

    def variant(b, j):
        v = jnp.where(j == 0, 0, jnp.where(j == n_blocks - 1, n_var - 1, min(1, n_var - 1)))
        return (v, 0, 0, 0)

    kv_specs = [pl.BlockSpec((1, tq, B_W), prev), pl.BlockSpec((1, tq, B_W), cur),
                pl.BlockSpec((1, tq, B_W), nxt)]
    return pl.pallas_call(
        _attn_b_kernel,
        grid=(bsz, n_blocks),
        in_specs=[pl.BlockSpec((1, tq, B_W), cur)] + kv_specs + kv_specs + [
            pl.BlockSpec((1, n_ctx, B_W), ctx), pl.BlockSpec((1, n_ctx, B_W), ctx),
            pl.BlockSpec((1, B_HEADS, tq, 3 * tq), variant)],
        out_specs=pl.BlockSpec((1, tq, B_W), cur),
        out_shape=jax.ShapeDtypeStruct((bsz, n_lat, B_W), BF16),
        compiler_params=_params("parallel", "parallel"),
        name="neighbourhood_attention",
    )(qb, kb, kb, kb, vb, vb, vb, kc_b, vc_b, table)


def _ctx_attn_kernel(sink_ref, qa_ref, qb_ref, ka_ref, va_ref, kb_ref, vb_ref, oa_ref, ob_ref):
    qa, qb = qa_ref[0], qb_ref[0]
    ka, va, kb, vb = ka_ref[0], va_ref[0], kb_ref[0], vb_ref[0]
    for hq in range(A_Q_HEADS):
        sl = slice(hq * HEAD_DIM, (hq + 1) * HEAD_DIM)
        hk = hq // A_GROUP
        kv = slice(hk * HEAD_DIM, (hk + 1) * HEAD_DIM)
        q = qa[:, sl]
        o = _attend([_dot_t(q, ka[:, kv])], [_with_ones(va[:, kv])], sink=sink_ref[hq])
        oa_ref[0, :, sl] = o.astype(BF16)
    for h in range(B_HEADS):
        sl = slice(h * HEAD_DIM, (h + 1) * HEAD_DIM)
        q = qb[:, sl]
        o = _attend([_dot_t(q, kb[:, sl])], [_with_ones(vb[:, sl])])
        ob_ref[0, :, sl] = o.astype(BF16)


def context_attention(qa, qb, ka, va, kb, vb, sink):
    bsz, n_ctx, _ = qa.shape
    blk = lambda w: pl.BlockSpec((1, n_ctx, w), lambda b, s: (b, 0, 0))
    grid_spec = pltpu.PrefetchScalarGridSpec(
        num_scalar_prefetch=1,
        grid=(bsz,),
        in_specs=[blk(A_Q_W), blk(B_W), blk(A_KV_W), blk(A_KV_W), blk(B_W), blk(B_W)],
        out_specs=[blk(A_Q_W), blk(B_W)],
    )
    return pl.pallas_call(
        _ctx_attn_kernel,
        grid_spec=grid_spec,
        out_shape=[jax.ShapeDtypeStruct((bsz, n_ctx, A_Q_W), BF16),
                   jax.ShapeDtypeStruct((bsz, n_ctx, B_W), BF16)],
        compiler_params=_params("parallel"),
        name="context_attention",
    )(sink, qa, qb, ka, va, kb, vb)


CONV_HALO = 16
F32_SUBLANES = 8
CONV_SHIFT_SPAN = (CONV_HALO + C_CONV_WIDTH // 2) // F32_SUBLANES * F32_SUBLANES


def _conv_kernel(prev_ref, cur_ref, next_ref, w_ref, b_ref, g_ref, beta_ref, out_ref, shifted_ref, *, ts):
    i = pl.program_id(1)
    n_i = pl.num_programs(1)
    ext = jnp.concatenate([jnp.where(i > 0, prev_ref[0], 0.0), cur_ref[0],
                           jnp.where(i < n_i - 1, next_ref[0], 0.0)], axis=0)
    for r in range(F32_SUBLANES):
        shifted_ref[r] = ext[r:r + ts + CONV_SHIFT_SPAN]
    acc = jnp.zeros((ts, C_CHANNELS), F32)
    for k in range(C_CONV_WIDTH):
        start = CONV_HALO - C_CONV_WIDTH // 2 + k
        aligned = start - start % F32_SUBLANES
        acc = acc + shifted_ref[start % F32_SUBLANES, aligned:aligned + ts] * w_ref[k:k + 1]
    y = _layer_norm(acc + b_ref[...]) * g_ref[...] + beta_ref[...]
    out_ref[0] = (y * jax.nn.sigmoid(y)).astype(BF16)


def conformer_conv(hc, conv_w, conv_b, ln_g, ln_b):
    bx, t, ch = hc.shape
    ts = min(512, t)
    per = ts // CONV_HALO
    last = t // CONV_HALO - 1
    row = lambda v: v.reshape(1, ch)
    const = lambda b, i: (0, 0)
    return pl.pallas_call(
        functools.partial(_conv_kernel, ts=ts),
        grid=(bx, t // ts),
        in_specs=[
            pl.BlockSpec((1, CONV_HALO, ch), lambda b, i: (b, jnp.maximum(i * per - 1, 0), 0)),
            pl.BlockSpec((1, ts, ch), lambda b, i: (b, i, 0)),
            pl.BlockSpec((1, CONV_HALO, ch), lambda b, i: (b, jnp.minimum((i + 1) * per, last), 0)),
            pl.BlockSpec((C_CONV_WIDTH, ch), const),
            pl.BlockSpec((1, ch), const), pl.BlockSpec((1, ch), const), pl.BlockSpec((1, ch), const),
        ],
        out_specs=pl.BlockSpec((1, ts, ch), lambda b, i: (b, i, 0)),
        out_shape=jax.ShapeDtypeStruct((bx, t, ch), BF16),
        scratch_shapes=[pltpu.VMEM((F32_SUBLANES, ts + CONV_SHIFT_SPAN, ch), F32)],
        compiler_params=_params("parallel", "parallel"),
        name="conformer_conv",
    )(hc, hc, hc, conv_w, row(conv_b), row(ln_g), row(ln_b))


def _outproj_kernel(oa_ref, ob_ref, oc_ref, x_ref, g1_ref, sh_ref, sc_ref, w_ref, lng_ref, lnb_ref,
                    wr_hi_ref, wr_lo_ref, x1_ref, h2_ref, afft_ref, aff_ref, *, alpha):
    o = (_dot(oa_ref[0], w_ref[0:A_Q_W])
         + _dot(ob_ref[0], w_ref[A_Q_W:A_Q_W + B_W])
         + _dot(oc_ref[0], w_ref[A_Q_W + B_W:]))
    y = _layer_norm(alpha * x_ref[0] + g1_ref[0] * o) * lng_ref[...] + lnb_ref[...]
    x1_ref[0] = y
    h2 = _layer_norm(y) * (1.0 + sc_ref[0]) + sh_ref[0]
    h_hi = h2.astype(BF16)
    rows_per_token = h2.shape[1] // LANES
    for j in range(rows_per_token):
        h2_ref[0, pl.ds(j, h2.shape[0], stride=rows_per_token), :] = h2[:, j * LANES:(j + 1) * LANES]
    h_lo = (h2 - h_hi.astype(F32)).astype(BF16)
    w_hi, w_lo = wr_hi_ref[...], wr_lo_ref[...]
    logits = _dot_t(h_hi, w_hi) + _dot_t(h_lo, w_hi) + _dot_t(h_hi, w_lo)
    e_n = jnp.exp(logits - logits.max(axis=1, keepdims=True))
    aff = e_n / e_n.sum(axis=1, keepdims=True)
    aff_ref[0] = aff
    eye = (lax.broadcasted_iota(jnp.int32, (N_EXPERTS, N_EXPERTS), 0)
           == lax.broadcasted_iota(jnp.int32, (N_EXPERTS, N_EXPERTS), 1)).astype(BF16)
    aff_t, rest = None, aff
    for _ in range(GATE_PARTS):
        part = rest.astype(BF16)
        term = _dot_t(eye, part)
        aff_t = term if aff_t is None else aff_t + term
        rest = rest - part.astype(F32)
    afft_ref[0] = aff_t


def out_projection(oa, ob, oc, x, g1, sh2, sc2, w_out_bf16, ln_g, ln_b, wr_hi, wr_lo, alpha):
    bx, t, d = x.shape
    tm = min(512, t)
    tok = lambda b, i: (b, i, 0)
    per_b = lambda b, i: (b, 0, 0)
    const = lambda b, i: (0, 0)
    vec = pl.BlockSpec((1, d), const)
    return pl.pallas_call(
        functools.partial(_outproj_kernel, alpha=alpha),
        grid=(bx, t // tm),
        in_specs=[
            pl.BlockSpec((1, tm, A_Q_W), tok), pl.BlockSpec((1, tm, B_W), tok),
            pl.BlockSpec((1, tm, C_CHANNELS), tok), pl.BlockSpec((1, tm, d), tok),
            pl.BlockSpec((1, 1, d), per_b), pl.BlockSpec((1, 1, d), per_b), pl.BlockSpec((1, 1, d), per_b),
            pl.BlockSpec(w_out_bf16.shape, const), vec, vec,
            pl.BlockSpec((N_EXPERTS, d), const), pl.BlockSpec((N_EXPERTS, d), const),
        ],
        out_specs=[pl.BlockSpec((1, tm, d), tok), pl.BlockSpec((1, tm * (d // LANES), LANES), tok),
                   pl.BlockSpec((1, N_EXPERTS, tm), lambda b, i: (b, 0, i)),
                   pl.BlockSpec((1, tm, N_EXPERTS), tok)],
        out_shape=[jax.ShapeDtypeStruct((bx, t, d), F32),
                   jax.ShapeDtypeStruct((bx, t * (d // LANES), LANES), F32),
                   jax.ShapeDtypeStruct((bx, N_EXPERTS, t), F32),
                   jax.ShapeDtypeStruct((bx, t, N_EXPERTS), F32)],
        compiler_params=_params("parallel", "parallel"),
        name="out_projection",
    )(oa, ob, oc, x, g1, sh2, sc2, w_out_bf16, ln_g.reshape(1, d), ln_b.reshape(1, d), wr_hi, wr_lo)


def _select_kernel(afft_ref, pos_ref, post_ref, off_ref, *, cap, n_tok):
    aff = afft_ref[0]

    def bit_step(j, bits):
        cand = bits | (jnp.int32(1) << (30 - j))
        cnt = jnp.sum((aff >= pltpu.bitcast(cand, F32)).astype(jnp.int32), axis=1, keepdims=True)
        return jnp.where(cnt >= cap, cand, bits)

    thr = pltpu.bitcast(lax.fori_loop(0, 31, bit_step, jnp.zeros((N_EXPERTS, 1), jnp.int32)), F32)
    above = (aff > thr).astype(F32)
    tied = (aff == thr).astype(F32)
    need = cap - jnp.sum(above, axis=1, keepdims=True)

    blk = LANES
    n_blk = n_tok // blk
    r_i = lax.broadcasted_iota(jnp.int32, (blk, blk), 0)
    c_i = lax.broadcasted_iota(jnp.int32, (blk, blk), 1)
    strict_upper = (r_i < c_i).astype(BF16)
    eye = (r_i == c_i).astype(BF16)

    run_tied = jnp.zeros((N_EXPERTS, 1), F32)
    run_sel = jnp.zeros((N_EXPERTS, 1), F32)
    offs = []
    for kb in range(n_blk):
        sl = slice(kb * blk, (kb + 1) * blk)
        offs.append(run_sel)
        t_f = tied[:, sl]
        rank_tied = run_tied + _dot(t_f.astype(BF16), strict_upper)
        run_tied = run_tied + t_f.sum(axis=1, keepdims=True)
        s_f = above[:, sl] + t_f * (rank_tied < need).astype(F32)
        rank = run_sel + _dot(s_f.astype(BF16), strict_upper)
        run_sel = run_sel + s_f.sum(axis=1, keepdims=True)
        pos = jnp.where(s_f > 0.5, rank.astype(jnp.int32), -1)
        pos_ref[0, :, sl] = pos
        p1 = pos + 1
        hi = (p1 >> 5).astype(F32).astype(BF16)
        lo = (p1 & 31).astype(F32).astype(BF16)
        p1_t = _dot_t(eye, hi) * 32.0 + _dot_t(eye, lo)
        post_ref[0, sl, :] = p1_t - 1.0
    off_ref[0] = jnp.concatenate(offs, axis=1).astype(jnp.int32)


def expert_choice_select(aff_t, cap):
    bx, n_e, t = aff_t.shape
    n_tb = t // OFFSET_BLOCK
    return pl.pallas_call(
        functools.partial(_select_kernel, cap=cap, n_tok=t),
        grid=(bx,),
        in_specs=[pl.BlockSpec((1, n_e, t), lambda b: (b, 0, 0))],
        out_specs=[pl.BlockSpec((1, n_e, t), lambda b: (b, 0, 0)),
                   pl.BlockSpec((1, t, n_e), lambda b: (b, 0, 0)),
                   pl.BlockSpec((1, n_e, n_tb), lambda b: (b, 0, 0))],
        out_shape=[jax.ShapeDtypeStruct((bx, n_e, t), jnp.int32),
                   jax.ShapeDtypeStruct((bx, t, n_e), F32),
                   jax.ShapeDtypeStruct((bx, n_e, n_tb), jnp.int32)],
        compiler_params=_params("parallel"),
        name="expert_choice_select",
    )(aff_t)


GATE_PARTS = 3
TOKEN_LANE = GATE_PARTS


def _slot_table_kernel(off_ref, pos_ref, aff_ref, tbl_ref, *, n_off, blocks_per_step):
    b, eg, kc = pl.program_id(0), pl.program_id(1), pl.program_id(2)
    epg = EXPERTS_PER_GATHER_STEP

    @pl.when(kc == 0)
    def _():
        tbl_ref[...] = jnp.zeros_like(tbl_ref)

    slot = lax.broadcasted_iota(jnp.int32, (GATHER_WINDOW, 1), 0)
    slot_2d = lax.broadcasted_iota(jnp.int32, (GATHER_WINDOW, GATHER_BLOCK), 0)
    lane = lax.broadcasted_iota(jnp.int32, (GATHER_BLOCK, LANES), 1)
    out_lane = lax.broadcasted_iota(jnp.int32, (GATHER_WINDOW, LANES), 1)
    local_token = lax.broadcasted_iota(jnp.int32, (GATHER_BLOCK, LANES), 0).astype(F32)
    expert_lane = lax.broadcasted_iota(jnp.int32, (GATHER_BLOCK, N_EXPERTS), 1)
    for kk in range(blocks_per_step):
        kb = kc * blocks_per_step + kk
        aff_blk = aff_ref[0, kk * GATHER_BLOCK:(kk + 1) * GATHER_BLOCK]
        first_token = (kb * GATHER_BLOCK).astype(F32)
        for ee in range(epg):
            e = eg * epg + ee
            off = off_ref[(b * N_EXPERTS + e) * n_off + kb * (GATHER_BLOCK // OFFSET_BLOCK)]
            base = pl.multiple_of((off >> 4) << 4, BF16_SUBLANES)
            onehot = (pos_ref[0, ee, 0, kk:kk + 1, :] - base == slot_2d).astype(BF16)
            gate = jnp.sum(jnp.where(expert_lane == e, aff_blk, 0.0), axis=1, keepdims=True)
            payload, rest = jnp.where(lane == TOKEN_LANE, local_token, 0.0), gate
            for k in range(GATE_PARTS):
                part = rest.astype(BF16).astype(F32)
                payload = jnp.where(lane == k, part, payload)
                rest = rest - part
            gathered = _dot(onehot, payload.astype(BF16)) + jnp.where(out_lane == TOKEN_LANE, first_token, 0.0)
            win = pl.ds(base, GATHER_WINDOW)
            tbl_ref[0, ee, win, :] = jnp.where(slot >= off - base, gathered, tbl_ref[0, ee, win, :])


def slot_table(aff, pos, block_off, cap_pad):
    bx, t, _ = aff.shape
    n_tb = t // GATHER_BLOCK
    blocks_per_step = min(4, n_tb)
    n_steps = n_tb // blocks_per_step
    epg = EXPERTS_PER_GATHER_STEP
    tokens = blocks_per_step * GATHER_BLOCK
    pos5 = pos.reshape(bx, N_EXPERTS, n_steps, blocks_per_step, GATHER_BLOCK)
    grid_spec = pltpu.PrefetchScalarGridSpec(
        num_scalar_prefetch=1,
        grid=(bx, N_EXPERTS // epg, n_steps),
        in_specs=[
            pl.BlockSpec((1, epg, 1, blocks_per_step, GATHER_BLOCK), lambda b, g, k, s: (b, g, k, 0, 0)),
            pl.BlockSpec((1, tokens, N_EXPERTS), lambda b, g, k, s: (b, k, 0)),
        ],
        out_specs=pl.BlockSpec((1, epg, cap_pad, LANES), lambda b, g, k, s: (b, g, 0, 0)),
    )
    return pl.pallas_call(
        functools.partial(_slot_table_kernel, n_off=t // OFFSET_BLOCK, blocks_per_step=blocks_per_step),
        grid_spec=grid_spec,
        out_shape=jax.ShapeDtypeStruct((bx, N_EXPERTS, cap_pad, LANES), F32),
        compiler_params=_params("parallel", "parallel", "arbitrary"),
        name="slot_table",
    )(block_off.reshape(-1), pos5, aff)


GATHER_UNROLL = 8


def _row_gather_kernel(idx_ref, src_ref, xs_ref, tile_ref, *, cap, rows_per_token, chunk_stride):
    b, e = pl.program_id(0), pl.program_id(1)
    first = (b * N_EXPERTS + e) * cap

    def group(g, carry):
        for u in range(GATHER_UNROLL):
            s = g * GATHER_UNROLL + u
            row = pl.multiple_of(idx_ref[first + s] * rows_per_token, rows_per_token)
            tile_ref[pl.ds(s, rows_per_token, stride=chunk_stride), :] = src_ref[0, pl.ds(row, rows_per_token), :]
        return carry

    lax.fori_loop(0, cap // GATHER_UNROLL, group, 0)
    for j in range(rows_per_token):
        xs_ref[0, 0, :, j * LANES:(j + 1) * LANES] = tile_ref[j * chunk_stride:j * chunk_stride + cap, :].astype(BF16)


def gather_rows(h2_rows, token_idx, cap, d):
    bx, n_rows, _ = h2_rows.shape
    rows_per_token = d // LANES
    chunk_stride = cap + 8
    grid_spec = pltpu.PrefetchScalarGridSpec(
        num_scalar_prefetch=1,
        grid=(bx, N_EXPERTS),
        in_specs=[pl.BlockSpec((1, n_rows, LANES), lambda b, e, s: (b, 0, 0), pipeline_mode=pl.Buffered(1))],
        out_specs=pl.BlockSpec((1, 1, cap, d), lambda b, e, s: (b, e, 0, 0)),
        scratch_shapes=[pltpu.VMEM((rows_per_token * chunk_stride, LANES), F32)],
    )
    return pl.pallas_call(
        functools.partial(_row_gather_kernel, cap=cap, rows_per_token=rows_per_token, chunk_stride=chunk_stride),
        grid_spec=grid_spec,
        out_shape=jax.ShapeDtypeStruct((bx, N_EXPERTS, cap, d), BF16),
        compiler_params=_params("parallel", "arbitrary"),
        name="gather_rows",
    )(token_idx, h2_rows)


def _ffn_kernel(xs_ref, gs_ref, wg_ref, wu_ref, wd_ref, ye_ref, wg_bf, wu_bf, wd_bf, *, cap, row_tile):
    @pl.when(pl.program_id(1) == 0)
    def _():
        wg_bf[...] = wg_ref[0, 0].astype(BF16)
        wu_bf[...] = wu_ref[0, 0].astype(BF16)
        wd_bf[...] = wd_ref[0, 0].astype(BF16)

    n_b = xs_ref.shape[0]
    if n_b == 1:
        tiles = [[(0, r0, row_tile)] for r0 in range(0, cap, row_tile)]
    else:
        tiles = [[(bb, 0, cap) for bb in range(n_b)]]
    for tile in tiles:
        x = jnp.concatenate([xs_ref[bb, 0, r0:r0 + n] for bb, r0, n in tile], axis=0)
        terms = jnp.concatenate([gs_ref[bb, 0, r0:r0 + n, 0:GATE_PARTS] for bb, r0, n in tile], axis=0)
        gate = _dot(x, wg_bf[...])
        up = _dot(x, wu_bf[...])
        hid = (gate * jax.nn.sigmoid(gate) * up).astype(BF16)
        g = jnp.sum(terms, axis=1, keepdims=True)
        ye = (_dot(hid, wd_bf[...]) * g).astype(BF16)
        row = 0
        for bb, r0, n in tile:
            ye_ref[bb, 0, r0:r0 + n] = ye[row:row + n]
            row += n
    for bb in range(n_b):
        ye_ref[bb, 0, cap:] = jnp.zeros((ye_ref.shape[2] - cap, ye_ref.shape[3]), BF16)


def expert_ffn(xs, gs, wg, wu, wd, layer, cap_pad):
    bx, n_e, cap, d = xs.shape
    ff = wg.shape[-1]
    max_rows = 512
    row_tile = min(max_rows, cap)
    n_b = bx if bx * cap <= max_rows else 1
    return pl.pallas_call(
        functools.partial(_ffn_kernel, cap=cap, row_tile=row_tile),
        grid=(n_e, bx // n_b),
        in_specs=[
            pl.BlockSpec((n_b, 1, cap, d), lambda e, b: (b, e, 0, 0)),
            pl.BlockSpec((n_b, 1, cap, LANES), lambda e, b: (b, e, 0, 0)),
            pl.BlockSpec((1, 1, d, ff), lambda e, b: (layer, e, 0, 0)),
            pl.BlockSpec((1, 1, d, ff), lambda e, b: (layer, e, 0, 0)),
            pl.BlockSpec((1, 1, ff, d), lambda e, b: (layer, e, 0, 0)),
        ],
        out_specs=pl.BlockSpec((n_b, 1, cap_pad, d), lambda e, b: (b, e, 0, 0)),
        out_shape=jax.ShapeDtypeStruct((bx, n_e, cap_pad, d), BF16),
        scratch_shapes=[pltpu.VMEM((d, ff), BF16), pltpu.VMEM((d, ff), BF16), pltpu.VMEM((ff, d), BF16)],
        compiler_params=_params("parallel", "arbitrary"),
        name="expert_ffn",
    )(xs, gs, wg, wu, wd)


EXPERTS_PER_SCATTER_STEP = 2


def _combine_kernel(off_ref, post_ref, ye_ref, x1_ref, g2_ref, lng_ref, lnb_ref,
                    out_ref, acc_ref, *, n_tb, blocks_per_step, alpha):
    b, tq, eg = pl.program_id(0), pl.program_id(1), pl.program_id(2)
    eps = EXPERTS_PER_SCATTER_STEP

    @pl.when(eg == 0)
    def _():
        acc_ref[...] = jnp.zeros_like(acc_ref)

    expert_lane = lax.broadcasted_iota(jnp.int32, (SCATTER_BLOCK, N_EXPERTS), 1)
    slot = lax.broadcasted_iota(jnp.int32, (SCATTER_BLOCK, SCATTER_WINDOW), 1).astype(F32)
    for kk in range(blocks_per_step):
        kb = tq * blocks_per_step + kk
        rows = slice(kk * SCATTER_BLOCK, (kk + 1) * SCATTER_BLOCK)
        post_blk = post_ref[0, rows, :]
        onehots, windows = [], []
        for ee in range(eps):
            e = eg * eps + ee
            off = off_ref[(b * N_EXPERTS + e) * n_tb + kb * (SCATTER_BLOCK // OFFSET_BLOCK)]
            base = pl.multiple_of((off >> 4) << 4, BF16_SUBLANES)
            pcol = jnp.sum(jnp.where(expert_lane == e, post_blk, 0.0), axis=1, keepdims=True)
            onehots.append((pcol - base.astype(F32) == slot).astype(BF16))
            windows.append(ye_ref[0, ee, pl.ds(base, SCATTER_WINDOW), :])
        acc_ref[rows, :] += _dot(jnp.concatenate(onehots, axis=1), jnp.concatenate(windows, axis=0))

    @pl.when(eg == N_EXPERTS // eps - 1)
    def _():
        y = alpha * x1_ref[0] + g2_ref[0] * acc_ref[...]
        out_ref[0] = _layer_norm(y) * lng_ref[...] + lnb_ref[...]


def combine_and_norm(ye, pos_t, block_off, x1, g2, ln_g, ln_b, alpha):
    bx, t, d = x1.shape
    cap_pad = ye.shape[2]
    n_tb = t // OFFSET_BLOCK
    blocks_per_step = min(16, t // SCATTER_BLOCK)
    tq = blocks_per_step * SCATTER_BLOCK
    eps = EXPERTS_PER_SCATTER_STEP
    tok = lambda b, q, e, s: (b, q, 0)
    const = lambda b, q, e, s: (0, 0)
    grid_spec = pltpu.PrefetchScalarGridSpec(
        num_scalar_prefetch=1,
        grid=(bx, t // tq, N_EXPERTS // eps),
        in_specs=[
            pl.BlockSpec((1, tq, N_EXPERTS), tok),
            pl.BlockSpec((1, eps, cap_pad, d), lambda b, q, e, s: (b, e, 0, 0)),
            pl.BlockSpec((1, tq, d), tok, pipeline_mode=pl.Buffered(1)),
            pl.BlockSpec((1, 1, d), lambda b, q, e, s: (b, 0, 0)),
            pl.BlockSpec((1, d), const), pl.BlockSpec((1, d), const),
        ],
        out_specs=pl.BlockSpec((1, tq, d), tok),
        scratch_shapes=[pltpu.VMEM((tq, d), F32)],
    )
    return pl.pallas_call(
        functools.partial(_combine_kernel, n_tb=n_tb, blocks_per_step=blocks_per_step, alpha=alpha),
        grid_spec=grid_spec,
        out_shape=jax.ShapeDtypeStruct((bx, t, d), F32),
        compiler_params=_params("parallel", "parallel", "arbitrary"),
        name="combine_and_norm",
    )(block_off.reshape(-1), pos_t, ye, x1, g2, ln_g.reshape(1, d), ln_b.reshape(1, d))


def _split_bf16(w):
    hi = w.astype(BF16)
    return hi, (w - hi.astype(F32)).astype(BF16)


def _mixer_tail(oa, ob, oc, x, mod, lw, alpha):
    g1, sh2, sc2, g2 = mod
    t = x.shape[1]
    cap = EC_CAPACITY * t // N_EXPERTS
    cap_pad = cap + SLOT_PAD
    x1, h2_rows, aff_t, aff = out_projection(oa, ob, oc, x, g1, sh2, sc2, lw["w_out"], lw["ln1_g"], lw["ln1_b"],
                                        lw["wr_hi"], lw["wr_lo"], alpha)
    pos, pos_t, block_off = expert_choice_select(aff_t, cap)
    table = slot_table(aff, pos, block_off, cap_pad)
    token_idx = table[:, :, :cap, TOKEN_LANE].astype(jnp.int32).reshape(-1)
    xs = gather_rows(h2_rows, token_idx, cap, x.shape[2])
    ye = expert_ffn(xs, table, lw["w_gate"], lw["w_up"], lw["w_down"], lw["layer"], cap_pad)
    return combine_and_norm(ye, pos_t, block_off, x1, g2, lw["ln2_g"], lw["ln2_b"], alpha)


def kernel(x, c, ctx, c_ctx, w_mod, b_mod, w_in, a_sink, nat_bias, conv_w, conv_b, conv_ln_g, conv_ln_b,
           w_out, ln1_g, ln1_b, w_router, w_gate, w_up, w_down, ln2_g, ln2_b):
    bsz, n_lat, d = x.shape
    depth = w_mod.shape[0]
    alpha = (2 * depth) ** 0.25
    cos_t, sin_t = rope_tables(n_lat)

    cond = jnp.concatenate([c, c_ctx[None, :], jnp.zeros((8 - bsz - 1, d), F32)], axis=0)
    mods = adaln_all(cond, w_mod, b_mod)

    for l in range(depth):
        last = l == depth - 1
        wr_hi, wr_lo = _split_bf16(w_router[l].T)
        lw = dict(w_out=w_out[l].astype(BF16), ln1_g=ln1_g[l], ln1_b=ln1_b[l], wr_hi=wr_hi, wr_lo=wr_lo,
                  w_gate=w_gate, w_up=w_up, w_down=w_down, layer=l,
                  ln2_g=ln2_g[l], ln2_b=ln2_b[l])
        w_in_l = w_in[l].astype(BF16)
        lat = [mods[l, :bsz, k * d:(k + 1) * d][:, None, :] for k in range(N_MOD)]
        cm = [jnp.broadcast_to(mods[l, bsz, k * d:(k + 1) * d][None, None, :], (bsz, 1, d))
              for k in range(N_MOD)]
        conv_args = (conv_w[l], conv_b[l], conv_ln_g[l], conv_ln_b[l])

        qa_c, ka_c, va_c, qb_c, kb_c, vb_c, hc_c = in_projection(ctx, cm[0], cm[1], w_in_l, cos_t, sin_t, rope=False)
        if not last:
            oa_c, ob_c = context_attention(qa_c, qb_c, ka_c, va_c, kb_c, vb_c, a_sink[l])
            oc_c = conformer_conv(hc_c, *conv_args)
            ctx_new = _mixer_tail(oa_c, ob_c, oc_c, ctx, (cm[2], cm[3], cm[4], cm[5]), lw, alpha)

        qa, ka, va, qb, kb, vb, hc = in_projection(x, lat[0], lat[1], w_in_l, cos_t, sin_t, rope=True)
        oa = window_attention(qa, ka, va, ka_c, va_c, a_sink[l])
        ob = neighbourhood_attention(qb, kb, vb, kb_c, vb_c, nat_bias[l])
        oc = conformer_conv(hc, *conv_args)
        x = _mixer_tail(oa, ob, oc, x, (lat[2], lat[3], lat[4], lat[5]), lw, alpha)
        if not last:
            ctx = ctx_new
    return x
```

```python
import functools

import numpy as np
import jax
import jax.numpy as jnp
from jax import lax
from jax.experimental import pallas as pl
from jax.experimental.pallas import tpu as pltpu

HEAD_DIM = 64
GRID_W = 64
A_Q_HEADS = 8
A_KV_HEADS = 2
A_GROUP = A_Q_HEADS // A_KV_HEADS
A_WINDOW = 128
B_HEADS = 4
NA_ROWS = 8
NA_COLS = 16
C_CHANNELS = 256
C_CONV_WIDTH = 31
A_Q_W = A_Q_HEADS * HEAD_DIM
A_KV_W = A_KV_HEADS * HEAD_DIM
B_W = B_HEADS * HEAD_DIM
OFF_AK = A_Q_W
OFF_AV = OFF_AK + A_KV_W
OFF_BQ = OFF_AV + A_KV_W
OFF_BK = OFF_BQ + B_W
OFF_BV = OFF_BK + B_W
OFF_C = OFF_BV + B_W
IN_WIDTH = OFF_C + 2 * C_CHANNELS
ROPE_WIDTH = A_Q_W + A_KV_W
N_EXPERTS = 16
EC_CAPACITY = 2
ROPE_BASE = 10000.0
LN_EPS = 1e-6
N_MOD = 6
NEG_INF = -1e30
QK_SCALE = HEAD_DIM ** -0.5

LANES = 128
BF16_SUBLANES = 16
MXU_DEPTH = 256
OFFSET_BLOCK = 128
GATHER_BLOCK = MXU_DEPTH
GATHER_WINDOW = GATHER_BLOCK + BF16_SUBLANES
SCATTER_BLOCK = OFFSET_BLOCK
SCATTER_WINDOW = MXU_DEPTH
SLOT_PAD = 3 * LANES
EXPERTS_PER_GATHER_STEP = 4
VMEM_LIMIT = 56 * 1024 * 1024

F32 = jnp.float32
BF16 = jnp.bfloat16


def _dot(a, b):
    return jnp.dot(a, b, preferred_element_type=F32)


def _dot_t(a, b):
    return lax.dot_general(a, b, (((1,), (1,)), ((), ())), preferred_element_type=F32)


def _layer_norm(x):
    mu = jnp.mean(x, axis=-1, keepdims=True)
    xc = x - mu
    var = jnp.mean(xc * xc, axis=-1, keepdims=True)
    return xc * lax.rsqrt(var + LN_EPS)


def _params(*sem):
    return pltpu.CompilerParams(dimension_semantics=sem, vmem_limit_bytes=VMEM_LIMIT)


def _mod_kernel(cond_ref, w_ref, b_ref, out_ref):
    cnd = cond_ref[...]
    act = cnd * jax.nn.sigmoid(cnd)
    out_ref[0] = jnp.dot(act, w_ref[0], preferred_element_type=F32,
                         precision=lax.Precision.HIGHEST) + b_ref[0]


def adaln_all(cond, w_mod, b_mod):
    n_layers, d, width = w_mod.shape
    rows = cond.shape[0]
    tn = 1536
    return pl.pallas_call(
        _mod_kernel,
        grid=(n_layers, width // tn),
        in_specs=[
            pl.BlockSpec((rows, d), lambda l, j: (0, 0)),
            pl.BlockSpec((1, d, tn), lambda l, j: (l, 0, j)),
            pl.BlockSpec((1, 1, tn), lambda l, j: (l, 0, j)),
        ],
        out_specs=pl.BlockSpec((1, rows, tn), lambda l, j: (l, 0, j)),
        out_shape=jax.ShapeDtypeStruct((n_layers, rows, width), F32),
        compiler_params=_params("parallel", "parallel"),
        name="adaln",
    )(cond, w_mod, b_mod.reshape(n_layers, 1, width))


def _inproj_kernel(x_ref, sh_ref, sc_ref, w_ref, cos_ref, sin_ref,
                   qa_ref, ka_ref, va_ref, qb_ref, kb_ref, vb_ref, hc_ref, *, rope):
    x = x_ref[0]
    h = _layer_norm(x) * (1.0 + sc_ref[0]) + sh_ref[0]
    u = _dot(h.astype(BF16), w_ref[...])

    def rotated(col):
        xq = u[:, col:col + LANES]
        if not rope:
            return xq
        lane = lax.broadcasted_iota(jnp.int32, xq.shape, 1)
        first = (lane & (HEAD_DIM // 2 - 1)) < (HEAD_DIM // 4)
        partner = jnp.where(first, pltpu.roll(xq, LANES - HEAD_DIM // 4, 1),
                            pltpu.roll(xq, HEAD_DIM // 4, 1))
        return xq * cos_ref[...] + partner * sin_ref[...]

    rot = [rotated(col) for col in range(0, ROPE_WIDTH, LANES)]
    n_q = A_Q_W // LANES
    qa_ref[0] = (jnp.concatenate(rot[:n_q], axis=1) * QK_SCALE).astype(BF16)
    ka_ref[0] = jnp.concatenate(rot[n_q:], axis=1).astype(BF16)
    va_ref[0] = u[:, OFF_AV:OFF_BQ].astype(BF16)
    qb_ref[0] = (u[:, OFF_BQ:OFF_BK] * QK_SCALE).astype(BF16)
    kb_ref[0] = u[:, OFF_BK:OFF_BV].astype(BF16)
    vb_ref[0] = u[:, OFF_BV:OFF_C].astype(BF16)
    a = u[:, OFF_C:OFF_C + C_CHANNELS]
    gate = u[:, OFF_C + C_CHANNELS:]
    hc_ref[0] = a * jax.nn.sigmoid(gate)


def in_projection(x, shift, scale, w_in_bf16, cos_t, sin_t, *, rope):
    bx, t, d = x.shape
    tm = min(512, t)
    widths = (A_Q_W, A_KV_W, A_KV_W, B_W, B_W, B_W, C_CHANNELS)
    dtypes = (BF16,) * 6 + (F32,)
    tok = lambda b, i: (b, i, 0)
    per_b = lambda b, i: (b, 0, 0)
    return pl.pallas_call(
        functools.partial(_inproj_kernel, rope=rope),
        grid=(bx, t // tm),
        in_specs=[
            pl.BlockSpec((1, tm, d), tok),
            pl.BlockSpec((1, 1, d), per_b),
            pl.BlockSpec((1, 1, d), per_b),
            pl.BlockSpec((d, IN_WIDTH), lambda b, i: (0, 0)),
            pl.BlockSpec((tm, LANES), lambda b, i: (i, 0)),
            pl.BlockSpec((tm, LANES), lambda b, i: (i, 0)),
        ],
        out_specs=[pl.BlockSpec((1, tm, w), tok) for w in widths],
        out_shape=[jax.ShapeDtypeStruct((bx, t, w), dt) for w, dt in zip(widths, dtypes)],
        compiler_params=_params("parallel", "parallel"),
        name="in_projection",
    )(x, shift, scale, w_in_bf16, cos_t, sin_t)


def rope_tables(n_tokens):
    t = jnp.arange(n_tokens, dtype=jnp.int32)
    row = (t // GRID_W).astype(F32)[:, None]
    col = (t % GRID_W).astype(F32)[:, None]
    n_freq = HEAD_DIM // 4
    inv_freq = ROPE_BASE ** (-jnp.arange(n_freq, dtype=F32) / n_freq)
    ang_r = row * inv_freq
    ang_c = col * inv_freq
    cos_h = jnp.concatenate([jnp.cos(ang_r), jnp.cos(ang_r), jnp.cos(ang_c), jnp.cos(ang_c)], axis=1)
    sin_h = jnp.concatenate([-jnp.sin(ang_r), jnp.sin(ang_r), -jnp.sin(ang_c), jnp.sin(ang_c)], axis=1)
    reps = LANES // HEAD_DIM
    return jnp.tile(cos_h, (1, reps)), jnp.tile(sin_h, (1, reps))


def _with_ones(v):
    return jnp.concatenate([v, jnp.ones_like(v)], axis=1)


def _attend(score_parts, values, sink=None):
    m = score_parts[0].max(axis=-1, keepdims=True)
    for s in score_parts[1:]:
        m = jnp.maximum(m, s.max(axis=-1, keepdims=True))
    if sink is not None:
        m = jnp.maximum(m, sink)
    acc = None
    for s, v in zip(score_parts, values):
        term = _dot(jnp.exp((s - m).astype(BF16)), v)
        acc = term if acc is None else acc + term
    den = acc[:, HEAD_DIM:HEAD_DIM + 1]
    if sink is not None:
        den = den + jnp.exp(sink - m)
    return acc[:, :HEAD_DIM] / den


def _attn_a_kernel(sink_ref, q_ref, kp_ref, kc_ref, kn_ref, vp_ref, vc_ref, vn_ref,
                   kctx_ref, vctx_ref, out_ref, *, n_lat, tq):
    i = pl.program_id(1)
    k_win = jnp.concatenate([kp_ref[0], kc_ref[0], kn_ref[0]], axis=0)
    v_win = jnp.concatenate([vp_ref[0], vc_ref[0], vn_ref[0]], axis=0)
    kctx = kctx_ref[0]
    vctx = vctx_ref[0]
    sub = A_WINDOW
    span = 3 * A_WINDOW
    rows = A_GROUP * sub
    row_i = lax.broadcasted_iota(jnp.int32, (rows, span), 0)
    col_i = lax.broadcasted_iota(jnp.int32, (rows, span), 1)
    rel = col_i - A_WINDOW - (row_i & (sub - 1))
    in_band = (rel <= A_WINDOW) & (rel >= -A_WINDOW)
    group_of_row = lax.broadcasted_iota(jnp.int32, (rows, 1), 0) >> (sub.bit_length() - 1)
    sinks = []
    for hk in range(A_KV_HEADS):
        sink = jnp.zeros((rows, 1), F32)
        for g in range(A_GROUP):
            sink = jnp.where(group_of_row == g, sink_ref[hk * A_GROUP + g], sink)
        sinks.append(sink)
    vctx_ext = [_with_ones(vctx[:, hk * HEAD_DIM:(hk + 1) * HEAD_DIM]) for hk in range(A_KV_HEADS)]
    v_ext = [_with_ones(v_win[:, hk * HEAD_DIM:(hk + 1) * HEAD_DIM]) for hk in range(A_KV_HEADS)]
    def scores(j, hk):
        kpos = i * tq + j * sub - A_WINDOW + col_i
        valid = in_band & (kpos >= 0) & (kpos < n_lat)
        sl = slice(hk * HEAD_DIM, (hk + 1) * HEAD_DIM)
        q_rows = q_ref[0, j * sub:(j + 1) * sub]
        q = jnp.concatenate([q_rows[:, h * HEAD_DIM:(h + 1) * HEAD_DIM]
                             for h in range(hk * A_GROUP, (hk + 1) * A_GROUP)], axis=0)
        s_win = jnp.where(valid, _dot_t(q, k_win[j * sub:j * sub + span, sl]), NEG_INF)
        return [s_win, _dot_t(q, kctx[:, sl])]

    units = [(j, hk) for j in range(tq // sub) for hk in range(A_KV_HEADS)]
    all_scores = [scores(j, hk) for j, hk in units]
    for (j, hk), s in zip(units, all_scores):
        v_sub = v_ext[hk][j * sub:j * sub + span]
        o = _attend(s, [v_sub, vctx_ext[hk]], sink=sinks[hk]).astype(BF16)
        for g in range(A_GROUP):
            h = hk * A_GROUP + g
            out_ref[0, j * sub:(j + 1) * sub, h * HEAD_DIM:(h + 1) * HEAD_DIM] = o[g * sub:(g + 1) * sub]


def window_attention(qa, ka, va, kc_a, vc_a, sink):
    bsz, n_lat, _ = qa.shape
    n_ctx = kc_a.shape[1]
    tq = min(512, n_lat)
    w = A_WINDOW
    per = tq // w
    last = n_lat // w - 1
    prev = lambda b, i, s: (b, jnp.maximum(i * per - 1, 0), 0)
    cur = lambda b, i, s: (b, i, 0)
    nxt = lambda b, i, s: (b, jnp.minimum((i + 1) * per, last), 0)
    ctx = lambda b, i, s: (b, 0, 0)
    kv_specs = [pl.BlockSpec((1, w, A_KV_W), prev), pl.BlockSpec((1, tq, A_KV_W), cur),
                pl.BlockSpec((1, w, A_KV_W), nxt)]
    grid_spec = pltpu.PrefetchScalarGridSpec(
        num_scalar_prefetch=1,
        grid=(bsz, n_lat // tq),
        in_specs=[pl.BlockSpec((1, tq, A_Q_W), cur)] + kv_specs + kv_specs + [
            pl.BlockSpec((1, n_ctx, A_KV_W), ctx), pl.BlockSpec((1, n_ctx, A_KV_W), ctx)],
        out_specs=pl.BlockSpec((1, tq, A_Q_W), cur),
    )
    return pl.pallas_call(
        functools.partial(_attn_a_kernel, n_lat=n_lat, tq=tq),
        grid_spec=grid_spec,
        out_shape=jax.ShapeDtypeStruct((bsz, n_lat, A_Q_W), BF16),
        compiler_params=_params("parallel", "parallel"),
        name="window_attention",
    )(sink, qa, ka, ka, ka, va, va, va, kc_a, vc_a)


NB_Q_ROWS = 4


def _attn_b_kernel(q_ref, kp_ref, kc_ref, kn_ref, vp_ref, vc_ref, vn_ref,
                   kctx_ref, vctx_ref, bias_ref, out_ref):
    k_win = jnp.concatenate([kp_ref[0], kc_ref[0], kn_ref[0]], axis=0)
    v_win = jnp.concatenate([vp_ref[0], vc_ref[0], vn_ref[0]], axis=0)
    q_all = q_ref[0]
    kctx = kctx_ref[0]
    vctx = vctx_ref[0]
    heads = [slice(h * HEAD_DIM, (h + 1) * HEAD_DIM) for h in range(B_HEADS)]
    scores = [[_dot_t(q_all[:, sl], k_win[:, sl]) + bias_ref[0, h], _dot_t(q_all[:, sl], kctx[:, sl])]
              for h, sl in enumerate(heads)]
    for sl, s in zip(heads, scores):
        o = _attend(s, [_with_ones(v_win[:, sl]), _with_ones(vctx[:, sl])])
        out_ref[0, :, sl] = o.astype(BF16)


def neighbourhood_bias(rel_bias, n_lat):
    rows = n_lat // GRID_W
    kr_n = min(NA_ROWS, rows)
    n_blocks = rows // NB_Q_ROWS
    n_heads, n_dr, n_dc = rel_bias.shape
    cols = np.arange(GRID_W)
    c_start = np.clip(cols - NA_COLS // 2, 0, GRID_W - NA_COLS)
    col_ok = (cols[None, :] >= c_start[:, None]) & (cols[None, :] < c_start[:, None] + NA_COLS)
    dc = np.clip(cols[None, :] - cols[:, None], -(NA_COLS - 1), NA_COLS - 1) + NA_COLS - 1
    pick_dc = (dc.reshape(-1)[None, :] == np.arange(n_dc)[:, None]).astype(np.float32)
    toeplitz = jnp.dot(rel_bias.reshape(n_heads * n_dr, n_dc), pick_dc, precision=lax.Precision.HIGHEST)
    toeplitz = jnp.where(col_ok.reshape(-1), toeplitz, NEG_INF).reshape(n_heads, n_dr, GRID_W, GRID_W)
    q_rl = np.arange(NB_Q_ROWS)
    k_rl = np.arange(3 * NB_Q_ROWS)
    row_ok, dr = [], []
    for j in sorted({0, min(1, n_blocks - 1), n_blocks - 1}):
        r = NB_Q_ROWS * j + q_rl
        kr = NB_Q_ROWS * (j - 1) + k_rl
        r_start = np.clip(r - kr_n // 2, 0, rows - kr_n)
        ok = (kr[None, :] >= r_start[:, None]) & (kr[None, :] < r_start[:, None] + kr_n)
        row_ok.append(ok & (kr[None, :] >= 0) & (kr[None, :] < rows))
        dr.append(np.clip(kr[None, :] - r[:, None] + NA_ROWS - 1, 0, n_dr - 1))
    row_ok = np.stack(row_ok)
    dr = np.stack(dr)
    tiles = jnp.stack([toeplitz[:, int(i)] for i in dr.reshape(-1)], axis=1)
    tiles = tiles.reshape((n_heads,) + dr.shape + (GRID_W, GRID_W))
    tiles = jnp.where(row_ok[None, :, :, :, None, None], tiles, NEG_INF)
    table = tiles.transpose(1, 0, 2, 4, 3, 5).reshape(
        dr.shape[0], n_heads, NB_Q_ROWS * GRID_W, 3 * NB_Q_ROWS * GRID_W)
    return table, n_blocks


def neighbourhood_attention(qb, kb, vb, kc_b, vc_b, rel_bias):
    bsz, n_lat, _ = qb.shape
    n_ctx = kc_b.shape[1]
    table, n_blocks = neighbourhood_bias(rel_bias, n_lat)
    n_var = table.shape[0]
    tq = NB_Q_ROWS * GRID_W
    prev = lambda b, j: (b, jnp.maximum(j - 1, 0), 0)
    cur = lambda b, j: (b, j, 0)
    nxt = lambda b, j: (b, jnp.minimum(j + 1, n_blocks - 1), 0)
    ctx = lambda b, j: (b, 0, 0)

    def variant(b, j):
        v = jnp.where(j == 0, 0, jnp.where(j == n_blocks - 1, n_var - 1, min(1, n_var - 1)))
        return (v, 0, 0, 0)

    kv_specs = [pl.BlockSpec((1, tq, B_W), prev), pl.BlockSpec((1, tq, B_W), cur),
                pl.BlockSpec((1, tq, B_W), nxt)]
    return pl.pallas_call(
        _attn_b_kernel,
        grid=(bsz, n_blocks),
        in_specs=[pl.BlockSpec((1, tq, B_W), cur)] + kv_specs + kv_specs + [
            pl.BlockSpec((1, n_ctx, B_W), ctx), pl.BlockSpec((1, n_ctx, B_W), ctx),
            pl.BlockSpec((1, B_HEADS, tq, 3 * tq), variant)],
        out_specs=pl.BlockSpec((1, tq, B_W), cur),
        out_shape=jax.ShapeDtypeStruct((bsz, n_lat, B_W), BF16),
        compiler_params=_params("parallel", "parallel"),
        name="neighbourhood_attention",
    )(qb, kb, kb, kb, vb, vb, vb, kc_b, vc_b, table)


def _ctx_attn_kernel(sink_ref, qa_ref, qb_ref, ka_ref, va_ref, kb_ref, vb_ref, oa_ref, ob_ref):
    qa, qb = qa_ref[0], qb_ref[0]
    ka, va, kb, vb = ka_ref[0], va_ref[0], kb_ref[0], vb_ref[0]
    for hq in range(A_Q_HEADS):
        sl = slice(hq * HEAD_DIM, (hq + 1) * HEAD_DIM)
        hk = hq // A_GROUP
        kv = slice(hk * HEAD_DIM, (hk + 1) * HEAD_DIM)
        q = qa[:, sl]
        o = _attend([_dot_t(q, ka[:, kv])], [_with_ones(va[:, kv])], sink=sink_ref[hq])
        oa_ref[0, :, sl] = o.astype(BF16)
    for h in range(B_HEADS):
        sl = slice(h * HEAD_DIM, (h + 1) * HEAD_DIM)
        q = qb[:, sl]
        o = _attend([_dot_t(q, kb[:, sl])], [_with_ones(vb[:, sl])])
        ob_ref[0, :, sl] = o.astype(BF16)


def context_attention(qa, qb, ka, va, kb, vb, sink):
    bsz, n_ctx, _ = qa.shape
    blk = lambda w: pl.BlockSpec((1, n_ctx, w), lambda b, s: (b, 0, 0))
    grid_spec = pltpu.PrefetchScalarGridSpec(
        num_scalar_prefetch=1,
        grid=(bsz,),
        in_specs=[blk(A_Q_W), blk(B_W), blk(A_KV_W), blk(A_KV_W), blk(B_W), blk(B_W)],
        out_specs=[blk(A_Q_W), blk(B_W)],
    )
    return pl.pallas_call(
        _ctx_attn_kernel,
        grid_spec=grid_spec,
        out_shape=[jax.ShapeDtypeStruct((bsz, n_ctx, A_Q_W), BF16),
                   jax.ShapeDtypeStruct((bsz, n_ctx, B_W), BF16)],
        compiler_params=_params("parallel"),
        name="context_attention",
    )(sink, qa, qb, ka, va, kb, vb)


CONV_HALO = 16
F32_SUBLANES = 8
CONV_SHIFT_SPAN = (CONV_HALO + C_CONV_WIDTH // 2) // F32_SUBLANES * F32_SUBLANES


def _conv_kernel(prev_ref, cur_ref, next_ref, w_ref, b_ref, g_ref, beta_ref, out_ref, shifted_ref, *, ts):
    i = pl.program_id(1)
    n_i = pl.num_programs(1)
    ext = jnp.concatenate([jnp.where(i > 0, prev_ref[0], 0.0), cur_ref[0],
                           jnp.where(i < n_i - 1, next_ref[0], 0.0)], axis=0)
    for r in range(F32_SUBLANES):
        shifted_ref[r] = ext[r:r + ts + CONV_SHIFT_SPAN]
    acc = jnp.zeros((ts, C_CHANNELS), F32)
    for k in range(C_CONV_WIDTH):
        start = CONV_HALO - C_CONV_WIDTH // 2 + k
        aligned = start - start % F32_SUBLANES
        acc = acc + shifted_ref[start % F32_SUBLANES, aligned:aligned + ts] * w_ref[k:k + 1]
    y = _layer_norm(acc + b_ref[...]) * g_ref[...] + beta_ref[...]
    out_ref[0] = (y * jax.nn.sigmoid(y)).astype(BF16)


def conformer_conv(hc, conv_w, conv_b, ln_g, ln_b):
    bx, t, ch = hc.shape
    ts = min(512, t)
    per = ts // CONV_HALO
    last = t // CONV_HALO - 1
    row = lambda v: v.reshape(1, ch)
    const = lambda b, i: (0, 0)
    return pl.pallas_call(
        functools.partial(_conv_kernel, ts=ts),
        grid=(bx, t // ts),
        in_specs=[
            pl.BlockSpec((1, CONV_HALO, ch), lambda b, i: (b, jnp.maximum(i * per - 1, 0), 0)),
            pl.BlockSpec((1, ts, ch), lambda b, i: (b, i, 0)),
            pl.BlockSpec((1, CONV_HALO, ch), lambda b, i: (b, jnp.minimum((i + 1) * per, last), 0)),
            pl.BlockSpec((C_CONV_WIDTH, ch), const),
            pl.BlockSpec((1, ch), const), pl.BlockSpec((1, ch), const), pl.BlockSpec((1, ch), const),
        ],
        out_specs=pl.BlockSpec((1, ts, ch), lambda b, i: (b, i, 0)),
        out_shape=jax.ShapeDtypeStruct((bx, t, ch), BF16),
        scratch_shapes=[pltpu.VMEM((F32_SUBLANES, ts + CONV_SHIFT_SPAN, ch), F32)],
        compiler_params=_params("parallel", "parallel"),
        name="conformer_conv",
    )(hc, hc, hc, conv_w, row(conv_b), row(ln_g), row(ln_b))


OUTPROJ_ROW_CHUNKS = 2


def _outproj_kernel(oa_ref, ob_ref, oc_ref, x_ref, g1_ref, sh_ref, sc_ref, w_ref, lng_ref, lnb_ref,
                    wr_hi_ref, wr_lo_ref, x1_ref, h2_ref, afft_ref, aff_ref, *, alpha):
    tm, d = x_ref.shape[1], x_ref.shape[2]
    rc = tm // OUTPROJ_ROW_CHUNKS
    chunks = [slice(k * rc, (k + 1) * rc) for k in range(OUTPROJ_ROW_CHUNKS)]
    outs = [(_dot(oa_ref[0, r], w_ref[0:A_Q_W])
             + _dot(ob_ref[0, r], w_ref[A_Q_W:A_Q_W + B_W])
             + _dot(oc_ref[0, r], w_ref[A_Q_W + B_W:])) for r in chunks]
    w_hi, w_lo = wr_hi_ref[...], wr_lo_ref[...]
    eye = (lax.broadcasted_iota(jnp.int32, (N_EXPERTS, N_EXPERTS), 0)
           == lax.broadcasted_iota(jnp.int32, (N_EXPERTS, N_EXPERTS), 1)).astype(BF16)
    rows_per_token = d // LANES
    for k, (r, o) in enumerate(zip(chunks, outs)):
        y = _layer_norm(alpha * x_ref[0, r] + g1_ref[0] * o) * lng_ref[...] + lnb_ref[...]
        x1_ref[0, r] = y
        h2 = _layer_norm(y) * (1.0 + sc_ref[0]) + sh_ref[0]
        h_hi = h2.astype(BF16)
        for j in range(rows_per_token):
            h2_ref[0, pl.ds(k * rc * rows_per_token + j, rc, stride=rows_per_token), :] = (
                h2[:, j * LANES:(j + 1) * LANES])
        h_lo = (h2 - h_hi.astype(F32)).astype(BF16)
        logits = _dot_t(h_hi, w_hi) + _dot_t(h_lo, w_hi) + _dot_t(h_hi, w_lo)
        e_n = jnp.exp(logits - logits.max(axis=1, keepdims=True))
        aff = e_n / e_n.sum(axis=1, keepdims=True)
        aff_ref[0, r] = aff
        aff_t, rest = None, aff
        for _ in range(GATE_PARTS):
            part = rest.astype(BF16)
            term = _dot_t(eye, part)
            aff_t = term if aff_t is None else aff_t + term
            rest = rest - part.astype(F32)
        afft_ref[0, :, r] = aff_t


def out_projection(oa, ob, oc, x, g1, sh2, sc2, w_out_bf16, ln_g, ln_b, wr_hi, wr_lo, alpha):
    bx, t, d = x.shape
    tm = min(512, t)
    tok = lambda b, i: (b, i, 0)
    per_b = lambda b, i: (b, 0, 0)
    const = lambda b, i: (0, 0)
    vec = pl.BlockSpec((1, d), const)
    return pl.pallas_call(
        functools.partial(_outproj_kernel, alpha=alpha),
        grid=(bx, t // tm),
        in_specs=[
            pl.BlockSpec((1, tm, A_Q_W), tok), pl.BlockSpec((1, tm, B_W), tok),
            pl.BlockSpec((1, tm, C_CHANNELS), tok), pl.BlockSpec((1, tm, d), tok),
            pl.BlockSpec((1, 1, d), per_b), pl.BlockSpec((1, 1, d), per_b), pl.BlockSpec((1, 1, d), per_b),
            pl.BlockSpec(w_out_bf16.shape, const), vec, vec,
            pl.BlockSpec((N_EXPERTS, d), const), pl.BlockSpec((N_EXPERTS, d), const),
        ],
        out_specs=[pl.BlockSpec((1, tm, d), tok), pl.BlockSpec((1, tm * (d // LANES), LANES), tok),
                   pl.BlockSpec((1, N_EXPERTS, tm), lambda b, i: (b, 0, i)),
                   pl.BlockSpec((1, tm, N_EXPERTS), tok)],
        out_shape=[jax.ShapeDtypeStruct((bx, t, d), F32),
                   jax.ShapeDtypeStruct((bx, t * (d // LANES), LANES), F32),
                   jax.ShapeDtypeStruct((bx, N_EXPERTS, t), F32),
                   jax.ShapeDtypeStruct((bx, t, N_EXPERTS), F32)],
        compiler_params=_params("parallel", "parallel"),
        name="out_projection",
    )(oa, ob, oc, x, g1, sh2, sc2, w_out_bf16, ln_g.reshape(1, d), ln_b.reshape(1, d), wr_hi, wr_lo)


def _select_kernel(afft_ref, pos_ref, post_ref, off_ref, *, cap, n_tok):
    aff = afft_ref[0]

    def bit_step(j, bits):
        cand = bits | (jnp.int32(1) << (30 - j))
        cnt = jnp.sum((aff >= pltpu.bitcast(cand, F32)).astype(jnp.int32), axis=1, keepdims=True)
        return jnp.where(cnt >= cap, cand, bits)

    thr = pltpu.bitcast(lax.fori_loop(0, 31, bit_step, jnp.zeros((N_EXPERTS, 1), jnp.int32)), F32)
    above = (aff > thr).astype(F32)
    tied = (aff == thr).astype(F32)
    need = cap - jnp.sum(above, axis=1, keepdims=True)

    blk = MXU_DEPTH
    blocks = [slice(k * blk, (k + 1) * blk) for k in range(n_tok // blk)]
    r_i = lax.broadcasted_iota(jnp.int32, (blk, blk), 0)
    c_i = lax.broadcasted_iota(jnp.int32, (blk, blk), 1)
    strict_upper = (r_i < c_i).astype(BF16)
    eye = (r_i == c_i).astype(BF16)

    def running(block_sums):
        run, total = [], jnp.zeros((N_EXPERTS, 1), F32)
        for s in block_sums:
            run.append(total)
            total = total + s
        return run

    tied_b = [tied[:, sl] for sl in blocks]
    tied_rank = [_dot(t.astype(BF16), strict_upper) for t in tied_b]
    tied_before = running([t.sum(axis=1, keepdims=True) for t in tied_b])
    sel_b = [above[:, sl] + t * ((before + rank) < need).astype(F32)
             for sl, t, before, rank in zip(blocks, tied_b, tied_before, tied_rank)]
    sel_rank = [_dot(s.astype(BF16), strict_upper) for s in sel_b]
    half_sums = [[s[:, h * OFFSET_BLOCK:(h + 1) * OFFSET_BLOCK].sum(axis=1, keepdims=True)
                  for h in range(blk // OFFSET_BLOCK)] for s in sel_b]
    offs = running([h for hs in half_sums for h in hs])
    per_blk = blk // OFFSET_BLOCK
    for k, (sl, s, rank) in enumerate(zip(blocks, sel_b, sel_rank)):
        pos = jnp.where(s > 0.5, (offs[k * per_blk] + rank).astype(jnp.int32), -1)
        pos_ref[0, :, sl] = pos
        p1 = pos + 1
        hi = (p1 >> 5).astype(F32).astype(BF16)
        lo = (p1 & 31).astype(F32).astype(BF16)
        p1_t = _dot_t(eye, hi) * 32.0 + _dot_t(eye, lo)
        post_ref[0, sl, :] = p1_t - 1.0
    off_ref[0] = jnp.concatenate(offs, axis=1).astype(jnp.int32)


def expert_choice_select(aff_t, cap):
    bx, n_e, t = aff_t.shape
    n_tb = t // OFFSET_BLOCK
    return pl.pallas_call(
        functools.partial(_select_kernel, cap=cap, n_tok=t),
        grid=(bx,),
        in_specs=[pl.BlockSpec((1, n_e, t), lambda b: (b, 0, 0))],
        out_specs=[pl.BlockSpec((1, n_e, t), lambda b: (b, 0, 0)),
                   pl.BlockSpec((1, t, n_e), lambda b: (b, 0, 0)),
                   pl.BlockSpec((1, n_e, n_tb), lambda b: (b, 0, 0))],
        out_shape=[jax.ShapeDtypeStruct((bx, n_e, t), jnp.int32),
                   jax.ShapeDtypeStruct((bx, t, n_e), F32),
                   jax.ShapeDtypeStruct((bx, n_e, n_tb), jnp.int32)],
        compiler_params=_params("parallel"),
        name="expert_choice_select",
    )(aff_t)


GATE_PARTS = 3
TOKEN_LANE = GATE_PARTS


def _slot_table_kernel(off_ref, pos_ref, aff_ref, tbl_ref, *, n_off, blocks_per_step):
    b, eg, kc = pl.program_id(0), pl.program_id(1), pl.program_id(2)
    epg = EXPERTS_PER_GATHER_STEP

    @pl.when(kc == 0)
    def _():
        tbl_ref[...] = jnp.zeros_like(tbl_ref)

    slot = lax.broadcasted_iota(jnp.int32, (GATHER_WINDOW, 1), 0)
    slot_2d = lax.broadcasted_iota(jnp.int32, (GATHER_WINDOW, GATHER_BLOCK), 0)
    lane = lax.broadcasted_iota(jnp.int32, (GATHER_BLOCK, LANES), 1)
    out_lane = lax.broadcasted_iota(jnp.int32, (GATHER_WINDOW, LANES), 1)
    local_token = lax.broadcasted_iota(jnp.int32, (GATHER_BLOCK, LANES), 0).astype(F32)
    expert_lane = lax.broadcasted_iota(jnp.int32, (GATHER_BLOCK, N_EXPERTS), 1)
    for kk in range(blocks_per_step):
        kb = kc * blocks_per_step + kk
        aff_blk = aff_ref[0, kk * GATHER_BLOCK:(kk + 1) * GATHER_BLOCK]
        first_token = (kb * GATHER_BLOCK).astype(F32)
        for ee in range(epg):
            e = eg * epg + ee
            off = off_ref[(b * N_EXPERTS + e) * n_off + kb * (GATHER_BLOCK // OFFSET_BLOCK)]
            base = pl.multiple_of((off >> 4) << 4, BF16_SUBLANES)
            onehot = (pos_ref[0, ee, 0, kk:kk + 1, :] - base == slot_2d).astype(BF16)
            gate = jnp.sum(jnp.where(expert_lane == e, aff_blk, 0.0), axis=1, keepdims=True)
            payload, rest = jnp.where(lane == TOKEN_LANE, local_token, 0.0), gate
            for k in range(GATE_PARTS):
                part = rest.astype(BF16).astype(F32)
                payload = jnp.where(lane == k, part, payload)
                rest = rest - part
            gathered = _dot(onehot, payload.astype(BF16)) + jnp.where(out_lane == TOKEN_LANE, first_token, 0.0)
            win = pl.ds(base, GATHER_WINDOW)
            tbl_ref[0, ee, win, :] = jnp.where(slot >= off - base, gathered, tbl_ref[0, ee, win, :])


def slot_table(aff, pos, block_off, cap_pad):
    bx, t, _ = aff.shape
    n_tb = t // GATHER_BLOCK
    blocks_per_step = min(4, n_tb)
    n_steps = n_tb // blocks_per_step
    epg = EXPERTS_PER_GATHER_STEP
    tokens = blocks_per_step * GATHER_BLOCK
    pos5 = pos.reshape(bx, N_EXPERTS, n_steps, blocks_per_step, GATHER_BLOCK)
    grid_spec = pltpu.PrefetchScalarGridSpec(
        num_scalar_prefetch=1,
        grid=(bx, N_EXPERTS // epg, n_steps),
        in_specs=[
            pl.BlockSpec((1, epg, 1, blocks_per_step, GATHER_BLOCK), lambda b, g, k, s: (b, g, k, 0, 0)),
            pl.BlockSpec((1, tokens, N_EXPERTS), lambda b, g, k, s: (b, k, 0)),
        ],
        out_specs=pl.BlockSpec((1, epg, cap_pad, LANES), lambda b, g, k, s: (b, g, 0, 0)),
    )
    return pl.pallas_call(
        functools.partial(_slot_table_kernel, n_off=t // OFFSET_BLOCK, blocks_per_step=blocks_per_step),
        grid_spec=grid_spec,
        out_shape=jax.ShapeDtypeStruct((bx, N_EXPERTS, cap_pad, LANES), F32),
        compiler_params=_params("parallel", "parallel", "arbitrary"),
        name="slot_table",
    )(block_off.reshape(-1), pos5, aff)


GATHER_UNROLL = 8


def _row_gather_kernel(idx_ref, src_ref, xs_ref, tile_ref, *, cap, rows_per_token, chunk_stride):
    b, e = pl.program_id(0), pl.program_id(1)
    first = (b * N_EXPERTS + e) * cap

    def group(g, carry):
        for u in range(GATHER_UNROLL):
            s = g * GATHER_UNROLL + u
            row = pl.multiple_of(idx_ref[first + s] * rows_per_token, rows_per_token)
            tile_ref[pl.ds(s, rows_per_token, stride=chunk_stride), :] = src_ref[0, pl.ds(row, rows_per_token), :]
        return carry

    lax.fori_loop(0, cap // GATHER_UNROLL, group, 0)
    for j in range(rows_per_token):
        xs_ref[0, 0, :, j * LANES:(j + 1) * LANES] = tile_ref[j * chunk_stride:j * chunk_stride + cap, :].astype(BF16)


def gather_rows(h2_rows, token_idx, cap, d):
    bx, n_rows, _ = h2_rows.shape
    rows_per_token = d // LANES
    chunk_stride = cap + 8
    grid_spec = pltpu.PrefetchScalarGridSpec(
        num_scalar_prefetch=1,
        grid=(bx, N_EXPERTS),
        in_specs=[pl.BlockSpec((1, n_rows, LANES), lambda b, e, s: (b, 0, 0), pipeline_mode=pl.Buffered(1))],
        out_specs=pl.BlockSpec((1, 1, cap, d), lambda b, e, s: (b, e, 0, 0)),
        scratch_shapes=[pltpu.VMEM((rows_per_token * chunk_stride, LANES), F32)],
    )
    return pl.pallas_call(
        functools.partial(_row_gather_kernel, cap=cap, rows_per_token=rows_per_token, chunk_stride=chunk_stride),
        grid_spec=grid_spec,
        out_shape=jax.ShapeDtypeStruct((bx, N_EXPERTS, cap, d), BF16),
        compiler_params=_params("parallel", "arbitrary"),
        name="gather_rows",
    )(token_idx, h2_rows)


def _ffn_kernel(xs_ref, gs_ref, wg_ref, wu_ref, wd_ref, ye_ref, wg_bf, wu_bf, wd_bf, *, cap, row_tile):
    @pl.when(pl.program_id(1) == 0)
    def _():
        wg_bf[...] = wg_ref[0, 0].astype(BF16)
        wu_bf[...] = wu_ref[0, 0].astype(BF16)
        wd_bf[...] = wd_ref[0, 0].astype(BF16)

    n_b = xs_ref.shape[0]
    if n_b == 1:
        tiles = [[(0, r0, row_tile)] for r0 in range(0, cap, row_tile)]
    else:
        tiles = [[(bb, 0, cap) for bb in range(n_b)]]
    for tile in tiles:
        x = jnp.concatenate([xs_ref[bb, 0, r0:r0 + n] for bb, r0, n in tile], axis=0)
        terms = jnp.concatenate([gs_ref[bb, 0, r0:r0 + n, 0:GATE_PARTS] for bb, r0, n in tile], axis=0)
        gate = _dot(x, wg_bf[...])
        up = _dot(x, wu_bf[...])
        hid = (gate * jax.nn.sigmoid(gate) * up).astype(BF16)
        g = jnp.sum(terms, axis=1, keepdims=True)
        ye = (_dot(hid, wd_bf[...]) * g).astype(BF16)
        row = 0
        for bb, r0, n in tile:
            ye_ref[bb, 0, r0:r0 + n] = ye[row:row + n]
            row += n
    for bb in range(n_b):
        ye_ref[bb, 0, cap:] = jnp.zeros((ye_ref.shape[2] - cap, ye_ref.shape[3]), BF16)


def expert_ffn(xs, gs, wg, wu, wd, layer, cap_pad):
    bx, n_e, cap, d = xs.shape
    ff = wg.shape[-1]
    max_rows = 512
    row_tile = min(max_rows, cap)
    n_b = bx if bx * cap <= max_rows else 1
    return pl.pallas_call(
        functools.partial(_ffn_kernel, cap=cap, row_tile=row_tile),
        grid=(n_e, bx // n_b),
        in_specs=[
            pl.BlockSpec((n_b, 1, cap, d), lambda e, b: (b, e, 0, 0)),
            pl.BlockSpec((n_b, 1, cap, LANES), lambda e, b: (b, e, 0, 0)),
            pl.BlockSpec((1, 1, d, ff), lambda e, b: (layer, e, 0, 0)),
            pl.BlockSpec((1, 1, d, ff), lambda e, b: (layer, e, 0, 0)),
            pl.BlockSpec((1, 1, ff, d), lambda e, b: (layer, e, 0, 0)),
        ],
        out_specs=pl.BlockSpec((n_b, 1, cap_pad, d), lambda e, b: (b, e, 0, 0)),
        out_shape=jax.ShapeDtypeStruct((bx, n_e, cap_pad, d), BF16),
        scratch_shapes=[pltpu.VMEM((d, ff), BF16), pltpu.VMEM((d, ff), BF16), pltpu.VMEM((ff, d), BF16)],
        compiler_params=_params("parallel", "arbitrary"),
        name="expert_ffn",
    )(xs, gs, wg, wu, wd)


EXPERTS_PER_SCATTER_STEP = 2


def _combine_kernel(off_ref, post_ref, ye_ref, x1_ref, g2_ref, lng_ref, lnb_ref,
                    out_ref, acc_ref, *, n_tb, blocks_per_step, alpha):
    b, tq, eg = pl.program_id(0), pl.program_id(1), pl.program_id(2)
    eps = EXPERTS_PER_SCATTER_STEP

    @pl.when(eg == 0)
    def _():
        acc_ref[...] = jnp.zeros_like(acc_ref)

    expert_lane = lax.broadcasted_iota(jnp.int32, (SCATTER_BLOCK, N_EXPERTS), 1)
    slot = lax.broadcasted_iota(jnp.int32, (SCATTER_BLOCK, SCATTER_WINDOW), 1).astype(F32)
    for kk in range(blocks_per_step):
        kb = tq * blocks_per_step + kk
        rows = slice(kk * SCATTER_BLOCK, (kk + 1) * SCATTER_BLOCK)
        post_blk = post_ref[0, rows, :]
        onehots, windows = [], []
        for ee in range(eps):
            e = eg * eps + ee
            off = off_ref[(b * N_EXPERTS + e) * n_tb + kb * (SCATTER_BLOCK // OFFSET_BLOCK)]
            base = pl.multiple_of((off >> 4) << 4, BF16_SUBLANES)
            pcol = jnp.sum(jnp.where(expert_lane == e, post_blk, 0.0), axis=1, keepdims=True)
            onehots.append((pcol - base.astype(F32) == slot).astype(BF16))
            windows.append(ye_ref[0, ee, pl.ds(base, SCATTER_WINDOW), :])
        acc_ref[rows, :] += _dot(jnp.concatenate(onehots, axis=1), jnp.concatenate(windows, axis=0))

    @pl.when(eg == N_EXPERTS // eps - 1)
    def _():
        y = alpha * x1_ref[0] + g2_ref[0] * acc_ref[...]
        out_ref[0] = _layer_norm(y) * lng_ref[...] + lnb_ref[...]


def combine_and_norm(ye, pos_t, block_off, x1, g2, ln_g, ln_b, alpha):
    bx, t, d = x1.shape
    cap_pad = ye.shape[2]
    n_tb = t // OFFSET_BLOCK
    blocks_per_step = min(16, t // SCATTER_BLOCK)
    tq = blocks_per_step * SCATTER_BLOCK
    eps = EXPERTS_PER_SCATTER_STEP
    tok = lambda b, q, e, s: (b, q, 0)
    const = lambda b, q, e, s: (0, 0)
    grid_spec = pltpu.PrefetchScalarGridSpec(
        num_scalar_prefetch=1,
        grid=(bx, t // tq, N_EXPERTS // eps),
        in_specs=[
            pl.BlockSpec((1, tq, N_EXPERTS), tok),
            pl.BlockSpec((1, eps, cap_pad, d), lambda b, q, e, s: (b, e, 0, 0)),
            pl.BlockSpec((1, tq, d), tok, pipeline_mode=pl.Buffered(1)),
            pl.BlockSpec((1, 1, d), lambda b, q, e, s: (b, 0, 0)),
            pl.BlockSpec((1, d), const), pl.BlockSpec((1, d), const),
        ],
        out_specs=pl.BlockSpec((1, tq, d), tok),
        scratch_shapes=[pltpu.VMEM((tq, d), F32)],
    )
    return pl.pallas_call(
        functools.partial(_combine_kernel, n_tb=n_tb, blocks_per_step=blocks_per_step, alpha=alpha),
        grid_spec=grid_spec,
        out_shape=jax.ShapeDtypeStruct((bx, t, d), F32),
        compiler_params=_params("parallel", "parallel", "arbitrary"),
        name="combine_and_norm",
    )(block_off.reshape(-1), pos_t, ye, x1, g2, ln_g.reshape(1, d), ln_b.reshape(1, d))


def _split_bf16(w):
    hi = w.astype(BF16)
    return hi, (w - hi.astype(F32)).astype(BF16)


def _mixer_tail(oa, ob, oc, x, mod, lw, alpha):
    g1, sh2, sc2, g2 = mod
    t = x.shape[1]
    cap = EC_CAPACITY * t // N_EXPERTS
    cap_pad = cap + SLOT_PAD
    x1, h2_rows, aff_t, aff = out_projection(oa, ob, oc, x, g1, sh2, sc2, lw["w_out"], lw["ln1_g"], lw["ln1_b"],
                                        lw["wr_hi"], lw["wr_lo"], alpha)
    pos, pos_t, block_off = expert_choice_select(aff_t, cap)
    table = slot_table(aff, pos, block_off, cap_pad)
    token_idx = table[:, :, :cap, TOKEN_LANE].astype(jnp.int32).reshape(-1)
    xs = gather_rows(h2_rows, token_idx, cap, x.shape[2])
    ye = expert_ffn(xs, table, lw["w_gate"], lw["w_up"], lw["w_down"], lw["layer"], cap_pad)
    return combine_and_norm(ye, pos_t, block_off, x1, g2, lw["ln2_g"], lw["ln2_b"], alpha)


def kernel(x, c, ctx, c_ctx, w_mod, b_mod, w_in, a_sink, nat_bias, conv_w, conv_b, conv_ln_g, conv_ln_b,
           w_out, ln1_g, ln1_b, w_router, w_gate, w_up, w_down, ln2_g, ln2_b):
    bsz, n_lat, d = x.shape
    depth = w_mod.shape[0]
    alpha = (2 * depth) ** 0.25
    cos_t, sin_t = rope_tables(n_lat)

    cond = jnp.concatenate([c, c_ctx[None, :], jnp.zeros((8 - bsz - 1, d), F32)], axis=0)
    mods = adaln_all(cond, w_mod, b_mod)

    for l in range(depth):
        last = l == depth - 1
        wr_hi, wr_lo = _split_bf16(w_router[l].T)
        lw = dict(w_out=w_out[l].astype(BF16), ln1_g=ln1_g[l], ln1_b=ln1_b[l], wr_hi=wr_hi, wr_lo=wr_lo,
                  w_gate=w_gate, w_up=w_up, w_down=w_down, layer=l,
                  ln2_g=ln2_g[l], ln2_b=ln2_b[l])
        w_in_l = w_in[l].astype(BF16)
        lat = [mods[l, :bsz, k * d:(k + 1) * d][:, None, :] for k in range(N_MOD)]
        cm = [jnp.broadcast_to(mods[l, bsz, k * d:(k + 1) * d][None, None, :], (bsz, 1, d))
              for k in range(N_MOD)]
        conv_args = (conv_w[l], conv_b[l], conv_ln_g[l], conv_ln_b[l])

        qa_c, ka_c, va_c, qb_c, kb_c, vb_c, hc_c = in_projection(ctx, cm[0], cm[1], w_in_l, cos_t, sin_t, rope=False)
        if not last:
            oa_c, ob_c = context_attention(qa_c, qb_c, ka_c, va_c, kb_c, vb_c, a_sink[l])
            oc_c = conformer_conv(hc_c, *conv_args)
            ctx_new = _mixer_tail(oa_c, ob_c, oc_c, ctx, (cm[2], cm[3], cm[4], cm[5]), lw, alpha)

        qa, ka, va, qb, kb, vb, hc = in_projection(x, lat[0], lat[1], w_in_l, cos_t, sin_t, rope=True)
        oa = window_attention(qa, ka, va, ka_c, va_c, a_sink[l])
        ob = neighbourhood_attention(qb, kb, vb, kb_c, vb_c, nat_bias[l])
        oc = conformer_conv(hc, *conv_args)
        x = _mixer_tail(oa, ob, oc, x, (lat[2], lat[3], lat[4], lat[5]), lw, alpha)
        if not last:
            ctx = ctx_new
    return x
```

```python
import functools

import numpy as np
import jax
import jax.numpy as jnp
from jax import lax
from jax.experimental import pallas as pl
from jax.experimental.pallas import tpu as pltpu

HEAD_DIM = 64
GRID_W = 64
A_Q_HEADS = 8
A_KV_HEADS = 2
A_GROUP = A_Q_HEADS // A_KV_HEADS
A_WINDOW = 128
B_HEADS = 4
NA_ROWS = 8
NA_COLS = 16
C_CHANNELS = 256
C_CONV_WIDTH = 31
A_Q_W = A_Q_HEADS * HEAD_DIM
A_KV_W = A_KV_HEADS * HEAD_DIM
B_W = B_HEADS * HEAD_DIM
OFF_AK = A_Q_W
OFF_AV = OFF_AK + A_KV_W
OFF_BQ = OFF_AV + A_KV_W
OFF_BK = OFF_BQ + B_W
OFF_BV = OFF_BK + B_W
OFF_C = OFF_BV + B_W
IN_WIDTH = OFF_C + 2 * C_CHANNELS
ROPE_WIDTH = A_Q_W + A_KV_W
N_EXPERTS = 16
EC_CAPACITY = 2
ROPE_BASE = 10000.0
LN_EPS = 1e-6
N_MOD = 6
NEG_INF = -1e30
QK_SCALE = HEAD_DIM ** -0.5

LANES = 128
WINDOW_ALIGN = 16
MXU_DEPTH = 256
OFFSET_BLOCK = 128
GATHER_BLOCK = MXU_DEPTH
GATHER_WINDOW = GATHER_BLOCK + WINDOW_ALIGN
SLOT_PAD = 3 * LANES
EXPERTS_PER_GATHER_STEP = 4
VMEM_LIMIT = 56 * 1024 * 1024

F32 = jnp.float32
BF16 = jnp.bfloat16


def _dot(a, b):
    return jnp.dot(a, b, preferred_element_type=F32)


def _dot_t(a, b):
    return lax.dot_general(a, b, (((1,), (1,)), ((), ())), preferred_element_type=F32)


def _layer_norm(x):
    mu = jnp.mean(x, axis=-1, keepdims=True)
    xc = x - mu
    var = jnp.mean(xc * xc, axis=-1, keepdims=True)
    return xc * lax.rsqrt(var + LN_EPS)


def _params(*sem):
    return pltpu.CompilerParams(dimension_semantics=sem, vmem_limit_bytes=VMEM_LIMIT)


def _mod_kernel(cond_ref, w_ref, b_ref, out_ref):
    cnd = cond_ref[...]
    act = cnd * jax.nn.sigmoid(cnd)
    out_ref[0] = jnp.dot(act, w_ref[0], preferred_element_type=F32,
                         precision=lax.Precision.HIGHEST) + b_ref[0]


def adaln_all(cond, w_mod, b_mod):
    n_layers, d, width = w_mod.shape
    rows = cond.shape[0]
    tn = 1536
    return pl.pallas_call(
        _mod_kernel,
        grid=(n_layers, width // tn),
        in_specs=[
            pl.BlockSpec((rows, d), lambda l, j: (0, 0)),
            pl.BlockSpec((1, d, tn), lambda l, j: (l, 0, j)),
            pl.BlockSpec((1, 1, tn), lambda l, j: (l, 0, j)),
        ],
        out_specs=pl.BlockSpec((1, rows, tn), lambda l, j: (l, 0, j)),
        out_shape=jax.ShapeDtypeStruct((n_layers, rows, width), F32),
        compiler_params=_params("parallel", "parallel"),
        name="adaln",
    )(cond, w_mod, b_mod.reshape(n_layers, 1, width))


def _inproj_kernel(x_ref, sh_ref, sc_ref, w_ref, cos_ref, sin_ref,
                   qa_ref, ka_ref, va_ref, qb_ref, kb_ref, vb_ref, hc_ref, *, rope):
    x = x_ref[0]
    h = _layer_norm(x) * (1.0 + sc_ref[0]) + sh_ref[0]
    u = _dot(h.astype(BF16), w_ref[...])

    def rotated(col):
        xq = u[:, col:col + LANES]
        if not rope:
            return xq
        lane = lax.broadcasted_iota(jnp.int32, xq.shape, 1)
        first = (lane & (HEAD_DIM // 2 - 1)) < (HEAD_DIM // 4)
        partner = jnp.where(first, pltpu.roll(xq, LANES - HEAD_DIM // 4, 1),
                            pltpu.roll(xq, HEAD_DIM // 4, 1))
        return xq * cos_ref[...] + partner * sin_ref[...]

    rot = [rotated(col) for col in range(0, ROPE_WIDTH, LANES)]
    n_q = A_Q_W // LANES
    qa_ref[0] = (jnp.concatenate(rot[:n_q], axis=1) * QK_SCALE).astype(BF16)
    ka_ref[0] = jnp.concatenate(rot[n_q:], axis=1).astype(BF16)
    va_ref[0] = u[:, OFF_AV:OFF_BQ].astype(BF16)
    qb_ref[0] = (u[:, OFF_BQ:OFF_BK] * QK_SCALE).astype(BF16)
    kb_ref[0] = u[:, OFF_BK:OFF_BV].astype(BF16)
    vb_ref[0] = u[:, OFF_BV:OFF_C].astype(BF16)
    a = u[:, OFF_C:OFF_C + C_CHANNELS]
    gate = u[:, OFF_C + C_CHANNELS:]
    hc_ref[0] = a * jax.nn.sigmoid(gate)


def in_projection(x, shift, scale, w_in_bf16, cos_t, sin_t, *, rope):
    bx, t, d = x.shape
    tm = min(512, t)
    widths = (A_Q_W, A_KV_W, A_KV_W, B_W, B_W, B_W, C_CHANNELS)
    dtypes = (BF16,) * 6 + (F32,)
    tok = lambda b, i: (b, i, 0)
    per_b = lambda b, i: (b, 0, 0)
    return pl.pallas_call(
        functools.partial(_inproj_kernel, rope=rope),
        grid=(bx, t // tm),
        in_specs=[
            pl.BlockSpec((1, tm, d), tok),
            pl.BlockSpec((1, 1, d), per_b),
            pl.BlockSpec((1, 1, d), per_b),
            pl.BlockSpec((d, IN_WIDTH), lambda b, i: (0, 0)),
            pl.BlockSpec((tm, LANES), lambda b, i: (i, 0)),
            pl.BlockSpec((tm, LANES), lambda b, i: (i, 0)),
        ],
        out_specs=[pl.BlockSpec((1, tm, w), tok) for w in widths],
        out_shape=[jax.ShapeDtypeStruct((bx, t, w), dt) for w, dt in zip(widths, dtypes)],
        compiler_params=_params("parallel", "parallel"),
        name="in_projection",
    )(x, shift, scale, w_in_bf16, cos_t, sin_t)


def rope_tables(n_tokens):
    t = jnp.arange(n_tokens, dtype=jnp.int32)
    row = (t // GRID_W).astype(F32)[:, None]
    col = (t % GRID_W).astype(F32)[:, None]
    n_freq = HEAD_DIM // 4
    inv_freq = ROPE_BASE ** (-jnp.arange(n_freq, dtype=F32) / n_freq)
    ang_r = row * inv_freq
    ang_c = col * inv_freq
    cos_h = jnp.concatenate([jnp.cos(ang_r), jnp.cos(ang_r), jnp.cos(ang_c), jnp.cos(ang_c)], axis=1)
    sin_h = jnp.concatenate([-jnp.sin(ang_r), jnp.sin(ang_r), -jnp.sin(ang_c), jnp.sin(ang_c)], axis=1)
    reps = LANES // HEAD_DIM
    return jnp.tile(cos_h, (1, reps)), jnp.tile(sin_h, (1, reps))


def _with_ones(v):
    return jnp.concatenate([v, jnp.ones_like(v)], axis=1)


def _attend(score_parts, values, sink=None):
    m = score_parts[0].max(axis=-1, keepdims=True)
    for s in score_parts[1:]:
        m = jnp.maximum(m, s.max(axis=-1, keepdims=True))
    if sink is not None:
        m = jnp.maximum(m, sink)
    acc = None
    for s, v in zip(score_parts, values):
        term = _dot(jnp.exp((s - m).astype(BF16)), v)
        acc = term if acc is None else acc + term
    den = acc[:, HEAD_DIM:HEAD_DIM + 1]
    if sink is not None:
        den = den + jnp.exp(sink - m)
    return acc[:, :HEAD_DIM] / den


def _attn_a_kernel(sink_ref, q_ref, kp_ref, kc_ref, kn_ref, vp_ref, vc_ref, vn_ref,
                   kctx_ref, vctx_ref, out_ref, *, n_lat, tq):
    i = pl.program_id(1)
    k_win = jnp.concatenate([kp_ref[0], kc_ref[0], kn_ref[0]], axis=0)
    v_win = jnp.concatenate([vp_ref[0], vc_ref[0], vn_ref[0]], axis=0)
    kctx = kctx_ref[0]
    vctx = vctx_ref[0]
    sub = A_WINDOW
    span = 3 * A_WINDOW
    rows = A_GROUP * sub
    row_i = lax.broadcasted_iota(jnp.int32, (rows, span), 0)
    col_i = lax.broadcasted_iota(jnp.int32, (rows, span), 1)
    rel = col_i - A_WINDOW - (row_i & (sub - 1))
    in_band = (rel <= A_WINDOW) & (rel >= -A_WINDOW)
    group_of_row = lax.broadcasted_iota(jnp.int32, (rows, 1), 0) >> (sub.bit_length() - 1)
    sinks = []
    for hk in range(A_KV_HEADS):
        sink = jnp.zeros((rows, 1), F32)
        for g in range(A_GROUP):
            sink = jnp.where(group_of_row == g, sink_ref[hk * A_GROUP + g], sink)
        sinks.append(sink)
    vctx_ext = [_with_ones(vctx[:, hk * HEAD_DIM:(hk + 1) * HEAD_DIM]) for hk in range(A_KV_HEADS)]
    v_ext = [_with_ones(v_win[:, hk * HEAD_DIM:(hk + 1) * HEAD_DIM]) for hk in range(A_KV_HEADS)]
    def scores(j, hk):
        kpos = i * tq + j * sub - A_WINDOW + col_i
        valid = in_band & (kpos >= 0) & (kpos < n_lat)
        sl = slice(hk * HEAD_DIM, (hk + 1) * HEAD_DIM)
        q_rows = q_ref[0, j * sub:(j + 1) * sub]
        q = jnp.concatenate([q_rows[:, h * HEAD_DIM:(h + 1) * HEAD_DIM]
                             for h in range(hk * A_GROUP, (hk + 1) * A_GROUP)], axis=0)
        s_win = jnp.where(valid, _dot_t(q, k_win[j * sub:j * sub + span, sl]), NEG_INF)
        return [s_win, _dot_t(q, kctx[:, sl])]

    units = [(j, hk) for j in range(tq // sub) for hk in range(A_KV_HEADS)]
    all_scores = [scores(j, hk) for j, hk in units]
    for (j, hk), s in zip(units, all_scores):
        v_sub = v_ext[hk][j * sub:j * sub + span]
        o = _attend(s, [v_sub, vctx_ext[hk]], sink=sinks[hk]).astype(BF16)
        for g in range(A_GROUP):
            h = hk * A_GROUP + g
            out_ref[0, j * sub:(j + 1) * sub, h * HEAD_DIM:(h + 1) * HEAD_DIM] = o[g * sub:(g + 1) * sub]


def window_attention(qa, ka, va, kc_a, vc_a, sink):
    bsz, n_lat, _ = qa.shape
    n_ctx = kc_a.shape[1]
    tq = min(512, n_lat)
    w = A_WINDOW
    per = tq // w
    last = n_lat // w - 1
    prev = lambda b, i, s: (b, jnp.maximum(i * per - 1, 0), 0)
    cur = lambda b, i, s: (b, i, 0)
    nxt = lambda b, i, s: (b, jnp.minimum((i + 1) * per, last), 0)
    ctx = lambda b, i, s: (b, 0, 0)
    kv_specs = [pl.BlockSpec((1, w, A_KV_W), prev), pl.BlockSpec((1, tq, A_KV_W), cur),
                pl.BlockSpec((1, w, A_KV_W), nxt)]
    grid_spec = pltpu.PrefetchScalarGridSpec(
        num_scalar_prefetch=1,
        grid=(bsz, n_lat // tq),
        in_specs=[pl.BlockSpec((1, tq, A_Q_W), cur)] + kv_specs + kv_specs + [
            pl.BlockSpec((1, n_ctx, A_KV_W), ctx), pl.BlockSpec((1, n_ctx, A_KV_W), ctx)],
        out_specs=pl.BlockSpec((1, tq, A_Q_W), cur),
    )
    return pl.pallas_call(
        functools.partial(_attn_a_kernel, n_lat=n_lat, tq=tq),
        grid_spec=grid_spec,
        out_shape=jax.ShapeDtypeStruct((bsz, n_lat, A_Q_W), BF16),
        compiler_params=_params("parallel", "parallel"),
        name="window_attention",
    )(sink, qa, ka, ka, ka, va, va, va, kc_a, vc_a)


NB_Q_ROWS = 4


def _attn_b_kernel(q_ref, kp_ref, kc_ref, kn_ref, vp_ref, vc_ref, vn_ref,
                   kctx_ref, vctx_ref, bias_ref, out_ref):
    k_win = jnp.concatenate([kp_ref[0], kc_ref[0], kn_ref[0]], axis=0)
    v_win = jnp.concatenate([vp_ref[0], vc_ref[0], vn_ref[0]], axis=0)
    q_all = q_ref[0]
    kctx = kctx_ref[0]
    vctx = vctx_ref[0]
    heads = [slice(h * HEAD_DIM, (h + 1) * HEAD_DIM) for h in range(B_HEADS)]
    scores = [[_dot_t(q_all[:, sl], k_win[:, sl]) + bias_ref[0, h], _dot_t(q_all[:, sl], kctx[:, sl])]
              for h, sl in enumerate(heads)]
    for sl, s in zip(heads, scores):
        o = _attend(s, [_with_ones(v_win[:, sl]), _with_ones(vctx[:, sl])])
        out_ref[0, :, sl] = o.astype(BF16)


def neighbourhood_bias(rel_bias, n_lat):
    rows = n_lat // GRID_W
    kr_n = min(NA_ROWS, rows)
    n_blocks = rows // NB_Q_ROWS
    n_heads, n_dr, n_dc = rel_bias.shape
    cols = np.arange(GRID_W)
    c_start = np.clip(cols - NA_COLS // 2, 0, GRID_W - NA_COLS)
    col_ok = (cols[None, :] >= c_start[:, None]) & (cols[None, :] < c_start[:, None] + NA_COLS)
    dc = np.clip(cols[None, :] - cols[:, None], -(NA_COLS - 1), NA_COLS - 1) + NA_COLS - 1
    pick_dc = (dc.reshape(-1)[None, :] == np.arange(n_dc)[:, None]).astype(np.float32)
    toeplitz = jnp.dot(rel_bias.reshape(n_heads * n_dr, n_dc), pick_dc, precision=lax.Precision.HIGHEST)
    toeplitz = jnp.where(col_ok.reshape(-1), toeplitz, NEG_INF).reshape(n_heads, n_dr, GRID_W, GRID_W)
    q_rl = np.arange(NB_Q_ROWS)
    k_rl = np.arange(3 * NB_Q_ROWS)
    row_ok, dr = [], []
    for j in sorted({0, min(1, n_blocks - 1), n_blocks - 1}):
        r = NB_Q_ROWS * j + q_rl
        kr = NB_Q_ROWS * (j - 1) + k_rl
        r_start = np.clip(r - kr_n // 2, 0, rows - kr_n)
        ok = (kr[None, :] >= r_start[:, None]) & (kr[None, :] < r_start[:, None] + kr_n)
        row_ok.append(ok & (kr[None, :] >= 0) & (kr[None, :] < rows))
        dr.append(np.clip(kr[None, :] - r[:, None] + NA_ROWS - 1, 0, n_dr - 1))
    row_ok = np.stack(row_ok)
    dr = np.stack(dr)
    tiles = jnp.stack([toeplitz[:, int(i)] for i in dr.reshape(-1)], axis=1)
    tiles = tiles.reshape((n_heads,) + dr.shape + (GRID_W, GRID_W))
    tiles = jnp.where(row_ok[None, :, :, :, None, None], tiles, NEG_INF)
    table = tiles.transpose(1, 0, 2, 4, 3, 5).reshape(
        dr.shape[0], n_heads, NB_Q_ROWS * GRID_W, 3 * NB_Q_ROWS * GRID_W)
    return table, n_blocks


def neighbourhood_attention(qb, kb, vb, kc_b, vc_b, rel_bias):
    bsz, n_lat, _ = qb.shape
    n_ctx = kc_b.shape[1]
    table, n_blocks = neighbourhood_bias(rel_bias, n_lat)
    n_var = table.shape[0]
    tq = NB_Q_ROWS * GRID_W
    prev = lambda b, j: (b, jnp.maximum(j - 1, 0), 0)
    cur = lambda b, j: (b, j, 0)
    nxt = lambda b, j: (b, jnp.minimum(j + 1, n_blocks - 1), 0)
    ctx = lambda b, j: (b, 0, 0)

    def variant(b, j):
        v = jnp.where(j == 0, 0, jnp.where(j == n_blocks - 1, n_var - 1, min(1, n_var - 1)))
        return (v, 0, 0, 0)

    kv_specs = [pl.BlockSpec((1, tq, B_W), prev), pl.BlockSpec((1, tq, B_W), cur),
                pl.BlockSpec((1, tq, B_W), nxt)]
    return pl.pallas_call(
        _attn_b_kernel,
        grid=(bsz, n_blocks),
        in_specs=[pl.BlockSpec((1, tq, B_W), cur)] + kv_specs + kv_specs + [
            pl.BlockSpec((1, n_ctx, B_W), ctx), pl.BlockSpec((1, n_ctx, B_W), ctx),
            pl.BlockSpec((1, B_HEADS, tq, 3 * tq), variant)],
        out_specs=pl.BlockSpec((1, tq, B_W), cur),
        out_shape=jax.ShapeDtypeStruct((bsz, n_lat, B_W), BF16),
        compiler_params=_params("parallel", "parallel"),
        name="neighbourhood_attention",
    )(qb, kb, kb, kb, vb, vb, vb, kc_b, vc_b, table)


def _ctx_attn_kernel(sink_ref, qa_ref, qb_ref, ka_ref, va_ref, kb_ref, vb_ref, oa_ref, ob_ref):
    qa, qb = qa_ref[0], qb_ref[0]
    ka, va, kb, vb = ka_ref[0], va_ref[0], kb_ref[0], vb_ref[0]
    for hq in range(A_Q_HEADS):
        sl = slice(hq * HEAD_DIM, (hq + 1) * HEAD_DIM)
        hk = hq // A_GROUP
        kv = slice(hk * HEAD_DIM, (hk + 1) * HEAD_DIM)
        q = qa[:, sl]
        o = _attend([_dot_t(q, ka[:, kv])], [_with_ones(va[:, kv])], sink=sink_ref[hq])
        oa_ref[0, :, sl] = o.astype(BF16)
    for h in range(B_HEADS):
        sl = slice(h * HEAD_DIM, (h + 1) * HEAD_DIM)
        q = qb[:, sl]
        o = _attend([_dot_t(q, kb[:, sl])], [_with_ones(vb[:, sl])])
        ob_ref[0, :, sl] = o.astype(BF16)


def context_attention(qa, qb, ka, va, kb, vb, sink):
    bsz, n_ctx, _ = qa.shape
    blk = lambda w: pl.BlockSpec((1, n_ctx, w), lambda b, s: (b, 0, 0))
    grid_spec = pltpu.PrefetchScalarGridSpec(
        num_scalar_prefetch=1,
        grid=(bsz,),
        in_specs=[blk(A_Q_W), blk(B_W), blk(A_KV_W), blk(A_KV_W), blk(B_W), blk(B_W)],
        out_specs=[blk(A_Q_W), blk(B_W)],
    )
    return pl.pallas_call(
        _ctx_attn_kernel,
        grid_spec=grid_spec,
        out_shape=[jax.ShapeDtypeStruct((bsz, n_ctx, A_Q_W), BF16),
                   jax.ShapeDtypeStruct((bsz, n_ctx, B_W), BF16)],
        compiler_params=_params("parallel"),
        name="context_attention",
    )(sink, qa, qb, ka, va, kb, vb)


CONV_HALO = 16
F32_SUBLANES = 8
CONV_SHIFT_SPAN = (CONV_HALO + C_CONV_WIDTH // 2) // F32_SUBLANES * F32_SUBLANES


def _conv_kernel(prev_ref, cur_ref, next_ref, w_ref, b_ref, g_ref, beta_ref, out_ref, shifted_ref, *, ts):
    i = pl.program_id(1)
    n_i = pl.num_programs(1)
    ext = jnp.concatenate([jnp.where(i > 0, prev_ref[0], 0.0), cur_ref[0],
                           jnp.where(i < n_i - 1, next_ref[0], 0.0)], axis=0)
    for r in range(F32_SUBLANES):
        shifted_ref[r] = ext[r:r + ts + CONV_SHIFT_SPAN]
    acc = jnp.zeros((ts, C_CHANNELS), F32)
    for k in range(C_CONV_WIDTH):
        start = CONV_HALO - C_CONV_WIDTH // 2 + k
        aligned = start - start % F32_SUBLANES
        acc = acc + shifted_ref[start % F32_SUBLANES, aligned:aligned + ts] * w_ref[k:k + 1]
    y = _layer_norm(acc + b_ref[...]) * g_ref[...] + beta_ref[...]
    out_ref[0] = (y * jax.nn.sigmoid(y)).astype(BF16)


def conformer_conv(hc, conv_w, conv_b, ln_g, ln_b):
    bx, t, ch = hc.shape
    ts = min(512, t)
    per = ts // CONV_HALO
    last = t // CONV_HALO - 1
    row = lambda v: v.reshape(1, ch)
    const = lambda b, i: (0, 0)
    return pl.pallas_call(
        functools.partial(_conv_kernel, ts=ts),
        grid=(bx, t // ts),
        in_specs=[
            pl.BlockSpec((1, CONV_HALO, ch), lambda b, i: (b, jnp.maximum(i * per - 1, 0), 0)),
            pl.BlockSpec((1, ts, ch), lambda b, i: (b, i, 0)),
            pl.BlockSpec((1, CONV_HALO, ch), lambda b, i: (b, jnp.minimum((i + 1) * per, last), 0)),
            pl.BlockSpec((C_CONV_WIDTH, ch), const),
            pl.BlockSpec((1, ch), const), pl.BlockSpec((1, ch), const), pl.BlockSpec((1, ch), const),
        ],
        out_specs=pl.BlockSpec((1, ts, ch), lambda b, i: (b, i, 0)),
        out_shape=jax.ShapeDtypeStruct((bx, t, ch), BF16),
        scratch_shapes=[pltpu.VMEM((F32_SUBLANES, ts + CONV_SHIFT_SPAN, ch), F32)],
        compiler_params=_params("parallel", "parallel"),
        name="conformer_conv",
    )(hc, hc, hc, conv_w, row(conv_b), row(ln_g), row(ln_b))


OUTPROJ_ROW_CHUNKS = 2


def _outproj_kernel(oa_ref, ob_ref, oc_ref, x_ref, g1_ref, sh_ref, sc_ref, w_ref, lng_ref, lnb_ref,
                    wr_hi_ref, wr_lo_ref, x1_ref, h2_ref, afft_ref, aff_ref, *, alpha):
    tm, d = x_ref.shape[1], x_ref.shape[2]
    rc = tm // OUTPROJ_ROW_CHUNKS
    chunks = [slice(k * rc, (k + 1) * rc) for k in range(OUTPROJ_ROW_CHUNKS)]
    outs = [(_dot(oa_ref[0, r], w_ref[0:A_Q_W])
             + _dot(ob_ref[0, r], w_ref[A_Q_W:A_Q_W + B_W])
             + _dot(oc_ref[0, r], w_ref[A_Q_W + B_W:])) for r in chunks]
    w_hi, w_lo = wr_hi_ref[...], wr_lo_ref[...]
    eye = (lax.broadcasted_iota(jnp.int32, (N_EXPERTS, N_EXPERTS), 0)
           == lax.broadcasted_iota(jnp.int32, (N_EXPERTS, N_EXPERTS), 1)).astype(BF16)
    rows_per_token = d // LANES
    for k, (r, o) in enumerate(zip(chunks, outs)):
        y = _layer_norm(alpha * x_ref[0, r] + g1_ref[0] * o) * lng_ref[...] + lnb_ref[...]
        x1_ref[0, r] = y
        h2 = _layer_norm(y) * (1.0 + sc_ref[0]) + sh_ref[0]
        h_hi = h2.astype(BF16)
        for j in range(rows_per_token):
            h2_ref[0, pl.ds(k * rc * rows_per_token + j, rc, stride=rows_per_token), :] = (
                h2[:, j * LANES:(j + 1) * LANES])
        h_lo = (h2 - h_hi.astype(F32)).astype(BF16)
        logits = _dot_t(h_hi, w_hi) + _dot_t(h_lo, w_hi) + _dot_t(h_hi, w_lo)
        e_n = jnp.exp(logits - logits.max(axis=1, keepdims=True))
        aff = e_n / e_n.sum(axis=1, keepdims=True)
        aff_ref[0, r] = aff
        aff_t, rest = None, aff
        for _ in range(GATE_PARTS):
            part = rest.astype(BF16)
            term = _dot_t(eye, part)
            aff_t = term if aff_t is None else aff_t + term
            rest = rest - part.astype(F32)
        afft_ref[0, :, r] = aff_t


def out_projection(oa, ob, oc, x, g1, sh2, sc2, w_out_bf16, ln_g, ln_b, wr_hi, wr_lo, alpha):
    bx, t, d = x.shape
    tm = min(512, t)
    tok = lambda b, i: (b, i, 0)
    per_b = lambda b, i: (b, 0, 0)
    const = lambda b, i: (0, 0)
    vec = pl.BlockSpec((1, d), const)
    return pl.pallas_call(
        functools.partial(_outproj_kernel, alpha=alpha),
        grid=(bx, t // tm),
        in_specs=[
            pl.BlockSpec((1, tm, A_Q_W), tok), pl.BlockSpec((1, tm, B_W), tok),
            pl.BlockSpec((1, tm, C_CHANNELS), tok), pl.BlockSpec((1, tm, d), tok),
            pl.BlockSpec((1, 1, d), per_b), pl.BlockSpec((1, 1, d), per_b), pl.BlockSpec((1, 1, d), per_b),
            pl.BlockSpec(w_out_bf16.shape, const), vec, vec,
            pl.BlockSpec((N_EXPERTS, d), const), pl.BlockSpec((N_EXPERTS, d), const),
        ],
        out_specs=[pl.BlockSpec((1, tm, d), tok), pl.BlockSpec((1, tm * (d // LANES), LANES), tok),
                   pl.BlockSpec((1, N_EXPERTS, tm), lambda b, i: (b, 0, i)),
                   pl.BlockSpec((1, tm, N_EXPERTS), tok)],
        out_shape=[jax.ShapeDtypeStruct((bx, t, d), F32),
                   jax.ShapeDtypeStruct((bx, t * (d // LANES), LANES), F32),
                   jax.ShapeDtypeStruct((bx, N_EXPERTS, t), F32),
                   jax.ShapeDtypeStruct((bx, t, N_EXPERTS), F32)],
        compiler_params=_params("parallel", "parallel"),
        name="out_projection",
    )(oa, ob, oc, x, g1, sh2, sc2, w_out_bf16, ln_g.reshape(1, d), ln_b.reshape(1, d), wr_hi, wr_lo)


def _select_kernel(afft_ref, pos_ref, off_ref, *, cap, n_tok):
    aff = afft_ref[0]

    def bit_step(j, bits):
        cand = bits | (jnp.int32(1) << (30 - j))
        cnt = jnp.sum((aff >= pltpu.bitcast(cand, F32)).astype(jnp.int32), axis=1, keepdims=True)
        return jnp.where(cnt >= cap, cand, bits)

    thr = pltpu.bitcast(lax.fori_loop(0, 31, bit_step, jnp.zeros((N_EXPERTS, 1), jnp.int32)), F32)
    above = (aff > thr).astype(F32)
    tied = (aff == thr).astype(F32)
    need = cap - jnp.sum(above, axis=1, keepdims=True)

    blk = MXU_DEPTH
    blocks = [slice(k * blk, (k + 1) * blk) for k in range(n_tok // blk)]
    r_i = lax.broadcasted_iota(jnp.int32, (blk, blk), 0)
    c_i = lax.broadcasted_iota(jnp.int32, (blk, blk), 1)
    strict_upper = (r_i < c_i).astype(BF16)

    def running(block_sums):
        run, total = [], jnp.zeros((N_EXPERTS, 1), F32)
        for s in block_sums:
            run.append(total)
            total = total + s
        return run

    tied_b = [tied[:, sl] for sl in blocks]
    tied_rank = [_dot(t.astype(BF16), strict_upper) for t in tied_b]
    tied_before = running([t.sum(axis=1, keepdims=True) for t in tied_b])
    sel_b = [above[:, sl] + t * ((before + rank) < need).astype(F32)
             for sl, t, before, rank in zip(blocks, tied_b, tied_before, tied_rank)]
    sel_rank = [_dot(s.astype(BF16), strict_upper) for s in sel_b]
    half_sums = [[s[:, h * OFFSET_BLOCK:(h + 1) * OFFSET_BLOCK].sum(axis=1, keepdims=True)
                  for h in range(blk // OFFSET_BLOCK)] for s in sel_b]
    offs = running([h for hs in half_sums for h in hs])
    per_blk = blk // OFFSET_BLOCK
    for k, (sl, s, rank) in enumerate(zip(blocks, sel_b, sel_rank)):
        pos_ref[0, :, sl] = jnp.where(s > 0.5, (offs[k * per_blk] + rank).astype(jnp.int32), -1)
    off_ref[0] = jnp.concatenate(offs, axis=1).astype(jnp.int32)


def expert_choice_select(aff_t, cap):
    bx, n_e, t = aff_t.shape
    n_tb = t // OFFSET_BLOCK
    return pl.pallas_call(
        functools.partial(_select_kernel, cap=cap, n_tok=t),
        grid=(bx,),
        in_specs=[pl.BlockSpec((1, n_e, t), lambda b: (b, 0, 0))],
        out_specs=[pl.BlockSpec((1, n_e, t), lambda b: (b, 0, 0)),
                   pl.BlockSpec((1, n_e, n_tb), lambda b: (b, 0, 0))],
        out_shape=[jax.ShapeDtypeStruct((bx, n_e, t), jnp.int32),
                   jax.ShapeDtypeStruct((bx, n_e, n_tb), jnp.int32)],
        compiler_params=_params("parallel"),
        name="expert_choice_select",
    )(aff_t)


GATE_PARTS = 3
TOKEN_LANE = GATE_PARTS


def _slot_table_kernel(off_ref, pos_ref, aff_ref, tbl_ref, *, n_off, blocks_per_step):
    b, eg, kc = pl.program_id(0), pl.program_id(1), pl.program_id(2)
    epg = EXPERTS_PER_GATHER_STEP

    @pl.when(kc == 0)
    def _():
        tbl_ref[...] = jnp.zeros_like(tbl_ref)

    slot = lax.broadcasted_iota(jnp.int32, (GATHER_WINDOW, 1), 0)
    slot_2d = lax.broadcasted_iota(jnp.int32, (GATHER_WINDOW, GATHER_BLOCK), 0)
    lane = lax.broadcasted_iota(jnp.int32, (GATHER_BLOCK, LANES), 1)
    out_lane = lax.broadcasted_iota(jnp.int32, (GATHER_WINDOW, LANES), 1)
    local_token = lax.broadcasted_iota(jnp.int32, (GATHER_BLOCK, LANES), 0).astype(F32)
    expert_lane = lax.broadcasted_iota(jnp.int32, (GATHER_BLOCK, N_EXPERTS), 1)
    for kk in range(blocks_per_step):
        kb = kc * blocks_per_step + kk
        aff_blk = aff_ref[0, kk * GATHER_BLOCK:(kk + 1) * GATHER_BLOCK]
        first_token = (kb * GATHER_BLOCK).astype(F32)
        for ee in range(epg):
            e = eg * epg + ee
            off = off_ref[(b * N_EXPERTS + e) * n_off + kb * (GATHER_BLOCK // OFFSET_BLOCK)]
            base = pl.multiple_of((off >> 4) << 4, WINDOW_ALIGN)
            onehot = (pos_ref[0, ee, 0, kk:kk + 1, :] - base == slot_2d).astype(BF16)
            gate = jnp.sum(jnp.where(expert_lane == e, aff_blk, 0.0), axis=1, keepdims=True)
            payload, rest = jnp.where(lane == TOKEN_LANE, local_token, 0.0), gate
            for k in range(GATE_PARTS):
                part = rest.astype(BF16).astype(F32)
                payload = jnp.where(lane == k, part, payload)
                rest = rest - part
            gathered = _dot(onehot, payload.astype(BF16)) + jnp.where(out_lane == TOKEN_LANE, first_token, 0.0)
            win = pl.ds(base, GATHER_WINDOW)
            tbl_ref[0, ee, win, :] = jnp.where(slot >= off - base, gathered, tbl_ref[0, ee, win, :])


def slot_table(aff, pos, block_off, cap_pad):
    bx, t, _ = aff.shape
    n_tb = t // GATHER_BLOCK
    blocks_per_step = min(4, n_tb)
    n_steps = n_tb // blocks_per_step
    epg = EXPERTS_PER_GATHER_STEP
    tokens = blocks_per_step * GATHER_BLOCK
    pos5 = pos.reshape(bx, N_EXPERTS, n_steps, blocks_per_step, GATHER_BLOCK)
    grid_spec = pltpu.PrefetchScalarGridSpec(
        num_scalar_prefetch=1,
        grid=(bx, N_EXPERTS // epg, n_steps),
        in_specs=[
            pl.BlockSpec((1, epg, 1, blocks_per_step, GATHER_BLOCK), lambda b, g, k, s: (b, g, k, 0, 0)),
            pl.BlockSpec((1, tokens, N_EXPERTS), lambda b, g, k, s: (b, k, 0)),
        ],
        out_specs=pl.BlockSpec((1, epg, cap_pad, LANES), lambda b, g, k, s: (b, g, 0, 0)),
    )
    return pl.pallas_call(
        functools.partial(_slot_table_kernel, n_off=t // OFFSET_BLOCK, blocks_per_step=blocks_per_step),
        grid_spec=grid_spec,
        out_shape=jax.ShapeDtypeStruct((bx, N_EXPERTS, cap_pad, LANES), F32),
        compiler_params=_params("parallel", "parallel", "arbitrary"),
        name="slot_table",
    )(block_off.reshape(-1), pos5, aff)


GATHER_UNROLL = 8


def _row_gather_kernel(idx_ref, src_ref, xs_ref, tile_ref, *, cap, rows_per_token, chunk_stride):
    b, e = pl.program_id(0), pl.program_id(1)
    first = (b * N_EXPERTS + e) * cap

    def group(g, carry):
        for u in range(GATHER_UNROLL):
            s = g * GATHER_UNROLL + u
            row = pl.multiple_of(idx_ref[first + s] * rows_per_token, rows_per_token)
            tile_ref[pl.ds(s, rows_per_token, stride=chunk_stride), :] = src_ref[0, pl.ds(row, rows_per_token), :]
        return carry

    lax.fori_loop(0, cap // GATHER_UNROLL, group, 0)
    for j in range(rows_per_token):
        xs_ref[0, 0, :, j * LANES:(j + 1) * LANES] = tile_ref[j * chunk_stride:j * chunk_stride + cap, :].astype(BF16)


def gather_rows(h2_rows, token_idx, cap, d):
    bx, n_rows, _ = h2_rows.shape
    rows_per_token = d // LANES
    chunk_stride = cap + 8
    grid_spec = pltpu.PrefetchScalarGridSpec(
        num_scalar_prefetch=1,
        grid=(bx, N_EXPERTS),
        in_specs=[pl.BlockSpec((1, n_rows, LANES), lambda b, e, s: (b, 0, 0), pipeline_mode=pl.Buffered(1))],
        out_specs=pl.BlockSpec((1, 1, cap, d), lambda b, e, s: (b, e, 0, 0)),
        scratch_shapes=[pltpu.VMEM((rows_per_token * chunk_stride, LANES), F32)],
    )
    return pl.pallas_call(
        functools.partial(_row_gather_kernel, cap=cap, rows_per_token=rows_per_token, chunk_stride=chunk_stride),
        grid_spec=grid_spec,
        out_shape=jax.ShapeDtypeStruct((bx, N_EXPERTS, cap, d), BF16),
        compiler_params=_params("parallel", "arbitrary"),
        name="gather_rows",
    )(token_idx, h2_rows)


def _ffn_kernel(xs_ref, gs_ref, wg_ref, wu_ref, wd_ref, ye_ref, wg_bf, wu_bf, wd_bf, *, cap, row_tile):
    @pl.when(pl.program_id(1) == 0)
    def _():
        wg_bf[...] = wg_ref[0, 0].astype(BF16)
        wu_bf[...] = wu_ref[0, 0].astype(BF16)
        wd_bf[...] = wd_ref[0, 0].astype(BF16)

    n_b = xs_ref.shape[0]
    if n_b == 1:
        tiles = [[(0, r0, row_tile)] for r0 in range(0, cap, row_tile)]
    else:
        tiles = [[(bb, 0, cap) for bb in range(n_b)]]
    for tile in tiles:
        x = jnp.concatenate([xs_ref[bb, 0, r0:r0 + n] for bb, r0, n in tile], axis=0)
        terms = jnp.concatenate([gs_ref[bb, 0, r0:r0 + n, 0:GATE_PARTS] for bb, r0, n in tile], axis=0)
        gate = _dot(x, wg_bf[...])
        up = _dot(x, wu_bf[...])
        hid = (gate * jax.nn.sigmoid(gate) * up).astype(BF16)
        g = jnp.sum(terms, axis=1, keepdims=True)
        ye = _dot(hid, wd_bf[...]) * g
        rows_per_slot = ye.shape[1] // LANES
        row = 0
        for bb, r0, n in tile:
            for j in range(rows_per_slot):
                ye_ref[bb, 0, pl.ds(r0 * rows_per_slot + j, n, stride=rows_per_slot), :] = (
                    ye[row:row + n, j * LANES:(j + 1) * LANES])
            row += n


def expert_ffn(xs, gs, wg, wu, wd, layer):
    bx, n_e, cap, d = xs.shape
    ff = wg.shape[-1]
    slot_rows = cap * (d // LANES)
    max_rows = 512
    row_tile = min(max_rows, cap)
    n_b = bx if bx * cap <= max_rows else 1
    return pl.pallas_call(
        functools.partial(_ffn_kernel, cap=cap, row_tile=row_tile),
        grid=(n_e, bx // n_b),
        in_specs=[
            pl.BlockSpec((n_b, 1, cap, d), lambda e, b: (b, e, 0, 0)),
            pl.BlockSpec((n_b, 1, cap, LANES), lambda e, b: (b, e, 0, 0)),
            pl.BlockSpec((1, 1, d, ff), lambda e, b: (layer, e, 0, 0)),
            pl.BlockSpec((1, 1, d, ff), lambda e, b: (layer, e, 0, 0)),
            pl.BlockSpec((1, 1, ff, d), lambda e, b: (layer, e, 0, 0)),
        ],
        out_specs=pl.BlockSpec((n_b, 1, slot_rows, LANES), lambda e, b: (b, e, 0, 0)),
        out_shape=jax.ShapeDtypeStruct((bx, n_e, slot_rows, LANES), F32),
        scratch_shapes=[pltpu.VMEM((d, ff), BF16), pltpu.VMEM((d, ff), BF16), pltpu.VMEM((ff, d), BF16)],
        compiler_params=_params("parallel", "arbitrary"),
        name="expert_ffn",
    )(xs, gs, wg, wu, wd)


SCATTER_UNROLL = 8


def _row_scatter_kernel(idx_ref, ye_ref, acc_ref, *, cap, rows_per_token):
    b, e = pl.program_id(0), pl.program_id(1)

    @pl.when(e == 0)
    def _():
        acc_ref[...] = jnp.zeros_like(acc_ref)

    first = (b * N_EXPERTS + e) * cap

    def group(g, carry):
        rows, sums = [], []
        for u in range(SCATTER_UNROLL):
            s = g * SCATTER_UNROLL + u
            row = pl.multiple_of(idx_ref[first + s] * rows_per_token, rows_per_token)
            src = pl.multiple_of(s * rows_per_token, rows_per_token)
            rows.append(row)
            sums.append(acc_ref[0, pl.ds(row, rows_per_token), :] + ye_ref[0, 0, pl.ds(src, rows_per_token), :])
        for row, total in zip(rows, sums):
            acc_ref[0, pl.ds(row, rows_per_token), :] = total
        return carry

    lax.fori_loop(0, cap // SCATTER_UNROLL, group, 0)


def scatter_rows(ye_rows, token_idx, n_tok, cap):
    bx, n_e, slot_rows, _ = ye_rows.shape
    rows_per_token = slot_rows // cap
    grid_spec = pltpu.PrefetchScalarGridSpec(
        num_scalar_prefetch=1,
        grid=(bx, n_e),
        in_specs=[pl.BlockSpec((1, 1, slot_rows, LANES), lambda b, e, s: (b, e, 0, 0))],
        out_specs=pl.BlockSpec((1, n_tok * rows_per_token, LANES), lambda b, e, s: (b, 0, 0),
                               pipeline_mode=pl.Buffered(1)),
    )
    return pl.pallas_call(
        functools.partial(_row_scatter_kernel, cap=cap, rows_per_token=rows_per_token),
        grid_spec=grid_spec,
        out_shape=jax.ShapeDtypeStruct((bx, n_tok * rows_per_token, LANES), F32),
        compiler_params=_params("parallel", "arbitrary"),
        name="scatter_rows",
    )(token_idx, ye_rows)


def _final_norm_kernel(moe_ref, x1_ref, g2_ref, lng_ref, lnb_ref, out_ref, *, alpha):
    tm, d = x1_ref.shape[1], x1_ref.shape[2]
    rows_per_token = d // LANES
    moe = jnp.concatenate([moe_ref[0, pl.ds(j, tm, stride=rows_per_token), :] for j in range(rows_per_token)],
                          axis=1)
    y = alpha * x1_ref[0] + g2_ref[0] * moe
    out_ref[0] = _layer_norm(y) * lng_ref[...] + lnb_ref[...]


def residual_norm(moe_rows, x1, g2, ln_g, ln_b, alpha):
    bx, t, d = x1.shape
    tm = min(512, t)
    rows_per_token = d // LANES
    tok = lambda b, i: (b, i, 0)
    const = lambda b, i: (0, 0)
    return pl.pallas_call(
        functools.partial(_final_norm_kernel, alpha=alpha),
        grid=(bx, t // tm),
        in_specs=[
            pl.BlockSpec((1, tm * rows_per_token, LANES), tok),
            pl.BlockSpec((1, tm, d), tok),
            pl.BlockSpec((1, 1, d), lambda b, i: (b, 0, 0)),
            pl.BlockSpec((1, d), const), pl.BlockSpec((1, d), const),
        ],
        out_specs=pl.BlockSpec((1, tm, d), tok),
        out_shape=jax.ShapeDtypeStruct((bx, t, d), F32),
        compiler_params=_params("parallel", "parallel"),
        name="residual_norm",
    )(moe_rows, x1, g2, ln_g.reshape(1, d), ln_b.reshape(1, d))


def _split_bf16(w):
    hi = w.astype(BF16)
    return hi, (w - hi.astype(F32)).astype(BF16)


def _mixer_tail(oa, ob, oc, x, mod, lw, alpha):
    g1, sh2, sc2, g2 = mod
    t = x.shape[1]
    cap = EC_CAPACITY * t // N_EXPERTS
    cap_pad = cap + SLOT_PAD
    x1, h2_rows, aff_t, aff = out_projection(oa, ob, oc, x, g1, sh2, sc2, lw["w_out"], lw["ln1_g"], lw["ln1_b"],
                                        lw["wr_hi"], lw["wr_lo"], alpha)
    pos, block_off = expert_choice_select(aff_t, cap)
    table = slot_table(aff, pos, block_off, cap_pad)
    token_idx = table[:, :, :cap, TOKEN_LANE].astype(jnp.int32).reshape(-1)
    xs = gather_rows(h2_rows, token_idx, cap, x.shape[2])
    ye_rows = expert_ffn(xs, table, lw["w_gate"], lw["w_up"], lw["w_down"], lw["layer"])
    moe_rows = scatter_rows(ye_rows, token_idx, t, cap)
    return residual_norm(moe_rows, x1, g2, lw["ln2_g"], lw["ln2_b"], alpha)


def kernel(x, c, ctx, c_ctx, w_mod, b_mod, w_in, a_sink, nat_bias, conv_w, conv_b, conv_ln_g, conv_ln_b,
           w_out, ln1_g, ln1_b, w_router, w_gate, w_up, w_down, ln2_g, ln2_b):
    bsz, n_lat, d = x.shape
    depth = w_mod.shape[0]
    alpha = (2 * depth) ** 0.25
    cos_t, sin_t = rope_tables(n_lat)

    cond = jnp.concatenate([c, c_ctx[None, :], jnp.zeros((8 - bsz - 1, d), F32)], axis=0)
    mods = adaln_all(cond, w_mod, b_mod)

    for l in range(depth):
        last = l == depth - 1
        wr_hi, wr_lo = _split_bf16(w_router[l].T)
        lw = dict(w_out=w_out[l].astype(BF16), ln1_g=ln1_g[l], ln1_b=ln1_b[l], wr_hi=wr_hi, wr_lo=wr_lo,
                  w_gate=w_gate, w_up=w_up, w_down=w_down, layer=l,
                  ln2_g=ln2_g[l], ln2_b=ln2_b[l])
        w_in_l = w_in[l].astype(BF16)
        lat = [mods[l, :bsz, k * d:(k + 1) * d][:, None, :] for k in range(N_MOD)]
        cm = [jnp.broadcast_to(mods[l, bsz, k * d:(k + 1) * d][None, None, :], (bsz, 1, d))
              for k in range(N_MOD)]
        conv_args = (conv_w[l], conv_b[l], conv_ln_g[l], conv_ln_b[l])

        qa_c, ka_c, va_c, qb_c, kb_c, vb_c, hc_c = in_projection(ctx, cm[0], cm[1], w_in_l, cos_t, sin_t, rope=False)
        if not last:
            oa_c, ob_c = context_attention(qa_c, qb_c, ka_c, va_c, kb_c, vb_c, a_sink[l])
            oc_c = conformer_conv(hc_c, *conv_args)
            ctx_new = _mixer_tail(oa_c, ob_c, oc_c, ctx, (cm[2], cm[3], cm[4], cm[5]), lw, alpha)

        qa, ka, va, qb, kb, vb, hc = in_projection(x, lat[0], lat[1], w_in_l, cos_t, sin_t, rope=True)
        oa = window_attention(qa, ka, va, ka_c, va_c, a_sink[l])
        ob = neighbourhood_attention(qb, kb, vb, kb_c, vb_c, nat_bias[l])
        oc = conformer_conv(hc, *conv_args)
        x = _mixer_tail(oa, ob, oc, x, (lat[2], lat[3], lat[4], lat[5]), lw, alpha)
        if not last:
            ctx = ctx_new
    return x
```

```python
import functools

import numpy as np
import jax
import jax.numpy as jnp
from jax import lax
from jax.experimental import pallas as pl
from jax.experimental.pallas import tpu as pltpu

HEAD_DIM = 64
GRID_W = 64
A_Q_HEADS = 8
A_KV_HEADS = 2
A_GROUP = A_Q_HEADS // A_KV_HEADS
A_WINDOW = 128
B_HEADS = 4
NA_ROWS = 8
NA_COLS = 16
C_CHANNELS = 256
C_CONV_WIDTH = 31
A_Q_W = A_Q_HEADS * HEAD_DIM
A_KV_W = A_KV_HEADS * HEAD_DIM
B_W = B_HEADS * HEAD_DIM
OFF_AK = A_Q_W
OFF_AV = OFF_AK + A_KV_W
OFF_BQ = OFF_AV + A_KV_W
OFF_BK = OFF_BQ + B_W
OFF_BV = OFF_BK + B_W
OFF_C = OFF_BV + B_W
IN_WIDTH = OFF_C + 2 * C_CHANNELS
ROPE_WIDTH = A_Q_W + A_KV_W
N_EXPERTS = 16
EC_CAPACITY = 2
ROPE_BASE = 10000.0
LN_EPS = 1e-6
N_MOD = 6
NEG_INF = -1e30
QK_SCALE = HEAD_DIM ** -0.5

LANES = 128
WINDOW_ALIGN = 16
MXU_DEPTH = 256
OFFSET_BLOCK = 128
GATHER_BLOCK = MXU_DEPTH
GATHER_WINDOW = GATHER_BLOCK + WINDOW_ALIGN
SLOT_PAD = 3 * LANES
EXPERTS_PER_GATHER_STEP = 16
VMEM_LIMIT = 56 * 1024 * 1024

F32 = jnp.float32
BF16 = jnp.bfloat16


def _dot(a, b):
    return jnp.dot(a, b, preferred_element_type=F32)


def _dot_t(a, b):
    return lax.dot_general(a, b, (((1,), (1,)), ((), ())), preferred_element_type=F32)


def _layer_norm(x):
    mu = jnp.mean(x, axis=-1, keepdims=True)
    xc = x - mu
    var = jnp.mean(xc * xc, axis=-1, keepdims=True)
    return xc * lax.rsqrt(var + LN_EPS)


def _params(*sem):
    return pltpu.CompilerParams(dimension_semantics=sem, vmem_limit_bytes=VMEM_LIMIT)


def _mod_kernel(cond_ref, w_ref, b_ref, out_ref):
    cnd = cond_ref[...]
    act = cnd * jax.nn.sigmoid(cnd)
    out_ref[0] = jnp.dot(act, w_ref[0], preferred_element_type=F32,
                         precision=lax.Precision.HIGHEST) + b_ref[0]


def adaln_all(cond, w_mod, b_mod):
    n_layers, d, width = w_mod.shape
    rows = cond.shape[0]
    tn = 1536
    return pl.pallas_call(
        _mod_kernel,
        grid=(n_layers, width // tn),
        in_specs=[
            pl.BlockSpec((rows, d), lambda l, j: (0, 0)),
            pl.BlockSpec((1, d, tn), lambda l, j: (l, 0, j)),
            pl.BlockSpec((1, 1, tn), lambda l, j: (l, 0, j)),
        ],
        out_specs=pl.BlockSpec((1, rows, tn), lambda l, j: (l, 0, j)),
        out_shape=jax.ShapeDtypeStruct((n_layers, rows, width), F32),
        compiler_params=_params("parallel", "parallel"),
        name="adaln",
    )(cond, w_mod, b_mod.reshape(n_layers, 1, width))


def _inproj_kernel(x_ref, sh_ref, sc_ref, w_ref, cos_ref, sin_ref,
                   qa_ref, ka_ref, va_ref, qb_ref, kb_ref, vb_ref, hc_ref, *, rope):
    x = x_ref[0]
    h = _layer_norm(x) * (1.0 + sc_ref[0]) + sh_ref[0]
    u = _dot(h.astype(BF16), w_ref[...])

    def rotated(col):
        xq = u[:, col:col + LANES]
        if not rope:
            return xq
        lane = lax.broadcasted_iota(jnp.int32, xq.shape, 1)
        first = (lane & (HEAD_DIM // 2 - 1)) < (HEAD_DIM // 4)
        partner = jnp.where(first, pltpu.roll(xq, LANES - HEAD_DIM // 4, 1),
                            pltpu.roll(xq, HEAD_DIM // 4, 1))
        return xq * cos_ref[...] + partner * sin_ref[...]

    rot = [rotated(col) for col in range(0, ROPE_WIDTH, LANES)]
    n_q = A_Q_W // LANES
    qa_ref[0] = (jnp.concatenate(rot[:n_q], axis=1) * QK_SCALE).astype(BF16)
    ka_ref[0] = jnp.concatenate(rot[n_q:], axis=1).astype(BF16)
    va_ref[0] = u[:, OFF_AV:OFF_BQ].astype(BF16)
    qb_ref[0] = (u[:, OFF_BQ:OFF_BK] * QK_SCALE).astype(BF16)
    kb_ref[0] = u[:, OFF_BK:OFF_BV].astype(BF16)
    vb_ref[0] = u[:, OFF_BV:OFF_C].astype(BF16)
    a = u[:, OFF_C:OFF_C + C_CHANNELS]
    gate = u[:, OFF_C + C_CHANNELS:]
    hc_ref[0] = a * jax.nn.sigmoid(gate)


def in_projection(x, shift, scale, w_in_bf16, cos_t, sin_t, *, rope):
    bx, t, d = x.shape
    tm = min(512, t)
    widths = (A_Q_W, A_KV_W, A_KV_W, B_W, B_W, B_W, C_CHANNELS)
    dtypes = (BF16,) * 6 + (F32,)
    tok = lambda b, i: (b, i, 0)
    per_b = lambda b, i: (b, 0, 0)
    return pl.pallas_call(
        functools.partial(_inproj_kernel, rope=rope),
        grid=(bx, t // tm),
        in_specs=[
            pl.BlockSpec((1, tm, d), tok),
            pl.BlockSpec((1, 1, d), per_b),
            pl.BlockSpec((1, 1, d), per_b),
            pl.BlockSpec((d, IN_WIDTH), lambda b, i: (0, 0)),
            pl.BlockSpec((tm, LANES), lambda b, i: (i, 0)),
            pl.BlockSpec((tm, LANES), lambda b, i: (i, 0)),
        ],
        out_specs=[pl.BlockSpec((1, tm, w), tok) for w in widths],
        out_shape=[jax.ShapeDtypeStruct((bx, t, w), dt) for w, dt in zip(widths, dtypes)],
        compiler_params=_params("parallel", "parallel"),
        name="in_projection",
    )(x, shift, scale, w_in_bf16, cos_t, sin_t)


def rope_tables(n_tokens):
    t = jnp.arange(n_tokens, dtype=jnp.int32)
    row = (t // GRID_W).astype(F32)[:, None]
    col = (t % GRID_W).astype(F32)[:, None]
    n_freq = HEAD_DIM // 4
    inv_freq = ROPE_BASE ** (-jnp.arange(n_freq, dtype=F32) / n_freq)
    ang_r = row * inv_freq
    ang_c = col * inv_freq
    cos_h = jnp.concatenate([jnp.cos(ang_r), jnp.cos(ang_r), jnp.cos(ang_c), jnp.cos(ang_c)], axis=1)
    sin_h = jnp.concatenate([-jnp.sin(ang_r), jnp.sin(ang_r), -jnp.sin(ang_c), jnp.sin(ang_c)], axis=1)
    reps = LANES // HEAD_DIM
    return jnp.tile(cos_h, (1, reps)), jnp.tile(sin_h, (1, reps))


def _with_ones(v):
    return jnp.concatenate([v, jnp.ones_like(v)], axis=1)


def _attend(score_parts, values, sink=None):
    m = score_parts[0].max(axis=-1, keepdims=True)
    for s in score_parts[1:]:
        m = jnp.maximum(m, s.max(axis=-1, keepdims=True))
    if sink is not None:
        m = jnp.maximum(m, sink)
    acc = None
    for s, v in zip(score_parts, values):
        term = _dot(jnp.exp((s - m).astype(BF16)), v)
        acc = term if acc is None else acc + term
    den = acc[:, HEAD_DIM:HEAD_DIM + 1]
    if sink is not None:
        den = den + jnp.exp(sink - m)
    return acc[:, :HEAD_DIM] / den


def _attn_a_kernel(sink_ref, q_ref, kp_ref, kc_ref, kn_ref, vp_ref, vc_ref, vn_ref,
                   kctx_ref, vctx_ref, out_ref, *, n_lat, tq):
    i = pl.program_id(1)
    k_win = jnp.concatenate([kp_ref[0], kc_ref[0], kn_ref[0]], axis=0)
    v_win = jnp.concatenate([vp_ref[0], vc_ref[0], vn_ref[0]], axis=0)
    kctx = kctx_ref[0]
    vctx = vctx_ref[0]
    sub = A_WINDOW
    span = 3 * A_WINDOW
    rows = A_GROUP * sub
    row_i = lax.broadcasted_iota(jnp.int32, (rows, span), 0)
    col_i = lax.broadcasted_iota(jnp.int32, (rows, span), 1)
    rel = col_i - A_WINDOW - (row_i & (sub - 1))
    in_band = (rel <= A_WINDOW) & (rel >= -A_WINDOW)
    group_of_row = lax.broadcasted_iota(jnp.int32, (rows, 1), 0) >> (sub.bit_length() - 1)
    sinks = []
    for hk in range(A_KV_HEADS):
        sink = jnp.zeros((rows, 1), F32)
        for g in range(A_GROUP):
            sink = jnp.where(group_of_row == g, sink_ref[hk * A_GROUP + g], sink)
        sinks.append(sink)
    vctx_ext = [_with_ones(vctx[:, hk * HEAD_DIM:(hk + 1) * HEAD_DIM]) for hk in range(A_KV_HEADS)]
    v_ext = [_with_ones(v_win[:, hk * HEAD_DIM:(hk + 1) * HEAD_DIM]) for hk in range(A_KV_HEADS)]
    def scores(j, hk):
        kpos = i * tq + j * sub - A_WINDOW + col_i
        valid = in_band & (kpos >= 0) & (kpos < n_lat)
        sl = slice(hk * HEAD_DIM, (hk + 1) * HEAD_DIM)
        q_rows = q_ref[0, j * sub:(j + 1) * sub]
        q = jnp.concatenate([q_rows[:, h * HEAD_DIM:(h + 1) * HEAD_DIM]
                             for h in range(hk * A_GROUP, (hk + 1) * A_GROUP)], axis=0)
        s_win = jnp.where(valid, _dot_t(q, k_win[j * sub:j * sub + span, sl]), NEG_INF)
        return [s_win, _dot_t(q, kctx[:, sl])]

    units = [(j, hk) for j in range(tq // sub) for hk in range(A_KV_HEADS)]
    all_scores = [scores(j, hk) for j, hk in units]
    for (j, hk), s in zip(units, all_scores):
        v_sub = v_ext[hk][j * sub:j * sub + span]
        o = _attend(s, [v_sub, vctx_ext[hk]], sink=sinks[hk]).astype(BF16)
        for g in range(A_GROUP):
            h = hk * A_GROUP + g
            out_ref[0, j * sub:(j + 1) * sub, h * HEAD_DIM:(h + 1) * HEAD_DIM] = o[g * sub:(g + 1) * sub]


def window_attention(qa, ka, va, kc_a, vc_a, sink):
    bsz, n_lat, _ = qa.shape
    n_ctx = kc_a.shape[1]
    tq = min(512, n_lat)
    w = A_WINDOW
    per = tq // w
    last = n_lat // w - 1
    prev = lambda b, i, s: (b, jnp.maximum(i * per - 1, 0), 0)
    cur = lambda b, i, s: (b, i, 0)
    nxt = lambda b, i, s: (b, jnp.minimum((i + 1) * per, last), 0)
    ctx = lambda b, i, s: (b, 0, 0)
    kv_specs = [pl.BlockSpec((1, w, A_KV_W), prev), pl.BlockSpec((1, tq, A_KV_W), cur),
                pl.BlockSpec((1, w, A_KV_W), nxt)]
    grid_spec = pltpu.PrefetchScalarGridSpec(
        num_scalar_prefetch=1,
        grid=(bsz, n_lat // tq),
        in_specs=[pl.BlockSpec((1, tq, A_Q_W), cur)] + kv_specs + kv_specs + [
            pl.BlockSpec((1, n_ctx, A_KV_W), ctx), pl.BlockSpec((1, n_ctx, A_KV_W), ctx)],
        out_specs=pl.BlockSpec((1, tq, A_Q_W), cur),
    )
    return pl.pallas_call(
        functools.partial(_attn_a_kernel, n_lat=n_lat, tq=tq),
        grid_spec=grid_spec,
        out_shape=jax.ShapeDtypeStruct((bsz, n_lat, A_Q_W), BF16),
        compiler_params=_params("parallel", "parallel"),
        name="window_attention",
    )(sink, qa, ka, ka, ka, va, va, va, kc_a, vc_a)


NB_Q_ROWS = 4


def _attn_b_kernel(q_ref, kp_ref, kc_ref, kn_ref, vp_ref, vc_ref, vn_ref,
                   kctx_ref, vctx_ref, bias_ref, out_ref):
    k_win = jnp.concatenate([kp_ref[0], kc_ref[0], kn_ref[0]], axis=0)
    v_win = jnp.concatenate([vp_ref[0], vc_ref[0], vn_ref[0]], axis=0)
    q_all = q_ref[0]
    kctx = kctx_ref[0]
    vctx = vctx_ref[0]
    heads = [slice(h * HEAD_DIM, (h + 1) * HEAD_DIM) for h in range(B_HEADS)]
    scores = [[_dot_t(q_all[:, sl], k_win[:, sl]) + bias_ref[0, h], _dot_t(q_all[:, sl], kctx[:, sl])]
              for h, sl in enumerate(heads)]
    for sl, s in zip(heads, scores):
        o = _attend(s, [_with_ones(v_win[:, sl]), _with_ones(vctx[:, sl])])
        out_ref[0, :, sl] = o.astype(BF16)


def neighbourhood_bias(rel_bias, n_lat):
    rows = n_lat // GRID_W
    kr_n = min(NA_ROWS, rows)
    n_blocks = rows // NB_Q_ROWS
    n_heads, n_dr, n_dc = rel_bias.shape
    cols = np.arange(GRID_W)
    c_start = np.clip(cols - NA_COLS // 2, 0, GRID_W - NA_COLS)
    col_ok = (cols[None, :] >= c_start[:, None]) & (cols[None, :] < c_start[:, None] + NA_COLS)
    dc = np.clip(cols[None, :] - cols[:, None], -(NA_COLS - 1), NA_COLS - 1) + NA_COLS - 1
    pick_dc = (dc.reshape(-1)[None, :] == np.arange(n_dc)[:, None]).astype(np.float32)
    toeplitz = jnp.dot(rel_bias.reshape(n_heads * n_dr, n_dc), pick_dc, precision=lax.Precision.HIGHEST)
    toeplitz = jnp.where(col_ok.reshape(-1), toeplitz, NEG_INF).reshape(n_heads, n_dr, GRID_W, GRID_W)
    q_rl = np.arange(NB_Q_ROWS)
    k_rl = np.arange(3 * NB_Q_ROWS)
    row_ok, dr = [], []
    for j in sorted({0, min(1, n_blocks - 1), n_blocks - 1}):
        r = NB_Q_ROWS * j + q_rl
        kr = NB_Q_ROWS * (j - 1) + k_rl
        r_start = np.clip(r - kr_n // 2, 0, rows - kr_n)
        ok = (kr[None, :] >= r_start[:, None]) & (kr[None, :] < r_start[:, None] + kr_n)
        row_ok.append(ok & (kr[None, :] >= 0) & (kr[None, :] < rows))
        dr.append(np.clip(kr[None, :] - r[:, None] + NA_ROWS - 1, 0, n_dr - 1))
    row_ok = np.stack(row_ok)
    dr = np.stack(dr)
    tiles = jnp.stack([toeplitz[:, int(i)] for i in dr.reshape(-1)], axis=1)
    tiles = tiles.reshape((n_heads,) + dr.shape + (GRID_W, GRID_W))
    tiles = jnp.where(row_ok[None, :, :, :, None, None], tiles, NEG_INF)
    table = tiles.transpose(1, 0, 2, 4, 3, 5).reshape(
        dr.shape[0], n_heads, NB_Q_ROWS * GRID_W, 3 * NB_Q_ROWS * GRID_W)
    return table, n_blocks


def neighbourhood_attention(qb, kb, vb, kc_b, vc_b, rel_bias):
    bsz, n_lat, _ = qb.shape
    n_ctx = kc_b.shape[1]
    table, n_blocks = neighbourhood_bias(rel_bias, n_lat)
    n_var = table.shape[0]
    tq = NB_Q_ROWS * GRID_W
    prev = lambda b, j: (b, jnp.maximum(j - 1, 0), 0)
    cur = lambda b, j: (b, j, 0)
    nxt = lambda b, j: (b, jnp.minimum(j + 1, n_blocks - 1), 0)
    ctx = lambda b, j: (b, 0, 0)

    def variant(b, j):
        v = jnp.where(j == 0, 0, jnp.where(j == n_blocks - 1, n_var - 1, min(1, n_var - 1)))
        return (v, 0, 0, 0)

    kv_specs = [pl.BlockSpec((1, tq, B_W), prev), pl.BlockSpec((1, tq, B_W), cur),
                pl.BlockSpec((1, tq, B_W), nxt)]
    return pl.pallas_call(
        _attn_b_kernel,
        grid=(bsz, n_blocks),
        in_specs=[pl.BlockSpec((1, tq, B_W), cur)] + kv_specs + kv_specs + [
            pl.BlockSpec((1, n_ctx, B_W), ctx), pl.BlockSpec((1, n_ctx, B_W), ctx),
            pl.BlockSpec((1, B_HEADS, tq, 3 * tq), variant)],
        out_specs=pl.BlockSpec((1, tq, B_W), cur),
        out_shape=jax.ShapeDtypeStruct((bsz, n_lat, B_W), BF16),
        compiler_params=_params("parallel", "parallel"),
        name="neighbourhood_attention",
    )(qb, kb, kb, kb, vb, vb, vb, kc_b, vc_b, table)


def _ctx_attn_kernel(sink_ref, qa_ref, qb_ref, ka_ref, va_ref, kb_ref, vb_ref, oa_ref, ob_ref):
    qa, qb = qa_ref[0], qb_ref[0]
    ka, va, kb, vb = ka_ref[0], va_ref[0], kb_ref[0], vb_ref[0]
    for hq in range(A_Q_HEADS):
        sl = slice(hq * HEAD_DIM, (hq + 1) * HEAD_DIM)
        hk = hq // A_GROUP
        kv = slice(hk * HEAD_DIM, (hk + 1) * HEAD_DIM)
        q = qa[:, sl]
        o = _attend([_dot_t(q, ka[:, kv])], [_with_ones(va[:, kv])], sink=sink_ref[hq])
        oa_ref[0, :, sl] = o.astype(BF16)
    for h in range(B_HEADS):
        sl = slice(h * HEAD_DIM, (h + 1) * HEAD_DIM)
        q = qb[:, sl]
        o = _attend([_dot_t(q, kb[:, sl])], [_with_ones(vb[:, sl])])
        ob_ref[0, :, sl] = o.astype(BF16)


def context_attention(qa, qb, ka, va, kb, vb, sink):
    bsz, n_ctx, _ = qa.shape
    blk = lambda w: pl.BlockSpec((1, n_ctx, w), lambda b, s: (b, 0, 0))
    grid_spec = pltpu.PrefetchScalarGridSpec(
        num_scalar_prefetch=1,
        grid=(bsz,),
        in_specs=[blk(A_Q_W), blk(B_W), blk(A_KV_W), blk(A_KV_W), blk(B_W), blk(B_W)],
        out_specs=[blk(A_Q_W), blk(B_W)],
    )
    return pl.pallas_call(
        _ctx_attn_kernel,
        grid_spec=grid_spec,
        out_shape=[jax.ShapeDtypeStruct((bsz, n_ctx, A_Q_W), BF16),
                   jax.ShapeDtypeStruct((bsz, n_ctx, B_W), BF16)],
        compiler_params=_params("parallel"),
        name="context_attention",
    )(sink, qa, qb, ka, va, kb, vb)


CONV_HALO = 16
F32_SUBLANES = 8
CONV_SHIFT_SPAN = (CONV_HALO + C_CONV_WIDTH // 2) // F32_SUBLANES * F32_SUBLANES


def _conv_kernel(prev_ref, cur_ref, next_ref, w_ref, b_ref, g_ref, beta_ref, out_ref, shifted_ref, *, ts):
    i = pl.program_id(1)
    n_i = pl.num_programs(1)
    ext = jnp.concatenate([jnp.where(i > 0, prev_ref[0], 0.0), cur_ref[0],
                           jnp.where(i < n_i - 1, next_ref[0], 0.0)], axis=0)
    for r in range(F32_SUBLANES):
        shifted_ref[r] = ext[r:r + ts + CONV_SHIFT_SPAN]
    acc = jnp.zeros((ts, C_CHANNELS), F32)
    for k in range(C_CONV_WIDTH):
        start = CONV_HALO - C_CONV_WIDTH // 2 + k
        aligned = start - start % F32_SUBLANES
        acc = acc + shifted_ref[start % F32_SUBLANES, aligned:aligned + ts] * w_ref[k:k + 1]
    y = _layer_norm(acc + b_ref[...]) * g_ref[...] + beta_ref[...]
    out_ref[0] = (y * jax.nn.sigmoid(y)).astype(BF16)


def conformer_conv(hc, conv_w, conv_b, ln_g, ln_b):
    bx, t, ch = hc.shape
    ts = min(512, t)
    per = ts // CONV_HALO
    last = t // CONV_HALO - 1
    row = lambda v: v.reshape(1, ch)
    const = lambda b, i: (0, 0)
    return pl.pallas_call(
        functools.partial(_conv_kernel, ts=ts),
        grid=(bx, t // ts),
        in_specs=[
            pl.BlockSpec((1, CONV_HALO, ch), lambda b, i: (b, jnp.maximum(i * per - 1, 0), 0)),
            pl.BlockSpec((1, ts, ch), lambda b, i: (b, i, 0)),
            pl.BlockSpec((1, CONV_HALO, ch), lambda b, i: (b, jnp.minimum((i + 1) * per, last), 0)),
            pl.BlockSpec((C_CONV_WIDTH, ch), const),
            pl.BlockSpec((1, ch), const), pl.BlockSpec((1, ch), const), pl.BlockSpec((1, ch), const),
        ],
        out_specs=pl.BlockSpec((1, ts, ch), lambda b, i: (b, i, 0)),
        out_shape=jax.ShapeDtypeStruct((bx, t, ch), BF16),
        scratch_shapes=[pltpu.VMEM((F32_SUBLANES, ts + CONV_SHIFT_SPAN, ch), F32)],
        compiler_params=_params("parallel", "parallel"),
        name="conformer_conv",
    )(hc, hc, hc, conv_w, row(conv_b), row(ln_g), row(ln_b))


OUTPROJ_ROW_CHUNKS = 2


def _outproj_kernel(oa_ref, ob_ref, oc_ref, x_ref, g1_ref, sh_ref, sc_ref, w_ref, lng_ref, lnb_ref,
                    wr_hi_ref, wr_lo_ref, x1_ref, h2_ref, afft_ref, aff_ref, *, alpha):
    tm, d = x_ref.shape[1], x_ref.shape[2]
    rc = tm // OUTPROJ_ROW_CHUNKS
    chunks = [slice(k * rc, (k + 1) * rc) for k in range(OUTPROJ_ROW_CHUNKS)]
    outs = [(_dot(oa_ref[0, r], w_ref[0:A_Q_W])
             + _dot(ob_ref[0, r], w_ref[A_Q_W:A_Q_W + B_W])
             + _dot(oc_ref[0, r], w_ref[A_Q_W + B_W:])) for r in chunks]
    w_hi, w_lo = wr_hi_ref[...], wr_lo_ref[...]
    eye = (lax.broadcasted_iota(jnp.int32, (N_EXPERTS, N_EXPERTS), 0)
           == lax.broadcasted_iota(jnp.int32, (N_EXPERTS, N_EXPERTS), 1)).astype(BF16)
    rows_per_token = d // LANES
    for k, (r, o) in enumerate(zip(chunks, outs)):
        y = _layer_norm(alpha * x_ref[0, r] + g1_ref[0] * o) * lng_ref[...] + lnb_ref[...]
        x1_ref[0, r] = y
        h2 = _layer_norm(y) * (1.0 + sc_ref[0]) + sh_ref[0]
        h_hi = h2.astype(BF16)
        for j in range(rows_per_token):
            h2_ref[0, pl.ds(k * rc * rows_per_token + j, rc, stride=rows_per_token), :] = (
                h2[:, j * LANES:(j + 1) * LANES])
        h_lo = (h2 - h_hi.astype(F32)).astype(BF16)
        logits = _dot_t(h_hi, w_hi) + _dot_t(h_lo, w_hi) + _dot_t(h_hi, w_lo)
        e_n = jnp.exp(logits - logits.max(axis=1, keepdims=True))
        aff = e_n / e_n.sum(axis=1, keepdims=True)
        aff_ref[0, r] = aff
        aff_t, rest = None, aff
        for _ in range(GATE_PARTS):
            part = rest.astype(BF16)
            term = _dot_t(eye, part)
            aff_t = term if aff_t is None else aff_t + term
            rest = rest - part.astype(F32)
        afft_ref[0, :, r] = aff_t


def out_projection(oa, ob, oc, x, g1, sh2, sc2, w_out_bf16, ln_g, ln_b, wr_hi, wr_lo, alpha):
    bx, t, d = x.shape
    tm = min(512, t)
    tok = lambda b, i: (b, i, 0)
    per_b = lambda b, i: (b, 0, 0)
    const = lambda b, i: (0, 0)
    vec = pl.BlockSpec((1, d), const)
    return pl.pallas_call(
        functools.partial(_outproj_kernel, alpha=alpha),
        grid=(bx, t // tm),
        in_specs=[
            pl.BlockSpec((1, tm, A_Q_W), tok), pl.BlockSpec((1, tm, B_W), tok),
            pl.BlockSpec((1, tm, C_CHANNELS), tok), pl.BlockSpec((1, tm, d), tok),
            pl.BlockSpec((1, 1, d), per_b), pl.BlockSpec((1, 1, d), per_b), pl.BlockSpec((1, 1, d), per_b),
            pl.BlockSpec(w_out_bf16.shape, const), vec, vec,
            pl.BlockSpec((N_EXPERTS, d), const), pl.BlockSpec((N_EXPERTS, d), const),
        ],
        out_specs=[pl.BlockSpec((1, tm, d), tok), pl.BlockSpec((1, tm * (d // LANES), LANES), tok),
                   pl.BlockSpec((1, N_EXPERTS, tm), lambda b, i: (b, 0, i)),
                   pl.BlockSpec((1, tm, N_EXPERTS), tok)],
        out_shape=[jax.ShapeDtypeStruct((bx, t, d), F32),
                   jax.ShapeDtypeStruct((bx, t * (d // LANES), LANES), F32),
                   jax.ShapeDtypeStruct((bx, N_EXPERTS, t), F32),
                   jax.ShapeDtypeStruct((bx, t, N_EXPERTS), F32)],
        compiler_params=_params("parallel", "parallel"),
        name="out_projection",
    )(oa, ob, oc, x, g1, sh2, sc2, w_out_bf16, ln_g.reshape(1, d), ln_b.reshape(1, d), wr_hi, wr_lo)


def _select_kernel(afft_ref, pos_ref, off_ref, *, cap, n_tok):
    aff = afft_ref[0]

    def bit_step(j, bits):
        cand = bits | (jnp.int32(1) << (30 - j))
        cnt = jnp.sum((aff >= pltpu.bitcast(cand, F32)).astype(jnp.int32), axis=1, keepdims=True)
        return jnp.where(cnt >= cap, cand, bits)

    thr = pltpu.bitcast(lax.fori_loop(0, 31, bit_step, jnp.zeros((N_EXPERTS, 1), jnp.int32)), F32)
    above = (aff > thr).astype(F32)
    tied = (aff == thr).astype(F32)
    need = cap - jnp.sum(above, axis=1, keepdims=True)

    blk = MXU_DEPTH
    blocks = [slice(k * blk, (k + 1) * blk) for k in range(n_tok // blk)]
    r_i = lax.broadcasted_iota(jnp.int32, (blk, blk), 0)
    c_i = lax.broadcasted_iota(jnp.int32, (blk, blk), 1)
    strict_upper = (r_i < c_i).astype(BF16)

    def running(block_sums):
        run, total = [], jnp.zeros((N_EXPERTS, 1), F32)
        for s in block_sums:
            run.append(total)
            total = total + s
        return run

    tied_b = [tied[:, sl] for sl in blocks]
    tied_rank = [_dot(t.astype(BF16), strict_upper) for t in tied_b]
    tied_before = running([t.sum(axis=1, keepdims=True) for t in tied_b])
    sel_b = [above[:, sl] + t * ((before + rank) < need).astype(F32)
             for sl, t, before, rank in zip(blocks, tied_b, tied_before, tied_rank)]
    sel_rank = [_dot(s.astype(BF16), strict_upper) for s in sel_b]
    half_sums = [[s[:, h * OFFSET_BLOCK:(h + 1) * OFFSET_BLOCK].sum(axis=1, keepdims=True)
                  for h in range(blk // OFFSET_BLOCK)] for s in sel_b]
    offs = running([h for hs in half_sums for h in hs])
    per_blk = blk // OFFSET_BLOCK
    for k, (sl, s, rank) in enumerate(zip(blocks, sel_b, sel_rank)):
        pos_ref[0, :, sl] = jnp.where(s > 0.5, (offs[k * per_blk] + rank).astype(jnp.int32), -1)
    off_ref[0] = jnp.concatenate(offs, axis=1).astype(jnp.int32)


def expert_choice_select(aff_t, cap):
    bx, n_e, t = aff_t.shape
    n_tb = t // OFFSET_BLOCK
    return pl.pallas_call(
        functools.partial(_select_kernel, cap=cap, n_tok=t),
        grid=(bx,),
        in_specs=[pl.BlockSpec((1, n_e, t), lambda b: (b, 0, 0))],
        out_specs=[pl.BlockSpec((1, n_e, t), lambda b: (b, 0, 0)),
                   pl.BlockSpec((1, n_e, n_tb), lambda b: (b, 0, 0))],
        out_shape=[jax.ShapeDtypeStruct((bx, n_e, t), jnp.int32),
                   jax.ShapeDtypeStruct((bx, n_e, n_tb), jnp.int32)],
        compiler_params=_params("parallel"),
        name="expert_choice_select",
    )(aff_t)


GATE_PARTS = 3
TOKEN_LANE = GATE_PARTS * N_EXPERTS


def _slot_table_kernel(off_ref, pos_ref, aff_ref, tbl_ref, *, n_off, blocks_per_step):
    b, eg, kc = pl.program_id(0), pl.program_id(1), pl.program_id(2)
    epg = EXPERTS_PER_GATHER_STEP

    @pl.when(kc == 0)
    def _():
        tbl_ref[...] = jnp.zeros_like(tbl_ref)

    slot = lax.broadcasted_iota(jnp.int32, (GATHER_WINDOW, 1), 0)
    slot_2d = lax.broadcasted_iota(jnp.int32, (GATHER_WINDOW, GATHER_BLOCK), 0)
    lane = lax.broadcasted_iota(jnp.int32, (GATHER_BLOCK, LANES), 1)
    out_lane = lax.broadcasted_iota(jnp.int32, (GATHER_WINDOW, LANES), 1)
    local_token = lax.broadcasted_iota(jnp.int32, (GATHER_BLOCK, LANES), 0).astype(F32)
    place_r = lax.broadcasted_iota(jnp.int32, (N_EXPERTS, LANES), 0)
    place_c = lax.broadcasted_iota(jnp.int32, (N_EXPERTS, LANES), 1)
    def payload_of(kk):
        payload = jnp.where(lane == TOKEN_LANE, local_token, 0.0)
        rest = aff_ref[0, kk * GATHER_BLOCK:(kk + 1) * GATHER_BLOCK]
        for k in range(GATE_PARTS):
            part = rest.astype(BF16)
            payload = payload + _dot(part, (place_c == place_r + k * N_EXPERTS).astype(BF16))
            rest = rest - part.astype(F32)
        return payload.astype(BF16)

    payloads = [payload_of(kk) for kk in range(blocks_per_step)]
    pending = []
    for kk in range(blocks_per_step):
        kb = kc * blocks_per_step + kk
        first_token = jnp.where(out_lane == TOKEN_LANE, (kb * GATHER_BLOCK).astype(F32), 0.0)
        for ee in range(epg):
            off = off_ref[(b * N_EXPERTS + eg * epg + ee) * n_off + kb * (GATHER_BLOCK // OFFSET_BLOCK)]
            base = pl.multiple_of((off >> 4) << 4, WINDOW_ALIGN)
            onehot = (pos_ref[0, ee, 0, kk:kk + 1, :] - base == slot_2d).astype(BF16)
            pending.append((ee, off, base, _dot(onehot, payloads[kk]) + first_token))
    for ee, off, base, gathered in pending:
        win = pl.ds(base, GATHER_WINDOW)
        tbl_ref[0, ee, win, :] = jnp.where(slot >= off - base, gathered, tbl_ref[0, ee, win, :])


def slot_table(aff, pos, block_off, cap_pad):
    bx, t, _ = aff.shape
    n_tb = t // GATHER_BLOCK
    blocks_per_step = min(4, n_tb)
    n_steps = n_tb // blocks_per_step
    epg = EXPERTS_PER_GATHER_STEP
    tokens = blocks_per_step * GATHER_BLOCK
    pos5 = pos.reshape(bx, N_EXPERTS, n_steps, blocks_per_step, GATHER_BLOCK)
    grid_spec = pltpu.PrefetchScalarGridSpec(
        num_scalar_prefetch=1,
        grid=(bx, N_EXPERTS // epg, n_steps),
        in_specs=[
            pl.BlockSpec((1, epg, 1, blocks_per_step, GATHER_BLOCK), lambda b, g, k, s: (b, g, k, 0, 0)),
            pl.BlockSpec((1, tokens, N_EXPERTS), lambda b, g, k, s: (b, k, 0)),
        ],
        out_specs=pl.BlockSpec((1, epg, cap_pad, LANES), lambda b, g, k, s: (b, g, 0, 0)),
    )
    return pl.pallas_call(
        functools.partial(_slot_table_kernel, n_off=t // OFFSET_BLOCK, blocks_per_step=blocks_per_step),
        grid_spec=grid_spec,
        out_shape=jax.ShapeDtypeStruct((bx, N_EXPERTS, cap_pad, LANES), F32),
        compiler_params=_params("parallel", "parallel", "arbitrary"),
        name="slot_table",
    )(block_off.reshape(-1), pos5, aff)


GATHER_UNROLL = 16


def _row_gather_kernel(idx_ref, src_ref, xs_ref, tile_ref, *, cap, rows_per_token, chunk_stride):
    b, e = pl.program_id(0), pl.program_id(1)
    first = (b * N_EXPERTS + e) * cap

    def group(g, carry):
        for u in range(GATHER_UNROLL):
            s = g * GATHER_UNROLL + u
            row = pl.multiple_of(idx_ref[first + s] * rows_per_token, rows_per_token)
            tile_ref[pl.ds(s, rows_per_token, stride=chunk_stride), :] = src_ref[0, pl.ds(row, rows_per_token), :]
        return carry

    lax.fori_loop(0, cap // GATHER_UNROLL, group, 0)
    for j in range(rows_per_token):
        xs_ref[0, 0, :, j * LANES:(j + 1) * LANES] = tile_ref[j * chunk_stride:j * chunk_stride + cap, :].astype(BF16)


def gather_rows(h2_rows, token_idx, cap, d):
    bx, n_rows, _ = h2_rows.shape
    rows_per_token = d // LANES
    chunk_stride = cap + 8
    grid_spec = pltpu.PrefetchScalarGridSpec(
        num_scalar_prefetch=1,
        grid=(bx, N_EXPERTS),
        in_specs=[pl.BlockSpec((1, n_rows, LANES), lambda b, e, s: (b, 0, 0), pipeline_mode=pl.Buffered(1))],
        out_specs=pl.BlockSpec((1, 1, cap, d), lambda b, e, s: (b, e, 0, 0)),
        scratch_shapes=[pltpu.VMEM((rows_per_token * chunk_stride, LANES), F32)],
    )
    return pl.pallas_call(
        functools.partial(_row_gather_kernel, cap=cap, rows_per_token=rows_per_token, chunk_stride=chunk_stride),
        grid_spec=grid_spec,
        out_shape=jax.ShapeDtypeStruct((bx, N_EXPERTS, cap, d), BF16),
        compiler_params=_params("parallel", "arbitrary"),
        name="gather_rows",
    )(token_idx, h2_rows)


def _ffn_kernel(xs_ref, gs_ref, wg_ref, wu_ref, wd_ref, ye_ref, wg_bf, wu_bf, wd_bf, *, cap, row_tile):
    @pl.when(pl.program_id(1) == 0)
    def _():
        wg_bf[...] = wg_ref[0, 0].astype(BF16)
        wu_bf[...] = wu_ref[0, 0].astype(BF16)
        wd_bf[...] = wd_ref[0, 0].astype(BF16)

    n_b = xs_ref.shape[0]
    if n_b == 1:
        tiles = [[(0, r0, row_tile)] for r0 in range(0, cap, row_tile)]
    else:
        tiles = [[(bb, 0, cap) for bb in range(n_b)]]
    for tile in tiles:
        x = jnp.concatenate([xs_ref[bb, 0, r0:r0 + n] for bb, r0, n in tile], axis=0)
        terms = jnp.concatenate([gs_ref[bb, 0, r0:r0 + n, :] for bb, r0, n in tile], axis=0)
        lane = lax.broadcasted_iota(jnp.int32, terms.shape, 1)
        own = ((lane & (N_EXPERTS - 1)) == pl.program_id(0)) & (lane < TOKEN_LANE)
        terms = jnp.where(own, terms, 0.0)
        gate = _dot(x, wg_bf[...])
        up = _dot(x, wu_bf[...])
        hid = (gate * jax.nn.sigmoid(gate) * up).astype(BF16)
        g = jnp.sum(terms, axis=1, keepdims=True)
        ye = _dot(hid, wd_bf[...]) * g
        rows_per_slot = ye.shape[1] // LANES
        row = 0
        for bb, r0, n in tile:
            for j in range(rows_per_slot):
                ye_ref[bb, 0, pl.ds(r0 * rows_per_slot + j, n, stride=rows_per_slot), :] = (
                    ye[row:row + n, j * LANES:(j + 1) * LANES])
            row += n


def expert_ffn(xs, gs, wg, wu, wd, layer):
    bx, n_e, cap, d = xs.shape
    ff = wg.shape[-1]
    slot_rows = cap * (d // LANES)
    max_rows = 512
    row_tile = min(max_rows, cap)
    n_b = bx if bx * cap <= max_rows else 1
    return pl.pallas_call(
        functools.partial(_ffn_kernel, cap=cap, row_tile=row_tile),
        grid=(n_e, bx // n_b),
        in_specs=[
            pl.BlockSpec((n_b, 1, cap, d), lambda e, b: (b, e, 0, 0)),
            pl.BlockSpec((n_b, 1, cap, LANES), lambda e, b: (b, e, 0, 0)),
            pl.BlockSpec((1, 1, d, ff), lambda e, b: (layer, e, 0, 0)),
            pl.BlockSpec((1, 1, d, ff), lambda e, b: (layer, e, 0, 0)),
            pl.BlockSpec((1, 1, ff, d), lambda e, b: (layer, e, 0, 0)),
        ],
        out_specs=pl.BlockSpec((n_b, 1, slot_rows, LANES), lambda e, b: (b, e, 0, 0)),
        out_shape=jax.ShapeDtypeStruct((bx, n_e, slot_rows, LANES), F32),
        scratch_shapes=[pltpu.VMEM((d, ff), BF16), pltpu.VMEM((d, ff), BF16), pltpu.VMEM((ff, d), BF16)],
        compiler_params=_params("parallel", "arbitrary"),
        name="expert_ffn",
    )(xs, gs, wg, wu, wd)


SCATTER_UNROLL = 16


def _row_scatter_kernel(idx_ref, ye_ref, acc_ref, *, cap, rows_per_token):
    b, e = pl.program_id(0), pl.program_id(1)

    @pl.when(e == 0)
    def _():
        acc_ref[...] = jnp.zeros_like(acc_ref)

    first = (b * N_EXPERTS + e) * cap

    def group(g, carry):
        rows, sums = [], []
        for u in range(SCATTER_UNROLL):
            s = g * SCATTER_UNROLL + u
            row = pl.multiple_of(idx_ref[first + s] * rows_per_token, rows_per_token)
            src = pl.multiple_of(s * rows_per_token, rows_per_token)
            rows.append(row)
            sums.append(acc_ref[0, pl.ds(row, rows_per_token), :] + ye_ref[0, 0, pl.ds(src, rows_per_token), :])
        for row, total in zip(rows, sums):
            acc_ref[0, pl.ds(row, rows_per_token), :] = total
        return carry

    lax.fori_loop(0, cap // SCATTER_UNROLL, group, 0)


def scatter_rows(ye_rows, token_idx, n_tok, cap):
    bx, n_e, slot_rows, _ = ye_rows.shape
    rows_per_token = slot_rows // cap
    grid_spec = pltpu.PrefetchScalarGridSpec(
        num_scalar_prefetch=1,
        grid=(bx, n_e),
        in_specs=[pl.BlockSpec((1, 1, slot_rows, LANES), lambda b, e, s: (b, e, 0, 0))],
        out_specs=pl.BlockSpec((1, n_tok * rows_per_token, LANES), lambda b, e, s: (b, 0, 0),
                               pipeline_mode=pl.Buffered(1)),
    )
    return pl.pallas_call(
        functools.partial(_row_scatter_kernel, cap=cap, rows_per_token=rows_per_token),
        grid_spec=grid_spec,
        out_shape=jax.ShapeDtypeStruct((bx, n_tok * rows_per_token, LANES), F32),
        compiler_params=_params("parallel", "arbitrary"),
        name="scatter_rows",
    )(token_idx, ye_rows)


def _final_norm_kernel(moe_ref, x1_ref, g2_ref, lng_ref, lnb_ref, out_ref, *, alpha):
    tm, d = x1_ref.shape[1], x1_ref.shape[2]
    rows_per_token = d // LANES
    moe = jnp.concatenate([moe_ref[0, pl.ds(j, tm, stride=rows_per_token), :] for j in range(rows_per_token)],
                          axis=1)
    y = alpha * x1_ref[0] + g2_ref[0] * moe
    out_ref[0] = _layer_norm(y) * lng_ref[...] + lnb_ref[...]


def residual_norm(moe_rows, x1, g2, ln_g, ln_b, alpha):
    bx, t, d = x1.shape
    tm = min(512, t)
    rows_per_token = d // LANES
    tok = lambda b, i: (b, i, 0)
    const = lambda b, i: (0, 0)
    return pl.pallas_call(
        functools.partial(_final_norm_kernel, alpha=alpha),
        grid=(bx, t // tm),
        in_specs=[
            pl.BlockSpec((1, tm * rows_per_token, LANES), tok),
            pl.BlockSpec((1, tm, d), tok),
            pl.BlockSpec((1, 1, d), lambda b, i: (b, 0, 0)),
            pl.BlockSpec((1, d), const), pl.BlockSpec((1, d), const),
        ],
        out_specs=pl.BlockSpec((1, tm, d), tok),
        out_shape=jax.ShapeDtypeStruct((bx, t, d), F32),
        compiler_params=_params("parallel", "parallel"),
        name="residual_norm",
    )(moe_rows, x1, g2, ln_g.reshape(1, d), ln_b.reshape(1, d))


def _split_bf16(w):
    hi = w.astype(BF16)
    return hi, (w - hi.astype(F32)).astype(BF16)


def _mixer_tail(oa, ob, oc, x, mod, lw, alpha):
    g1, sh2, sc2, g2 = mod
    t = x.shape[1]
    cap = EC_CAPACITY * t // N_EXPERTS
    cap_pad = cap + SLOT_PAD
    x1, h2_rows, aff_t, aff = out_projection(oa, ob, oc, x, g1, sh2, sc2, lw["w_out"], lw["ln1_g"], lw["ln1_b"],
                                        lw["wr_hi"], lw["wr_lo"], alpha)
    pos, block_off = expert_choice_select(aff_t, cap)
    table = slot_table(aff, pos, block_off, cap_pad)
    token_idx = table[:, :, :cap, TOKEN_LANE].astype(jnp.int32).reshape(-1)
    xs = gather_rows(h2_rows, token_idx, cap, x.shape[2])
    ye_rows = expert_ffn(xs, table, lw["w_gate"], lw["w_up"], lw["w_down"], lw["layer"])
    moe_rows = scatter_rows(ye_rows, token_idx, t, cap)
    return residual_norm(moe_rows, x1, g2, lw["ln2_g"], lw["ln2_b"], alpha)


def kernel(x, c, ctx, c_ctx, w_mod, b_mod, w_in, a_sink, nat_bias, conv_w, conv_b, conv_ln_g, conv_ln_b,
           w_out, ln1_g, ln1_b, w_router, w_gate, w_up, w_down, ln2_g, ln2_b):
    bsz, n_lat, d = x.shape
    depth = w_mod.shape[0]
    alpha = (2 * depth) ** 0.25
    cos_t, sin_t = rope_tables(n_lat)

    cond = jnp.concatenate([c, c_ctx[None, :], jnp.zeros((8 - bsz - 1, d), F32)], axis=0)
    mods = adaln_all(cond, w_mod, b_mod)

    for l in range(depth):
        last = l == depth - 1
        wr_hi, wr_lo = _split_bf16(w_router[l].T)
        lw = dict(w_out=w_out[l].astype(BF16), ln1_g=ln1_g[l], ln1_b=ln1_b[l], wr_hi=wr_hi, wr_lo=wr_lo,
                  w_gate=w_gate, w_up=w_up, w_down=w_down, layer=l,
                  ln2_g=ln2_g[l], ln2_b=ln2_b[l])
        w_in_l = w_in[l].astype(BF16)
        lat = [mods[l, :bsz, k * d:(k + 1) * d][:, None, :] for k in range(N_MOD)]
        cm = [jnp.broadcast_to(mods[l, bsz, k * d:(k + 1) * d][None, None, :], (bsz, 1, d))
              for k in range(N_MOD)]
        conv_args = (conv_w[l], conv_b[l], conv_ln_g[l], conv_ln_b[l])

        qa_c, ka_c, va_c, qb_c, kb_c, vb_c, hc_c = in_projection(ctx, cm[0], cm[1], w_in_l, cos_t, sin_t, rope=False)
        if not last:
            oa_c, ob_c = context_attention(qa_c, qb_c, ka_c, va_c, kb_c, vb_c, a_sink[l])
            oc_c = conformer_conv(hc_c, *conv_args)
            ctx_new = _mixer_tail(oa_c, ob_c, oc_c, ctx, (cm[2], cm[3], cm[4], cm[5]), lw, alpha)

        qa, ka, va, qb, kb, vb, hc = in_projection(x, lat[0], lat[1], w_in_l, cos_t, sin_t, rope=True)
        oa = window_attention(qa, ka, va, ka_c, va_c, a_sink[l])
        ob = neighbourhood_attention(qb, kb, vb, kb_c, vb_c, nat_bias[l])
        oc = conformer_conv(hc, *conv_args)
        x = _mixer_tail(oa, ob, oc, x, (lat[2], lat[3], lat[4], lat[5]), lw, alpha)
        if not last:
            ctx = ctx_new
    return x
```

```python
import functools

import numpy as np
import jax
import jax.numpy as jnp
from jax import lax
from jax.experimental import pallas as pl
from jax.experimental.pallas import tpu as pltpu

HEAD_DIM = 64
GRID_W = 64
A_Q_HEADS = 8
A_KV_HEADS = 2
A_GROUP = A_Q_HEADS // A_KV_HEADS
A_WINDOW = 128
B_HEADS = 4
NA_ROWS = 8
NA_COLS = 16
C_CHANNELS = 256
C_CONV_WIDTH = 31
A_Q_W = A_Q_HEADS * HEAD_DIM
A_KV_W = A_KV_HEADS * HEAD_DIM
B_W = B_HEADS * HEAD_DIM
OFF_AK = A_Q_W
OFF_AV = OFF_AK + A_KV_W
OFF_BQ = OFF_AV + A_KV_W
OFF_BK = OFF_BQ + B_W
OFF_BV = OFF_BK + B_W
OFF_C = OFF_BV + B_W
IN_WIDTH = OFF_C + 2 * C_CHANNELS
ROPE_WIDTH = A_Q_W + A_KV_W
N_EXPERTS = 16
EC_CAPACITY = 2
ROPE_BASE = 10000.0
LN_EPS = 1e-6
N_MOD = 6
NEG_INF = -1e30
QK_SCALE = HEAD_DIM ** -0.5

LANES = 128
WINDOW_ALIGN = 16
MXU_DEPTH = 256
OFFSET_BLOCK = 128
GATHER_BLOCK = MXU_DEPTH
GATHER_WINDOW = GATHER_BLOCK + WINDOW_ALIGN
SLOT_PAD = 3 * LANES
EXPERTS_PER_GATHER_STEP = 16
VMEM_LIMIT = 56 * 1024 * 1024

F32 = jnp.float32
BF16 = jnp.bfloat16


def _dot(a, b):
    return jnp.dot(a, b, preferred_element_type=F32)


def _dot_t(a, b):
    return lax.dot_general(a, b, (((1,), (1,)), ((), ())), preferred_element_type=F32)


def _layer_norm(x):
    mu = jnp.mean(x, axis=-1, keepdims=True)
    xc = x - mu
    var = jnp.mean(xc * xc, axis=-1, keepdims=True)
    return xc * lax.rsqrt(var + LN_EPS)


def _params(*sem):
    return pltpu.CompilerParams(dimension_semantics=sem, vmem_limit_bytes=VMEM_LIMIT)


def _mod_kernel(cond_ref, w_ref, b_ref, out_ref):
    cnd = cond_ref[...]
    act = cnd * jax.nn.sigmoid(cnd)
    out_ref[0] = jnp.dot(act, w_ref[0], preferred_element_type=F32,
                         precision=lax.Precision.HIGHEST) + b_ref[0]


def adaln_all(cond, w_mod, b_mod):
    n_layers, d, width = w_mod.shape
    rows = cond.shape[0]
    tn = 1536
    return pl.pallas_call(
        _mod_kernel,
        grid=(n_layers, width // tn),
        in_specs=[
            pl.BlockSpec((rows, d), lambda l, j: (0, 0)),
            pl.BlockSpec((1, d, tn), lambda l, j: (l, 0, j)),
            pl.BlockSpec((1, 1, tn), lambda l, j: (l, 0, j)),
        ],
        out_specs=pl.BlockSpec((1, rows, tn), lambda l, j: (l, 0, j)),
        out_shape=jax.ShapeDtypeStruct((n_layers, rows, width), F32),
        compiler_params=_params("parallel", "parallel"),
        name="adaln",
    )(cond, w_mod, b_mod.reshape(n_layers, 1, width))


def _inproj_kernel(x_ref, sh_ref, sc_ref, w_ref, cos_ref, sin_ref,
                   qa_ref, ka_ref, va_ref, qb_ref, kb_ref, vb_ref, hc_ref, *, rope):
    x = x_ref[0]
    h = _layer_norm(x) * (1.0 + sc_ref[0]) + sh_ref[0]
    u = _dot(h.astype(BF16), w_ref[...])

    def rotated(col):
        xq = u[:, col:col + LANES]
        if not rope:
            return xq
        lane = lax.broadcasted_iota(jnp.int32, xq.shape, 1)
        first = (lane & (HEAD_DIM // 2 - 1)) < (HEAD_DIM // 4)
        partner = jnp.where(first, pltpu.roll(xq, LANES - HEAD_DIM // 4, 1),
                            pltpu.roll(xq, HEAD_DIM // 4, 1))
        return xq * cos_ref[...] + partner * sin_ref[...]

    rot = [rotated(col) for col in range(0, ROPE_WIDTH, LANES)]
    n_q = A_Q_W // LANES
    qa_ref[0] = (jnp.concatenate(rot[:n_q], axis=1) * QK_SCALE).astype(BF16)
    ka_ref[0] = jnp.concatenate(rot[n_q:], axis=1).astype(BF16)
    va_ref[0] = u[:, OFF_AV:OFF_BQ].astype(BF16)
    qb_ref[0] = (u[:, OFF_BQ:OFF_BK] * QK_SCALE).astype(BF16)
    kb_ref[0] = u[:, OFF_BK:OFF_BV].astype(BF16)
    vb_ref[0] = u[:, OFF_BV:OFF_C].astype(BF16)
    a = u[:, OFF_C:OFF_C + C_CHANNELS]
    gate = u[:, OFF_C + C_CHANNELS:]
    hc_ref[0] = a * jax.nn.sigmoid(gate)


def in_projection(x, shift, scale, w_in_bf16, cos_t, sin_t, *, rope):
    bx, t, d = x.shape
    tm = min(512, t)
    widths = (A_Q_W, A_KV_W, A_KV_W, B_W, B_W, B_W, C_CHANNELS)
    dtypes = (BF16,) * 6 + (F32,)
    tok = lambda b, i: (b, i, 0)
    per_b = lambda b, i: (b, 0, 0)
    return pl.pallas_call(
        functools.partial(_inproj_kernel, rope=rope),
        grid=(bx, t // tm),
        in_specs=[
            pl.BlockSpec((1, tm, d), tok),
            pl.BlockSpec((1, 1, d), per_b),
            pl.BlockSpec((1, 1, d), per_b),
            pl.BlockSpec((d, IN_WIDTH), lambda b, i: (0, 0)),
            pl.BlockSpec((tm, LANES), lambda b, i: (i, 0)),
            pl.BlockSpec((tm, LANES), lambda b, i: (i, 0)),
        ],
        out_specs=[pl.BlockSpec((1, tm, w), tok) for w in widths],
        out_shape=[jax.ShapeDtypeStruct((bx, t, w), dt) for w, dt in zip(widths, dtypes)],
        compiler_params=_params("parallel", "parallel"),
        name="in_projection",
    )(x, shift, scale, w_in_bf16, cos_t, sin_t)


def rope_tables(n_tokens):
    t = jnp.arange(n_tokens, dtype=jnp.int32)
    row = (t // GRID_W).astype(F32)[:, None]
    col = (t % GRID_W).astype(F32)[:, None]
    n_freq = HEAD_DIM // 4
    inv_freq = ROPE_BASE ** (-jnp.arange(n_freq, dtype=F32) / n_freq)
    ang_r = row * inv_freq
    ang_c = col * inv_freq
    cos_h = jnp.concatenate([jnp.cos(ang_r), jnp.cos(ang_r), jnp.cos(ang_c), jnp.cos(ang_c)], axis=1)
    sin_h = jnp.concatenate([-jnp.sin(ang_r), jnp.sin(ang_r), -jnp.sin(ang_c), jnp.sin(ang_c)], axis=1)
    reps = LANES // HEAD_DIM
    return jnp.tile(cos_h, (1, reps)), jnp.tile(sin_h, (1, reps))


def _with_ones(v):
    return jnp.concatenate([v, jnp.ones_like(v)], axis=1)


def _attend(score_parts, values, sink=None):
    m = score_parts[0].max(axis=-1, keepdims=True)
    for s in score_parts[1:]:
        m = jnp.maximum(m, s.max(axis=-1, keepdims=True))
    if sink is not None:
        m = jnp.maximum(m, sink)
    acc = None
    for s, v in zip(score_parts, values):
        term = _dot(jnp.exp((s - m).astype(BF16)), v)
        acc = term if acc is None else acc + term
    den = acc[:, HEAD_DIM:HEAD_DIM + 1]
    if sink is not None:
        den = den + jnp.exp(sink - m)
    return acc[:, :HEAD_DIM] / den


def _attn_a_kernel(sink_ref, q_ref, kp_ref, kc_ref, kn_ref, vp_ref, vc_ref, vn_ref,
                   kctx_ref, vctx_ref, out_ref, *, n_lat, tq):
    i = pl.program_id(1)
    k_win = jnp.concatenate([kp_ref[0], kc_ref[0], kn_ref[0]], axis=0)
    v_win = jnp.concatenate([vp_ref[0], vc_ref[0], vn_ref[0]], axis=0)
    kctx = kctx_ref[0]
    vctx = vctx_ref[0]
    sub = A_WINDOW
    span = 3 * A_WINDOW
    rows = A_GROUP * sub
    q_onehot = ((lax.broadcasted_iota(jnp.int32, (rows, sub), 0) & (sub - 1))
                == lax.broadcasted_iota(jnp.int32, (rows, sub), 1)).astype(BF16)
    key_i = lax.broadcasted_iota(jnp.int32, (span, sub), 0)
    qry_i = lax.broadcasted_iota(jnp.int32, (span, sub), 1)
    rel = key_i - A_WINDOW - qry_i
    in_band = (rel <= A_WINDOW) & (rel >= -A_WINDOW)
    group_of_row = lax.broadcasted_iota(jnp.int32, (rows, 1), 0) >> (sub.bit_length() - 1)
    sinks = []
    for hk in range(A_KV_HEADS):
        sink = jnp.zeros((rows, 1), F32)
        for g in range(A_GROUP):
            sink = jnp.where(group_of_row == g, sink_ref[hk * A_GROUP + g], sink)
        sinks.append(sink)
    vctx_ext = [_with_ones(vctx[:, hk * HEAD_DIM:(hk + 1) * HEAD_DIM]) for hk in range(A_KV_HEADS)]
    v_ext = [_with_ones(v_win[:, hk * HEAD_DIM:(hk + 1) * HEAD_DIM]) for hk in range(A_KV_HEADS)]
    def mask_columns(j):
        kpos = i * tq + j * sub - A_WINDOW + key_i
        valid = in_band & (kpos >= 0) & (kpos < n_lat)
        return jnp.where(valid, 0.0, NEG_INF).astype(BF16)

    masks = [mask_columns(j) for j in range(tq // sub)]

    def scores(j, hk):
        sl = slice(hk * HEAD_DIM, (hk + 1) * HEAD_DIM)
        q_rows = q_ref[0, j * sub:(j + 1) * sub]
        q = jnp.concatenate([q_rows[:, h * HEAD_DIM:(h + 1) * HEAD_DIM]
                             for h in range(hk * A_GROUP, (hk + 1) * A_GROUP)], axis=0)
        q_aug = jnp.concatenate([q_onehot, q], axis=1)
        k_aug = jnp.concatenate([masks[j], k_win[j * sub:j * sub + span, sl]], axis=1)
        return [_dot_t(q_aug, k_aug), _dot_t(q, kctx[:, sl])]

    units = [(j, hk) for j in range(tq // sub) for hk in range(A_KV_HEADS)]
    all_scores = [scores(j, hk) for j, hk in units]
    for (j, hk), s in zip(units, all_scores):
        v_sub = v_ext[hk][j * sub:j * sub + span]
        o = _attend(s, [v_sub, vctx_ext[hk]], sink=sinks[hk]).astype(BF16)
        for g in range(A_GROUP):
            h = hk * A_GROUP + g
            out_ref[0, j * sub:(j + 1) * sub, h * HEAD_DIM:(h + 1) * HEAD_DIM] = o[g * sub:(g + 1) * sub]


def window_attention(qa, ka, va, kc_a, vc_a, sink):
    bsz, n_lat, _ = qa.shape
    n_ctx = kc_a.shape[1]
    tq = min(512, n_lat)
    w = A_WINDOW
    per = tq // w
    last = n_lat // w - 1
    prev = lambda b, i, s: (b, jnp.maximum(i * per - 1, 0), 0)
    cur = lambda b, i, s: (b, i, 0)
    nxt = lambda b, i, s: (b, jnp.minimum((i + 1) * per, last), 0)
    ctx = lambda b, i, s: (b, 0, 0)
    kv_specs = [pl.BlockSpec((1, w, A_KV_W), prev), pl.BlockSpec((1, tq, A_KV_W), cur),
                pl.BlockSpec((1, w, A_KV_W), nxt)]
    grid_spec = pltpu.PrefetchScalarGridSpec(
        num_scalar_prefetch=1,
        grid=(bsz, n_lat // tq),
        in_specs=[pl.BlockSpec((1, tq, A_Q_W), cur)] + kv_specs + kv_specs + [
            pl.BlockSpec((1, n_ctx, A_KV_W), ctx), pl.BlockSpec((1, n_ctx, A_KV_W), ctx)],
        out_specs=pl.BlockSpec((1, tq, A_Q_W), cur),
    )
    return pl.pallas_call(
        functools.partial(_attn_a_kernel, n_lat=n_lat, tq=tq),
        grid_spec=grid_spec,
        out_shape=jax.ShapeDtypeStruct((bsz, n_lat, A_Q_W), BF16),
        compiler_params=_params("parallel", "parallel"),
        name="window_attention",
    )(sink, qa, ka, ka, ka, va, va, va, kc_a, vc_a)


NB_Q_ROWS = 4


def _attn_b_kernel(q_ref, kp_ref, kc_ref, kn_ref, vp_ref, vc_ref, vn_ref,
                   kctx_ref, vctx_ref, bias_ref, out_ref):
    k_win = jnp.concatenate([kp_ref[0], kc_ref[0], kn_ref[0]], axis=0)
    v_win = jnp.concatenate([vp_ref[0], vc_ref[0], vn_ref[0]], axis=0)
    q_all = q_ref[0]
    kctx = kctx_ref[0]
    vctx = vctx_ref[0]
    heads = [slice(h * HEAD_DIM, (h + 1) * HEAD_DIM) for h in range(B_HEADS)]
    scores = [[_dot_t(q_all[:, sl], k_win[:, sl]) + bias_ref[0, h], _dot_t(q_all[:, sl], kctx[:, sl])]
              for h, sl in enumerate(heads)]
    for sl, s in zip(heads, scores):
        o = _attend(s, [_with_ones(v_win[:, sl]), _with_ones(vctx[:, sl])])
        out_ref[0, :, sl] = o.astype(BF16)


def neighbourhood_bias(rel_bias, n_lat):
    rows = n_lat // GRID_W
    kr_n = min(NA_ROWS, rows)
    n_blocks = rows // NB_Q_ROWS
    n_heads, n_dr, n_dc = rel_bias.shape
    cols = np.arange(GRID_W)
    c_start = np.clip(cols - NA_COLS // 2, 0, GRID_W - NA_COLS)
    col_ok = (cols[None, :] >= c_start[:, None]) & (cols[None, :] < c_start[:, None] + NA_COLS)
    dc = np.clip(cols[None, :] - cols[:, None], -(NA_COLS - 1), NA_COLS - 1) + NA_COLS - 1
    pick_dc = (dc.reshape(-1)[None, :] == np.arange(n_dc)[:, None]).astype(np.float32)
    toeplitz = jnp.dot(rel_bias.reshape(n_heads * n_dr, n_dc), pick_dc, precision=lax.Precision.HIGHEST)
    toeplitz = jnp.where(col_ok.reshape(-1), toeplitz, NEG_INF).reshape(n_heads, n_dr, GRID_W, GRID_W)
    q_rl = np.arange(NB_Q_ROWS)
    k_rl = np.arange(3 * NB_Q_ROWS)
    row_ok, dr = [], []
    for j in sorted({0, min(1, n_blocks - 1), n_blocks - 1}):
        r = NB_Q_ROWS * j + q_rl
        kr = NB_Q_ROWS * (j - 1) + k_rl
        r_start = np.clip(r - kr_n // 2, 0, rows - kr_n)
        ok = (kr[None, :] >= r_start[:, None]) & (kr[None, :] < r_start[:, None] + kr_n)
        row_ok.append(ok & (kr[None, :] >= 0) & (kr[None, :] < rows))
        dr.append(np.clip(kr[None, :] - r[:, None] + NA_ROWS - 1, 0, n_dr - 1))
    row_ok = np.stack(row_ok)
    dr = np.stack(dr)
    tiles = jnp.stack([toeplitz[:, int(i)] for i in dr.reshape(-1)], axis=1)
    tiles = tiles.reshape((n_heads,) + dr.shape + (GRID_W, GRID_W))
    tiles = jnp.where(row_ok[None, :, :, :, None, None], tiles, NEG_INF)
    table = tiles.transpose(1, 0, 2, 4, 3, 5).reshape(
        dr.shape[0], n_heads, NB_Q_ROWS * GRID_W, 3 * NB_Q_ROWS * GRID_W)
    return table, n_blocks


def neighbourhood_attention(qb, kb, vb, kc_b, vc_b, rel_bias):
    bsz, n_lat, _ = qb.shape
    n_ctx = kc_b.shape[1]
    table, n_blocks = neighbourhood_bias(rel_bias, n_lat)
    n_var = table.shape[0]
    tq = NB_Q_ROWS * GRID_W
    prev = lambda b, j: (b, jnp.maximum(j - 1, 0), 0)
    cur = lambda b, j: (b, j, 0)
    nxt = lambda b, j: (b, jnp.minimum(j + 1, n_blocks - 1), 0)
    ctx = lambda b, j: (b, 0, 0)

    def variant(b, j):
        v = jnp.where(j == 0, 0, jnp.where(j == n_blocks - 1, n_var - 1, min(1, n_var - 1)))
        return (v, 0, 0, 0)

    kv_specs = [pl.BlockSpec((1, tq, B_W), prev), pl.BlockSpec((1, tq, B_W), cur),
                pl.BlockSpec((1, tq, B_W), nxt)]
    return pl.pallas_call(
        _attn_b_kernel,
        grid=(bsz, n_blocks),
        in_specs=[pl.BlockSpec((1, tq, B_W), cur)] + kv_specs + kv_specs + [
            pl.BlockSpec((1, n_ctx, B_W), ctx), pl.BlockSpec((1, n_ctx, B_W), ctx),
            pl.BlockSpec((1, B_HEADS, tq, 3 * tq), variant)],
        out_specs=pl.BlockSpec((1, tq, B_W), cur),
        out_shape=jax.ShapeDtypeStruct((bsz, n_lat, B_W), BF16),
        compiler_params=_params("parallel", "parallel"),
        name="neighbourhood_attention",
    )(qb, kb, kb, kb, vb, vb, vb, kc_b, vc_b, table)


def _ctx_attn_kernel(sink_ref, qa_ref, qb_ref, ka_ref, va_ref, kb_ref, vb_ref, oa_ref, ob_ref):
    qa, qb = qa_ref[0], qb_ref[0]
    ka, va, kb, vb = ka_ref[0], va_ref[0], kb_ref[0], vb_ref[0]
    for hq in range(A_Q_HEADS):
        sl = slice(hq * HEAD_DIM, (hq + 1) * HEAD_DIM)
        hk = hq // A_GROUP
        kv = slice(hk * HEAD_DIM, (hk + 1) * HEAD_DIM)
        q = qa[:, sl]
        o = _attend([_dot_t(q, ka[:, kv])], [_with_ones(va[:, kv])], sink=sink_ref[hq])
        oa_ref[0, :, sl] = o.astype(BF16)
    for h in range(B_HEADS):
        sl = slice(h * HEAD_DIM, (h + 1) * HEAD_DIM)
        q = qb[:, sl]
        o = _attend([_dot_t(q, kb[:, sl])], [_with_ones(vb[:, sl])])
        ob_ref[0, :, sl] = o.astype(BF16)


def context_attention(qa, qb, ka, va, kb, vb, sink):
    bsz, n_ctx, _ = qa.shape
    blk = lambda w: pl.BlockSpec((1, n_ctx, w), lambda b, s: (b, 0, 0))
    grid_spec = pltpu.PrefetchScalarGridSpec(
        num_scalar_prefetch=1,
        grid=(bsz,),
        in_specs=[blk(A_Q_W), blk(B_W), blk(A_KV_W), blk(A_KV_W), blk(B_W), blk(B_W)],
        out_specs=[blk(A_Q_W), blk(B_W)],
    )
    return pl.pallas_call(
        _ctx_attn_kernel,
        grid_spec=grid_spec,
        out_shape=[jax.ShapeDtypeStruct((bsz, n_ctx, A_Q_W), BF16),
                   jax.ShapeDtypeStruct((bsz, n_ctx, B_W), BF16)],
        compiler_params=_params("parallel"),
        name="context_attention",
    )(sink, qa, qb, ka, va, kb, vb)


CONV_HALO = 16
F32_SUBLANES = 8
CONV_SHIFT_SPAN = (CONV_HALO + C_CONV_WIDTH // 2) // F32_SUBLANES * F32_SUBLANES


def _conv_kernel(prev_ref, cur_ref, next_ref, w_ref, b_ref, g_ref, beta_ref, out_ref, shifted_ref, *, ts):
    i = pl.program_id(1)
    n_i = pl.num_programs(1)
    ext = jnp.concatenate([jnp.where(i > 0, prev_ref[0], 0.0), cur_ref[0],
                           jnp.where(i < n_i - 1, next_ref[0], 0.0)], axis=0)
    for r in range(F32_SUBLANES):
        shifted_ref[r] = ext[r:r + ts + CONV_SHIFT_SPAN]
    acc = jnp.zeros((ts, C_CHANNELS), F32)
    for k in range(C_CONV_WIDTH):
        start = CONV_HALO - C_CONV_WIDTH // 2 + k
        aligned = start - start % F32_SUBLANES
        acc = acc + shifted_ref[start % F32_SUBLANES, aligned:aligned + ts] * w_ref[k:k + 1]
    y = _layer_norm(acc + b_ref[...]) * g_ref[...] + beta_ref[...]
    out_ref[0] = (y * jax.nn.sigmoid(y)).astype(BF16)


def conformer_conv(hc, conv_w, conv_b, ln_g, ln_b):
    bx, t, ch = hc.shape
    ts = min(512, t)
    per = ts // CONV_HALO
    last = t // CONV_HALO - 1
    row = lambda v: v.reshape(1, ch)
    const = lambda b, i: (0, 0)
    return pl.pallas_call(
        functools.partial(_conv_kernel, ts=ts),
        grid=(bx, t // ts),
        in_specs=[
            pl.BlockSpec((1, CONV_HALO, ch), lambda b, i: (b, jnp.maximum(i * per - 1, 0), 0)),
            pl.BlockSpec((1, ts, ch), lambda b, i: (b, i, 0)),
            pl.BlockSpec((1, CONV_HALO, ch), lambda b, i: (b, jnp.minimum((i + 1) * per, last), 0)),
            pl.BlockSpec((C_CONV_WIDTH, ch), const),
            pl.BlockSpec((1, ch), const), pl.BlockSpec((1, ch), const), pl.BlockSpec((1, ch), const),
        ],
        out_specs=pl.BlockSpec((1, ts, ch), lambda b, i: (b, i, 0)),
        out_shape=jax.ShapeDtypeStruct((bx, t, ch), BF16),
        scratch_shapes=[pltpu.VMEM((F32_SUBLANES, ts + CONV_SHIFT_SPAN, ch), F32)],
        compiler_params=_params("parallel", "parallel"),
        name="conformer_conv",
    )(hc, hc, hc, conv_w, row(conv_b), row(ln_g), row(ln_b))


OUTPROJ_ROW_CHUNKS = 2


def _outproj_kernel(oa_ref, ob_ref, oc_ref, x_ref, g1_ref, sh_ref, sc_ref, w_ref, lng_ref, lnb_ref,
                    wr_hi_ref, wr_lo_ref, x1_ref, h2_ref, afft_ref, aff_ref, *, alpha):
    tm, d = x_ref.shape[1], x_ref.shape[2]
    rc = tm // OUTPROJ_ROW_CHUNKS
    chunks = [slice(k * rc, (k + 1) * rc) for k in range(OUTPROJ_ROW_CHUNKS)]
    outs = [(_dot(oa_ref[0, r], w_ref[0:A_Q_W])
             + _dot(ob_ref[0, r], w_ref[A_Q_W:A_Q_W + B_W])
             + _dot(oc_ref[0, r], w_ref[A_Q_W + B_W:])) for r in chunks]
    w_hi, w_lo = wr_hi_ref[...], wr_lo_ref[...]
    eye = (lax.broadcasted_iota(jnp.int32, (N_EXPERTS, N_EXPERTS), 0)
           == lax.broadcasted_iota(jnp.int32, (N_EXPERTS, N_EXPERTS), 1)).astype(BF16)
    rows_per_token = d // LANES
    for k, (r, o) in enumerate(zip(chunks, outs)):
        y = _layer_norm(alpha * x_ref[0, r] + g1_ref[0] * o) * lng_ref[...] + lnb_ref[...]
        x1_ref[0, r] = y
        h2 = _layer_norm(y) * (1.0 + sc_ref[0]) + sh_ref[0]
        h_hi = h2.astype(BF16)
        for j in range(rows_per_token):
            h2_ref[0, pl.ds(k * rc * rows_per_token + j, rc, stride=rows_per_token), :] = (
                h2[:, j * LANES:(j + 1) * LANES])
        h_lo = (h2 - h_hi.astype(F32)).astype(BF16)
        logits = _dot_t(h_hi, w_hi) + _dot_t(h_lo, w_hi) + _dot_t(h_hi, w_lo)
        e_n = jnp.exp(logits - logits.max(axis=1, keepdims=True))
        aff = e_n / e_n.sum(axis=1, keepdims=True)
        aff_ref[0, r] = aff
        aff_t, rest = None, aff
        for _ in range(GATE_PARTS):
            part = rest.astype(BF16)
            term = _dot_t(eye, part)
            aff_t = term if aff_t is None else aff_t + term
            rest = rest - part.astype(F32)
        afft_ref[0, :, r] = aff_t


def out_projection(oa, ob, oc, x, g1, sh2, sc2, w_out_bf16, ln_g, ln_b, wr_hi, wr_lo, alpha):
    bx, t, d = x.shape
    tm = min(512, t)
    tok = lambda b, i: (b, i, 0)
    per_b = lambda b, i: (b, 0, 0)
    const = lambda b, i: (0, 0)
    vec = pl.BlockSpec((1, d), const)
    return pl.pallas_call(
        functools.partial(_outproj_kernel, alpha=alpha),
        grid=(bx, t // tm),
        in_specs=[
            pl.BlockSpec((1, tm, A_Q_W), tok), pl.BlockSpec((1, tm, B_W), tok),
            pl.BlockSpec((1, tm, C_CHANNELS), tok), pl.BlockSpec((1, tm, d), tok),
            pl.BlockSpec((1, 1, d), per_b), pl.BlockSpec((1, 1, d), per_b), pl.BlockSpec((1, 1, d), per_b),
            pl.BlockSpec(w_out_bf16.shape, const), vec, vec,
            pl.BlockSpec((N_EXPERTS, d), const), pl.BlockSpec((N_EXPERTS, d), const),
        ],
        out_specs=[pl.BlockSpec((1, tm, d), tok), pl.BlockSpec((1, tm * (d // LANES), LANES), tok),
                   pl.BlockSpec((1, N_EXPERTS, tm), lambda b, i: (b, 0, i)),
                   pl.BlockSpec((1, tm, N_EXPERTS), tok)],
        out_shape=[jax.ShapeDtypeStruct((bx, t, d), F32),
                   jax.ShapeDtypeStruct((bx, t * (d // LANES), LANES), F32),
                   jax.ShapeDtypeStruct((bx, N_EXPERTS, t), F32),
                   jax.ShapeDtypeStruct((bx, t, N_EXPERTS), F32)],
        compiler_params=_params("parallel", "parallel"),
        name="out_projection",
    )(oa, ob, oc, x, g1, sh2, sc2, w_out_bf16, ln_g.reshape(1, d), ln_b.reshape(1, d), wr_hi, wr_lo)


def _select_kernel(afft_ref, pos_ref, off_ref, *, cap, n_tok):
    aff = afft_ref[0]

    def bit_step(j, bits):
        cand = bits | (jnp.int32(1) << (30 - j))
        cnt = jnp.sum((aff >= pltpu.bitcast(cand, F32)).astype(jnp.int32), axis=1, keepdims=True)
        return jnp.where(cnt >= cap, cand, bits)

    thr = pltpu.bitcast(lax.fori_loop(0, 31, bit_step, jnp.zeros((N_EXPERTS, 1), jnp.int32)), F32)
    above = (aff > thr).astype(F32)
    tied = (aff == thr).astype(F32)
    need = cap - jnp.sum(above, axis=1, keepdims=True)

    blk = MXU_DEPTH
    blocks = [slice(k * blk, (k + 1) * blk) for k in range(n_tok // blk)]
    r_i = lax.broadcasted_iota(jnp.int32, (blk, blk), 0)
    c_i = lax.broadcasted_iota(jnp.int32, (blk, blk), 1)
    strict_upper = (r_i < c_i).astype(BF16)

    def running(block_sums):
        run, total = [], jnp.zeros((N_EXPERTS, 1), F32)
        for s in block_sums:
            run.append(total)
            total = total + s
        return run

    tied_b = [tied[:, sl] for sl in blocks]
    tied_rank = [_dot(t.astype(BF16), strict_upper) for t in tied_b]
    tied_before = running([t.sum(axis=1, keepdims=True) for t in tied_b])
    sel_b = [above[:, sl] + t * ((before + rank) < need).astype(F32)
             for sl, t, before, rank in zip(blocks, tied_b, tied_before, tied_rank)]
    sel_rank = [_dot(s.astype(BF16), strict_upper) for s in sel_b]
    half_sums = [[s[:, h * OFFSET_BLOCK:(h + 1) * OFFSET_BLOCK].sum(axis=1, keepdims=True)
                  for h in range(blk // OFFSET_BLOCK)] for s in sel_b]
    offs = running([h for hs in half_sums for h in hs])
    per_blk = blk // OFFSET_BLOCK
    for k, (sl, s, rank) in enumerate(zip(blocks, sel_b, sel_rank)):
        pos_ref[0, :, sl] = jnp.where(s > 0.5, (offs[k * per_blk] + rank).astype(jnp.int32), -1)
    off_ref[0] = jnp.concatenate(offs, axis=1).astype(jnp.int32)


def expert_choice_select(aff_t, cap):
    bx, n_e, t = aff_t.shape
    n_tb = t // OFFSET_BLOCK
    return pl.pallas_call(
        functools.partial(_select_kernel, cap=cap, n_tok=t),
        grid=(bx,),
        in_specs=[pl.BlockSpec((1, n_e, t), lambda b: (b, 0, 0))],
        out_specs=[pl.BlockSpec((1, n_e, t), lambda b: (b, 0, 0)),
                   pl.BlockSpec((1, n_e, n_tb), lambda b: (b, 0, 0))],
        out_shape=[jax.ShapeDtypeStruct((bx, n_e, t), jnp.int32),
                   jax.ShapeDtypeStruct((bx, n_e, n_tb), jnp.int32)],
        compiler_params=_params("parallel"),
        name="expert_choice_select",
    )(aff_t)


GATE_PARTS = 3
TOKEN_LANE = GATE_PARTS * N_EXPERTS


def _slot_table_kernel(off_ref, pos_ref, aff_ref, tbl_ref, *, n_off, blocks_per_step):
    b, eg, kc = pl.program_id(0), pl.program_id(1), pl.program_id(2)
    epg = EXPERTS_PER_GATHER_STEP

    @pl.when(kc == 0)
    def _():
        tbl_ref[...] = jnp.zeros_like(tbl_ref)

    slot = lax.broadcasted_iota(jnp.int32, (GATHER_WINDOW, 1), 0)
    slot_2d = lax.broadcasted_iota(jnp.int32, (GATHER_WINDOW, GATHER_BLOCK), 0)
    lane = lax.broadcasted_iota(jnp.int32, (GATHER_BLOCK, LANES), 1)
    out_lane = lax.broadcasted_iota(jnp.int32, (GATHER_WINDOW, LANES), 1)
    local_token = lax.broadcasted_iota(jnp.int32, (GATHER_BLOCK, LANES), 0).astype(F32)
    place_r = lax.broadcasted_iota(jnp.int32, (N_EXPERTS, LANES), 0)
    place_c = lax.broadcasted_iota(jnp.int32, (N_EXPERTS, LANES), 1)
    def payload_of(kk):
        payload = jnp.where(lane == TOKEN_LANE, local_token, 0.0)
        rest = aff_ref[0, kk * GATHER_BLOCK:(kk + 1) * GATHER_BLOCK]
        for k in range(GATE_PARTS):
            part = rest.astype(BF16)
            payload = payload + _dot(part, (place_c == place_r + k * N_EXPERTS).astype(BF16))
            rest = rest - part.astype(F32)
        return payload.astype(BF16)

    payloads = [payload_of(kk) for kk in range(blocks_per_step)]
    pending = []
    for kk in range(blocks_per_step):
        kb = kc * blocks_per_step + kk
        first_token = jnp.where(out_lane == TOKEN_LANE, (kb * GATHER_BLOCK).astype(F32), 0.0)
        for ee in range(epg):
            off = off_ref[(b * N_EXPERTS + eg * epg + ee) * n_off + kb * (GATHER_BLOCK // OFFSET_BLOCK)]
            base = pl.multiple_of((off >> 4) << 4, WINDOW_ALIGN)
            onehot = (pos_ref[0, ee, 0, kk:kk + 1, :] - base == slot_2d).astype(BF16)
            pending.append((ee, off, base, _dot(onehot, payloads[kk]) + first_token))
    for ee, off, base, gathered in pending:
        win = pl.ds(base, GATHER_WINDOW)
        tbl_ref[0, ee, win, :] = jnp.where(slot >= off - base, gathered, tbl_ref[0, ee, win, :])


def slot_table(aff, pos, block_off, cap_pad):
    bx, t, _ = aff.shape
    n_tb = t // GATHER_BLOCK
    blocks_per_step = min(4, n_tb)
    n_steps = n_tb // blocks_per_step
    epg = EXPERTS_PER_GATHER_STEP
    tokens = blocks_per_step * GATHER_BLOCK
    pos5 = pos.reshape(bx, N_EXPERTS, n_steps, blocks_per_step, GATHER_BLOCK)
    grid_spec = pltpu.PrefetchScalarGridSpec(
        num_scalar_prefetch=1,
        grid=(bx, N_EXPERTS // epg, n_steps),
        in_specs=[
            pl.BlockSpec((1, epg, 1, blocks_per_step, GATHER_BLOCK), lambda b, g, k, s: (b, g, k, 0, 0)),
            pl.BlockSpec((1, tokens, N_EXPERTS), lambda b, g, k, s: (b, k, 0)),
        ],
        out_specs=pl.BlockSpec((1, epg, cap_pad, LANES), lambda b, g, k, s: (b, g, 0, 0)),
    )
    return pl.pallas_call(
        functools.partial(_slot_table_kernel, n_off=t // OFFSET_BLOCK, blocks_per_step=blocks_per_step),
        grid_spec=grid_spec,
        out_shape=jax.ShapeDtypeStruct((bx, N_EXPERTS, cap_pad, LANES), F32),
        compiler_params=_params("parallel", "parallel", "arbitrary"),
        name="slot_table",
    )(block_off.reshape(-1), pos5, aff)


GATHER_UNROLL = 16


def _row_gather_kernel(idx_ref, src_ref, xs_ref, tile_ref, *, cap, rows_per_token, chunk_stride):
    b, e = pl.program_id(0), pl.program_id(1)
    first = (b * N_EXPERTS + e) * cap

    def group(g, carry):
        for u in range(GATHER_UNROLL):
            s = g * GATHER_UNROLL + u
            row = pl.multiple_of(idx_ref[first + s] * rows_per_token, rows_per_token)
            tile_ref[pl.ds(s, rows_per_token, stride=chunk_stride), :] = src_ref[0, pl.ds(row, rows_per_token), :]
        return carry

    lax.fori_loop(0, cap // GATHER_UNROLL, group, 0)
    for j in range(rows_per_token):
        xs_ref[0, 0, :, j * LANES:(j + 1) * LANES] = tile_ref[j * chunk_stride:j * chunk_stride + cap, :].astype(BF16)


def gather_rows(h2_rows, token_idx, cap, d):
    bx, n_rows, _ = h2_rows.shape
    rows_per_token = d // LANES
    chunk_stride = cap + 8
    grid_spec = pltpu.PrefetchScalarGridSpec(
        num_scalar_prefetch=1,
        grid=(bx, N_EXPERTS),
        in_specs=[pl.BlockSpec((1, n_rows, LANES), lambda b, e, s: (b, 0, 0), pipeline_mode=pl.Buffered(1))],
        out_specs=pl.BlockSpec((1, 1, cap, d), lambda b, e, s: (b, e, 0, 0)),
        scratch_shapes=[pltpu.VMEM((rows_per_token * chunk_stride, LANES), F32)],
    )
    return pl.pallas_call(
        functools.partial(_row_gather_kernel, cap=cap, rows_per_token=rows_per_token, chunk_stride=chunk_stride),
        grid_spec=grid_spec,
        out_shape=jax.ShapeDtypeStruct((bx, N_EXPERTS, cap, d), BF16),
        compiler_params=_params("parallel", "arbitrary"),
        name="gather_rows",
    )(token_idx, h2_rows)


def _ffn_kernel(xs_ref, gs_ref, wg_ref, wu_ref, wd_ref, ye_ref, wg_bf, wu_bf, wd_bf, *, cap, row_tile):
    @pl.when(pl.program_id(1) == 0)
    def _():
        wg_bf[...] = wg_ref[0, 0].astype(BF16)
        wu_bf[...] = wu_ref[0, 0].astype(BF16)
        wd_bf[...] = wd_ref[0, 0].astype(BF16)

    n_b = xs_ref.shape[0]
    if n_b == 1:
        tiles = [[(0, r0, row_tile)] for r0 in range(0, cap, row_tile)]
    else:
        tiles = [[(bb, 0, cap) for bb in range(n_b)]]
    for tile in tiles:
        x = jnp.concatenate([xs_ref[bb, 0, r0:r0 + n] for bb, r0, n in tile], axis=0)
        terms = jnp.concatenate([gs_ref[bb, 0, r0:r0 + n, :] for bb, r0, n in tile], axis=0)
        lane = lax.broadcasted_iota(jnp.int32, terms.shape, 1)
        own = ((lane & (N_EXPERTS - 1)) == pl.program_id(0)) & (lane < TOKEN_LANE)
        terms = jnp.where(own, terms, 0.0)
        gate = _dot(x, wg_bf[...])
        up = _dot(x, wu_bf[...])
        hid = (gate * jax.nn.sigmoid(gate) * up).astype(BF16)
        g = jnp.sum(terms, axis=1, keepdims=True)
        ye = _dot(hid, wd_bf[...]) * g
        rows_per_slot = ye.shape[1] // LANES
        row = 0
        for bb, r0, n in tile:
            for j in range(rows_per_slot):
                ye_ref[bb, 0, pl.ds(r0 * rows_per_slot + j, n, stride=rows_per_slot), :] = (
                    ye[row:row + n, j * LANES:(j + 1) * LANES])
            row += n


def expert_ffn(xs, gs, wg, wu, wd, layer):
    bx, n_e, cap, d = xs.shape
    ff = wg.shape[-1]
    slot_rows = cap * (d // LANES)
    max_rows = 512
    row_tile = min(max_rows, cap)
    n_b = bx if bx * cap <= max_rows else 1
    return pl.pallas_call(
        functools.partial(_ffn_kernel, cap=cap, row_tile=row_tile),
        grid=(n_e, bx // n_b),
        in_specs=[
            pl.BlockSpec((n_b, 1, cap, d), lambda e, b: (b, e, 0, 0)),
            pl.BlockSpec((n_b, 1, cap, LANES), lambda e, b: (b, e, 0, 0)),
            pl.BlockSpec((1, 1, d, ff), lambda e, b: (layer, e, 0, 0)),
            pl.BlockSpec((1, 1, d, ff), lambda e, b: (layer, e, 0, 0)),
            pl.BlockSpec((1, 1, ff, d), lambda e, b: (layer, e, 0, 0)),
        ],
        out_specs=pl.BlockSpec((n_b, 1, slot_rows, LANES), lambda e, b: (b, e, 0, 0)),
        out_shape=jax.ShapeDtypeStruct((bx, n_e, slot_rows, LANES), F32),
        scratch_shapes=[pltpu.VMEM((d, ff), BF16), pltpu.VMEM((d, ff), BF16), pltpu.VMEM((ff, d), BF16)],
        compiler_params=_params("parallel", "arbitrary"),
        name="expert_ffn",
    )(xs, gs, wg, wu, wd)


SCATTER_UNROLL = 16


def _row_scatter_kernel(idx_ref, ye_ref, acc_ref, *, cap, rows_per_token):
    b, e = pl.program_id(0), pl.program_id(1)

    @pl.when(e == 0)
    def _():
        acc_ref[...] = jnp.zeros_like(acc_ref)

    first = (b * N_EXPERTS + e) * cap

    def group(g, carry):
        rows, sums = [], []
        for u in range(SCATTER_UNROLL):
            s = g * SCATTER_UNROLL + u
            row = pl.multiple_of(idx_ref[first + s] * rows_per_token, rows_per_token)
            src = pl.multiple_of(s * rows_per_token, rows_per_token)
            rows.append(row)
            sums.append(acc_ref[0, pl.ds(row, rows_per_token), :] + ye_ref[0, 0, pl.ds(src, rows_per_token), :])
        for row, total in zip(rows, sums):
            acc_ref[0, pl.ds(row, rows_per_token), :] = total
        return carry

    lax.fori_loop(0, cap // SCATTER_UNROLL, group, 0)


def scatter_rows(ye_rows, token_idx, n_tok, cap):
    bx, n_e, slot_rows, _ = ye_rows.shape
    rows_per_token = slot_rows // cap
    grid_spec = pltpu.PrefetchScalarGridSpec(
        num_scalar_prefetch=1,
        grid=(bx, n_e),
        in_specs=[pl.BlockSpec((1, 1, slot_rows, LANES), lambda b, e, s: (b, e, 0, 0))],
        out_specs=pl.BlockSpec((1, n_tok * rows_per_token, LANES), lambda b, e, s: (b, 0, 0),
                               pipeline_mode=pl.Buffered(1)),
    )
    return pl.pallas_call(
        functools.partial(_row_scatter_kernel, cap=cap, rows_per_token=rows_per_token),
        grid_spec=grid_spec,
        out_shape=jax.ShapeDtypeStruct((bx, n_tok * rows_per_token, LANES), F32),
        compiler_params=_params("parallel", "arbitrary"),
        name="scatter_rows",
    )(token_idx, ye_rows)


def _final_norm_kernel(moe_ref, x1_ref, g2_ref, lng_ref, lnb_ref, out_ref, *, alpha):
    tm, d = x1_ref.shape[1], x1_ref.shape[2]
    rows_per_token = d // LANES
    moe = jnp.concatenate([moe_ref[0, pl.ds(j, tm, stride=rows_per_token), :] for j in range(rows_per_token)],
                          axis=1)
    y = alpha * x1_ref[0] + g2_ref[0] * moe
    out_ref[0] = _layer_norm(y) * lng_ref[...] + lnb_ref[...]


def residual_norm(moe_rows, x1, g2, ln_g, ln_b, alpha):
    bx, t, d = x1.shape
    tm = min(512, t)
    rows_per_token = d // LANES
    tok = lambda b, i: (b, i, 0)
    const = lambda b, i: (0, 0)
    return pl.pallas_call(
        functools.partial(_final_norm_kernel, alpha=alpha),
        grid=(bx, t // tm),
        in_specs=[
            pl.BlockSpec((1, tm * rows_per_token, LANES), tok),
            pl.BlockSpec((1, tm, d), tok),
            pl.BlockSpec((1, 1, d), lambda b, i: (b, 0, 0)),
            pl.BlockSpec((1, d), const), pl.BlockSpec((1, d), const),
        ],
        out_specs=pl.BlockSpec((1, tm, d), tok),
        out_shape=jax.ShapeDtypeStruct((bx, t, d), F32),
        compiler_params=_params("parallel", "parallel"),
        name="residual_norm",
    )(moe_rows, x1, g2, ln_g.reshape(1, d), ln_b.reshape(1, d))


def _split_bf16(w):
    hi = w.astype(BF16)
    return hi, (w - hi.astype(F32)).astype(BF16)


def _mixer_tail(oa, ob, oc, x, mod, lw, alpha):
    g1, sh2, sc2, g2 = mod
    t = x.shape[1]
    cap = EC_CAPACITY * t // N_EXPERTS
    cap_pad = cap + SLOT_PAD
    x1, h2_rows, aff_t, aff = out_projection(oa, ob, oc, x, g1, sh2, sc2, lw["w_out"], lw["ln1_g"], lw["ln1_b"],
                                        lw["wr_hi"], lw["wr_lo"], alpha)
    pos, block_off = expert_choice_select(aff_t, cap)
    table = slot_table(aff, pos, block_off, cap_pad)
    token_idx = table[:, :, :cap, TOKEN_LANE].astype(jnp.int32).reshape(-1)
    xs = gather_rows(h2_rows, token_idx, cap, x.shape[2])
    ye_rows = expert_ffn(xs, table, lw["w_gate"], lw["w_up"], lw["w_down"], lw["layer"])
    moe_rows = scatter_rows(ye_rows, token_idx, t, cap)
    return residual_norm(moe_rows, x1, g2, lw["ln2_g"], lw["ln2_b"], alpha)


def kernel(x, c, ctx, c_ctx, w_mod, b_mod, w_in, a_sink, nat_bias, conv_w, conv_b, conv_ln_g, conv_ln_b,
           w_out, ln1_g, ln1_b, w_router, w_gate, w_up, w_down, ln2_g, ln2_b):
    bsz, n_lat, d = x.shape
    depth = w_mod.shape[0]
    alpha = (2 * depth) ** 0.25
    cos_t, sin_t = rope_tables(n_lat)

    cond = jnp.concatenate([c, c_ctx[None, :], jnp.zeros((8 - bsz - 1, d), F32)], axis=0)
    mods = adaln_all(cond, w_mod, b_mod)

    for l in range(depth):
        last = l == depth - 1
        wr_hi, wr_lo = _split_bf16(w_router[l].T)
        lw = dict(w_out=w_out[l].astype(BF16), ln1_g=ln1_g[l], ln1_b=ln1_b[l], wr_hi=wr_hi, wr_lo=wr_lo,
                  w_gate=w_gate, w_up=w_up, w_down=w_down, layer=l,
                  ln2_g=ln2_g[l], ln2_b=ln2_b[l])
        w_in_l = w_in[l].astype(BF16)
        lat = [mods[l, :bsz, k * d:(k + 1) * d][:, None, :] for k in range(N_MOD)]
        cm = [jnp.broadcast_to(mods[l, bsz, k * d:(k + 1) * d][None, None, :], (bsz, 1, d))
              for k in range(N_MOD)]
        conv_args = (conv_w[l], conv_b[l], conv_ln_g[l], conv_ln_b[l])

        qa_c, ka_c, va_c, qb_c, kb_c, vb_c, hc_c = in_projection(ctx, cm[0], cm[1], w_in_l, cos_t, sin_t, rope=False)
        if not last:
            oa_c, ob_c = context_attention(qa_c, qb_c, ka_c, va_c, kb_c, vb_c, a_sink[l])
            oc_c = conformer_conv(hc_c, *conv_args)
            ctx_new = _mixer_tail(oa_c, ob_c, oc_c, ctx, (cm[2], cm[3], cm[4], cm[5]), lw, alpha)

        qa, ka, va, qb, kb, vb, hc = in_projection(x, lat[0], lat[1], w_in_l, cos_t, sin_t, rope=True)
        oa = window_attention(qa, ka, va, ka_c, va_c, a_sink[l])
        ob = neighbourhood_attention(qb, kb, vb, kb_c, vb_c, nat_bias[l])
        oc = conformer_conv(hc, *conv_args)
        x = _mixer_tail(oa, ob, oc, x, (lat[2], lat[3], lat[4], lat[5]), lw, alpha)
        if not last:
            ctx = ctx_new
    return x
```

```python
import functools

import numpy as np
import jax
import jax.numpy as jnp
from jax import lax
from jax.experimental import pallas as pl
from jax.experimental.pallas import tpu as pltpu

HEAD_DIM = 64
GRID_W = 64
A_Q_HEADS = 8
A_KV_HEADS = 2
A_GROUP = A_Q_HEADS // A_KV_HEADS
A_WINDOW = 128
B_HEADS = 4
NA_ROWS = 8
NA_COLS = 16
C_CHANNELS = 256
C_CONV_WIDTH = 31
A_Q_W = A_Q_HEADS * HEAD_DIM
A_KV_W = A_KV_HEADS * HEAD_DIM
B_W = B_HEADS * HEAD_DIM
OFF_AK = A_Q_W
OFF_AV = OFF_AK + A_KV_W
OFF_BQ = OFF_AV + A_KV_W
OFF_BK = OFF_BQ + B_W
OFF_BV = OFF_BK + B_W
OFF_C = OFF_BV + B_W
IN_WIDTH = OFF_C + 2 * C_CHANNELS
ROPE_WIDTH = A_Q_W + A_KV_W
N_EXPERTS = 16
EC_CAPACITY = 2
ROPE_BASE = 10000.0
LN_EPS = 1e-6
N_MOD = 6
NEG_INF = -1e30
QK_SCALE = HEAD_DIM ** -0.5

LANES = 128
WINDOW_ALIGN = 16
MXU_DEPTH = 256
OFFSET_BLOCK = 128
GATHER_BLOCK = MXU_DEPTH
GATHER_WINDOW = GATHER_BLOCK + WINDOW_ALIGN
SLOT_PAD = 3 * LANES
EXPERTS_PER_GATHER_STEP = 16
VMEM_LIMIT = 56 * 1024 * 1024

F32 = jnp.float32
BF16 = jnp.bfloat16


def _dot(a, b):
    return jnp.dot(a, b, preferred_element_type=F32)


def _dot_t(a, b):
    return lax.dot_general(a, b, (((1,), (1,)), ((), ())), preferred_element_type=F32)


def _layer_norm(x):
    mu = jnp.mean(x, axis=-1, keepdims=True)
    xc = x - mu
    var = jnp.mean(xc * xc, axis=-1, keepdims=True)
    return xc * lax.rsqrt(var + LN_EPS)


def _params(*sem):
    return pltpu.CompilerParams(dimension_semantics=sem, vmem_limit_bytes=VMEM_LIMIT)


def _mod_kernel(cond_ref, w_ref, b_ref, out_ref):
    cnd = cond_ref[...]
    act = cnd * jax.nn.sigmoid(cnd)
    out_ref[0] = jnp.dot(act, w_ref[0], preferred_element_type=F32,
                         precision=lax.Precision.HIGHEST) + b_ref[0]


def adaln_all(cond, w_mod, b_mod):
    n_layers, d, width = w_mod.shape
    rows = cond.shape[0]
    tn = 1536
    return pl.pallas_call(
        _mod_kernel,
        grid=(n_layers, width // tn),
        in_specs=[
            pl.BlockSpec((rows, d), lambda l, j: (0, 0)),
            pl.BlockSpec((1, d, tn), lambda l, j: (l, 0, j)),
            pl.BlockSpec((1, 1, tn), lambda l, j: (l, 0, j)),
        ],
        out_specs=pl.BlockSpec((1, rows, tn), lambda l, j: (l, 0, j)),
        out_shape=jax.ShapeDtypeStruct((n_layers, rows, width), F32),
        compiler_params=_params("parallel", "parallel"),
        name="adaln",
    )(cond, w_mod, b_mod.reshape(n_layers, 1, width))


def _inproj_kernel(x_ref, sh_ref, sc_ref, w_ref, cos_ref, sin_ref,
                   qa_ref, ka_ref, va_ref, qb_ref, kb_ref, vb_ref, hc_ref, *, rope):
    x = x_ref[0]
    h = _layer_norm(x) * (1.0 + sc_ref[0]) + sh_ref[0]
    u = _dot(h.astype(BF16), w_ref[...])

    def rotated(col):
        xq = u[:, col:col + LANES]
        if not rope:
            return xq
        lane = lax.broadcasted_iota(jnp.int32, xq.shape, 1)
        first = (lane & (HEAD_DIM // 2 - 1)) < (HEAD_DIM // 4)
        partner = jnp.where(first, pltpu.roll(xq, LANES - HEAD_DIM // 4, 1),
                            pltpu.roll(xq, HEAD_DIM // 4, 1))
        return xq * cos_ref[...] + partner * sin_ref[...]

    rot = [rotated(col) for col in range(0, ROPE_WIDTH, LANES)]
    n_q = A_Q_W // LANES
    qa_ref[0] = (jnp.concatenate(rot[:n_q], axis=1) * QK_SCALE).astype(BF16)
    ka_ref[0] = jnp.concatenate(rot[n_q:], axis=1).astype(BF16)
    va_ref[0] = u[:, OFF_AV:OFF_BQ].astype(BF16)
    qb_ref[0] = (u[:, OFF_BQ:OFF_BK] * QK_SCALE).astype(BF16)
    kb_ref[0] = u[:, OFF_BK:OFF_BV].astype(BF16)
    vb_ref[0] = u[:, OFF_BV:OFF_C].astype(BF16)
    a = u[:, OFF_C:OFF_C + C_CHANNELS]
    gate = u[:, OFF_C + C_CHANNELS:]
    hc_ref[0] = a * jax.nn.sigmoid(gate)


def in_projection(x, shift, scale, w_in_bf16, cos_t, sin_t, *, rope):
    bx, t, d = x.shape
    tm = min(512, t)
    widths = (A_Q_W, A_KV_W, A_KV_W, B_W, B_W, B_W, C_CHANNELS)
    dtypes = (BF16,) * 6 + (F32,)
    tok = lambda b, i: (b, i, 0)
    per_b = lambda b, i: (b, 0, 0)
    return pl.pallas_call(
        functools.partial(_inproj_kernel, rope=rope),
        grid=(bx, t // tm),
        in_specs=[
            pl.BlockSpec((1, tm, d), tok),
            pl.BlockSpec((1, 1, d), per_b),
            pl.BlockSpec((1, 1, d), per_b),
            pl.BlockSpec((d, IN_WIDTH), lambda b, i: (0, 0)),
            pl.BlockSpec((tm, LANES), lambda b, i: (i, 0)),
            pl.BlockSpec((tm, LANES), lambda b, i: (i, 0)),
        ],
        out_specs=[pl.BlockSpec((1, tm, w), tok) for w in widths],
        out_shape=[jax.ShapeDtypeStruct((bx, t, w), dt) for w, dt in zip(widths, dtypes)],
        compiler_params=_params("parallel", "parallel"),
        name="in_projection",
    )(x, shift, scale, w_in_bf16, cos_t, sin_t)


def rope_tables(n_tokens):
    t = jnp.arange(n_tokens, dtype=jnp.int32)
    row = (t // GRID_W).astype(F32)[:, None]
    col = (t % GRID_W).astype(F32)[:, None]
    n_freq = HEAD_DIM // 4
    inv_freq = ROPE_BASE ** (-jnp.arange(n_freq, dtype=F32) / n_freq)
    ang_r = row * inv_freq
    ang_c = col * inv_freq
    cos_h = jnp.concatenate([jnp.cos(ang_r), jnp.cos(ang_r), jnp.cos(ang_c), jnp.cos(ang_c)], axis=1)
    sin_h = jnp.concatenate([-jnp.sin(ang_r), jnp.sin(ang_r), -jnp.sin(ang_c), jnp.sin(ang_c)], axis=1)
    reps = LANES // HEAD_DIM
    return jnp.tile(cos_h, (1, reps)), jnp.tile(sin_h, (1, reps))


def _with_ones(v):
    return jnp.concatenate([v, jnp.ones_like(v)], axis=1)


def _attend(score_parts, values, sink=None):
    m = score_parts[0].max(axis=-1, keepdims=True)
    for s in score_parts[1:]:
        m = jnp.maximum(m, s.max(axis=-1, keepdims=True))
    if sink is not None:
        m = jnp.maximum(m, sink)
    acc = None
    for s, v in zip(score_parts, values):
        term = _dot(jnp.exp((s - m).astype(BF16)), v)
        acc = term if acc is None else acc + term
    den = acc[:, HEAD_DIM:HEAD_DIM + 1]
    if sink is not None:
        den = den + jnp.exp(sink - m)
    return acc[:, :HEAD_DIM] / den


def _attn_a_kernel(sink_ref, q_ref, kp_ref, kc_ref, kn_ref, vp_ref, vc_ref, vn_ref,
                   kctx_ref, vctx_ref, out_ref, *, n_lat, tq):
    i = pl.program_id(1)
    k_win = jnp.concatenate([kp_ref[0], kc_ref[0], kn_ref[0]], axis=0)
    v_win = jnp.concatenate([vp_ref[0], vc_ref[0], vn_ref[0]], axis=0)
    kctx = kctx_ref[0]
    vctx = vctx_ref[0]
    sub = A_WINDOW
    span = 3 * A_WINDOW
    rows = A_GROUP * sub
    q_onehot = ((lax.broadcasted_iota(jnp.int32, (rows, sub), 0) & (sub - 1))
                == lax.broadcasted_iota(jnp.int32, (rows, sub), 1)).astype(BF16)
    key_i = lax.broadcasted_iota(jnp.int32, (span, sub), 0)
    qry_i = lax.broadcasted_iota(jnp.int32, (span, sub), 1)
    rel = key_i - A_WINDOW - qry_i
    in_band = (rel <= A_WINDOW) & (rel >= -A_WINDOW)
    group_of_row = lax.broadcasted_iota(jnp.int32, (rows, 1), 0) >> (sub.bit_length() - 1)
    sinks = []
    for hk in range(A_KV_HEADS):
        sink = jnp.zeros((rows, 1), F32)
        for g in range(A_GROUP):
            sink = jnp.where(group_of_row == g, sink_ref[hk * A_GROUP + g], sink)
        sinks.append(sink)
    vctx_ext = [_with_ones(vctx[:, hk * HEAD_DIM:(hk + 1) * HEAD_DIM]) for hk in range(A_KV_HEADS)]
    v_ext = [_with_ones(v_win[:, hk * HEAD_DIM:(hk + 1) * HEAD_DIM]) for hk in range(A_KV_HEADS)]
    def mask_columns(j):
        kpos = i * tq + j * sub - A_WINDOW + key_i
        valid = in_band & (kpos >= 0) & (kpos < n_lat)
        return jnp.where(valid, 0.0, NEG_INF).astype(BF16)

    masks = [mask_columns(j) for j in range(tq // sub)]

    def scores(j, hk):
        sl = slice(hk * HEAD_DIM, (hk + 1) * HEAD_DIM)
        q_rows = q_ref[0, j * sub:(j + 1) * sub]
        q = jnp.concatenate([q_rows[:, h * HEAD_DIM:(h + 1) * HEAD_DIM]
                             for h in range(hk * A_GROUP, (hk + 1) * A_GROUP)], axis=0)
        q_aug = jnp.concatenate([q_onehot, q], axis=1)
        k_aug = jnp.concatenate([masks[j], k_win[j * sub:j * sub + span, sl]], axis=1)
        return [_dot_t(q_aug, k_aug), _dot_t(q, kctx[:, sl])]

    units = [(j, hk) for j in range(tq // sub) for hk in range(A_KV_HEADS)]
    all_scores = [scores(j, hk) for j, hk in units]
    for (j, hk), s in zip(units, all_scores):
        v_sub = v_ext[hk][j * sub:j * sub + span]
        o = _attend(s, [v_sub, vctx_ext[hk]], sink=sinks[hk]).astype(BF16)
        for g in range(A_GROUP):
            h = hk * A_GROUP + g
            out_ref[0, j * sub:(j + 1) * sub, h * HEAD_DIM:(h + 1) * HEAD_DIM] = o[g * sub:(g + 1) * sub]


def window_attention(qa, ka, va, kc_a, vc_a, sink):
    bsz, n_lat, _ = qa.shape
    n_ctx = kc_a.shape[1]
    tq = min(512, n_lat)
    w = A_WINDOW
    per = tq // w
    last = n_lat // w - 1
    prev = lambda b, i, s: (b, jnp.maximum(i * per - 1, 0), 0)
    cur = lambda b, i, s: (b, i, 0)
    nxt = lambda b, i, s: (b, jnp.minimum((i + 1) * per, last), 0)
    ctx = lambda b, i, s: (b, 0, 0)
    kv_specs = [pl.BlockSpec((1, w, A_KV_W), prev), pl.BlockSpec((1, tq, A_KV_W), cur),
                pl.BlockSpec((1, w, A_KV_W), nxt)]
    grid_spec = pltpu.PrefetchScalarGridSpec(
        num_scalar_prefetch=1,
        grid=(bsz, n_lat // tq),
        in_specs=[pl.BlockSpec((1, tq, A_Q_W), cur)] + kv_specs + kv_specs + [
            pl.BlockSpec((1, n_ctx, A_KV_W), ctx), pl.BlockSpec((1, n_ctx, A_KV_W), ctx)],
        out_specs=pl.BlockSpec((1, tq, A_Q_W), cur),
    )
    return pl.pallas_call(
        functools.partial(_attn_a_kernel, n_lat=n_lat, tq=tq),
        grid_spec=grid_spec,
        out_shape=jax.ShapeDtypeStruct((bsz, n_lat, A_Q_W), BF16),
        compiler_params=_params("parallel", "parallel"),
        name="window_attention",
    )(sink, qa, ka, ka, ka, va, va, va, kc_a, vc_a)


NB_Q_ROWS = 4


def _attn_b_kernel(q_ref, kp_ref, kc_ref, kn_ref, vp_ref, vc_ref, vn_ref,
                   kctx_ref, vctx_ref, bias_ref, out_ref):
    k_win = jnp.concatenate([kp_ref[0], kc_ref[0], kn_ref[0]], axis=0)
    v_win = jnp.concatenate([vp_ref[0], vc_ref[0], vn_ref[0]], axis=0)
    q_all = q_ref[0]
    kctx = kctx_ref[0]
    vctx = vctx_ref[0]
    heads = [slice(h * HEAD_DIM, (h + 1) * HEAD_DIM) for h in range(B_HEADS)]
    scores = [[_dot_t(q_all[:, sl], k_win[:, sl]) + bias_ref[0, h], _dot_t(q_all[:, sl], kctx[:, sl])]
              for h, sl in enumerate(heads)]
    for sl, s in zip(heads, scores):
        o = _attend(s, [_with_ones(v_win[:, sl]), _with_ones(vctx[:, sl])])
        out_ref[0, :, sl] = o.astype(BF16)


def neighbourhood_bias(rel_bias, n_lat):
    rows = n_lat // GRID_W
    kr_n = min(NA_ROWS, rows)
    n_blocks = rows // NB_Q_ROWS
    n_heads, n_dr, n_dc = rel_bias.shape
    cols = np.arange(GRID_W)
    c_start = np.clip(cols - NA_COLS // 2, 0, GRID_W - NA_COLS)
    col_ok = (cols[None, :] >= c_start[:, None]) & (cols[None, :] < c_start[:, None] + NA_COLS)
    dc = np.clip(cols[None, :] - cols[:, None], -(NA_COLS - 1), NA_COLS - 1) + NA_COLS - 1
    pick_dc = (dc.reshape(-1)[None, :] == np.arange(n_dc)[:, None]).astype(np.float32)
    toeplitz = jnp.dot(rel_bias.reshape(n_heads * n_dr, n_dc), pick_dc, precision=lax.Precision.HIGHEST)
    toeplitz = jnp.where(col_ok.reshape(-1), toeplitz, NEG_INF).reshape(n_heads, n_dr, GRID_W, GRID_W)
    q_rl = np.arange(NB_Q_ROWS)
    k_rl = np.arange(3 * NB_Q_ROWS)
    row_ok, dr = [], []
    for j in sorted({0, min(1, n_blocks - 1), n_blocks - 1}):
        r = NB_Q_ROWS * j + q_rl
        kr = NB_Q_ROWS * (j - 1) + k_rl
        r_start = np.clip(r - kr_n // 2, 0, rows - kr_n)
        ok = (kr[None, :] >= r_start[:, None]) & (kr[None, :] < r_start[:, None] + kr_n)
        row_ok.append(ok & (kr[None, :] >= 0) & (kr[None, :] < rows))
        dr.append(np.clip(kr[None, :] - r[:, None] + NA_ROWS - 1, 0, n_dr - 1))
    row_ok = np.stack(row_ok)
    dr = np.stack(dr)
    tiles = jnp.stack([toeplitz[:, int(i)] for i in dr.reshape(-1)], axis=1)
    tiles = tiles.reshape((n_heads,) + dr.shape + (GRID_W, GRID_W))
    tiles = jnp.where(row_ok[None, :, :, :, None, None], tiles, NEG_INF)
    table = tiles.transpose(1, 0, 2, 4, 3, 5).reshape(
        dr.shape[0], n_heads, NB_Q_ROWS * GRID_W, 3 * NB_Q_ROWS * GRID_W)
    return table, n_blocks


def neighbourhood_attention(qb, kb, vb, kc_b, vc_b, rel_bias):
    bsz, n_lat, _ = qb.shape
    n_ctx = kc_b.shape[1]
    table, n_blocks = neighbourhood_bias(rel_bias, n_lat)
    n_var = table.shape[0]
    tq = NB_Q_ROWS * GRID_W
    prev = lambda b, j: (b, jnp.maximum(j - 1, 0), 0)
    cur = lambda b, j: (b, j, 0)
    nxt = lambda b, j: (b, jnp.minimum(j + 1, n_blocks - 1), 0)
    ctx = lambda b, j: (b, 0, 0)

    def variant(b, j):
        v = jnp.where(j == 0, 0, jnp.where(j == n_blocks - 1, n_var - 1, min(1, n_var - 1)))
        return (v, 0, 0, 0)

    kv_specs = [pl.BlockSpec((1, tq, B_W), prev), pl.BlockSpec((1, tq, B_W), cur),
                pl.BlockSpec((1, tq, B_W), nxt)]
    return pl.pallas_call(
        _attn_b_kernel,
        grid=(bsz, n_blocks),
        in_specs=[pl.BlockSpec((1, tq, B_W), cur)] + kv_specs + kv_specs + [
            pl.BlockSpec((1, n_ctx, B_W), ctx), pl.BlockSpec((1, n_ctx, B_W), ctx),
            pl.BlockSpec((1, B_HEADS, tq, 3 * tq), variant)],
        out_specs=pl.BlockSpec((1, tq, B_W), cur),
        out_shape=jax.ShapeDtypeStruct((bsz, n_lat, B_W), BF16),
        compiler_params=_params("parallel", "parallel"),
        name="neighbourhood_attention",
    )(qb, kb, kb, kb, vb, vb, vb, kc_b, vc_b, table)


def _ctx_attn_kernel(sink_ref, qa_ref, qb_ref, ka_ref, va_ref, kb_ref, vb_ref, oa_ref, ob_ref):
    qa, qb = qa_ref[0], qb_ref[0]
    ka, va, kb, vb = ka_ref[0], va_ref[0], kb_ref[0], vb_ref[0]
    for hq in range(A_Q_HEADS):
        sl = slice(hq * HEAD_DIM, (hq + 1) * HEAD_DIM)
        hk = hq // A_GROUP
        kv = slice(hk * HEAD_DIM, (hk + 1) * HEAD_DIM)
        q = qa[:, sl]
        o = _attend([_dot_t(q, ka[:, kv])], [_with_ones(va[:, kv])], sink=sink_ref[hq])
        oa_ref[0, :, sl] = o.astype(BF16)
    for h in range(B_HEADS):
        sl = slice(h * HEAD_DIM, (h + 1) * HEAD_DIM)
        q = qb[:, sl]
        o = _attend([_dot_t(q, kb[:, sl])], [_with_ones(vb[:, sl])])
        ob_ref[0, :, sl] = o.astype(BF16)


def context_attention(qa, qb, ka, va, kb, vb, sink):
    bsz, n_ctx, _ = qa.shape
    blk = lambda w: pl.BlockSpec((1, n_ctx, w), lambda b, s: (b, 0, 0))
    grid_spec = pltpu.PrefetchScalarGridSpec(
        num_scalar_prefetch=1,
        grid=(bsz,),
        in_specs=[blk(A_Q_W), blk(B_W), blk(A_KV_W), blk(A_KV_W), blk(B_W), blk(B_W)],
        out_specs=[blk(A_Q_W), blk(B_W)],
    )
    return pl.pallas_call(
        _ctx_attn_kernel,
        grid_spec=grid_spec,
        out_shape=[jax.ShapeDtypeStruct((bsz, n_ctx, A_Q_W), BF16),
                   jax.ShapeDtypeStruct((bsz, n_ctx, B_W), BF16)],
        compiler_params=_params("parallel"),
        name="context_attention",
    )(sink, qa, qb, ka, va, kb, vb)


CONV_HALO = 16
F32_SUBLANES = 8
CONV_SHIFT_SPAN = (CONV_HALO + C_CONV_WIDTH // 2) // F32_SUBLANES * F32_SUBLANES


def _conv_kernel(prev_ref, cur_ref, next_ref, w_ref, b_ref, g_ref, beta_ref, out_ref, shifted_ref, *, ts):
    i = pl.program_id(1)
    n_i = pl.num_programs(1)
    ext = jnp.concatenate([jnp.where(i > 0, prev_ref[0], 0.0), cur_ref[0],
                           jnp.where(i < n_i - 1, next_ref[0], 0.0)], axis=0)
    for r in range(F32_SUBLANES):
        shifted_ref[r] = ext[r:r + ts + CONV_SHIFT_SPAN]
    acc = jnp.zeros((ts, C_CHANNELS), F32)
    for k in range(C_CONV_WIDTH):
        start = CONV_HALO - C_CONV_WIDTH // 2 + k
        aligned = start - start % F32_SUBLANES
        acc = acc + shifted_ref[start % F32_SUBLANES, aligned:aligned + ts] * w_ref[k:k + 1]
    y = _layer_norm(acc + b_ref[...]) * g_ref[...] + beta_ref[...]
    out_ref[0] = (y * jax.nn.sigmoid(y)).astype(BF16)


def conformer_conv(hc, conv_w, conv_b, ln_g, ln_b):
    bx, t, ch = hc.shape
    ts = min(512, t)
    per = ts // CONV_HALO
    last = t // CONV_HALO - 1
    row = lambda v: v.reshape(1, ch)
    const = lambda b, i: (0, 0)
    return pl.pallas_call(
        functools.partial(_conv_kernel, ts=ts),
        grid=(bx, t // ts),
        in_specs=[
            pl.BlockSpec((1, CONV_HALO, ch), lambda b, i: (b, jnp.maximum(i * per - 1, 0), 0)),
            pl.BlockSpec((1, ts, ch), lambda b, i: (b, i, 0)),
            pl.BlockSpec((1, CONV_HALO, ch), lambda b, i: (b, jnp.minimum((i + 1) * per, last), 0)),
            pl.BlockSpec((C_CONV_WIDTH, ch), const),
            pl.BlockSpec((1, ch), const), pl.BlockSpec((1, ch), const), pl.BlockSpec((1, ch), const),
        ],
        out_specs=pl.BlockSpec((1, ts, ch), lambda b, i: (b, i, 0)),
        out_shape=jax.ShapeDtypeStruct((bx, t, ch), BF16),
        scratch_shapes=[pltpu.VMEM((F32_SUBLANES, ts + CONV_SHIFT_SPAN, ch), F32)],
        compiler_params=_params("parallel", "parallel"),
        name="conformer_conv",
    )(hc, hc, hc, conv_w, row(conv_b), row(ln_g), row(ln_b))


OUTPROJ_ROW_CHUNKS = 4


def _outproj_kernel(oa_ref, ob_ref, oc_ref, x_ref, g1_ref, sh_ref, sc_ref, w_ref, lng_ref, lnb_ref,
                    wr_hi_ref, wr_lo_ref, x1_ref, h2_ref, afft_ref, aff_ref, *, alpha):
    tm, d = x_ref.shape[1], x_ref.shape[2]
    rc = tm // OUTPROJ_ROW_CHUNKS
    chunks = [slice(k * rc, (k + 1) * rc) for k in range(OUTPROJ_ROW_CHUNKS)]
    outs = [(_dot(oa_ref[0, r], w_ref[0:A_Q_W])
             + _dot(ob_ref[0, r], w_ref[A_Q_W:A_Q_W + B_W])
             + _dot(oc_ref[0, r], w_ref[A_Q_W + B_W:])) for r in chunks]
    w_hi, w_lo = wr_hi_ref[...], wr_lo_ref[...]
    eye = (lax.broadcasted_iota(jnp.int32, (N_EXPERTS, N_EXPERTS), 0)
           == lax.broadcasted_iota(jnp.int32, (N_EXPERTS, N_EXPERTS), 1)).astype(BF16)
    rows_per_token = d // LANES
    ys = [_layer_norm(alpha * x_ref[0, r] + g1_ref[0] * o) * lng_ref[...] + lnb_ref[...]
          for r, o in zip(chunks, outs)]
    for r, y in zip(chunks, ys):
        x1_ref[0, r] = y
    h2s = [_layer_norm(y) * (1.0 + sc_ref[0]) + sh_ref[0] for y in ys]
    his = [h2.astype(BF16) for h2 in h2s]
    los = [(h2 - hi.astype(F32)).astype(BF16) for h2, hi in zip(h2s, his)]
    all_logits = [_dot_t(hi, w_hi) + _dot_t(lo, w_hi) + _dot_t(hi, w_lo) for hi, lo in zip(his, los)]
    for k, h2 in enumerate(h2s):
        for j in range(rows_per_token):
            h2_ref[0, pl.ds(k * rc * rows_per_token + j, rc, stride=rows_per_token), :] = (
                h2[:, j * LANES:(j + 1) * LANES])
    for r, logits in zip(chunks, all_logits):
        e_n = jnp.exp(logits - logits.max(axis=1, keepdims=True))
        aff = e_n / e_n.sum(axis=1, keepdims=True)
        aff_ref[0, r] = aff
        aff_t, rest = None, aff
        for _ in range(GATE_PARTS):
            part = rest.astype(BF16)
            term = _dot_t(eye, part)
            aff_t = term if aff_t is None else aff_t + term
            rest = rest - part.astype(F32)
        afft_ref[0, :, r] = aff_t


def out_projection(oa, ob, oc, x, g1, sh2, sc2, w_out_bf16, ln_g, ln_b, wr_hi, wr_lo, alpha):
    bx, t, d = x.shape
    tm = min(512, t)
    tok = lambda b, i: (b, i, 0)
    per_b = lambda b, i: (b, 0, 0)
    const = lambda b, i: (0, 0)
    vec = pl.BlockSpec((1, d), const)
    return pl.pallas_call(
        functools.partial(_outproj_kernel, alpha=alpha),
        grid=(bx, t // tm),
        in_specs=[
            pl.BlockSpec((1, tm, A_Q_W), tok), pl.BlockSpec((1, tm, B_W), tok),
            pl.BlockSpec((1, tm, C_CHANNELS), tok), pl.BlockSpec((1, tm, d), tok),
            pl.BlockSpec((1, 1, d), per_b), pl.BlockSpec((1, 1, d), per_b), pl.BlockSpec((1, 1, d), per_b),
            pl.BlockSpec(w_out_bf16.shape, const), vec, vec,
            pl.BlockSpec((N_EXPERTS, d), const), pl.BlockSpec((N_EXPERTS, d), const),
        ],
        out_specs=[pl.BlockSpec((1, tm, d), tok), pl.BlockSpec((1, tm * (d // LANES), LANES), tok),
                   pl.BlockSpec((1, N_EXPERTS, tm), lambda b, i: (b, 0, i)),
                   pl.BlockSpec((1, tm, N_EXPERTS), tok)],
        out_shape=[jax.ShapeDtypeStruct((bx, t, d), F32),
                   jax.ShapeDtypeStruct((bx, t * (d // LANES), LANES), F32),
                   jax.ShapeDtypeStruct((bx, N_EXPERTS, t), F32),
                   jax.ShapeDtypeStruct((bx, t, N_EXPERTS), F32)],
        compiler_params=_params("parallel", "parallel"),
        name="out_projection",
    )(oa, ob, oc, x, g1, sh2, sc2, w_out_bf16, ln_g.reshape(1, d), ln_b.reshape(1, d), wr_hi, wr_lo)


def _select_kernel(afft_ref, pos_ref, off_ref, *, cap, n_tok):
    aff = afft_ref[0]

    def bit_step(j, bits):
        cand = bits | (jnp.int32(1) << (30 - j))
        cnt = jnp.sum((aff >= pltpu.bitcast(cand, F32)).astype(jnp.int32), axis=1, keepdims=True)
        return jnp.where(cnt >= cap, cand, bits)

    thr = pltpu.bitcast(lax.fori_loop(0, 31, bit_step, jnp.zeros((N_EXPERTS, 1), jnp.int32)), F32)
    above = (aff > thr).astype(F32)
    tied = (aff == thr).astype(F32)
    need = cap - jnp.sum(above, axis=1, keepdims=True)

    blk = MXU_DEPTH
    blocks = [slice(k * blk, (k + 1) * blk) for k in range(n_tok // blk)]
    r_i = lax.broadcasted_iota(jnp.int32, (blk, blk), 0)
    c_i = lax.broadcasted_iota(jnp.int32, (blk, blk), 1)
    strict_upper = (r_i < c_i).astype(BF16)

    def running(block_sums):
        run, total = [], jnp.zeros((N_EXPERTS, 1), F32)
        for s in block_sums:
            run.append(total)
            total = total + s
        return run

    tied_b = [tied[:, sl] for sl in blocks]
    tied_rank = [_dot(t.astype(BF16), strict_upper) for t in tied_b]
    tied_before = running([t.sum(axis=1, keepdims=True) for t in tied_b])
    sel_b = [above[:, sl] + t * ((before + rank) < need).astype(F32)
             for sl, t, before, rank in zip(blocks, tied_b, tied_before, tied_rank)]
    sel_rank = [_dot(s.astype(BF16), strict_upper) for s in sel_b]
    half_sums = [[s[:, h * OFFSET_BLOCK:(h + 1) * OFFSET_BLOCK].sum(axis=1, keepdims=True)
                  for h in range(blk // OFFSET_BLOCK)] for s in sel_b]
    offs = running([h for hs in half_sums for h in hs])
    per_blk = blk // OFFSET_BLOCK
    for k, (sl, s, rank) in enumerate(zip(blocks, sel_b, sel_rank)):
        pos_ref[0, :, sl] = jnp.where(s > 0.5, (offs[k * per_blk] + rank).astype(jnp.int32), -1)
    off_ref[0] = jnp.concatenate(offs, axis=1).astype(jnp.int32)


def expert_choice_select(aff_t, cap):
    bx, n_e, t = aff_t.shape
    n_tb = t // OFFSET_BLOCK
    return pl.pallas_call(
        functools.partial(_select_kernel, cap=cap, n_tok=t),
        grid=(bx,),
        in_specs=[pl.BlockSpec((1, n_e, t), lambda b: (b, 0, 0))],
        out_specs=[pl.BlockSpec((1, n_e, t), lambda b: (b, 0, 0)),
                   pl.BlockSpec((1, n_e, n_tb), lambda b: (b, 0, 0))],
        out_shape=[jax.ShapeDtypeStruct((bx, n_e, t), jnp.int32),
                   jax.ShapeDtypeStruct((bx, n_e, n_tb), jnp.int32)],
        compiler_params=_params("parallel"),
        name="expert_choice_select",
    )(aff_t)


GATE_PARTS = 3
TOKEN_LANE = GATE_PARTS * N_EXPERTS


def _slot_table_kernel(off_ref, pos_ref, aff_ref, tbl_ref, *, n_off, blocks_per_step):
    b, eg, kc = pl.program_id(0), pl.program_id(1), pl.program_id(2)
    epg = EXPERTS_PER_GATHER_STEP

    @pl.when(kc == 0)
    def _():
        tbl_ref[...] = jnp.zeros_like(tbl_ref)

    slot = lax.broadcasted_iota(jnp.int32, (GATHER_WINDOW, 1), 0)
    slot_2d = lax.broadcasted_iota(jnp.int32, (GATHER_WINDOW, GATHER_BLOCK), 0)
    lane = lax.broadcasted_iota(jnp.int32, (GATHER_BLOCK, LANES), 1)
    out_lane = lax.broadcasted_iota(jnp.int32, (GATHER_WINDOW, LANES), 1)
    local_token = lax.broadcasted_iota(jnp.int32, (GATHER_BLOCK, LANES), 0).astype(F32)
    place_r = lax.broadcasted_iota(jnp.int32, (N_EXPERTS, LANES), 0)
    place_c = lax.broadcasted_iota(jnp.int32, (N_EXPERTS, LANES), 1)
    def payload_of(kk):
        payload = jnp.where(lane == TOKEN_LANE, local_token, 0.0)
        rest = aff_ref[0, kk * GATHER_BLOCK:(kk + 1) * GATHER_BLOCK]
        for k in range(GATE_PARTS):
            part = rest.astype(BF16)
            payload = payload + _dot(part, (place_c == place_r + k * N_EXPERTS).astype(BF16))
            rest = rest - part.astype(F32)
        return payload.astype(BF16)

    payloads = [payload_of(kk) for kk in range(blocks_per_step)]
    pending = []
    for kk in range(blocks_per_step):
        kb = kc * blocks_per_step + kk
        first_token = jnp.where(out_lane == TOKEN_LANE, (kb * GATHER_BLOCK).astype(F32), 0.0)
        for ee in range(epg):
            off = off_ref[(b * N_EXPERTS + eg * epg + ee) * n_off + kb * (GATHER_BLOCK // OFFSET_BLOCK)]
            base = pl.multiple_of((off >> 4) << 4, WINDOW_ALIGN)
            onehot = (pos_ref[0, ee, 0, kk:kk + 1, :] - base == slot_2d).astype(BF16)
            pending.append((ee, off, base, _dot(onehot, payloads[kk]) + first_token))
    for ee, off, base, gathered in pending:
        win = pl.ds(base, GATHER_WINDOW)
        tbl_ref[0, ee, win, :] = jnp.where(slot >= off - base, gathered, tbl_ref[0, ee, win, :])


def slot_table(aff, pos, block_off, cap_pad):
    bx, t, _ = aff.shape
    n_tb = t // GATHER_BLOCK
    blocks_per_step = min(4, n_tb)
    n_steps = n_tb // blocks_per_step
    epg = EXPERTS_PER_GATHER_STEP
    tokens = blocks_per_step * GATHER_BLOCK
    pos5 = pos.reshape(bx, N_EXPERTS, n_steps, blocks_per_step, GATHER_BLOCK)
    grid_spec = pltpu.PrefetchScalarGridSpec(
        num_scalar_prefetch=1,
        grid=(bx, N_EXPERTS // epg, n_steps),
        in_specs=[
            pl.BlockSpec((1, epg, 1, blocks_per_step, GATHER_BLOCK), lambda b, g, k, s: (b, g, k, 0, 0)),
            pl.BlockSpec((1, tokens, N_EXPERTS), lambda b, g, k, s: (b, k, 0)),
        ],
        out_specs=pl.BlockSpec((1, epg, cap_pad, LANES), lambda b, g, k, s: (b, g, 0, 0)),
    )
    return pl.pallas_call(
        functools.partial(_slot_table_kernel, n_off=t // OFFSET_BLOCK, blocks_per_step=blocks_per_step),
        grid_spec=grid_spec,
        out_shape=jax.ShapeDtypeStruct((bx, N_EXPERTS, cap_pad, LANES), F32),
        compiler_params=_params("parallel", "parallel", "arbitrary"),
        name="slot_table",
    )(block_off.reshape(-1), pos5, aff)


GATHER_UNROLL = 16


def _row_gather_kernel(idx_ref, src_ref, xs_ref, tile_ref, *, cap, rows_per_token, chunk_stride):
    b, e = pl.program_id(0), pl.program_id(1)
    first = (b * N_EXPERTS + e) * cap

    def group(g, carry):
        for u in range(GATHER_UNROLL):
            s = g * GATHER_UNROLL + u
            row = pl.multiple_of(idx_ref[first + s], rows_per_token)
            tile_ref[pl.ds(s, rows_per_token, stride=chunk_stride), :] = src_ref[0, pl.ds(row, rows_per_token), :]
        return carry

    lax.fori_loop(0, cap // GATHER_UNROLL, group, 0)
    for j in range(rows_per_token):
        xs_ref[0, 0, :, j * LANES:(j + 1) * LANES] = tile_ref[j * chunk_stride:j * chunk_stride + cap, :].astype(BF16)


def gather_rows(h2_rows, token_row, cap, d):
    bx, n_rows, _ = h2_rows.shape
    rows_per_token = d // LANES
    chunk_stride = cap + 8
    grid_spec = pltpu.PrefetchScalarGridSpec(
        num_scalar_prefetch=1,
        grid=(bx, N_EXPERTS),
        in_specs=[pl.BlockSpec((1, n_rows, LANES), lambda b, e, s: (b, 0, 0), pipeline_mode=pl.Buffered(1))],
        out_specs=pl.BlockSpec((1, 1, cap, d), lambda b, e, s: (b, e, 0, 0)),
        scratch_shapes=[pltpu.VMEM((rows_per_token * chunk_stride, LANES), F32)],
    )
    return pl.pallas_call(
        functools.partial(_row_gather_kernel, cap=cap, rows_per_token=rows_per_token, chunk_stride=chunk_stride),
        grid_spec=grid_spec,
        out_shape=jax.ShapeDtypeStruct((bx, N_EXPERTS, cap, d), BF16),
        compiler_params=_params("parallel", "arbitrary"),
        name="gather_rows",
    )(token_row, h2_rows)


def _ffn_kernel(xs_ref, gs_ref, wg_ref, wu_ref, wd_ref, ye_ref, wg_bf, wu_bf, wd_bf, *, cap, row_tile):
    @pl.when(pl.program_id(1) == 0)
    def _():
        wg_bf[...] = wg_ref[0, 0].astype(BF16)
        wu_bf[...] = wu_ref[0, 0].astype(BF16)
        wd_bf[...] = wd_ref[0, 0].astype(BF16)

    n_b = xs_ref.shape[0]
    if n_b == 1:
        tiles = [[(0, r0, row_tile)] for r0 in range(0, cap, row_tile)]
    else:
        tiles = [[(bb, 0, cap) for bb in range(n_b)]]
    for tile in tiles:
        x = jnp.concatenate([xs_ref[bb, 0, r0:r0 + n] for bb, r0, n in tile], axis=0)
        terms = jnp.concatenate([gs_ref[bb, 0, r0:r0 + n, :] for bb, r0, n in tile], axis=0)
        lane = lax.broadcasted_iota(jnp.int32, terms.shape, 1)
        own = ((lane & (N_EXPERTS - 1)) == pl.program_id(0)) & (lane < TOKEN_LANE)
        terms = jnp.where(own, terms, 0.0)
        gate = _dot(x, wg_bf[...])
        up = _dot(x, wu_bf[...])
        hid = (gate * jax.nn.sigmoid(gate) * up).astype(BF16)
        g = jnp.sum(terms, axis=1, keepdims=True)
        ye = _dot(hid, wd_bf[...]) * g
        rows_per_slot = ye.shape[1] // LANES
        row = 0
        for bb, r0, n in tile:
            for j in range(rows_per_slot):
                ye_ref[bb, 0, pl.ds(r0 * rows_per_slot + j, n, stride=rows_per_slot), :] = (
                    ye[row:row + n, j * LANES:(j + 1) * LANES])
            row += n


def expert_ffn(xs, gs, wg, wu, wd, layer):
    bx, n_e, cap, d = xs.shape
    ff = wg.shape[-1]
    slot_rows = cap * (d // LANES)
    max_rows = 512
    row_tile = min(max_rows, cap)
    n_b = bx if bx * cap <= max_rows else 1
    return pl.pallas_call(
        functools.partial(_ffn_kernel, cap=cap, row_tile=row_tile),
        grid=(n_e, bx // n_b),
        in_specs=[
            pl.BlockSpec((n_b, 1, cap, d), lambda e, b: (b, e, 0, 0)),
            pl.BlockSpec((n_b, 1, cap, LANES), lambda e, b: (b, e, 0, 0)),
            pl.BlockSpec((1, 1, d, ff), lambda e, b: (layer, e, 0, 0)),
            pl.BlockSpec((1, 1, d, ff), lambda e, b: (layer, e, 0, 0)),
            pl.BlockSpec((1, 1, ff, d), lambda e, b: (layer, e, 0, 0)),
        ],
        out_specs=pl.BlockSpec((n_b, 1, slot_rows, LANES), lambda e, b: (b, e, 0, 0)),
        out_shape=jax.ShapeDtypeStruct((bx, n_e, slot_rows, LANES), F32),
        scratch_shapes=[pltpu.VMEM((d, ff), BF16), pltpu.VMEM((d, ff), BF16), pltpu.VMEM((ff, d), BF16)],
        compiler_params=_params("parallel", "arbitrary"),
        name="expert_ffn",
    )(xs, gs, wg, wu, wd)


SCATTER_UNROLL = 16


def _row_scatter_kernel(idx_ref, ye_ref, acc_ref, *, cap, rows_per_token):
    b, e = pl.program_id(0), pl.program_id(1)

    @pl.when(e == 0)
    def _():
        acc_ref[...] = jnp.zeros_like(acc_ref)

    first = (b * N_EXPERTS + e) * cap

    def group(g, carry):
        rows, sums = [], []
        for u in range(SCATTER_UNROLL):
            s = g * SCATTER_UNROLL + u
            row = pl.multiple_of(idx_ref[first + s], rows_per_token)
            src = pl.multiple_of(s * rows_per_token, rows_per_token)
            rows.append(row)
            sums.append(acc_ref[0, pl.ds(row, rows_per_token), :] + ye_ref[0, 0, pl.ds(src, rows_per_token), :])
        for row, total in zip(rows, sums):
            acc_ref[0, pl.ds(row, rows_per_token), :] = total
        return carry

    lax.fori_loop(0, cap // SCATTER_UNROLL, group, 0)


def scatter_rows(ye_rows, token_row, n_tok, cap):
    bx, n_e, slot_rows, _ = ye_rows.shape
    rows_per_token = slot_rows // cap
    grid_spec = pltpu.PrefetchScalarGridSpec(
        num_scalar_prefetch=1,
        grid=(bx, n_e),
        in_specs=[pl.BlockSpec((1, 1, slot_rows, LANES), lambda b, e, s: (b, e, 0, 0))],
        out_specs=pl.BlockSpec((1, n_tok * rows_per_token, LANES), lambda b, e, s: (b, 0, 0),
                               pipeline_mode=pl.Buffered(1)),
    )
    return pl.pallas_call(
        functools.partial(_row_scatter_kernel, cap=cap, rows_per_token=rows_per_token),
        grid_spec=grid_spec,
        out_shape=jax.ShapeDtypeStruct((bx, n_tok * rows_per_token, LANES), F32),
        compiler_params=_params("parallel", "arbitrary"),
        name="scatter_rows",
    )(token_row, ye_rows)


def _final_norm_kernel(moe_ref, x1_ref, g2_ref, lng_ref, lnb_ref, out_ref, *, alpha):
    tm, d = x1_ref.shape[1], x1_ref.shape[2]
    rows_per_token = d // LANES
    moe = jnp.concatenate([moe_ref[0, pl.ds(j, tm, stride=rows_per_token), :] for j in range(rows_per_token)],
                          axis=1)
    y = alpha * x1_ref[0] + g2_ref[0] * moe
    out_ref[0] = _layer_norm(y) * lng_ref[...] + lnb_ref[...]


def residual_norm(moe_rows, x1, g2, ln_g, ln_b, alpha):
    bx, t, d = x1.shape
    tm = min(512, t)
    rows_per_token = d // LANES
    tok = lambda b, i: (b, i, 0)
    const = lambda b, i: (0, 0)
    return pl.pallas_call(
        functools.partial(_final_norm_kernel, alpha=alpha),
        grid=(bx, t // tm),
        in_specs=[
            pl.BlockSpec((1, tm * rows_per_token, LANES), tok),
            pl.BlockSpec((1, tm, d), tok),
            pl.BlockSpec((1, 1, d), lambda b, i: (b, 0, 0)),
            pl.BlockSpec((1, d), const), pl.BlockSpec((1, d), const),
        ],
        out_specs=pl.BlockSpec((1, tm, d), tok),
        out_shape=jax.ShapeDtypeStruct((bx, t, d), F32),
        compiler_params=_params("parallel", "parallel"),
        name="residual_norm",
    )(moe_rows, x1, g2, ln_g.reshape(1, d), ln_b.reshape(1, d))


def _split_bf16(w):
    hi = w.astype(BF16)
    return hi, (w - hi.astype(F32)).astype(BF16)


def _mixer_tail(oa, ob, oc, x, mod, lw, alpha):
    g1, sh2, sc2, g2 = mod
    t = x.shape[1]
    cap = EC_CAPACITY * t // N_EXPERTS
    cap_pad = cap + SLOT_PAD
    x1, h2_rows, aff_t, aff = out_projection(oa, ob, oc, x, g1, sh2, sc2, lw["w_out"], lw["ln1_g"], lw["ln1_b"],
                                        lw["wr_hi"], lw["wr_lo"], alpha)
    pos, block_off = expert_choice_select(aff_t, cap)
    table = slot_table(aff, pos, block_off, cap_pad)
    token_row = (table[:, :, :cap, TOKEN_LANE].astype(jnp.int32) * (x.shape[2] // LANES)).reshape(-1)
    xs = gather_rows(h2_rows, token_row, cap, x.shape[2])
    ye_rows = expert_ffn(xs, table, lw["w_gate"], lw["w_up"], lw["w_down"], lw["layer"])
    moe_rows = scatter_rows(ye_rows, token_row, t, cap)
    return residual_norm(moe_rows, x1, g2, lw["ln2_g"], lw["ln2_b"], alpha)


def kernel(x, c, ctx, c_ctx, w_mod, b_mod, w_in, a_sink, nat_bias, conv_w, conv_b, conv_ln_g, conv_ln_b,
           w_out, ln1_g, ln1_b, w_router, w_gate, w_up, w_down, ln2_g, ln2_b):
    bsz, n_lat, d = x.shape
    depth = w_mod.shape[0]
    alpha = (2 * depth) ** 0.25
    cos_t, sin_t = rope_tables(n_lat)

    cond = jnp.concatenate([c, c_ctx[None, :], jnp.zeros((8 - bsz - 1, d), F32)], axis=0)
    mods = adaln_all(cond, w_mod, b_mod)

    for l in range(depth):
        last = l == depth - 1
        wr_hi, wr_lo = _split_bf16(w_router[l].T)
        lw = dict(w_out=w_out[l].astype(BF16), ln1_g=ln1_g[l], ln1_b=ln1_b[l], wr_hi=wr_hi, wr_lo=wr_lo,
                  w_gate=w_gate, w_up=w_up, w_down=w_down, layer=l,
                  ln2_g=ln2_g[l], ln2_b=ln2_b[l])
        w_in_l = w_in[l].astype(BF16)
        lat = [mods[l, :bsz, k * d:(k + 1) * d][:, None, :] for k in range(N_MOD)]
        cm = [jnp.broadcast_to(mods[l, bsz, k * d:(k + 1) * d][None, None, :], (bsz, 1, d))
              for k in range(N_MOD)]
        conv_args = (conv_w[l], conv_b[l], conv_ln_g[l], conv_ln_b[l])

        qa_c, ka_c, va_c, qb_c, kb_c, vb_c, hc_c = in_projection(ctx, cm[0], cm[1], w_in_l, cos_t, sin_t, rope=False)
        if not last:
            oa_c, ob_c = context_attention(qa_c, qb_c, ka_c, va_c, kb_c, vb_c, a_sink[l])
            oc_c = conformer_conv(hc_c, *conv_args)
            ctx_new = _mixer_tail(oa_c, ob_c, oc_c, ctx, (cm[2], cm[3], cm[4], cm[5]), lw, alpha)

        qa, ka, va, qb, kb, vb, hc = in_projection(x, lat[0], lat[1], w_in_l, cos_t, sin_t, rope=True)
        oa = window_attention(qa, ka, va, ka_c, va_c, a_sink[l])
        ob = neighbourhood_attention(qb, kb, vb, kb_c, vb_c, nat_bias[l])
        oc = conformer_conv(hc, *conv_args)
        x = _mixer_tail(oa, ob, oc, x, (lat[2], lat[3], lat[4], lat[5]), lw, alpha)
        if not last:
            ctx = ctx_new
    return x
```

```python
import functools

import numpy as np
import jax
import jax.numpy as jnp
from jax import lax
from jax.experimental import pallas as pl
from jax.experimental.pallas import tpu as pltpu

HEAD_DIM = 64
GRID_W = 64
A_Q_HEADS = 8
A_KV_HEADS = 2
A_GROUP = A_Q_HEADS // A_KV_HEADS
A_WINDOW = 128
B_HEADS = 4
NA_ROWS = 8
NA_COLS = 16
C_CHANNELS = 256
C_CONV_WIDTH = 31
A_Q_W = A_Q_HEADS * HEAD_DIM
A_KV_W = A_KV_HEADS * HEAD_DIM
B_W = B_HEADS * HEAD_DIM
OFF_AK = A_Q_W
OFF_AV = OFF_AK + A_KV_W
OFF_BQ = OFF_AV + A_KV_W
OFF_BK = OFF_BQ + B_W
OFF_BV = OFF_BK + B_W
OFF_C = OFF_BV + B_W
IN_WIDTH = OFF_C + 2 * C_CHANNELS
ROPE_WIDTH = A_Q_W + A_KV_W
N_EXPERTS = 16
EC_CAPACITY = 2
ROPE_BASE = 10000.0
LN_EPS = 1e-6
N_MOD = 6
NEG_INF = -1e30
QK_SCALE = HEAD_DIM ** -0.5

LANES = 128
WINDOW_ALIGN = 16
MXU_DEPTH = 256
OFFSET_BLOCK = 128
GATHER_BLOCK = MXU_DEPTH
GATHER_WINDOW = GATHER_BLOCK + WINDOW_ALIGN
SLOT_PAD = 3 * LANES
EXPERTS_PER_GATHER_STEP = 16
VMEM_LIMIT = 56 * 1024 * 1024

F32 = jnp.float32
BF16 = jnp.bfloat16


def _dot(a, b):
    return jnp.dot(a, b, preferred_element_type=F32)


def _dot_t(a, b):
    return lax.dot_general(a, b, (((1,), (1,)), ((), ())), preferred_element_type=F32)


def _layer_norm(x):
    mu = jnp.mean(x, axis=-1, keepdims=True)
    xc = x - mu
    var = jnp.mean(xc * xc, axis=-1, keepdims=True)
    return xc * lax.rsqrt(var + LN_EPS)


def _params(*sem):
    return pltpu.CompilerParams(dimension_semantics=sem, vmem_limit_bytes=VMEM_LIMIT)


def _mod_kernel(cond_ref, w_ref, b_ref, out_ref):
    cnd = cond_ref[...]
    act = cnd * jax.nn.sigmoid(cnd)
    out_ref[0] = jnp.dot(act, w_ref[0], preferred_element_type=F32,
                         precision=lax.Precision.HIGHEST) + b_ref[0]


def adaln_all(cond, w_mod, b_mod):
    n_layers, d, width = w_mod.shape
    rows = cond.shape[0]
    tn = 1536
    return pl.pallas_call(
        _mod_kernel,
        grid=(n_layers, width // tn),
        in_specs=[
            pl.BlockSpec((rows, d), lambda l, j: (0, 0)),
            pl.BlockSpec((1, d, tn), lambda l, j: (l, 0, j)),
            pl.BlockSpec((1, 1, tn), lambda l, j: (l, 0, j)),
        ],
        out_specs=pl.BlockSpec((1, rows, tn), lambda l, j: (l, 0, j)),
        out_shape=jax.ShapeDtypeStruct((n_layers, rows, width), F32),
        compiler_params=_params("parallel", "parallel"),
        name="adaln",
    )(cond, w_mod, b_mod.reshape(n_layers, 1, width))


def _inproj_kernel(x_ref, sh_ref, sc_ref, w_ref, cos_ref, sin_ref,
                   qa_ref, ka_ref, va_ref, qb_ref, kb_ref, vb_ref, hc_ref, *, rope):
    x = x_ref[0]
    h = _layer_norm(x) * (1.0 + sc_ref[0]) + sh_ref[0]
    u = _dot(h.astype(BF16), w_ref[...])

    def rotated(col):
        xq = u[:, col:col + LANES]
        if not rope:
            return xq
        lane = lax.broadcasted_iota(jnp.int32, xq.shape, 1)
        first = (lane & (HEAD_DIM // 2 - 1)) < (HEAD_DIM // 4)
        partner = jnp.where(first, pltpu.roll(xq, LANES - HEAD_DIM // 4, 1),
                            pltpu.roll(xq, HEAD_DIM // 4, 1))
        return xq * cos_ref[...] + partner * sin_ref[...]

    rot = [rotated(col) for col in range(0, ROPE_WIDTH, LANES)]
    n_q = A_Q_W // LANES
    qa_ref[0] = (jnp.concatenate(rot[:n_q], axis=1) * QK_SCALE).astype(BF16)
    ka_ref[0] = jnp.concatenate(rot[n_q:], axis=1).astype(BF16)
    va_ref[0] = u[:, OFF_AV:OFF_BQ].astype(BF16)
    qb_ref[0] = (u[:, OFF_BQ:OFF_BK] * QK_SCALE).astype(BF16)
    kb_ref[0] = u[:, OFF_BK:OFF_BV].astype(BF16)
    vb_ref[0] = u[:, OFF_BV:OFF_C].astype(BF16)
    a = u[:, OFF_C:OFF_C + C_CHANNELS]
    gate = u[:, OFF_C + C_CHANNELS:]
    hc_ref[0] = a * jax.nn.sigmoid(gate)


def in_projection(x, shift, scale, w_in_bf16, cos_t, sin_t, *, rope):
    bx, t, d = x.shape
    tm = min(512, t)
    widths = (A_Q_W, A_KV_W, A_KV_W, B_W, B_W, B_W, C_CHANNELS)
    dtypes = (BF16,) * 6 + (F32,)
    tok = lambda b, i: (b, i, 0)
    per_b = lambda b, i: (b, 0, 0)
    return pl.pallas_call(
        functools.partial(_inproj_kernel, rope=rope),
        grid=(bx, t // tm),
        in_specs=[
            pl.BlockSpec((1, tm, d), tok),
            pl.BlockSpec((1, 1, d), per_b),
            pl.BlockSpec((1, 1, d), per_b),
            pl.BlockSpec((d, IN_WIDTH), lambda b, i: (0, 0)),
            pl.BlockSpec((tm, LANES), lambda b, i: (i, 0)),
            pl.BlockSpec((tm, LANES), lambda b, i: (i, 0)),
        ],
        out_specs=[pl.BlockSpec((1, tm, w), tok) for w in widths],
        out_shape=[jax.ShapeDtypeStruct((bx, t, w), dt) for w, dt in zip(widths, dtypes)],
        compiler_params=_params("parallel", "parallel"),
        name="in_projection",
    )(x, shift, scale, w_in_bf16, cos_t, sin_t)


def rope_tables(n_tokens):
    t = jnp.arange(n_tokens, dtype=jnp.int32)
    row = (t // GRID_W).astype(F32)[:, None]
    col = (t % GRID_W).astype(F32)[:, None]
    n_freq = HEAD_DIM // 4
    inv_freq = ROPE_BASE ** (-jnp.arange(n_freq, dtype=F32) / n_freq)
    ang_r = row * inv_freq
    ang_c = col * inv_freq
    cos_h = jnp.concatenate([jnp.cos(ang_r), jnp.cos(ang_r), jnp.cos(ang_c), jnp.cos(ang_c)], axis=1)
    sin_h = jnp.concatenate([-jnp.sin(ang_r), jnp.sin(ang_r), -jnp.sin(ang_c), jnp.sin(ang_c)], axis=1)
    reps = LANES // HEAD_DIM
    return jnp.tile(cos_h, (1, reps)), jnp.tile(sin_h, (1, reps))


def _with_ones(v):
    return jnp.concatenate([v, jnp.ones_like(v)], axis=1)


def _attend(score_parts, values, sink=None):
    m = score_parts[0].max(axis=-1, keepdims=True)
    for s in score_parts[1:]:
        m = jnp.maximum(m, s.max(axis=-1, keepdims=True))
    if sink is not None:
        m = jnp.maximum(m, sink)
    acc = None
    for s, v in zip(score_parts, values):
        term = _dot(jnp.exp((s - m).astype(BF16)), v)
        acc = term if acc is None else acc + term
    den = acc[:, HEAD_DIM:HEAD_DIM + 1]
    if sink is not None:
        den = den + jnp.exp(sink - m)
    return acc[:, :HEAD_DIM] / den


def _attn_a_kernel(sink_ref, q_ref, kp_ref, kc_ref, kn_ref, vp_ref, vc_ref, vn_ref,
                   kctx_ref, vctx_ref, out_ref, *, n_lat, tq):
    i = pl.program_id(1)
    k_win = jnp.concatenate([kp_ref[0], kc_ref[0], kn_ref[0]], axis=0)
    v_win = jnp.concatenate([vp_ref[0], vc_ref[0], vn_ref[0]], axis=0)
    kctx = kctx_ref[0]
    vctx = vctx_ref[0]
    sub = A_WINDOW
    span = 3 * A_WINDOW
    rows = A_GROUP * sub
    q_onehot = ((lax.broadcasted_iota(jnp.int32, (rows, sub), 0) & (sub - 1))
                == lax.broadcasted_iota(jnp.int32, (rows, sub), 1)).astype(BF16)
    key_i = lax.broadcasted_iota(jnp.int32, (span, sub), 0)
    qry_i = lax.broadcasted_iota(jnp.int32, (span, sub), 1)
    rel = key_i - A_WINDOW - qry_i
    in_band = (rel <= A_WINDOW) & (rel >= -A_WINDOW)
    group_of_row = lax.broadcasted_iota(jnp.int32, (rows, 1), 0) >> (sub.bit_length() - 1)
    sinks = []
    for hk in range(A_KV_HEADS):
        sink = jnp.zeros((rows, 1), F32)
        for g in range(A_GROUP):
            sink = jnp.where(group_of_row == g, sink_ref[hk * A_GROUP + g], sink)
        sinks.append(sink)
    vctx_ext = [_with_ones(vctx[:, hk * HEAD_DIM:(hk + 1) * HEAD_DIM]) for hk in range(A_KV_HEADS)]
    v_ext = [_with_ones(v_win[:, hk * HEAD_DIM:(hk + 1) * HEAD_DIM]) for hk in range(A_KV_HEADS)]
    def mask_columns(j):
        kpos = i * tq + j * sub - A_WINDOW + key_i
        valid = in_band & (kpos >= 0) & (kpos < n_lat)
        return jnp.where(valid, 0.0, NEG_INF).astype(BF16)

    masks = [mask_columns(j) for j in range(tq // sub)]

    def scores(j, hk):
        sl = slice(hk * HEAD_DIM, (hk + 1) * HEAD_DIM)
        q_rows = q_ref[0, j * sub:(j + 1) * sub]
        q = jnp.concatenate([q_rows[:, h * HEAD_DIM:(h + 1) * HEAD_DIM]
                             for h in range(hk * A_GROUP, (hk + 1) * A_GROUP)], axis=0)
        q_aug = jnp.concatenate([q_onehot, q], axis=1)
        k_aug = jnp.concatenate([masks[j], k_win[j * sub:j * sub + span, sl]], axis=1)
        return [_dot_t(q_aug, k_aug), _dot_t(q, kctx[:, sl])]

    units = [(j, hk) for j in range(tq // sub) for hk in range(A_KV_HEADS)]
    all_scores = [scores(j, hk) for j, hk in units]
    for (j, hk), s in zip(units, all_scores):
        v_sub = v_ext[hk][j * sub:j * sub + span]
        o = _attend(s, [v_sub, vctx_ext[hk]], sink=sinks[hk]).astype(BF16)
        for g in range(A_GROUP):
            h = hk * A_GROUP + g
            out_ref[0, j * sub:(j + 1) * sub, h * HEAD_DIM:(h + 1) * HEAD_DIM] = o[g * sub:(g + 1) * sub]


def window_attention(qa, ka, va, kc_a, vc_a, sink):
    bsz, n_lat, _ = qa.shape
    n_ctx = kc_a.shape[1]
    tq = min(512, n_lat)
    w = A_WINDOW
    per = tq // w
    last = n_lat // w - 1
    prev = lambda b, i, s: (b, jnp.maximum(i * per - 1, 0), 0)
    cur = lambda b, i, s: (b, i, 0)
    nxt = lambda b, i, s: (b, jnp.minimum((i + 1) * per, last), 0)
    ctx = lambda b, i, s: (b, 0, 0)
    kv_specs = [pl.BlockSpec((1, w, A_KV_W), prev), pl.BlockSpec((1, tq, A_KV_W), cur),
                pl.BlockSpec((1, w, A_KV_W), nxt)]
    grid_spec = pltpu.PrefetchScalarGridSpec(
        num_scalar_prefetch=1,
        grid=(bsz, n_lat // tq),
        in_specs=[pl.BlockSpec((1, tq, A_Q_W), cur)] + kv_specs + kv_specs + [
            pl.BlockSpec((1, n_ctx, A_KV_W), ctx), pl.BlockSpec((1, n_ctx, A_KV_W), ctx)],
        out_specs=pl.BlockSpec((1, tq, A_Q_W), cur),
    )
    return pl.pallas_call(
        functools.partial(_attn_a_kernel, n_lat=n_lat, tq=tq),
        grid_spec=grid_spec,
        out_shape=jax.ShapeDtypeStruct((bsz, n_lat, A_Q_W), BF16),
        compiler_params=_params("parallel", "parallel"),
        name="window_attention",
    )(sink, qa, ka, ka, ka, va, va, va, kc_a, vc_a)


NB_Q_ROWS = 4
NB_BLOCKS_PER_STEP = 2


def _attn_b_kernel(q_ref, *refs):
    n_kv = NB_BLOCKS_PER_STEP + 2
    k_refs, v_refs = refs[:n_kv], refs[n_kv:2 * n_kv]
    kctx_ref, vctx_ref = refs[2 * n_kv:2 * n_kv + 2]
    bias_refs = refs[2 * n_kv + 2:-1]
    out_ref = refs[-1]
    tb = k_refs[0].shape[1]
    k_all = jnp.concatenate([r[0] for r in k_refs], axis=0)
    v_all = jnp.concatenate([r[0] for r in v_refs], axis=0)
    kctx = kctx_ref[0]
    vctx = vctx_ref[0]
    heads = [slice(h * HEAD_DIM, (h + 1) * HEAD_DIM) for h in range(B_HEADS)]
    units = [(sb, h) for sb in range(NB_BLOCKS_PER_STEP) for h in range(B_HEADS)]

    def scores(sb, h):
        q = q_ref[0, sb * tb:(sb + 1) * tb, heads[h]]
        return [_dot_t(q, k_all[sb * tb:(sb + 3) * tb, heads[h]]) + bias_refs[sb][0, h],
                _dot_t(q, kctx[:, heads[h]])]

    all_scores = [scores(sb, h) for sb, h in units]
    v_ext = [_with_ones(v_all[:, sl]) for sl in heads]
    vctx_ext = [_with_ones(vctx[:, sl]) for sl in heads]
    for (sb, h), s in zip(units, all_scores):
        o = _attend(s, [v_ext[h][sb * tb:(sb + 3) * tb], vctx_ext[h]])
        out_ref[0, sb * tb:(sb + 1) * tb, heads[h]] = o.astype(BF16)


def neighbourhood_bias(rel_bias, n_lat):
    rows = n_lat // GRID_W
    kr_n = min(NA_ROWS, rows)
    n_blocks = rows // NB_Q_ROWS
    n_heads, n_dr, n_dc = rel_bias.shape
    cols = np.arange(GRID_W)
    c_start = np.clip(cols - NA_COLS // 2, 0, GRID_W - NA_COLS)
    col_ok = (cols[None, :] >= c_start[:, None]) & (cols[None, :] < c_start[:, None] + NA_COLS)
    dc = np.clip(cols[None, :] - cols[:, None], -(NA_COLS - 1), NA_COLS - 1) + NA_COLS - 1
    pick_dc = (dc.reshape(-1)[None, :] == np.arange(n_dc)[:, None]).astype(np.float32)
    toeplitz = jnp.dot(rel_bias.reshape(n_heads * n_dr, n_dc), pick_dc, precision=lax.Precision.HIGHEST)
    toeplitz = jnp.where(col_ok.reshape(-1), toeplitz, NEG_INF).reshape(n_heads, n_dr, GRID_W, GRID_W)
    q_rl = np.arange(NB_Q_ROWS)
    k_rl = np.arange(3 * NB_Q_ROWS)
    row_ok, dr = [], []
    for j in sorted({0, min(1, n_blocks - 1), n_blocks - 1}):
        r = NB_Q_ROWS * j + q_rl
        kr = NB_Q_ROWS * (j - 1) + k_rl
        r_start = np.clip(r - kr_n // 2, 0, rows - kr_n)
        ok = (kr[None, :] >= r_start[:, None]) & (kr[None, :] < r_start[:, None] + kr_n)
        row_ok.append(ok & (kr[None, :] >= 0) & (kr[None, :] < rows))
        dr.append(np.clip(kr[None, :] - r[:, None] + NA_ROWS - 1, 0, n_dr - 1))
    row_ok = np.stack(row_ok)
    dr = np.stack(dr)
    tiles = jnp.stack([toeplitz[:, int(i)] for i in dr.reshape(-1)], axis=1)
    tiles = tiles.reshape((n_heads,) + dr.shape + (GRID_W, GRID_W))
    tiles = jnp.where(row_ok[None, :, :, :, None, None], tiles, NEG_INF)
    table = tiles.transpose(1, 0, 2, 4, 3, 5).reshape(
        dr.shape[0], n_heads, NB_Q_ROWS * GRID_W, 3 * NB_Q_ROWS * GRID_W)
    return table, n_blocks


def neighbourhood_attention(qb, kb, vb, kc_b, vc_b, rel_bias):
    bsz, n_lat, _ = qb.shape
    n_ctx = kc_b.shape[1]
    table, n_blocks = neighbourhood_bias(rel_bias, n_lat)
    assert n_blocks % NB_BLOCKS_PER_STEP == 0
    n_var = table.shape[0]
    tb = NB_Q_ROWS * GRID_W
    last = n_blocks - 1
    cur = lambda b, j: (b, j, 0)
    ctx = lambda b, j: (b, 0, 0)

    def key_block(off):
        return lambda b, j: (b, jnp.clip(NB_BLOCKS_PER_STEP * j + off, 0, last), 0)

    def variant(sb):
        def index(b, j):
            g = NB_BLOCKS_PER_STEP * j + sb
            return (jnp.where(g == 0, 0, jnp.where(g == last, n_var - 1, min(1, n_var - 1))), 0, 0, 0)
        return index

    kv_specs = [pl.BlockSpec((1, tb, B_W), key_block(off)) for off in range(-1, NB_BLOCKS_PER_STEP + 1)]
    return pl.pallas_call(
        _attn_b_kernel,
        grid=(bsz, n_blocks // NB_BLOCKS_PER_STEP),
        in_specs=[pl.BlockSpec((1, NB_BLOCKS_PER_STEP * tb, B_W), cur)] + kv_specs + kv_specs + [
            pl.BlockSpec((1, n_ctx, B_W), ctx), pl.BlockSpec((1, n_ctx, B_W), ctx)] + [
            pl.BlockSpec((1, B_HEADS, tb, 3 * tb), variant(sb)) for sb in range(NB_BLOCKS_PER_STEP)],
        out_specs=pl.BlockSpec((1, NB_BLOCKS_PER_STEP * tb, B_W), cur),
        out_shape=jax.ShapeDtypeStruct((bsz, n_lat, B_W), BF16),
        compiler_params=_params("parallel", "parallel"),
        name="neighbourhood_attention",
    )(qb, *([kb] * len(kv_specs)), *([vb] * len(kv_specs)), kc_b, vc_b, *([table] * NB_BLOCKS_PER_STEP))


def _ctx_attn_kernel(sink_ref, qa_ref, qb_ref, ka_ref, va_ref, kb_ref, vb_ref, oa_ref, ob_ref):
    qa, qb = qa_ref[0], qb_ref[0]
    ka, va, kb, vb = ka_ref[0], va_ref[0], kb_ref[0], vb_ref[0]
    for hq in range(A_Q_HEADS):
        sl = slice(hq * HEAD_DIM, (hq + 1) * HEAD_DIM)
        hk = hq // A_GROUP
        kv = slice(hk * HEAD_DIM, (hk + 1) * HEAD_DIM)
        q = qa[:, sl]
        o = _attend([_dot_t(q, ka[:, kv])], [_with_ones(va[:, kv])], sink=sink_ref[hq])
        oa_ref[0, :, sl] = o.astype(BF16)
    for h in range(B_HEADS):
        sl = slice(h * HEAD_DIM, (h + 1) * HEAD_DIM)
        q = qb[:, sl]
        o = _attend([_dot_t(q, kb[:, sl])], [_with_ones(vb[:, sl])])
        ob_ref[0, :, sl] = o.astype(BF16)


def context_attention(qa, qb, ka, va, kb, vb, sink):
    bsz, n_ctx, _ = qa.shape
    blk = lambda w: pl.BlockSpec((1, n_ctx, w), lambda b, s: (b, 0, 0))
    grid_spec = pltpu.PrefetchScalarGridSpec(
        num_scalar_prefetch=1,
        grid=(bsz,),
        in_specs=[blk(A_Q_W), blk(B_W), blk(A_KV_W), blk(A_KV_W), blk(B_W), blk(B_W)],
        out_specs=[blk(A_Q_W), blk(B_W)],
    )
    return pl.pallas_call(
        _ctx_attn_kernel,
        grid_spec=grid_spec,
        out_shape=[jax.ShapeDtypeStruct((bsz, n_ctx, A_Q_W), BF16),
                   jax.ShapeDtypeStruct((bsz, n_ctx, B_W), BF16)],
        compiler_params=_params("parallel"),
        name="context_attention",
    )(sink, qa, qb, ka, va, kb, vb)


CONV_HALO = 16
F32_SUBLANES = 8
CONV_SHIFT_SPAN = (CONV_HALO + C_CONV_WIDTH // 2) // F32_SUBLANES * F32_SUBLANES


def _conv_kernel(prev_ref, cur_ref, next_ref, w_ref, b_ref, g_ref, beta_ref, out_ref, shifted_ref, *, ts):
    i = pl.program_id(1)
    n_i = pl.num_programs(1)
    ext = jnp.concatenate([jnp.where(i > 0, prev_ref[0], 0.0), cur_ref[0],
                           jnp.where(i < n_i - 1, next_ref[0], 0.0)], axis=0)
    for r in range(F32_SUBLANES):
        shifted_ref[r] = ext[r:r + ts + CONV_SHIFT_SPAN]
    acc = jnp.zeros((ts, C_CHANNELS), F32)
    for k in range(C_CONV_WIDTH):
        start = CONV_HALO - C_CONV_WIDTH // 2 + k
        aligned = start - start % F32_SUBLANES
        acc = acc + shifted_ref[start % F32_SUBLANES, aligned:aligned + ts] * w_ref[k:k + 1]
    y = _layer_norm(acc + b_ref[...]) * g_ref[...] + beta_ref[...]
    out_ref[0] = (y * jax.nn.sigmoid(y)).astype(BF16)


def conformer_conv(hc, conv_w, conv_b, ln_g, ln_b):
    bx, t, ch = hc.shape
    ts = min(512, t)
    per = ts // CONV_HALO
    last = t // CONV_HALO - 1
    row = lambda v: v.reshape(1, ch)
    const = lambda b, i: (0, 0)
    return pl.pallas_call(
        functools.partial(_conv_kernel, ts=ts),
        grid=(bx, t // ts),
        in_specs=[
            pl.BlockSpec((1, CONV_HALO, ch), lambda b, i: (b, jnp.maximum(i * per - 1, 0), 0)),
            pl.BlockSpec((1, ts, ch), lambda b, i: (b, i, 0)),
            pl.BlockSpec((1, CONV_HALO, ch), lambda b, i: (b, jnp.minimum((i + 1) * per, last), 0)),
            pl.BlockSpec((C_CONV_WIDTH, ch), const),
            pl.BlockSpec((1, ch), const), pl.BlockSpec((1, ch), const), pl.BlockSpec((1, ch), const),
        ],
        out_specs=pl.BlockSpec((1, ts, ch), lambda b, i: (b, i, 0)),
        out_shape=jax.ShapeDtypeStruct((bx, t, ch), BF16),
        scratch_shapes=[pltpu.VMEM((F32_SUBLANES, ts + CONV_SHIFT_SPAN, ch), F32)],
        compiler_params=_params("parallel", "parallel"),
        name="conformer_conv",
    )(hc, hc, hc, conv_w, row(conv_b), row(ln_g), row(ln_b))


OUTPROJ_ROW_CHUNKS = 4


def _outproj_kernel(oa_ref, ob_ref, oc_ref, x_ref, g1_ref, sh_ref, sc_ref, w_ref, lng_ref, lnb_ref,
                    wr_hi_ref, wr_lo_ref, x1_ref, h2_ref, afft_ref, aff_ref, *, alpha):
    tm, d = x_ref.shape[1], x_ref.shape[2]
    rc = tm // OUTPROJ_ROW_CHUNKS
    chunks = [slice(k * rc, (k + 1) * rc) for k in range(OUTPROJ_ROW_CHUNKS)]
    outs = [(_dot(oa_ref[0, r], w_ref[0:A_Q_W])
             + _dot(ob_ref[0, r], w_ref[A_Q_W:A_Q_W + B_W])
             + _dot(oc_ref[0, r], w_ref[A_Q_W + B_W:])) for r in chunks]
    w_hi, w_lo = wr_hi_ref[...], wr_lo_ref[...]
    eye = (lax.broadcasted_iota(jnp.int32, (N_EXPERTS, N_EXPERTS), 0)
           == lax.broadcasted_iota(jnp.int32, (N_EXPERTS, N_EXPERTS), 1)).astype(BF16)
    rows_per_token = d // LANES
    ys = [_layer_norm(alpha * x_ref[0, r] + g1_ref[0] * o) * lng_ref[...] + lnb_ref[...]
          for r, o in zip(chunks, outs)]
    for r, y in zip(chunks, ys):
        x1_ref[0, r] = y
    h2s = [_layer_norm(y) * (1.0 + sc_ref[0]) + sh_ref[0] for y in ys]
    his = [h2.astype(BF16) for h2 in h2s]
    los = [(h2 - hi.astype(F32)).astype(BF16) for h2, hi in zip(h2s, his)]
    all_logits = [_dot_t(hi, w_hi) + _dot_t(lo, w_hi) + _dot_t(hi, w_lo) for hi, lo in zip(his, los)]
    for k, h2 in enumerate(h2s):
        for j in range(rows_per_token):
            h2_ref[0, pl.ds(k * rc * rows_per_token + j, rc, stride=rows_per_token), :] = (
                h2[:, j * LANES:(j + 1) * LANES])
    for r, logits in zip(chunks, all_logits):
        e_n = jnp.exp(logits - logits.max(axis=1, keepdims=True))
        aff = e_n / e_n.sum(axis=1, keepdims=True)
        aff_ref[0, r] = aff
        aff_t, rest = None, aff
        for _ in range(GATE_PARTS):
            part = rest.astype(BF16)
            term = _dot_t(eye, part)
            aff_t = term if aff_t is None else aff_t + term
            rest = rest - part.astype(F32)
        afft_ref[0, :, r] = aff_t


def out_projection(oa, ob, oc, x, g1, sh2, sc2, w_out_bf16, ln_g, ln_b, wr_hi, wr_lo, alpha):
    bx, t, d = x.shape
    tm = min(512, t)
    tok = lambda b, i: (b, i, 0)
    per_b = lambda b, i: (b, 0, 0)
    const = lambda b, i: (0, 0)
    vec = pl.BlockSpec((1, d), const)
    return pl.pallas_call(
        functools.partial(_outproj_kernel, alpha=alpha),
        grid=(bx, t // tm),
        in_specs=[
            pl.BlockSpec((1, tm, A_Q_W), tok), pl.BlockSpec((1, tm, B_W), tok),
            pl.BlockSpec((1, tm, C_CHANNELS), tok), pl.BlockSpec((1, tm, d), tok),
            pl.BlockSpec((1, 1, d), per_b), pl.BlockSpec((1, 1, d), per_b), pl.BlockSpec((1, 1, d), per_b),
            pl.BlockSpec(w_out_bf16.shape, const), vec, vec,
            pl.BlockSpec((N_EXPERTS, d), const), pl.BlockSpec((N_EXPERTS, d), const),
        ],
        out_specs=[pl.BlockSpec((1, tm, d), tok), pl.BlockSpec((1, tm * (d // LANES), LANES), tok),
                   pl.BlockSpec((1, N_EXPERTS, tm), lambda b, i: (b, 0, i)),
                   pl.BlockSpec((1, tm, N_EXPERTS), tok)],
        out_shape=[jax.ShapeDtypeStruct((bx, t, d), F32),
                   jax.ShapeDtypeStruct((bx, t * (d // LANES), LANES), F32),
                   jax.ShapeDtypeStruct((bx, N_EXPERTS, t), F32),
                   jax.ShapeDtypeStruct((bx, t, N_EXPERTS), F32)],
        compiler_params=_params("parallel", "parallel"),
        name="out_projection",
    )(oa, ob, oc, x, g1, sh2, sc2, w_out_bf16, ln_g.reshape(1, d), ln_b.reshape(1, d), wr_hi, wr_lo)


def _select_kernel(afft_ref, pos_ref, off_ref, *, cap, n_tok):
    aff = afft_ref[0]

    def bit_step(j, bits):
        cand = bits | (jnp.int32(1) << (30 - j))
        cnt = jnp.sum((aff >= pltpu.bitcast(cand, F32)).astype(jnp.int32), axis=1, keepdims=True)
        return jnp.where(cnt >= cap, cand, bits)

    thr = pltpu.bitcast(lax.fori_loop(0, 31, bit_step, jnp.zeros((N_EXPERTS, 1), jnp.int32)), F32)
    above = (aff > thr).astype(F32)
    tied = (aff == thr).astype(F32)
    need = cap - jnp.sum(above, axis=1, keepdims=True)

    blk = MXU_DEPTH
    blocks = [slice(k * blk, (k + 1) * blk) for k in range(n_tok // blk)]
    r_i = lax.broadcasted_iota(jnp.int32, (blk, blk), 0)
    c_i = lax.broadcasted_iota(jnp.int32, (blk, blk), 1)
    strict_upper = (r_i < c_i).astype(BF16)

    def running(block_sums):
        run, total = [], jnp.zeros((N_EXPERTS, 1), F32)
        for s in block_sums:
            run.append(total)
            total = total + s
        return run

    tied_b = [tied[:, sl] for sl in blocks]
    tied_rank = [_dot(t.astype(BF16), strict_upper) for t in tied_b]
    tied_before = running([t.sum(axis=1, keepdims=True) for t in tied_b])
    sel_b = [above[:, sl] + t * ((before + rank) < need).astype(F32)
             for sl, t, before, rank in zip(blocks, tied_b, tied_before, tied_rank)]
    sel_rank = [_dot(s.astype(BF16), strict_upper) for s in sel_b]
    half_sums = [[s[:, h * OFFSET_BLOCK:(h + 1) * OFFSET_BLOCK].sum(axis=1, keepdims=True)
                  for h in range(blk // OFFSET_BLOCK)] for s in sel_b]
    offs = running([h for hs in half_sums for h in hs])
    per_blk = blk // OFFSET_BLOCK
    for k, (sl, s, rank) in enumerate(zip(blocks, sel_b, sel_rank)):
        pos_ref[0, :, sl] = jnp.where(s > 0.5, (offs[k * per_blk] + rank).astype(jnp.int32), -1)
    off_ref[0] = jnp.concatenate(offs, axis=1).astype(jnp.int32)


def expert_choice_select(aff_t, cap):
    bx, n_e, t = aff_t.shape
    n_tb = t // OFFSET_BLOCK
    return pl.pallas_call(
        functools.partial(_select_kernel, cap=cap, n_tok=t),
        grid=(bx,),
        in_specs=[pl.BlockSpec((1, n_e, t), lambda b: (b, 0, 0))],
        out_specs=[pl.BlockSpec((1, n_e, t), lambda b: (b, 0, 0)),
                   pl.BlockSpec((1, n_e, n_tb), lambda b: (b, 0, 0))],
        out_shape=[jax.ShapeDtypeStruct((bx, n_e, t), jnp.int32),
                   jax.ShapeDtypeStruct((bx, n_e, n_tb), jnp.int32)],
        compiler_params=_params("parallel"),
        name="expert_choice_select",
    )(aff_t)


GATE_PARTS = 3
TOKEN_LANE = GATE_PARTS * N_EXPERTS


def _slot_table_kernel(off_ref, pos_ref, aff_ref, tbl_ref, *, n_off, blocks_per_step):
    b, eg, kc = pl.program_id(0), pl.program_id(1), pl.program_id(2)
    epg = EXPERTS_PER_GATHER_STEP

    @pl.when(kc == 0)
    def _():
        tbl_ref[...] = jnp.zeros_like(tbl_ref)

    slot = lax.broadcasted_iota(jnp.int32, (GATHER_WINDOW, 1), 0)
    slot_2d = lax.broadcasted_iota(jnp.int32, (GATHER_WINDOW, GATHER_BLOCK), 0)
    lane = lax.broadcasted_iota(jnp.int32, (GATHER_BLOCK, LANES), 1)
    out_lane = lax.broadcasted_iota(jnp.int32, (GATHER_WINDOW, LANES), 1)
    local_token = lax.broadcasted_iota(jnp.int32, (GATHER_BLOCK, LANES), 0).astype(F32)
    place_r = lax.broadcasted_iota(jnp.int32, (N_EXPERTS, LANES), 0)
    place_c = lax.broadcasted_iota(jnp.int32, (N_EXPERTS, LANES), 1)
    def payload_of(kk):
        payload = jnp.where(lane == TOKEN_LANE, local_token, 0.0)
        rest = aff_ref[0, kk * GATHER_BLOCK:(kk + 1) * GATHER_BLOCK]
        for k in range(GATE_PARTS):
            part = rest.astype(BF16)
            payload = payload + _dot(part, (place_c == place_r + k * N_EXPERTS).astype(BF16))
            rest = rest - part.astype(F32)
        return payload.astype(BF16)

    payloads = [payload_of(kk) for kk in range(blocks_per_step)]
    pending = []
    for kk in range(blocks_per_step):
        kb = kc * blocks_per_step + kk
        first_token = jnp.where(out_lane == TOKEN_LANE, (kb * GATHER_BLOCK).astype(F32), 0.0)
        for ee in range(epg):
            off = off_ref[(b * N_EXPERTS + eg * epg + ee) * n_off + kb * (GATHER_BLOCK // OFFSET_BLOCK)]
            base = pl.multiple_of((off >> 4) << 4, WINDOW_ALIGN)
            onehot = (pos_ref[0, ee, 0, kk:kk + 1, :] - base == slot_2d).astype(BF16)
            pending.append((ee, off, base, _dot(onehot, payloads[kk]) + first_token))
    for ee, off, base, gathered in pending:
        win = pl.ds(base, GATHER_WINDOW)
        tbl_ref[0, ee, win, :] = jnp.where(slot >= off - base, gathered, tbl_ref[0, ee, win, :])


def slot_table(aff, pos, block_off, cap_pad):
    bx, t, _ = aff.shape
    n_tb = t // GATHER_BLOCK
    blocks_per_step = min(4, n_tb)
    n_steps = n_tb // blocks_per_step
    epg = EXPERTS_PER_GATHER_STEP
    tokens = blocks_per_step * GATHER_BLOCK
    pos5 = pos.reshape(bx, N_EXPERTS, n_steps, blocks_per_step, GATHER_BLOCK)
    grid_spec = pltpu.PrefetchScalarGridSpec(
        num_scalar_prefetch=1,
        grid=(bx, N_EXPERTS // epg, n_steps),
        in_specs=[
            pl.BlockSpec((1, epg, 1, blocks_per_step, GATHER_BLOCK), lambda b, g, k, s: (b, g, k, 0, 0)),
            pl.BlockSpec((1, tokens, N_EXPERTS), lambda b, g, k, s: (b, k, 0)),
        ],
        out_specs=pl.BlockSpec((1, epg, cap_pad, LANES), lambda b, g, k, s: (b, g, 0, 0)),
    )
    return pl.pallas_call(
        functools.partial(_slot_table_kernel, n_off=t // OFFSET_BLOCK, blocks_per_step=blocks_per_step),
        grid_spec=grid_spec,
        out_shape=jax.ShapeDtypeStruct((bx, N_EXPERTS, cap_pad, LANES), F32),
        compiler_params=_params("parallel", "parallel", "arbitrary"),
        name="slot_table",
    )(block_off.reshape(-1), pos5, aff)


GATHER_UNROLL = 16


def _row_gather_kernel(idx_ref, src_ref, xs_ref, tile_ref, *, cap, rows_per_token, chunk_stride):
    b, e = pl.program_id(0), pl.program_id(1)
    first = (b * N_EXPERTS + e) * cap

    def group(g, carry):
        for u in range(GATHER_UNROLL):
            s = g * GATHER_UNROLL + u
            row = pl.multiple_of(idx_ref[first + s], rows_per_token)
            tile_ref[pl.ds(s, rows_per_token, stride=chunk_stride), :] = src_ref[0, pl.ds(row, rows_per_token), :]
        return carry

    lax.fori_loop(0, cap // GATHER_UNROLL, group, 0)
    for j in range(rows_per_token):
        xs_ref[0, 0, :, j * LANES:(j + 1) * LANES] = tile_ref[j * chunk_stride:j * chunk_stride + cap, :].astype(BF16)


def gather_rows(h2_rows, token_row, cap, d):
    bx, n_rows, _ = h2_rows.shape
    rows_per_token = d // LANES
    chunk_stride = cap + 8
    grid_spec = pltpu.PrefetchScalarGridSpec(
        num_scalar_prefetch=1,
        grid=(bx, N_EXPERTS),
        in_specs=[pl.BlockSpec((1, n_rows, LANES), lambda b, e, s: (b, 0, 0), pipeline_mode=pl.Buffered(1))],
        out_specs=pl.BlockSpec((1, 1, cap, d), lambda b, e, s: (b, e, 0, 0)),
        scratch_shapes=[pltpu.VMEM((rows_per_token * chunk_stride, LANES), F32)],
    )
    return pl.pallas_call(
        functools.partial(_row_gather_kernel, cap=cap, rows_per_token=rows_per_token, chunk_stride=chunk_stride),
        grid_spec=grid_spec,
        out_shape=jax.ShapeDtypeStruct((bx, N_EXPERTS, cap, d), BF16),
        compiler_params=_params("parallel", "arbitrary"),
        name="gather_rows",
    )(token_row, h2_rows)


def _ffn_kernel(xs_ref, gs_ref, wg_ref, wu_ref, wd_ref, ye_ref, wg_bf, wu_bf, wd_bf, *, cap, row_tile):
    @pl.when(pl.program_id(1) == 0)
    def _():
        wg_bf[...] = wg_ref[0, 0].astype(BF16)
        wu_bf[...] = wu_ref[0, 0].astype(BF16)
        wd_bf[...] = wd_ref[0, 0].astype(BF16)

    n_b = xs_ref.shape[0]
    if n_b == 1:
        tiles = [[(0, r0, row_tile)] for r0 in range(0, cap, row_tile)]
    else:
        tiles = [[(bb, 0, cap) for bb in range(n_b)]]
    for tile in tiles:
        x = jnp.concatenate([xs_ref[bb, 0, r0:r0 + n] for bb, r0, n in tile], axis=0)
        terms = jnp.concatenate([gs_ref[bb, 0, r0:r0 + n, :] for bb, r0, n in tile], axis=0)
        lane = lax.broadcasted_iota(jnp.int32, terms.shape, 1)
        own = ((lane & (N_EXPERTS - 1)) == pl.program_id(0)) & (lane < TOKEN_LANE)
        terms = jnp.where(own, terms, 0.0)
        gate = _dot(x, wg_bf[...])
        up = _dot(x, wu_bf[...])
        hid = (gate * jax.nn.sigmoid(gate) * up).astype(BF16)
        g = jnp.sum(terms, axis=1, keepdims=True)
        ye = _dot(hid, wd_bf[...]) * g
        rows_per_slot = ye.shape[1] // LANES
        row = 0
        for bb, r0, n in tile:
            for j in range(rows_per_slot):
                ye_ref[bb, 0, pl.ds(r0 * rows_per_slot + j, n, stride=rows_per_slot), :] = (
                    ye[row:row + n, j * LANES:(j + 1) * LANES])
            row += n


def expert_ffn(xs, gs, wg, wu, wd, layer):
    bx, n_e, cap, d = xs.shape
    ff = wg.shape[-1]
    slot_rows = cap * (d // LANES)
    max_rows = 512
    row_tile = min(max_rows, cap)
    n_b = bx if bx * cap <= max_rows else 1
    return pl.pallas_call(
        functools.partial(_ffn_kernel, cap=cap, row_tile=row_tile),
        grid=(n_e, bx // n_b),
        in_specs=[
            pl.BlockSpec((n_b, 1, cap, d), lambda e, b: (b, e, 0, 0)),
            pl.BlockSpec((n_b, 1, cap, LANES), lambda e, b: (b, e, 0, 0)),
            pl.BlockSpec((1, 1, d, ff), lambda e, b: (layer, e, 0, 0)),
            pl.BlockSpec((1, 1, d, ff), lambda e, b: (layer, e, 0, 0)),
            pl.BlockSpec((1, 1, ff, d), lambda e, b: (layer, e, 0, 0)),
        ],
        out_specs=pl.BlockSpec((n_b, 1, slot_rows, LANES), lambda e, b: (b, e, 0, 0)),
        out_shape=jax.ShapeDtypeStruct((bx, n_e, slot_rows, LANES), F32),
        scratch_shapes=[pltpu.VMEM((d, ff), BF16), pltpu.VMEM((d, ff), BF16), pltpu.VMEM((ff, d), BF16)],
        compiler_params=_params("parallel", "arbitrary"),
        name="expert_ffn",
    )(xs, gs, wg, wu, wd)


SCATTER_UNROLL = 16


def _row_scatter_kernel(idx_ref, ye_ref, acc_ref, *, cap, rows_per_token):
    b, e = pl.program_id(0), pl.program_id(1)

    @pl.when(e == 0)
    def _():
        acc_ref[...] = jnp.zeros_like(acc_ref)

    first = (b * N_EXPERTS + e) * cap

    def group(g, carry):
        rows, sums = [], []
        for u in range(SCATTER_UNROLL):
            s = g * SCATTER_UNROLL + u
            row = pl.multiple_of(idx_ref[first + s], rows_per_token)
            src = pl.multiple_of(s * rows_per_token, rows_per_token)
            rows.append(row)
            sums.append(acc_ref[0, pl.ds(row, rows_per_token), :] + ye_ref[0, 0, pl.ds(src, rows_per_token), :])
        for row, total in zip(rows, sums):
            acc_ref[0, pl.ds(row, rows_per_token), :] = total
        return carry

    lax.fori_loop(0, cap // SCATTER_UNROLL, group, 0)


def scatter_rows(ye_rows, token_row, n_tok, cap):
    bx, n_e, slot_rows, _ = ye_rows.shape
    rows_per_token = slot_rows // cap
    grid_spec = pltpu.PrefetchScalarGridSpec(
        num_scalar_prefetch=1,
        grid=(bx, n_e),
        in_specs=[pl.BlockSpec((1, 1, slot_rows, LANES), lambda b, e, s: (b, e, 0, 0))],
        out_specs=pl.BlockSpec((1, n_tok * rows_per_token, LANES), lambda b, e, s: (b, 0, 0),
                               pipeline_mode=pl.Buffered(1)),
    )
    return pl.pallas_call(
        functools.partial(_row_scatter_kernel, cap=cap, rows_per_token=rows_per_token),
        grid_spec=grid_spec,
        out_shape=jax.ShapeDtypeStruct((bx, n_tok * rows_per_token, LANES), F32),
        compiler_params=_params("parallel", "arbitrary"),
        name="scatter_rows",
    )(token_row, ye_rows)


def _final_norm_kernel(moe_ref, x1_ref, g2_ref, lng_ref, lnb_ref, out_ref, *, alpha):
    tm, d = x1_ref.shape[1], x1_ref.shape[2]
    rows_per_token = d // LANES
    moe = jnp.concatenate([moe_ref[0, pl.ds(j, tm, stride=rows_per_token), :] for j in range(rows_per_token)],
                          axis=1)
    y = alpha * x1_ref[0] + g2_ref[0] * moe
    out_ref[0] = _layer_norm(y) * lng_ref[...] + lnb_ref[...]


def residual_norm(moe_rows, x1, g2, ln_g, ln_b, alpha):
    bx, t, d = x1.shape
    tm = min(512, t)
    rows_per_token = d // LANES
    tok = lambda b, i: (b, i, 0)
    const = lambda b, i: (0, 0)
    return pl.pallas_call(
        functools.partial(_final_norm_kernel, alpha=alpha),
        grid=(bx, t // tm),
        in_specs=[
            pl.BlockSpec((1, tm * rows_per_token, LANES), tok),
            pl.BlockSpec((1, tm, d), tok),
            pl.BlockSpec((1, 1, d), lambda b, i: (b, 0, 0)),
            pl.BlockSpec((1, d), const), pl.BlockSpec((1, d), const),
        ],
        out_specs=pl.BlockSpec((1, tm, d), tok),
        out_shape=jax.ShapeDtypeStruct((bx, t, d), F32),
        compiler_params=_params("parallel", "parallel"),
        name="residual_norm",
    )(moe_rows, x1, g2, ln_g.reshape(1, d), ln_b.reshape(1, d))


def _split_bf16(w):
    hi = w.astype(BF16)
    return hi, (w - hi.astype(F32)).astype(BF16)


def _mixer_tail(oa, ob, oc, x, mod, lw, alpha):
    g1, sh2, sc2, g2 = mod
    t = x.shape[1]
    cap = EC_CAPACITY * t // N_EXPERTS
    cap_pad = cap + SLOT_PAD
    x1, h2_rows, aff_t, aff = out_projection(oa, ob, oc, x, g1, sh2, sc2, lw["w_out"], lw["ln1_g"], lw["ln1_b"],
                                        lw["wr_hi"], lw["wr_lo"], alpha)
    pos, block_off = expert_choice_select(aff_t, cap)
    table = slot_table(aff, pos, block_off, cap_pad)
    token_row = (table[:, :, :cap, TOKEN_LANE].astype(jnp.int32) * (x.shape[2] // LANES)).reshape(-1)
    xs = gather_rows(h2_rows, token_row, cap, x.shape[2])
    ye_rows = expert_ffn(xs, table, lw["w_gate"], lw["w_up"], lw["w_down"], lw["layer"])
    moe_rows = scatter_rows(ye_rows, token_row, t, cap)
    return residual_norm(moe_rows, x1, g2, lw["ln2_g"], lw["ln2_b"], alpha)


def kernel(x, c, ctx, c_ctx, w_mod, b_mod, w_in, a_sink, nat_bias, conv_w, conv_b, conv_ln_g, conv_ln_b,
           w_out, ln1_g, ln1_b, w_router, w_gate, w_up, w_down, ln2_g, ln2_b):
    bsz, n_lat, d = x.shape
    depth = w_mod.shape[0]
    alpha = (2 * depth) ** 0.25
    cos_t, sin_t = rope_tables(n_lat)

    cond = jnp.concatenate([c, c_ctx[None, :], jnp.zeros((8 - bsz - 1, d), F32)], axis=0)
    mods = adaln_all(cond, w_mod, b_mod)

    for l in range(depth):
        last = l == depth - 1
        wr_hi, wr_lo = _split_bf16(w_router[l].T)
        lw = dict(w_out=w_out[l].astype(BF16), ln1_g=ln1_g[l], ln1_b=ln1_b[l], wr_hi=wr_hi, wr_lo=wr_lo,
                  w_gate=w_gate, w_up=w_up, w_down=w_down, layer=l,
                  ln2_g=ln2_g[l], ln2_b=ln2_b[l])
        w_in_l = w_in[l].astype(BF16)
        lat = [mods[l, :bsz, k * d:(k + 1) * d][:, None, :] for k in range(N_MOD)]
        cm = [jnp.broadcast_to(mods[l, bsz, k * d:(k + 1) * d][None, None, :], (bsz, 1, d))
              for k in range(N_MOD)]
        conv_args = (conv_w[l], conv_b[l], conv_ln_g[l], conv_ln_b[l])

        qa_c, ka_c, va_c, qb_c, kb_c, vb_c, hc_c = in_projection(ctx, cm[0], cm[1], w_in_l, cos_t, sin_t, rope=False)
        if not last:
            oa_c, ob_c = context_attention(qa_c, qb_c, ka_c, va_c, kb_c, vb_c, a_sink[l])
            oc_c = conformer_conv(hc_c, *conv_args)
            ctx_new = _mixer_tail(oa_c, ob_c, oc_c, ctx, (cm[2], cm[3], cm[4], cm[5]), lw, alpha)

        qa, ka, va, qb, kb, vb, hc = in_projection(x, lat[0], lat[1], w_in_l, cos_t, sin_t, rope=True)
        oa = window_attention(qa, ka, va, ka_c, va_c, a_sink[l])
        ob = neighbourhood_attention(qb, kb, vb, kb_c, vb_c, nat_bias[l])
        oc = conformer_conv(hc, *conv_args)
        x = _mixer_tail(oa, ob, oc, x, (lat[2], lat[3], lat[4], lat[5]), lw, alpha)
        if not last:
            ctx = ctx_new
    return x
```

```python
import functools

import numpy as np
import jax
import jax.numpy as jnp
from jax import lax
from jax.experimental import pallas as pl
from jax.experimental.pallas import tpu as pltpu

HEAD_DIM = 64
GRID_W = 64
A_Q_HEADS = 8
A_KV_HEADS = 2
A_GROUP = A_Q_HEADS // A_KV_HEADS
A_WINDOW = 128
B_HEADS = 4
NA_ROWS = 8
NA_COLS = 16
C_CHANNELS = 256
C_CONV_WIDTH = 31
A_Q_W = A_Q_HEADS * HEAD_DIM
A_KV_W = A_KV_HEADS * HEAD_DIM
B_W = B_HEADS * HEAD_DIM
OFF_AK = A_Q_W
OFF_AV = OFF_AK + A_KV_W
OFF_BQ = OFF_AV + A_KV_W
OFF_BK = OFF_BQ + B_W
OFF_BV = OFF_BK + B_W
OFF_C = OFF_BV + B_W
IN_WIDTH = OFF_C + 2 * C_CHANNELS
ROPE_WIDTH = A_Q_W + A_KV_W
N_EXPERTS = 16
EC_CAPACITY = 2
ROPE_BASE = 10000.0
LN_EPS = 1e-6
N_MOD = 6
NEG_INF = -1e30
QK_SCALE = HEAD_DIM ** -0.5

LANES = 128
WINDOW_ALIGN = 16
MXU_DEPTH = 256
OFFSET_BLOCK = 128
GATHER_BLOCK = MXU_DEPTH
GATHER_WINDOW = GATHER_BLOCK + WINDOW_ALIGN
SLOT_PAD = 3 * LANES
EXPERTS_PER_GATHER_STEP = 16
VMEM_LIMIT = 56 * 1024 * 1024

F32 = jnp.float32
BF16 = jnp.bfloat16


def _dot(a, b):
    return jnp.dot(a, b, preferred_element_type=F32)


def _dot_t(a, b):
    return lax.dot_general(a, b, (((1,), (1,)), ((), ())), preferred_element_type=F32)


def _layer_norm(x):
    mu = jnp.mean(x, axis=-1, keepdims=True)
    xc = x - mu
    var = jnp.mean(xc * xc, axis=-1, keepdims=True)
    return xc * lax.rsqrt(var + LN_EPS)


def _params(*sem):
    return pltpu.CompilerParams(dimension_semantics=sem, vmem_limit_bytes=VMEM_LIMIT)


def _mod_kernel(cond_ref, w_ref, b_ref, out_ref):
    cnd = cond_ref[...]
    act = cnd * jax.nn.sigmoid(cnd)
    out_ref[0] = jnp.dot(act, w_ref[0], preferred_element_type=F32,
                         precision=lax.Precision.HIGHEST) + b_ref[0]


def adaln_all(cond, w_mod, b_mod):
    n_layers, d, width = w_mod.shape
    rows = cond.shape[0]
    tn = 1536
    return pl.pallas_call(
        _mod_kernel,
        grid=(n_layers, width // tn),
        in_specs=[
            pl.BlockSpec((rows, d), lambda l, j: (0, 0)),
            pl.BlockSpec((1, d, tn), lambda l, j: (l, 0, j)),
            pl.BlockSpec((1, 1, tn), lambda l, j: (l, 0, j)),
        ],
        out_specs=pl.BlockSpec((1, rows, tn), lambda l, j: (l, 0, j)),
        out_shape=jax.ShapeDtypeStruct((n_layers, rows, width), F32),
        compiler_params=_params("parallel", "parallel"),
        name="adaln",
    )(cond, w_mod, b_mod.reshape(n_layers, 1, width))


def _inproj_kernel(x_ref, sh_ref, sc_ref, w_ref, cos_ref, sin_ref,
                   qa_ref, ka_ref, va_ref, qb_ref, kb_ref, vb_ref, hc_ref, *, rope):
    x = x_ref[0]
    h = _layer_norm(x) * (1.0 + sc_ref[0]) + sh_ref[0]
    u = _dot(h.astype(BF16), w_ref[...])

    def rotated(col):
        xq = u[:, col:col + LANES]
        if not rope:
            return xq
        lane = lax.broadcasted_iota(jnp.int32, xq.shape, 1)
        first = (lane & (HEAD_DIM // 2 - 1)) < (HEAD_DIM // 4)
        partner = jnp.where(first, pltpu.roll(xq, LANES - HEAD_DIM // 4, 1),
                            pltpu.roll(xq, HEAD_DIM // 4, 1))
        return xq * cos_ref[...] + partner * sin_ref[...]

    rot = [rotated(col) for col in range(0, ROPE_WIDTH, LANES)]
    n_q = A_Q_W // LANES
    qa_ref[0] = (jnp.concatenate(rot[:n_q], axis=1) * QK_SCALE).astype(BF16)
    ka_ref[0] = jnp.concatenate(rot[n_q:], axis=1).astype(BF16)
    va_ref[0] = u[:, OFF_AV:OFF_BQ].astype(BF16)
    qb_ref[0] = (u[:, OFF_BQ:OFF_BK] * QK_SCALE).astype(BF16)
    kb_ref[0] = u[:, OFF_BK:OFF_BV].astype(BF16)
    vb_ref[0] = u[:, OFF_BV:OFF_C].astype(BF16)
    a = u[:, OFF_C:OFF_C + C_CHANNELS]
    gate = u[:, OFF_C + C_CHANNELS:]
    hc_ref[0] = a * jax.nn.sigmoid(gate)


def in_projection(x, shift, scale, w_in_bf16, cos_t, sin_t, *, rope):
    bx, t, d = x.shape
    tm = min(512, t)
    widths = (A_Q_W, A_KV_W, A_KV_W, B_W, B_W, B_W, C_CHANNELS)
    dtypes = (BF16,) * 6 + (F32,)
    tok = lambda b, i: (b, i, 0)
    per_b = lambda b, i: (b, 0, 0)
    return pl.pallas_call(
        functools.partial(_inproj_kernel, rope=rope),
        grid=(bx, t // tm),
        in_specs=[
            pl.BlockSpec((1, tm, d), tok),
            pl.BlockSpec((1, 1, d), per_b),
            pl.BlockSpec((1, 1, d), per_b),
            pl.BlockSpec((d, IN_WIDTH), lambda b, i: (0, 0)),
            pl.BlockSpec((tm, LANES), lambda b, i: (i, 0)),
            pl.BlockSpec((tm, LANES), lambda b, i: (i, 0)),
        ],
        out_specs=[pl.BlockSpec((1, tm, w), tok) for w in widths],
        out_shape=[jax.ShapeDtypeStruct((bx, t, w), dt) for w, dt in zip(widths, dtypes)],
        compiler_params=_params("parallel", "parallel"),
        name="in_projection",
    )(x, shift, scale, w_in_bf16, cos_t, sin_t)


def rope_tables(n_tokens):
    t = jnp.arange(n_tokens, dtype=jnp.int32)
    row = (t // GRID_W).astype(F32)[:, None]
    col = (t % GRID_W).astype(F32)[:, None]
    n_freq = HEAD_DIM // 4
    inv_freq = ROPE_BASE ** (-jnp.arange(n_freq, dtype=F32) / n_freq)
    ang_r = row * inv_freq
    ang_c = col * inv_freq
    cos_h = jnp.concatenate([jnp.cos(ang_r), jnp.cos(ang_r), jnp.cos(ang_c), jnp.cos(ang_c)], axis=1)
    sin_h = jnp.concatenate([-jnp.sin(ang_r), jnp.sin(ang_r), -jnp.sin(ang_c), jnp.sin(ang_c)], axis=1)
    reps = LANES // HEAD_DIM
    return jnp.tile(cos_h, (1, reps)), jnp.tile(sin_h, (1, reps))


def _with_ones(v):
    return jnp.concatenate([v, jnp.ones_like(v)], axis=1)


def _attend(score_parts, values, sink=None):
    m = score_parts[0].max(axis=-1, keepdims=True)
    for s in score_parts[1:]:
        m = jnp.maximum(m, s.max(axis=-1, keepdims=True))
    if sink is not None:
        m = jnp.maximum(m, sink)
    acc = None
    for s, v in zip(score_parts, values):
        term = _dot(jnp.exp((s - m).astype(BF16)), v)
        acc = term if acc is None else acc + term
    den = acc[:, HEAD_DIM:HEAD_DIM + 1]
    if sink is not None:
        den = den + jnp.exp(sink - m)
    return acc[:, :HEAD_DIM] / den


def _attn_a_kernel(sink_ref, q_ref, kp_ref, kc_ref, kn_ref, vp_ref, vc_ref, vn_ref,
                   kctx_ref, vctx_ref, out_ref, *, n_lat, tq):
    i = pl.program_id(1)
    k_win = jnp.concatenate([kp_ref[0], kc_ref[0], kn_ref[0]], axis=0)
    v_win = jnp.concatenate([vp_ref[0], vc_ref[0], vn_ref[0]], axis=0)
    kctx = kctx_ref[0]
    vctx = vctx_ref[0]
    sub = A_WINDOW
    span = 3 * A_WINDOW
    rows = A_GROUP * sub
    q_onehot = ((lax.broadcasted_iota(jnp.int32, (rows, sub), 0) & (sub - 1))
                == lax.broadcasted_iota(jnp.int32, (rows, sub), 1)).astype(BF16)
    key_i = lax.broadcasted_iota(jnp.int32, (span, sub), 0)
    qry_i = lax.broadcasted_iota(jnp.int32, (span, sub), 1)
    rel = key_i - A_WINDOW - qry_i
    in_band = (rel <= A_WINDOW) & (rel >= -A_WINDOW)
    group_of_row = lax.broadcasted_iota(jnp.int32, (rows, 1), 0) >> (sub.bit_length() - 1)
    sinks = []
    for hk in range(A_KV_HEADS):
        sink = jnp.zeros((rows, 1), F32)
        for g in range(A_GROUP):
            sink = jnp.where(group_of_row == g, sink_ref[hk * A_GROUP + g], sink)
        sinks.append(sink)
    vctx_ext = [_with_ones(vctx[:, hk * HEAD_DIM:(hk + 1) * HEAD_DIM]) for hk in range(A_KV_HEADS)]
    v_ext = [_with_ones(v_win[:, hk * HEAD_DIM:(hk + 1) * HEAD_DIM]) for hk in range(A_KV_HEADS)]
    def mask_columns(j):
        kpos = i * tq + j * sub - A_WINDOW + key_i
        valid = in_band & (kpos >= 0) & (kpos < n_lat)
        return jnp.where(valid, 0.0, NEG_INF).astype(BF16)

    masks = [mask_columns(j) for j in range(tq // sub)]

    def scores(j, hk):
        sl = slice(hk * HEAD_DIM, (hk + 1) * HEAD_DIM)
        q_rows = q_ref[0, j * sub:(j + 1) * sub]
        q = jnp.concatenate([q_rows[:, h * HEAD_DIM:(h + 1) * HEAD_DIM]
                             for h in range(hk * A_GROUP, (hk + 1) * A_GROUP)], axis=0)
        q_aug = jnp.concatenate([q_onehot, q], axis=1)
        k_aug = jnp.concatenate([masks[j], k_win[j * sub:j * sub + span, sl]], axis=1)
        return [_dot_t(q_aug, k_aug), _dot_t(q, kctx[:, sl])]

    units = [(j, hk) for j in range(tq // sub) for hk in range(A_KV_HEADS)]
    all_scores = [scores(j, hk) for j, hk in units]
    for (j, hk), s in zip(units, all_scores):
        v_sub = v_ext[hk][j * sub:j * sub + span]
        o = _attend(s, [v_sub, vctx_ext[hk]], sink=sinks[hk]).astype(BF16)
        for g in range(A_GROUP):
            h = hk * A_GROUP + g
            out_ref[0, j * sub:(j + 1) * sub, h * HEAD_DIM:(h + 1) * HEAD_DIM] = o[g * sub:(g + 1) * sub]


def window_attention(qa, ka, va, kc_a, vc_a, sink):
    bsz, n_lat, _ = qa.shape
    n_ctx = kc_a.shape[1]
    tq = min(512, n_lat)
    w = A_WINDOW
    per = tq // w
    last = n_lat // w - 1
    prev = lambda b, i, s: (b, jnp.maximum(i * per - 1, 0), 0)
    cur = lambda b, i, s: (b, i, 0)
    nxt = lambda b, i, s: (b, jnp.minimum((i + 1) * per, last), 0)
    ctx = lambda b, i, s: (b, 0, 0)
    kv_specs = [pl.BlockSpec((1, w, A_KV_W), prev), pl.BlockSpec((1, tq, A_KV_W), cur),
                pl.BlockSpec((1, w, A_KV_W), nxt)]
    grid_spec = pltpu.PrefetchScalarGridSpec(
        num_scalar_prefetch=1,
        grid=(bsz, n_lat // tq),
        in_specs=[pl.BlockSpec((1, tq, A_Q_W), cur)] + kv_specs + kv_specs + [
            pl.BlockSpec((1, n_ctx, A_KV_W), ctx), pl.BlockSpec((1, n_ctx, A_KV_W), ctx)],
        out_specs=pl.BlockSpec((1, tq, A_Q_W), cur),
    )
    return pl.pallas_call(
        functools.partial(_attn_a_kernel, n_lat=n_lat, tq=tq),
        grid_spec=grid_spec,
        out_shape=jax.ShapeDtypeStruct((bsz, n_lat, A_Q_W), BF16),
        compiler_params=_params("parallel", "parallel"),
        name="window_attention",
    )(sink, qa, ka, ka, ka, va, va, va, kc_a, vc_a)


NB_Q_ROWS = 4
NB_BLOCKS_PER_STEP = 2


def _attn_b_kernel(q_ref, *refs):
    n_kv = NB_BLOCKS_PER_STEP + 2
    k_refs, v_refs = refs[:n_kv], refs[n_kv:2 * n_kv]
    kctx_ref, vctx_ref = refs[2 * n_kv:2 * n_kv + 2]
    bias_refs = refs[2 * n_kv + 2:-1]
    out_ref = refs[-1]
    tb = k_refs[0].shape[1]
    k_all = jnp.concatenate([r[0] for r in k_refs], axis=0)
    v_all = jnp.concatenate([r[0] for r in v_refs], axis=0)
    kctx = kctx_ref[0]
    vctx = vctx_ref[0]
    heads = [slice(h * HEAD_DIM, (h + 1) * HEAD_DIM) for h in range(B_HEADS)]
    units = [(sb, h) for sb in range(NB_BLOCKS_PER_STEP) for h in range(B_HEADS)]

    def scores(sb, h):
        q = q_ref[0, sb * tb:(sb + 1) * tb, heads[h]]
        return [_dot_t(q, k_all[sb * tb:(sb + 3) * tb, heads[h]]) + bias_refs[sb][0, h],
                _dot_t(q, kctx[:, heads[h]])]

    all_scores = [scores(sb, h) for sb, h in units]
    v_ext = [_with_ones(v_all[:, sl]) for sl in heads]
    vctx_ext = [_with_ones(vctx[:, sl]) for sl in heads]
    for (sb, h), s in zip(units, all_scores):
        o = _attend(s, [v_ext[h][sb * tb:(sb + 3) * tb], vctx_ext[h]])
        out_ref[0, sb * tb:(sb + 1) * tb, heads[h]] = o.astype(BF16)


def neighbourhood_bias(rel_bias, n_lat):
    rows = n_lat // GRID_W
    kr_n = min(NA_ROWS, rows)
    n_blocks = rows // NB_Q_ROWS
    n_heads, n_dr, n_dc = rel_bias.shape
    cols = np.arange(GRID_W)
    c_start = np.clip(cols - NA_COLS // 2, 0, GRID_W - NA_COLS)
    col_ok = (cols[None, :] >= c_start[:, None]) & (cols[None, :] < c_start[:, None] + NA_COLS)
    dc = np.clip(cols[None, :] - cols[:, None], -(NA_COLS - 1), NA_COLS - 1) + NA_COLS - 1
    pick_dc = (dc.reshape(-1)[None, :] == np.arange(n_dc)[:, None]).astype(np.float32)
    toeplitz = jnp.dot(rel_bias.reshape(n_heads * n_dr, n_dc), pick_dc, precision=lax.Precision.HIGHEST)
    toeplitz = jnp.where(col_ok.reshape(-1), toeplitz, NEG_INF).reshape(n_heads, n_dr, GRID_W, GRID_W)
    q_rl = np.arange(NB_Q_ROWS)
    k_rl = np.arange(3 * NB_Q_ROWS)
    row_ok, dr = [], []
    for j in sorted({0, min(1, n_blocks - 1), n_blocks - 1}):
        r = NB_Q_ROWS * j + q_rl
        kr = NB_Q_ROWS * (j - 1) + k_rl
        r_start = np.clip(r - kr_n // 2, 0, rows - kr_n)
        ok = (kr[None, :] >= r_start[:, None]) & (kr[None, :] < r_start[:, None] + kr_n)
        row_ok.append(ok & (kr[None, :] >= 0) & (kr[None, :] < rows))
        dr.append(np.clip(kr[None, :] - r[:, None] + NA_ROWS - 1, 0, n_dr - 1))
    row_ok = np.stack(row_ok)
    dr = np.stack(dr)
    tiles = jnp.stack([toeplitz[:, int(i)] for i in dr.reshape(-1)], axis=1)
    tiles = tiles.reshape((n_heads,) + dr.shape + (GRID_W, GRID_W))
    tiles = jnp.where(row_ok[None, :, :, :, None, None], tiles, NEG_INF)
    table = tiles.transpose(1, 0, 2, 4, 3, 5).reshape(
        dr.shape[0], n_heads, NB_Q_ROWS * GRID_W, 3 * NB_Q_ROWS * GRID_W)
    return table, n_blocks


def neighbourhood_attention(qb, kb, vb, kc_b, vc_b, rel_bias):
    bsz, n_lat, _ = qb.shape
    n_ctx = kc_b.shape[1]
    table, n_blocks = neighbourhood_bias(rel_bias, n_lat)
    assert n_blocks % NB_BLOCKS_PER_STEP == 0
    n_var = table.shape[0]
    tb = NB_Q_ROWS * GRID_W
    last = n_blocks - 1
    cur = lambda b, j: (b, j, 0)
    ctx = lambda b, j: (b, 0, 0)

    def key_block(off):
        return lambda b, j: (b, jnp.clip(NB_BLOCKS_PER_STEP * j + off, 0, last), 0)

    def variant(sb):
        def index(b, j):
            g = NB_BLOCKS_PER_STEP * j + sb
            return (jnp.where(g == 0, 0, jnp.where(g == last, n_var - 1, min(1, n_var - 1))), 0, 0, 0)
        return index

    kv_specs = [pl.BlockSpec((1, tb, B_W), key_block(off)) for off in range(-1, NB_BLOCKS_PER_STEP + 1)]
    return pl.pallas_call(
        _attn_b_kernel,
        grid=(bsz, n_blocks // NB_BLOCKS_PER_STEP),
        in_specs=[pl.BlockSpec((1, NB_BLOCKS_PER_STEP * tb, B_W), cur)] + kv_specs + kv_specs + [
            pl.BlockSpec((1, n_ctx, B_W), ctx), pl.BlockSpec((1, n_ctx, B_W), ctx)] + [
            pl.BlockSpec((1, B_HEADS, tb, 3 * tb), variant(sb)) for sb in range(NB_BLOCKS_PER_STEP)],
        out_specs=pl.BlockSpec((1, NB_BLOCKS_PER_STEP * tb, B_W), cur),
        out_shape=jax.ShapeDtypeStruct((bsz, n_lat, B_W), BF16),
        compiler_params=_params("parallel", "parallel"),
        name="neighbourhood_attention",
    )(qb, *([kb] * len(kv_specs)), *([vb] * len(kv_specs)), kc_b, vc_b, *([table] * NB_BLOCKS_PER_STEP))


def _ctx_attn_kernel(sink_ref, qa_ref, qb_ref, ka_ref, va_ref, kb_ref, vb_ref, oa_ref, ob_ref):
    qa, qb = qa_ref[0], qb_ref[0]
    ka, va, kb, vb = ka_ref[0], va_ref[0], kb_ref[0], vb_ref[0]
    for hq in range(A_Q_HEADS):
        sl = slice(hq * HEAD_DIM, (hq + 1) * HEAD_DIM)
        hk = hq // A_GROUP
        kv = slice(hk * HEAD_DIM, (hk + 1) * HEAD_DIM)
        q = qa[:, sl]
        o = _attend([_dot_t(q, ka[:, kv])], [_with_ones(va[:, kv])], sink=sink_ref[hq])
        oa_ref[0, :, sl] = o.astype(BF16)
    for h in range(B_HEADS):
        sl = slice(h * HEAD_DIM, (h + 1) * HEAD_DIM)
        q = qb[:, sl]
        o = _attend([_dot_t(q, kb[:, sl])], [_with_ones(vb[:, sl])])
        ob_ref[0, :, sl] = o.astype(BF16)


def context_attention(qa, qb, ka, va, kb, vb, sink):
    bsz, n_ctx, _ = qa.shape
    blk = lambda w: pl.BlockSpec((1, n_ctx, w), lambda b, s: (b, 0, 0))
    grid_spec = pltpu.PrefetchScalarGridSpec(
        num_scalar_prefetch=1,
        grid=(bsz,),
        in_specs=[blk(A_Q_W), blk(B_W), blk(A_KV_W), blk(A_KV_W), blk(B_W), blk(B_W)],
        out_specs=[blk(A_Q_W), blk(B_W)],
    )
    return pl.pallas_call(
        _ctx_attn_kernel,
        grid_spec=grid_spec,
        out_shape=[jax.ShapeDtypeStruct((bsz, n_ctx, A_Q_W), BF16),
                   jax.ShapeDtypeStruct((bsz, n_ctx, B_W), BF16)],
        compiler_params=_params("parallel"),
        name="context_attention",
    )(sink, qa, qb, ka, va, kb, vb)


CONV_HALO = 16
F32_SUBLANES = 8
CONV_SHIFT_SPAN = (CONV_HALO + C_CONV_WIDTH // 2) // F32_SUBLANES * F32_SUBLANES


def _conv_kernel(prev_ref, cur_ref, next_ref, w_ref, b_ref, g_ref, beta_ref, out_ref, shifted_ref, *, ts):
    i = pl.program_id(1)
    n_i = pl.num_programs(1)
    ext = jnp.concatenate([jnp.where(i > 0, prev_ref[0], 0.0), cur_ref[0],
                           jnp.where(i < n_i - 1, next_ref[0], 0.0)], axis=0)
    for r in range(F32_SUBLANES):
        shifted_ref[r] = ext[r:r + ts + CONV_SHIFT_SPAN]
    acc = jnp.zeros((ts, C_CHANNELS), F32)
    for k in range(C_CONV_WIDTH):
        start = CONV_HALO - C_CONV_WIDTH // 2 + k
        aligned = start - start % F32_SUBLANES
        acc = acc + shifted_ref[start % F32_SUBLANES, aligned:aligned + ts] * w_ref[k:k + 1]
    y = _layer_norm(acc + b_ref[...]) * g_ref[...] + beta_ref[...]
    out_ref[0] = (y * jax.nn.sigmoid(y)).astype(BF16)


def conformer_conv(hc, conv_w, conv_b, ln_g, ln_b):
    bx, t, ch = hc.shape
    ts = min(512, t)
    per = ts // CONV_HALO
    last = t // CONV_HALO - 1
    row = lambda v: v.reshape(1, ch)
    const = lambda b, i: (0, 0)
    return pl.pallas_call(
        functools.partial(_conv_kernel, ts=ts),
        grid=(bx, t // ts),
        in_specs=[
            pl.BlockSpec((1, CONV_HALO, ch), lambda b, i: (b, jnp.maximum(i * per - 1, 0), 0)),
            pl.BlockSpec((1, ts, ch), lambda b, i: (b, i, 0)),
            pl.BlockSpec((1, CONV_HALO, ch), lambda b, i: (b, jnp.minimum((i + 1) * per, last), 0)),
            pl.BlockSpec((C_CONV_WIDTH, ch), const),
            pl.BlockSpec((1, ch), const), pl.BlockSpec((1, ch), const), pl.BlockSpec((1, ch), const),
        ],
        out_specs=pl.BlockSpec((1, ts, ch), lambda b, i: (b, i, 0)),
        out_shape=jax.ShapeDtypeStruct((bx, t, ch), BF16),
        scratch_shapes=[pltpu.VMEM((F32_SUBLANES, ts + CONV_SHIFT_SPAN, ch), F32)],
        compiler_params=_params("parallel", "parallel"),
        name="conformer_conv",
    )(hc, hc, hc, conv_w, row(conv_b), row(ln_g), row(ln_b))


OUTPROJ_ROW_CHUNKS = 4


def _outproj_kernel(oa_ref, ob_ref, oc_ref, x_ref, g1_ref, sh_ref, sc_ref, w_ref, lng_ref, lnb_ref,
                    wr_hi_ref, wr_lo_ref, x1_ref, h2_ref, afft_ref, aff_ref, *, alpha):
    tm, d = x_ref.shape[1], x_ref.shape[2]
    rc = tm // OUTPROJ_ROW_CHUNKS
    chunks = [slice(k * rc, (k + 1) * rc) for k in range(OUTPROJ_ROW_CHUNKS)]
    outs = [(_dot(oa_ref[0, r], w_ref[0:A_Q_W])
             + _dot(ob_ref[0, r], w_ref[A_Q_W:A_Q_W + B_W])
             + _dot(oc_ref[0, r], w_ref[A_Q_W + B_W:])) for r in chunks]
    w_hi, w_lo = wr_hi_ref[...], wr_lo_ref[...]
    eye = (lax.broadcasted_iota(jnp.int32, (N_EXPERTS, N_EXPERTS), 0)
           == lax.broadcasted_iota(jnp.int32, (N_EXPERTS, N_EXPERTS), 1)).astype(BF16)
    rows_per_token = d // LANES
    ys = [_layer_norm(alpha * x_ref[0, r] + g1_ref[0] * o) * lng_ref[...] + lnb_ref[...]
          for r, o in zip(chunks, outs)]
    for r, y in zip(chunks, ys):
        x1_ref[0, r] = y
    h2s = [_layer_norm(y) * (1.0 + sc_ref[0]) + sh_ref[0] for y in ys]
    his = [h2.astype(BF16) for h2 in h2s]
    los = [(h2 - hi.astype(F32)).astype(BF16) for h2, hi in zip(h2s, his)]
    all_logits = [_dot_t(hi, w_hi) + _dot_t(lo, w_hi) + _dot_t(hi, w_lo) for hi, lo in zip(his, los)]
    for k, h2 in enumerate(h2s):
        for j in range(rows_per_token):
            h2_ref[0, pl.ds(k * rc * rows_per_token + j, rc, stride=rows_per_token), :] = (
                h2[:, j * LANES:(j + 1) * LANES])
    for r, logits in zip(chunks, all_logits):
        e_n = jnp.exp(logits - logits.max(axis=1, keepdims=True))
        aff = e_n / e_n.sum(axis=1, keepdims=True)
        aff_ref[0, r] = aff
        aff_t, rest = None, aff
        for _ in range(GATE_PARTS):
            part = rest.astype(BF16)
            term = _dot_t(eye, part)
            aff_t = term if aff_t is None else aff_t + term
            rest = rest - part.astype(F32)
        afft_ref[0, :, r] = aff_t


def out_projection(oa, ob, oc, x, g1, sh2, sc2, w_out_bf16, ln_g, ln_b, wr_hi, wr_lo, alpha):
    bx, t, d = x.shape
    tm = min(512, t)
    tok = lambda b, i: (b, i, 0)
    per_b = lambda b, i: (b, 0, 0)
    const = lambda b, i: (0, 0)
    vec = pl.BlockSpec((1, d), const)
    return pl.pallas_call(
        functools.partial(_outproj_kernel, alpha=alpha),
        grid=(bx, t // tm),
        in_specs=[
            pl.BlockSpec((1, tm, A_Q_W), tok), pl.BlockSpec((1, tm, B_W), tok),
            pl.BlockSpec((1, tm, C_CHANNELS), tok), pl.BlockSpec((1, tm, d), tok),
            pl.BlockSpec((1, 1, d), per_b), pl.BlockSpec((1, 1, d), per_b), pl.BlockSpec((1, 1, d), per_b),
            pl.BlockSpec(w_out_bf16.shape, const), vec, vec,
            pl.BlockSpec((N_EXPERTS, d), const), pl.BlockSpec((N_EXPERTS, d), const),
        ],
        out_specs=[pl.BlockSpec((1, tm, d), tok), pl.BlockSpec((1, tm * (d // LANES), LANES), tok),
                   pl.BlockSpec((1, N_EXPERTS, tm), lambda b, i: (b, 0, i)),
                   pl.BlockSpec((1, tm, N_EXPERTS), tok)],
        out_shape=[jax.ShapeDtypeStruct((bx, t, d), F32),
                   jax.ShapeDtypeStruct((bx, t * (d // LANES), LANES), F32),
                   jax.ShapeDtypeStruct((bx, N_EXPERTS, t), F32),
                   jax.ShapeDtypeStruct((bx, t, N_EXPERTS), F32)],
        compiler_params=_params("parallel", "parallel"),
        name="out_projection",
    )(oa, ob, oc, x, g1, sh2, sc2, w_out_bf16, ln_g.reshape(1, d), ln_b.reshape(1, d), wr_hi, wr_lo)


def _select_kernel(afft_ref, pos_ref, off_ref, *, cap, n_tok):
    aff = afft_ref[0]

    def bit_step(j, bits):
        cand = bits | (jnp.int32(1) << (30 - j))
        cnt = jnp.sum((aff >= pltpu.bitcast(cand, F32)).astype(jnp.int32), axis=1, keepdims=True)
        return jnp.where(cnt >= cap, cand, bits)

    thr = pltpu.bitcast(lax.fori_loop(0, 31, bit_step, jnp.zeros((N_EXPERTS, 1), jnp.int32)), F32)
    above = (aff > thr).astype(F32)
    tied = (aff == thr).astype(F32)
    need = cap - jnp.sum(above, axis=1, keepdims=True)

    blk = MXU_DEPTH
    blocks = [slice(k * blk, (k + 1) * blk) for k in range(n_tok // blk)]
    r_i = lax.broadcasted_iota(jnp.int32, (blk, blk), 0)
    c_i = lax.broadcasted_iota(jnp.int32, (blk, blk), 1)
    strict_upper = (r_i < c_i).astype(BF16)

    def running(block_sums):
        run, total = [], jnp.zeros((N_EXPERTS, 1), F32)
        for s in block_sums:
            run.append(total)
            total = total + s
        return run

    tied_b = [tied[:, sl] for sl in blocks]
    tied_rank = [_dot(t.astype(BF16), strict_upper) for t in tied_b]
    tied_before = running([t.sum(axis=1, keepdims=True) for t in tied_b])
    sel_b = [above[:, sl] + t * ((before + rank) < need).astype(F32)
             for sl, t, before, rank in zip(blocks, tied_b, tied_before, tied_rank)]
    sel_rank = [_dot(s.astype(BF16), strict_upper) for s in sel_b]
    half_sums = [[s[:, h * OFFSET_BLOCK:(h + 1) * OFFSET_BLOCK].sum(axis=1, keepdims=True)
                  for h in range(blk // OFFSET_BLOCK)] for s in sel_b]
    offs = running([h for hs in half_sums for h in hs])
    per_blk = blk // OFFSET_BLOCK
    for k, (sl, s, rank) in enumerate(zip(blocks, sel_b, sel_rank)):
        pos_ref[0, :, sl] = jnp.where(s > 0.5, (offs[k * per_blk] + rank).astype(jnp.int32), -1)
    off_ref[0] = jnp.concatenate(offs, axis=1).astype(jnp.int32)


def expert_choice_select(aff_t, cap):
    bx, n_e, t = aff_t.shape
    n_tb = t // OFFSET_BLOCK
    return pl.pallas_call(
        functools.partial(_select_kernel, cap=cap, n_tok=t),
        grid=(bx,),
        in_specs=[pl.BlockSpec((1, n_e, t), lambda b: (b, 0, 0))],
        out_specs=[pl.BlockSpec((1, n_e, t), lambda b: (b, 0, 0)),
                   pl.BlockSpec((1, n_e, n_tb), lambda b: (b, 0, 0))],
        out_shape=[jax.ShapeDtypeStruct((bx, n_e, t), jnp.int32),
                   jax.ShapeDtypeStruct((bx, n_e, n_tb), jnp.int32)],
        compiler_params=_params("parallel"),
        name="expert_choice_select",
    )(aff_t)


GATE_PARTS = 3
TOKEN_LANE = GATE_PARTS * N_EXPERTS


def _slot_table_kernel(off_ref, pos_ref, aff_ref, tbl_ref, *, n_off, blocks_per_step):
    b, eg, kc = pl.program_id(0), pl.program_id(1), pl.program_id(2)
    epg = EXPERTS_PER_GATHER_STEP

    @pl.when(kc == 0)
    def _():
        tbl_ref[...] = jnp.zeros_like(tbl_ref)

    slot = lax.broadcasted_iota(jnp.int32, (GATHER_WINDOW, 1), 0)
    slot_2d = lax.broadcasted_iota(jnp.int32, (GATHER_WINDOW, GATHER_BLOCK), 0)
    lane = lax.broadcasted_iota(jnp.int32, (GATHER_BLOCK, LANES), 1)
    out_lane = lax.broadcasted_iota(jnp.int32, (GATHER_WINDOW, LANES), 1)
    local_token = lax.broadcasted_iota(jnp.int32, (GATHER_BLOCK, LANES), 0).astype(F32)
    place_r = lax.broadcasted_iota(jnp.int32, (N_EXPERTS, LANES), 0)
    place_c = lax.broadcasted_iota(jnp.int32, (N_EXPERTS, LANES), 1)
    def payload_of(kk):
        payload = jnp.where(lane == TOKEN_LANE, local_token, 0.0)
        rest = aff_ref[0, kk * GATHER_BLOCK:(kk + 1) * GATHER_BLOCK]
        for k in range(GATE_PARTS):
            part = rest.astype(BF16)
            payload = payload + _dot(part, (place_c == place_r + k * N_EXPERTS).astype(BF16))
            rest = rest - part.astype(F32)
        return payload.astype(BF16)

    payloads = [payload_of(kk) for kk in range(blocks_per_step)]
    pending = []
    for kk in range(blocks_per_step):
        kb = kc * blocks_per_step + kk
        first_token = jnp.where(out_lane == TOKEN_LANE, (kb * GATHER_BLOCK).astype(F32), 0.0)
        for ee in range(epg):
            off = off_ref[(b * N_EXPERTS + eg * epg + ee) * n_off + kb * (GATHER_BLOCK // OFFSET_BLOCK)]
            base = pl.multiple_of((off >> 4) << 4, WINDOW_ALIGN)
            onehot = (pos_ref[0, ee, 0, kk:kk + 1, :] - base == slot_2d).astype(BF16)
            pending.append((ee, off, base, _dot(onehot, payloads[kk]) + first_token))
    for ee, off, base, gathered in pending:
        win = pl.ds(base, GATHER_WINDOW)
        tbl_ref[0, ee, win, :] = jnp.where(slot >= off - base, gathered, tbl_ref[0, ee, win, :])


def slot_table(aff, pos, block_off, cap_pad):
    bx, t, _ = aff.shape
    n_tb = t // GATHER_BLOCK
    blocks_per_step = min(4, n_tb)
    n_steps = n_tb // blocks_per_step
    epg = EXPERTS_PER_GATHER_STEP
    tokens = blocks_per_step * GATHER_BLOCK
    pos5 = pos.reshape(bx, N_EXPERTS, n_steps, blocks_per_step, GATHER_BLOCK)
    grid_spec = pltpu.PrefetchScalarGridSpec(
        num_scalar_prefetch=1,
        grid=(bx, N_EXPERTS // epg, n_steps),
        in_specs=[
            pl.BlockSpec((1, epg, 1, blocks_per_step, GATHER_BLOCK), lambda b, g, k, s: (b, g, k, 0, 0)),
            pl.BlockSpec((1, tokens, N_EXPERTS), lambda b, g, k, s: (b, k, 0)),
        ],
        out_specs=pl.BlockSpec((1, epg, cap_pad, LANES), lambda b, g, k, s: (b, g, 0, 0)),
    )
    return pl.pallas_call(
        functools.partial(_slot_table_kernel, n_off=t // OFFSET_BLOCK, blocks_per_step=blocks_per_step),
        grid_spec=grid_spec,
        out_shape=jax.ShapeDtypeStruct((bx, N_EXPERTS, cap_pad, LANES), F32),
        compiler_params=_params("parallel", "parallel", "arbitrary"),
        name="slot_table",
    )(block_off.reshape(-1), pos5, aff)


GATHER_UNROLL = 16


def _row_gather_kernel(idx_ref, src_ref, xs_ref, tile_ref, *, cap, rows_per_token, chunk_stride):
    b, e = pl.program_id(0), pl.program_id(1)
    first = (b * N_EXPERTS + e) * cap

    def group(g, carry):
        for u in range(GATHER_UNROLL):
            s = g * GATHER_UNROLL + u
            row = pl.multiple_of(idx_ref[first + s], rows_per_token)
            tile_ref[pl.ds(s, rows_per_token, stride=chunk_stride), :] = src_ref[0, pl.ds(row, rows_per_token), :]
        return carry

    lax.fori_loop(0, cap // GATHER_UNROLL, group, 0)
    for j in range(rows_per_token):
        xs_ref[0, 0, :, j * LANES:(j + 1) * LANES] = tile_ref[j * chunk_stride:j * chunk_stride + cap, :].astype(BF16)


def gather_rows(h2_rows, token_row, cap, d):
    bx, n_rows, _ = h2_rows.shape
    rows_per_token = d // LANES
    chunk_stride = cap + 8
    grid_spec = pltpu.PrefetchScalarGridSpec(
        num_scalar_prefetch=1,
        grid=(bx, N_EXPERTS),
        in_specs=[pl.BlockSpec((1, n_rows, LANES), lambda b, e, s: (b, 0, 0), pipeline_mode=pl.Buffered(1))],
        out_specs=pl.BlockSpec((1, 1, cap, d), lambda b, e, s: (b, e, 0, 0)),
        scratch_shapes=[pltpu.VMEM((rows_per_token * chunk_stride, LANES), F32)],
    )
    return pl.pallas_call(
        functools.partial(_row_gather_kernel, cap=cap, rows_per_token=rows_per_token, chunk_stride=chunk_stride),
        grid_spec=grid_spec,
        out_shape=jax.ShapeDtypeStruct((bx, N_EXPERTS, cap, d), BF16),
        compiler_params=_params("parallel", "arbitrary"),
        name="gather_rows",
    )(token_row, h2_rows)


def _ffn_kernel(xs_ref, gs_ref, wg_ref, wu_ref, wd_ref, ye_ref, wg_bf, wu_bf, wd_bf, *, cap, row_tile):
    @pl.when(pl.program_id(1) == 0)
    def _():
        wg_bf[...] = wg_ref[0, 0].astype(BF16)
        wu_bf[...] = wu_ref[0, 0].astype(BF16)
        wd_bf[...] = wd_ref[0, 0].astype(BF16)

    n_b = xs_ref.shape[0]
    if n_b == 1:
        tiles = [[(0, r0, row_tile)] for r0 in range(0, cap, row_tile)]
    else:
        tiles = [[(bb, 0, cap) for bb in range(n_b)]]
    for tile in tiles:
        x = jnp.concatenate([xs_ref[bb, 0, r0:r0 + n] for bb, r0, n in tile], axis=0)
        terms = jnp.concatenate([gs_ref[bb, 0, r0:r0 + n, :] for bb, r0, n in tile], axis=0)
        lane = lax.broadcasted_iota(jnp.int32, terms.shape, 1)
        own = ((lane & (N_EXPERTS - 1)) == pl.program_id(0)) & (lane < TOKEN_LANE)
        terms = jnp.where(own, terms, 0.0)
        gate = _dot(x, wg_bf[...])
        up = _dot(x, wu_bf[...])
        hid = (gate * jax.nn.sigmoid(gate) * up).astype(BF16)
        g = jnp.sum(terms, axis=1, keepdims=True)
        ye = _dot(hid, wd_bf[...]) * g
        rows_per_slot = ye.shape[1] // LANES
        row = 0
        for bb, r0, n in tile:
            for j in range(rows_per_slot):
                ye_ref[bb, 0, pl.ds(r0 * rows_per_slot + j, n, stride=rows_per_slot), :] = (
                    ye[row:row + n, j * LANES:(j + 1) * LANES])
            row += n


def expert_ffn(xs, gs, wg, wu, wd, layer):
    bx, n_e, cap, d = xs.shape
    ff = wg.shape[-1]
    slot_rows = cap * (d // LANES)
    max_rows = 512
    row_tile = min(max_rows, cap)
    n_b = bx if bx * cap <= max_rows else 1
    return pl.pallas_call(
        functools.partial(_ffn_kernel, cap=cap, row_tile=row_tile),
        grid=(n_e, bx // n_b),
        in_specs=[
            pl.BlockSpec((n_b, 1, cap, d), lambda e, b: (b, e, 0, 0)),
            pl.BlockSpec((n_b, 1, cap, LANES), lambda e, b: (b, e, 0, 0)),
            pl.BlockSpec((1, 1, d, ff), lambda e, b: (layer, e, 0, 0)),
            pl.BlockSpec((1, 1, d, ff), lambda e, b: (layer, e, 0, 0)),
            pl.BlockSpec((1, 1, ff, d), lambda e, b: (layer, e, 0, 0)),
        ],
        out_specs=pl.BlockSpec((n_b, 1, slot_rows, LANES), lambda e, b: (b, e, 0, 0)),
        out_shape=jax.ShapeDtypeStruct((bx, n_e, slot_rows, LANES), F32),
        scratch_shapes=[pltpu.VMEM((d, ff), BF16), pltpu.VMEM((d, ff), BF16), pltpu.VMEM((ff, d), BF16)],
        compiler_params=_params("parallel", "arbitrary"),
        name="expert_ffn",
    )(xs, gs, wg, wu, wd)


SCATTER_UNROLL = 16


def _scatter_norm_kernel(idx_ref, ye_ref, x1_ref, g2_ref, lng_ref, lnb_ref, out_ref, acc_ref, *,
                         cap, rows_per_token, alpha):
    b, step = pl.program_id(0), pl.program_id(1)

    @pl.when(step == 0)
    def _():
        acc_ref[...] = jnp.zeros_like(acc_ref)

    @pl.when(step < N_EXPERTS)
    def _():
        first = (b * N_EXPERTS + step) * cap

        def group(g, carry):
            rows, sums = [], []
            for u in range(SCATTER_UNROLL):
                s = g * SCATTER_UNROLL + u
                row = pl.multiple_of(idx_ref[first + s], rows_per_token)
                src = pl.multiple_of(s * rows_per_token, rows_per_token)
                rows.append(row)
                sums.append(acc_ref[pl.ds(row, rows_per_token), :] + ye_ref[0, 0, pl.ds(src, rows_per_token), :])
            for row, total in zip(rows, sums):
                acc_ref[pl.ds(row, rows_per_token), :] = total
            return carry

        lax.fori_loop(0, cap // SCATTER_UNROLL, group, 0)

    @pl.when(step >= N_EXPERTS)
    def _():
        tm = x1_ref.shape[1]
        tile_row = pl.multiple_of((step - N_EXPERTS) * (tm * rows_per_token), rows_per_token)
        moe = jnp.concatenate([acc_ref[pl.ds(tile_row + j, tm, stride=rows_per_token), :]
                               for j in range(rows_per_token)], axis=1)
        y = alpha * x1_ref[0] + g2_ref[0] * moe
        out_ref[0] = _layer_norm(y) * lng_ref[...] + lnb_ref[...]


def scatter_and_norm(ye_rows, token_row, x1, g2, ln_g, ln_b, alpha, cap):
    bx, t, d = x1.shape
    n_e, slot_rows = ye_rows.shape[1], ye_rows.shape[2]
    rows_per_token = d // LANES
    tm = min(512, t)
    tile = lambda b, i, s: (b, jnp.maximum(i - n_e, 0), 0)
    const = lambda b, i, s: (0, 0)
    grid_spec = pltpu.PrefetchScalarGridSpec(
        num_scalar_prefetch=1,
        grid=(bx, n_e + t // tm),
        in_specs=[
            pl.BlockSpec((1, 1, slot_rows, LANES), lambda b, i, s: (b, jnp.minimum(i, n_e - 1), 0, 0)),
            pl.BlockSpec((1, tm, d), tile),
            pl.BlockSpec((1, 1, d), lambda b, i, s: (b, 0, 0)),
            pl.BlockSpec((1, d), const), pl.BlockSpec((1, d), const),
        ],
        out_specs=pl.BlockSpec((1, tm, d), tile),
        scratch_shapes=[pltpu.VMEM((t * rows_per_token, LANES), F32)],
    )
    return pl.pallas_call(
        functools.partial(_scatter_norm_kernel, cap=cap, rows_per_token=rows_per_token, alpha=alpha),
        grid_spec=grid_spec,
        out_shape=jax.ShapeDtypeStruct((bx, t, d), F32),
        compiler_params=_params("parallel", "arbitrary"),
        name="scatter_and_norm",
    )(token_row, ye_rows, x1, g2, ln_g.reshape(1, d), ln_b.reshape(1, d))


def _split_bf16(w):
    hi = w.astype(BF16)
    return hi, (w - hi.astype(F32)).astype(BF16)


def _mixer_tail(oa, ob, oc, x, mod, lw, alpha):
    g1, sh2, sc2, g2 = mod
    t = x.shape[1]
    cap = EC_CAPACITY * t // N_EXPERTS
    cap_pad = cap + SLOT_PAD
    x1, h2_rows, aff_t, aff = out_projection(oa, ob, oc, x, g1, sh2, sc2, lw["w_out"], lw["ln1_g"], lw["ln1_b"],
                                        lw["wr_hi"], lw["wr_lo"], alpha)
    pos, block_off = expert_choice_select(aff_t, cap)
    table = slot_table(aff, pos, block_off, cap_pad)
    token_row = (table[:, :, :cap, TOKEN_LANE].astype(jnp.int32) * (x.shape[2] // LANES)).reshape(-1)
    xs = gather_rows(h2_rows, token_row, cap, x.shape[2])
    ye_rows = expert_ffn(xs, table, lw["w_gate"], lw["w_up"], lw["w_down"], lw["layer"])
    return scatter_and_norm(ye_rows, token_row, x1, g2, lw["ln2_g"], lw["ln2_b"], alpha, cap)


def kernel(x, c, ctx, c_ctx, w_mod, b_mod, w_in, a_sink, nat_bias, conv_w, conv_b, conv_ln_g, conv_ln_b,
           w_out, ln1_g, ln1_b, w_router, w_gate, w_up, w_down, ln2_g, ln2_b):
    bsz, n_lat, d = x.shape
    depth = w_mod.shape[0]
    alpha = (2 * depth) ** 0.25
    cos_t, sin_t = rope_tables(n_lat)

    cond = jnp.concatenate([c, c_ctx[None, :], jnp.zeros((8 - bsz - 1, d), F32)], axis=0)
    mods = adaln_all(cond, w_mod, b_mod)

    for l in range(depth):
        last = l == depth - 1
        wr_hi, wr_lo = _split_bf16(w_router[l].T)
        lw = dict(w_out=w_out[l].astype(BF16), ln1_g=ln1_g[l], ln1_b=ln1_b[l], wr_hi=wr_hi, wr_lo=wr_lo,
                  w_gate=w_gate, w_up=w_up, w_down=w_down, layer=l,
                  ln2_g=ln2_g[l], ln2_b=ln2_b[l])
        w_in_l = w_in[l].astype(BF16)
        lat = [mods[l, :bsz, k * d:(k + 1) * d][:, None, :] for k in range(N_MOD)]
        cm = [jnp.broadcast_to(mods[l, bsz, k * d:(k + 1) * d][None, None, :], (bsz, 1, d))
              for k in range(N_MOD)]
        conv_args = (conv_w[l], conv_b[l], conv_ln_g[l], conv_ln_b[l])

        qa_c, ka_c, va_c, qb_c, kb_c, vb_c, hc_c = in_projection(ctx, cm[0], cm[1], w_in_l, cos_t, sin_t, rope=False)
        if not last:
            oa_c, ob_c = context_attention(qa_c, qb_c, ka_c, va_c, kb_c, vb_c, a_sink[l])
            oc_c = conformer_conv(hc_c, *conv_args)
            ctx_new = _mixer_tail(oa_c, ob_c, oc_c, ctx, (cm[2], cm[3], cm[4], cm[5]), lw, alpha)

        qa, ka, va, qb, kb, vb, hc = in_projection(x, lat[0], lat[1], w_in_l, cos_t, sin_t, rope=True)
        oa = window_attention(qa, ka, va, ka_c, va_c, a_sink[l])
        ob = neighbourhood_attention(qb, kb, vb, kb_c, vb_c, nat_bias[l])
        oc = conformer_conv(hc, *conv_args)
        x = _mixer_tail(oa, ob, oc, x, (lat[2], lat[3], lat[4], lat[5]), lw, alpha)
        if not last:
            ctx = ctx_new
    return x
```

```python
import functools

import numpy as np
import jax
import jax.numpy as jnp
from jax import lax
from jax.experimental import pallas as pl
from jax.experimental.pallas import tpu as pltpu

HEAD_DIM = 64
GRID_W = 64
A_Q_HEADS = 8
A_KV_HEADS = 2
A_GROUP = A_Q_HEADS // A_KV_HEADS
A_WINDOW = 128
B_HEADS = 4
NA_ROWS = 8
NA_COLS = 16
C_CHANNELS = 256
C_CONV_WIDTH = 31
A_Q_W = A_Q_HEADS * HEAD_DIM
A_KV_W = A_KV_HEADS * HEAD_DIM
B_W = B_HEADS * HEAD_DIM
OFF_AK = A_Q_W
OFF_AV = OFF_AK + A_KV_W
OFF_BQ = OFF_AV + A_KV_W
OFF_BK = OFF_BQ + B_W
OFF_BV = OFF_BK + B_W
OFF_C = OFF_BV + B_W
IN_WIDTH = OFF_C + 2 * C_CHANNELS
ROPE_WIDTH = A_Q_W + A_KV_W
N_EXPERTS = 16
EC_CAPACITY = 2
ROPE_BASE = 10000.0
LN_EPS = 1e-6
N_MOD = 6
NEG_INF = -1e30
QK_SCALE = HEAD_DIM ** -0.5

LANES = 128
WINDOW_ALIGN = 16
MXU_DEPTH = 256
OFFSET_BLOCK = 128
GATHER_BLOCK = MXU_DEPTH
GATHER_WINDOW = GATHER_BLOCK + WINDOW_ALIGN
SLOT_PAD = 3 * LANES
EXPERTS_PER_GATHER_STEP = 16
VMEM_LIMIT = 56 * 1024 * 1024

F32 = jnp.float32
BF16 = jnp.bfloat16


def _dot(a, b):
    return jnp.dot(a, b, preferred_element_type=F32)


def _dot_t(a, b):
    return lax.dot_general(a, b, (((1,), (1,)), ((), ())), preferred_element_type=F32)


def _layer_norm(x):
    mu = jnp.mean(x, axis=-1, keepdims=True)
    xc = x - mu
    var = jnp.mean(xc * xc, axis=-1, keepdims=True)
    return xc * lax.rsqrt(var + LN_EPS)


def _params(*sem):
    return pltpu.CompilerParams(dimension_semantics=sem, vmem_limit_bytes=VMEM_LIMIT)


def _mod_kernel(cond_ref, w_ref, b_ref, out_ref):
    cnd = cond_ref[...]
    act = cnd * jax.nn.sigmoid(cnd)
    out_ref[0] = jnp.dot(act, w_ref[0], preferred_element_type=F32,
                         precision=lax.Precision.HIGHEST) + b_ref[0]


def adaln_all(cond, w_mod, b_mod):
    n_layers, d, width = w_mod.shape
    rows = cond.shape[0]
    tn = 1536
    return pl.pallas_call(
        _mod_kernel,
        grid=(n_layers, width // tn),
        in_specs=[
            pl.BlockSpec((rows, d), lambda l, j: (0, 0)),
            pl.BlockSpec((1, d, tn), lambda l, j: (l, 0, j)),
            pl.BlockSpec((1, 1, tn), lambda l, j: (l, 0, j)),
        ],
        out_specs=pl.BlockSpec((1, rows, tn), lambda l, j: (l, 0, j)),
        out_shape=jax.ShapeDtypeStruct((n_layers, rows, width), F32),
        compiler_params=_params("parallel", "parallel"),
        name="adaln",
    )(cond, w_mod, b_mod.reshape(n_layers, 1, width))


def _inproj_kernel(x_ref, sh_ref, sc_ref, w_ref, cos_ref, sin_ref,
                   qa_ref, ka_ref, va_ref, qb_ref, kb_ref, vb_ref, hc_ref, *, rope):
    x = x_ref[0]
    h = _layer_norm(x) * (1.0 + sc_ref[0]) + sh_ref[0]
    u = _dot(h.astype(BF16), w_ref[...])

    def rotated(col):
        xq = u[:, col:col + LANES]
        if not rope:
            return xq
        lane = lax.broadcasted_iota(jnp.int32, xq.shape, 1)
        first = (lane & (HEAD_DIM // 2 - 1)) < (HEAD_DIM // 4)
        partner = jnp.where(first, pltpu.roll(xq, LANES - HEAD_DIM // 4, 1),
                            pltpu.roll(xq, HEAD_DIM // 4, 1))
        return xq * cos_ref[...] + partner * sin_ref[...]

    rot = [rotated(col) for col in range(0, ROPE_WIDTH, LANES)]
    n_q = A_Q_W // LANES
    qa_ref[0] = (jnp.concatenate(rot[:n_q], axis=1) * QK_SCALE).astype(BF16)
    ka_ref[0] = jnp.concatenate(rot[n_q:], axis=1).astype(BF16)
    va_ref[0] = u[:, OFF_AV:OFF_BQ].astype(BF16)
    qb_ref[0] = (u[:, OFF_BQ:OFF_BK] * QK_SCALE).astype(BF16)
    kb_ref[0] = u[:, OFF_BK:OFF_BV].astype(BF16)
    vb_ref[0] = u[:, OFF_BV:OFF_C].astype(BF16)
    a = u[:, OFF_C:OFF_C + C_CHANNELS]
    gate = u[:, OFF_C + C_CHANNELS:]
    hc_ref[0] = a * jax.nn.sigmoid(gate)


def in_projection(x, shift, scale, w_in_bf16, cos_t, sin_t, *, rope):
    bx, t, d = x.shape
    tm = min(1024, t)
    widths = (A_Q_W, A_KV_W, A_KV_W, B_W, B_W, B_W, C_CHANNELS)
    dtypes = (BF16,) * 6 + (F32,)
    tok = lambda b, i: (b, i, 0)
    per_b = lambda b, i: (b, 0, 0)
    return pl.pallas_call(
        functools.partial(_inproj_kernel, rope=rope),
        grid=(bx, t // tm),
        in_specs=[
            pl.BlockSpec((1, tm, d), tok),
            pl.BlockSpec((1, 1, d), per_b),
            pl.BlockSpec((1, 1, d), per_b),
            pl.BlockSpec((d, IN_WIDTH), lambda b, i: (0, 0)),
            pl.BlockSpec((tm, LANES), lambda b, i: (i, 0)),
            pl.BlockSpec((tm, LANES), lambda b, i: (i, 0)),
        ],
        out_specs=[pl.BlockSpec((1, tm, w), tok) for w in widths],
        out_shape=[jax.ShapeDtypeStruct((bx, t, w), dt) for w, dt in zip(widths, dtypes)],
        compiler_params=_params("parallel", "parallel"),
        name="in_projection",
    )(x, shift, scale, w_in_bf16, cos_t, sin_t)


def rope_tables(n_tokens):
    t = jnp.arange(n_tokens, dtype=jnp.int32)
    row = (t // GRID_W).astype(F32)[:, None]
    col = (t % GRID_W).astype(F32)[:, None]
    n_freq = HEAD_DIM // 4
    inv_freq = ROPE_BASE ** (-jnp.arange(n_freq, dtype=F32) / n_freq)
    ang_r = row * inv_freq
    ang_c = col * inv_freq
    cos_h = jnp.concatenate([jnp.cos(ang_r), jnp.cos(ang_r), jnp.cos(ang_c), jnp.cos(ang_c)], axis=1)
    sin_h = jnp.concatenate([-jnp.sin(ang_r), jnp.sin(ang_r), -jnp.sin(ang_c), jnp.sin(ang_c)], axis=1)
    reps = LANES // HEAD_DIM
    return jnp.tile(cos_h, (1, reps)), jnp.tile(sin_h, (1, reps))


def _with_ones(v):
    return jnp.concatenate([v, jnp.ones_like(v)], axis=1)


def _attend(score_parts, values, sink=None):
    m = score_parts[0].max(axis=-1, keepdims=True)
    for s in score_parts[1:]:
        m = jnp.maximum(m, s.max(axis=-1, keepdims=True))
    if sink is not None:
        m = jnp.maximum(m, sink)
    acc = None
    for s, v in zip(score_parts, values):
        term = _dot(jnp.exp((s - m).astype(BF16)), v)
        acc = term if acc is None else acc + term
    den = acc[:, HEAD_DIM:HEAD_DIM + 1]
    if sink is not None:
        den = den + jnp.exp(sink - m)
    return acc[:, :HEAD_DIM] / den


def _attn_a_kernel(sink_ref, q_ref, kp_ref, kc_ref, kn_ref, vp_ref, vc_ref, vn_ref,
                   kctx_ref, vctx_ref, out_ref, *, n_lat, tq):
    i = pl.program_id(1)
    k_win = jnp.concatenate([kp_ref[0], kc_ref[0], kn_ref[0]], axis=0)
    v_win = jnp.concatenate([vp_ref[0], vc_ref[0], vn_ref[0]], axis=0)
    kctx = kctx_ref[0]
    vctx = vctx_ref[0]
    sub = A_WINDOW
    span = 3 * A_WINDOW
    rows = A_GROUP * sub
    q_onehot = ((lax.broadcasted_iota(jnp.int32, (rows, sub), 0) & (sub - 1))
                == lax.broadcasted_iota(jnp.int32, (rows, sub), 1)).astype(BF16)
    key_i = lax.broadcasted_iota(jnp.int32, (span, sub), 0)
    qry_i = lax.broadcasted_iota(jnp.int32, (span, sub), 1)
    rel = key_i - A_WINDOW - qry_i
    in_band = (rel <= A_WINDOW) & (rel >= -A_WINDOW)
    group_of_row = lax.broadcasted_iota(jnp.int32, (rows, 1), 0) >> (sub.bit_length() - 1)
    sinks = []
    for hk in range(A_KV_HEADS):
        sink = jnp.zeros((rows, 1), F32)
        for g in range(A_GROUP):
            sink = jnp.where(group_of_row == g, sink_ref[hk * A_GROUP + g], sink)
        sinks.append(sink)
    vctx_ext = [_with_ones(vctx[:, hk * HEAD_DIM:(hk + 1) * HEAD_DIM]) for hk in range(A_KV_HEADS)]
    v_ext = [_with_ones(v_win[:, hk * HEAD_DIM:(hk + 1) * HEAD_DIM]) for hk in range(A_KV_HEADS)]
    def mask_columns(j):
        kpos = i * tq + j * sub - A_WINDOW + key_i
        valid = in_band & (kpos >= 0) & (kpos < n_lat)
        return jnp.where(valid, 0.0, NEG_INF).astype(BF16)

    masks = [mask_columns(j) for j in range(tq // sub)]

    def scores(j, hk):
        sl = slice(hk * HEAD_DIM, (hk + 1) * HEAD_DIM)
        q_rows = q_ref[0, j * sub:(j + 1) * sub]
        q = jnp.concatenate([q_rows[:, h * HEAD_DIM:(h + 1) * HEAD_DIM]
                             for h in range(hk * A_GROUP, (hk + 1) * A_GROUP)], axis=0)
        q_aug = jnp.concatenate([q_onehot, q], axis=1)
        k_aug = jnp.concatenate([masks[j], k_win[j * sub:j * sub + span, sl]], axis=1)
        return [_dot_t(q_aug, k_aug), _dot_t(q, kctx[:, sl])]

    units = [(j, hk) for j in range(tq // sub) for hk in range(A_KV_HEADS)]
    all_scores = [scores(j, hk) for j, hk in units]
    for (j, hk), s in zip(units, all_scores):
        v_sub = v_ext[hk][j * sub:j * sub + span]
        o = _attend(s, [v_sub, vctx_ext[hk]], sink=sinks[hk]).astype(BF16)
        for g in range(A_GROUP):
            h = hk * A_GROUP + g
            out_ref[0, j * sub:(j + 1) * sub, h * HEAD_DIM:(h + 1) * HEAD_DIM] = o[g * sub:(g + 1) * sub]


def window_attention(qa, ka, va, kc_a, vc_a, sink):
    bsz, n_lat, _ = qa.shape
    n_ctx = kc_a.shape[1]
    tq = min(1024, n_lat)
    w = A_WINDOW
    per = tq // w
    last = n_lat // w - 1
    prev = lambda b, i, s: (b, jnp.maximum(i * per - 1, 0), 0)
    cur = lambda b, i, s: (b, i, 0)
    nxt = lambda b, i, s: (b, jnp.minimum((i + 1) * per, last), 0)
    ctx = lambda b, i, s: (b, 0, 0)
    kv_specs = [pl.BlockSpec((1, w, A_KV_W), prev), pl.BlockSpec((1, tq, A_KV_W), cur),
                pl.BlockSpec((1, w, A_KV_W), nxt)]
    grid_spec = pltpu.PrefetchScalarGridSpec(
        num_scalar_prefetch=1,
        grid=(bsz, n_lat // tq),
        in_specs=[pl.BlockSpec((1, tq, A_Q_W), cur)] + kv_specs + kv_specs + [
            pl.BlockSpec((1, n_ctx, A_KV_W), ctx), pl.BlockSpec((1, n_ctx, A_KV_W), ctx)],
        out_specs=pl.BlockSpec((1, tq, A_Q_W), cur),
    )
    return pl.pallas_call(
        functools.partial(_attn_a_kernel, n_lat=n_lat, tq=tq),
        grid_spec=grid_spec,
        out_shape=jax.ShapeDtypeStruct((bsz, n_lat, A_Q_W), BF16),
        compiler_params=_params("parallel", "parallel"),
        name="window_attention",
    )(sink, qa, ka, ka, ka, va, va, va, kc_a, vc_a)


NB_Q_ROWS = 4
NB_BLOCKS_PER_STEP = 4


def _attn_b_kernel(q_ref, *refs):
    n_kv = NB_BLOCKS_PER_STEP + 2
    k_refs, v_refs = refs[:n_kv], refs[n_kv:2 * n_kv]
    kctx_ref, vctx_ref = refs[2 * n_kv:2 * n_kv + 2]
    bias_refs = refs[2 * n_kv + 2:-1]
    out_ref = refs[-1]
    tb = k_refs[0].shape[1]
    k_all = jnp.concatenate([r[0] for r in k_refs], axis=0)
    v_all = jnp.concatenate([r[0] for r in v_refs], axis=0)
    kctx = kctx_ref[0]
    vctx = vctx_ref[0]
    heads = [slice(h * HEAD_DIM, (h + 1) * HEAD_DIM) for h in range(B_HEADS)]
    units = [(sb, h) for sb in range(NB_BLOCKS_PER_STEP) for h in range(B_HEADS)]

    def scores(sb, h):
        q = q_ref[0, sb * tb:(sb + 1) * tb, heads[h]]
        return [_dot_t(q, k_all[sb * tb:(sb + 3) * tb, heads[h]]) + bias_refs[sb][0, h],
                _dot_t(q, kctx[:, heads[h]])]

    all_scores = [scores(sb, h) for sb, h in units]
    v_ext = [_with_ones(v_all[:, sl]) for sl in heads]
    vctx_ext = [_with_ones(vctx[:, sl]) for sl in heads]
    for (sb, h), s in zip(units, all_scores):
        o = _attend(s, [v_ext[h][sb * tb:(sb + 3) * tb], vctx_ext[h]])
        out_ref[0, sb * tb:(sb + 1) * tb, heads[h]] = o.astype(BF16)


def neighbourhood_bias(rel_bias, n_lat):
    rows = n_lat // GRID_W
    kr_n = min(NA_ROWS, rows)
    n_blocks = rows // NB_Q_ROWS
    n_heads, n_dr, n_dc = rel_bias.shape
    cols = np.arange(GRID_W)
    c_start = np.clip(cols - NA_COLS // 2, 0, GRID_W - NA_COLS)
    col_ok = (cols[None, :] >= c_start[:, None]) & (cols[None, :] < c_start[:, None] + NA_COLS)
    dc = np.clip(cols[None, :] - cols[:, None], -(NA_COLS - 1), NA_COLS - 1) + NA_COLS - 1
    pick_dc = (dc.reshape(-1)[None, :] == np.arange(n_dc)[:, None]).astype(np.float32)
    toeplitz = jnp.dot(rel_bias.reshape(n_heads * n_dr, n_dc), pick_dc, precision=lax.Precision.HIGHEST)
    toeplitz = jnp.where(col_ok.reshape(-1), toeplitz, NEG_INF).reshape(n_heads, n_dr, GRID_W, GRID_W)
    q_rl = np.arange(NB_Q_ROWS)
    k_rl = np.arange(3 * NB_Q_ROWS)
    row_ok, dr = [], []
    for j in sorted({0, min(1, n_blocks - 1), n_blocks - 1}):
        r = NB_Q_ROWS * j + q_rl
        kr = NB_Q_ROWS * (j - 1) + k_rl
        r_start = np.clip(r - kr_n // 2, 0, rows - kr_n)
        ok = (kr[None, :] >= r_start[:, None]) & (kr[None, :] < r_start[:, None] + kr_n)
        row_ok.append(ok & (kr[None, :] >= 0) & (kr[None, :] < rows))
        dr.append(np.clip(kr[None, :] - r[:, None] + NA_ROWS - 1, 0, n_dr - 1))
    row_ok = np.stack(row_ok)
    dr = np.stack(dr)
    tiles = jnp.stack([toeplitz[:, int(i)] for i in dr.reshape(-1)], axis=1)
    tiles = tiles.reshape((n_heads,) + dr.shape + (GRID_W, GRID_W))
    tiles = jnp.where(row_ok[None, :, :, :, None, None], tiles, NEG_INF)
    table = tiles.transpose(1, 0, 2, 4, 3, 5).reshape(
        dr.shape[0], n_heads, NB_Q_ROWS * GRID_W, 3 * NB_Q_ROWS * GRID_W)
    return table, n_blocks


def neighbourhood_attention(qb, kb, vb, kc_b, vc_b, rel_bias):
    bsz, n_lat, _ = qb.shape
    n_ctx = kc_b.shape[1]
    table, n_blocks = neighbourhood_bias(rel_bias, n_lat)
    assert n_blocks % NB_BLOCKS_PER_STEP == 0
    n_var = table.shape[0]
    tb = NB_Q_ROWS * GRID_W
    last = n_blocks - 1
    cur = lambda b, j: (b, j, 0)
    ctx = lambda b, j: (b, 0, 0)

    def key_block(off):
        return lambda b, j: (b, jnp.clip(NB_BLOCKS_PER_STEP * j + off, 0, last), 0)

    def variant(sb):
        def index(b, j):
            g = NB_BLOCKS_PER_STEP * j + sb
            return (jnp.where(g == 0, 0, jnp.where(g == last, n_var - 1, min(1, n_var - 1))), 0, 0, 0)
        return index

    kv_specs = [pl.BlockSpec((1, tb, B_W), key_block(off)) for off in range(-1, NB_BLOCKS_PER_STEP + 1)]
    return pl.pallas_call(
        _attn_b_kernel,
        grid=(bsz, n_blocks // NB_BLOCKS_PER_STEP),
        in_specs=[pl.BlockSpec((1, NB_BLOCKS_PER_STEP * tb, B_W), cur)] + kv_specs + kv_specs + [
            pl.BlockSpec((1, n_ctx, B_W), ctx), pl.BlockSpec((1, n_ctx, B_W), ctx)] + [
            pl.BlockSpec((1, B_HEADS, tb, 3 * tb), variant(sb)) for sb in range(NB_BLOCKS_PER_STEP)],
        out_specs=pl.BlockSpec((1, NB_BLOCKS_PER_STEP * tb, B_W), cur),
        out_shape=jax.ShapeDtypeStruct((bsz, n_lat, B_W), BF16),
        compiler_params=_params("parallel", "parallel"),
        name="neighbourhood_attention",
    )(qb, *([kb] * len(kv_specs)), *([vb] * len(kv_specs)), kc_b, vc_b, *([table] * NB_BLOCKS_PER_STEP))


def _ctx_attn_kernel(sink_ref, qa_ref, qb_ref, ka_ref, va_ref, kb_ref, vb_ref, oa_ref, ob_ref):
    qa, qb = qa_ref[0], qb_ref[0]
    ka, va, kb, vb = ka_ref[0], va_ref[0], kb_ref[0], vb_ref[0]
    for hq in range(A_Q_HEADS):
        sl = slice(hq * HEAD_DIM, (hq + 1) * HEAD_DIM)
        hk = hq // A_GROUP
        kv = slice(hk * HEAD_DIM, (hk + 1) * HEAD_DIM)
        q = qa[:, sl]
        o = _attend([_dot_t(q, ka[:, kv])], [_with_ones(va[:, kv])], sink=sink_ref[hq])
        oa_ref[0, :, sl] = o.astype(BF16)
    for h in range(B_HEADS):
        sl = slice(h * HEAD_DIM, (h + 1) * HEAD_DIM)
        q = qb[:, sl]
        o = _attend([_dot_t(q, kb[:, sl])], [_with_ones(vb[:, sl])])
        ob_ref[0, :, sl] = o.astype(BF16)


def context_attention(qa, qb, ka, va, kb, vb, sink):
    bsz, n_ctx, _ = qa.shape
    blk = lambda w: pl.BlockSpec((1, n_ctx, w), lambda b, s: (b, 0, 0))
    grid_spec = pltpu.PrefetchScalarGridSpec(
        num_scalar_prefetch=1,
        grid=(bsz,),
        in_specs=[blk(A_Q_W), blk(B_W), blk(A_KV_W), blk(A_KV_W), blk(B_W), blk(B_W)],
        out_specs=[blk(A_Q_W), blk(B_W)],
    )
    return pl.pallas_call(
        _ctx_attn_kernel,
        grid_spec=grid_spec,
        out_shape=[jax.ShapeDtypeStruct((bsz, n_ctx, A_Q_W), BF16),
                   jax.ShapeDtypeStruct((bsz, n_ctx, B_W), BF16)],
        compiler_params=_params("parallel"),
        name="context_attention",
    )(sink, qa, qb, ka, va, kb, vb)


CONV_HALO = 16
F32_SUBLANES = 8
CONV_SHIFT_SPAN = (CONV_HALO + C_CONV_WIDTH // 2) // F32_SUBLANES * F32_SUBLANES


def _conv_kernel(prev_ref, cur_ref, next_ref, w_ref, b_ref, g_ref, beta_ref, out_ref, shifted_ref, *, ts):
    i = pl.program_id(1)
    n_i = pl.num_programs(1)
    ext = jnp.concatenate([jnp.where(i > 0, prev_ref[0], 0.0), cur_ref[0],
                           jnp.where(i < n_i - 1, next_ref[0], 0.0)], axis=0)
    for r in range(F32_SUBLANES):
        shifted_ref[r] = ext[r:r + ts + CONV_SHIFT_SPAN]
    acc = jnp.zeros((ts, C_CHANNELS), F32)
    for k in range(C_CONV_WIDTH):
        start = CONV_HALO - C_CONV_WIDTH // 2 + k
        aligned = start - start % F32_SUBLANES
        acc = acc + shifted_ref[start % F32_SUBLANES, aligned:aligned + ts] * w_ref[k:k + 1]
    y = _layer_norm(acc + b_ref[...]) * g_ref[...] + beta_ref[...]
    out_ref[0] = (y * jax.nn.sigmoid(y)).astype(BF16)


def conformer_conv(hc, conv_w, conv_b, ln_g, ln_b):
    bx, t, ch = hc.shape
    ts = min(512, t)
    per = ts // CONV_HALO
    last = t // CONV_HALO - 1
    row = lambda v: v.reshape(1, ch)
    const = lambda b, i: (0, 0)
    return pl.pallas_call(
        functools.partial(_conv_kernel, ts=ts),
        grid=(bx, t // ts),
        in_specs=[
            pl.BlockSpec((1, CONV_HALO, ch), lambda b, i: (b, jnp.maximum(i * per - 1, 0), 0)),
            pl.BlockSpec((1, ts, ch), lambda b, i: (b, i, 0)),
            pl.BlockSpec((1, CONV_HALO, ch), lambda b, i: (b, jnp.minimum((i + 1) * per, last), 0)),
            pl.BlockSpec((C_CONV_WIDTH, ch), const),
            pl.BlockSpec((1, ch), const), pl.BlockSpec((1, ch), const), pl.BlockSpec((1, ch), const),
        ],
        out_specs=pl.BlockSpec((1, ts, ch), lambda b, i: (b, i, 0)),
        out_shape=jax.ShapeDtypeStruct((bx, t, ch), BF16),
        scratch_shapes=[pltpu.VMEM((F32_SUBLANES, ts + CONV_SHIFT_SPAN, ch), F32)],
        compiler_params=_params("parallel", "parallel"),
        name="conformer_conv",
    )(hc, hc, hc, conv_w, row(conv_b), row(ln_g), row(ln_b))


OUTPROJ_CHUNK_ROWS = 128


def _outproj_kernel(oa_ref, ob_ref, oc_ref, x_ref, g1_ref, sh_ref, sc_ref, w_ref, lng_ref, lnb_ref,
                    wr_hi_ref, wr_lo_ref, x1_ref, h2_ref, afft_ref, aff_ref, *, alpha):
    tm, d = x_ref.shape[1], x_ref.shape[2]
    rc = min(OUTPROJ_CHUNK_ROWS, tm)
    chunks = [slice(k * rc, (k + 1) * rc) for k in range(tm // rc)]
    outs = [(_dot(oa_ref[0, r], w_ref[0:A_Q_W])
             + _dot(ob_ref[0, r], w_ref[A_Q_W:A_Q_W + B_W])
             + _dot(oc_ref[0, r], w_ref[A_Q_W + B_W:])) for r in chunks]
    w_hi, w_lo = wr_hi_ref[...], wr_lo_ref[...]
    eye = (lax.broadcasted_iota(jnp.int32, (N_EXPERTS, N_EXPERTS), 0)
           == lax.broadcasted_iota(jnp.int32, (N_EXPERTS, N_EXPERTS), 1)).astype(BF16)
    rows_per_token = d // LANES
    ys = [_layer_norm(alpha * x_ref[0, r] + g1_ref[0] * o) * lng_ref[...] + lnb_ref[...]
          for r, o in zip(chunks, outs)]
    for r, y in zip(chunks, ys):
        x1_ref[0, r] = y
    h2s = [_layer_norm(y) * (1.0 + sc_ref[0]) + sh_ref[0] for y in ys]
    his = [h2.astype(BF16) for h2 in h2s]
    los = [(h2 - hi.astype(F32)).astype(BF16) for h2, hi in zip(h2s, his)]
    all_logits = [_dot_t(hi, w_hi) + _dot_t(lo, w_hi) + _dot_t(hi, w_lo) for hi, lo in zip(his, los)]
    for k, h2 in enumerate(h2s):
        for j in range(rows_per_token):
            h2_ref[0, pl.ds(k * rc * rows_per_token + j, rc, stride=rows_per_token), :] = (
                h2[:, j * LANES:(j + 1) * LANES])
    for r, logits in zip(chunks, all_logits):
        e_n = jnp.exp(logits - logits.max(axis=1, keepdims=True))
        aff = e_n / e_n.sum(axis=1, keepdims=True)
        aff_ref[0, r] = aff
        aff_t, rest = None, aff
        for _ in range(GATE_PARTS):
            part = rest.astype(BF16)
            term = _dot_t(eye, part)
            aff_t = term if aff_t is None else aff_t + term
            rest = rest - part.astype(F32)
        afft_ref[0, :, r] = aff_t


def out_projection(oa, ob, oc, x, g1, sh2, sc2, w_out_bf16, ln_g, ln_b, wr_hi, wr_lo, alpha):
    bx, t, d = x.shape
    tm = min(1024, t)
    tok = lambda b, i: (b, i, 0)
    per_b = lambda b, i: (b, 0, 0)
    const = lambda b, i: (0, 0)
    vec = pl.BlockSpec((1, d), const)
    return pl.pallas_call(
        functools.partial(_outproj_kernel, alpha=alpha),
        grid=(bx, t // tm),
        in_specs=[
            pl.BlockSpec((1, tm, A_Q_W), tok), pl.BlockSpec((1, tm, B_W), tok),
            pl.BlockSpec((1, tm, C_CHANNELS), tok), pl.BlockSpec((1, tm, d), tok),
            pl.BlockSpec((1, 1, d), per_b), pl.BlockSpec((1, 1, d), per_b), pl.BlockSpec((1, 1, d), per_b),
            pl.BlockSpec(w_out_bf16.shape, const), vec, vec,
            pl.BlockSpec((N_EXPERTS, d), const), pl.BlockSpec((N_EXPERTS, d), const),
        ],
        out_specs=[pl.BlockSpec((1, tm, d), tok), pl.BlockSpec((1, tm * (d // LANES), LANES), tok),
                   pl.BlockSpec((1, N_EXPERTS, tm), lambda b, i: (b, 0, i)),
                   pl.BlockSpec((1, tm, N_EXPERTS), tok)],
        out_shape=[jax.ShapeDtypeStruct((bx, t, d), F32),
                   jax.ShapeDtypeStruct((bx, t * (d // LANES), LANES), F32),
                   jax.ShapeDtypeStruct((bx, N_EXPERTS, t), F32),
                   jax.ShapeDtypeStruct((bx, t, N_EXPERTS), F32)],
        compiler_params=_params("parallel", "parallel"),
        name="out_projection",
    )(oa, ob, oc, x, g1, sh2, sc2, w_out_bf16, ln_g.reshape(1, d), ln_b.reshape(1, d), wr_hi, wr_lo)


def _select_kernel(afft_ref, pos_ref, off_ref, *, cap, n_tok):
    aff = afft_ref[0]

    def bit_step(j, bits):
        cand = bits | (jnp.int32(1) << (30 - j))
        cnt = jnp.sum((aff >= pltpu.bitcast(cand, F32)).astype(jnp.int32), axis=1, keepdims=True)
        return jnp.where(cnt >= cap, cand, bits)

    thr = pltpu.bitcast(lax.fori_loop(0, 31, bit_step, jnp.zeros((N_EXPERTS, 1), jnp.int32)), F32)
    above = (aff > thr).astype(F32)
    tied = (aff == thr).astype(F32)
    need = cap - jnp.sum(above, axis=1, keepdims=True)

    blk = MXU_DEPTH
    blocks = [slice(k * blk, (k + 1) * blk) for k in range(n_tok // blk)]
    r_i = lax.broadcasted_iota(jnp.int32, (blk, blk), 0)
    c_i = lax.broadcasted_iota(jnp.int32, (blk, blk), 1)
    strict_upper = (r_i < c_i).astype(BF16)

    def running(block_sums):
        run, total = [], jnp.zeros((N_EXPERTS, 1), F32)
        for s in block_sums:
            run.append(total)
            total = total + s
        return run

    tied_b = [tied[:, sl] for sl in blocks]
    tied_rank = [_dot(t.astype(BF16), strict_upper) for t in tied_b]
    tied_before = running([t.sum(axis=1, keepdims=True) for t in tied_b])
    sel_b = [above[:, sl] + t * ((before + rank) < need).astype(F32)
             for sl, t, before, rank in zip(blocks, tied_b, tied_before, tied_rank)]
    sel_rank = [_dot(s.astype(BF16), strict_upper) for s in sel_b]
    half_sums = [[s[:, h * OFFSET_BLOCK:(h + 1) * OFFSET_BLOCK].sum(axis=1, keepdims=True)
                  for h in range(blk // OFFSET_BLOCK)] for s in sel_b]
    offs = running([h for hs in half_sums for h in hs])
    per_blk = blk // OFFSET_BLOCK
    for k, (sl, s, rank) in enumerate(zip(blocks, sel_b, sel_rank)):
        pos_ref[0, :, sl] = jnp.where(s > 0.5, (offs[k * per_blk] + rank).astype(jnp.int32), -1)
    off_ref[0] = jnp.concatenate(offs, axis=1).astype(jnp.int32)


def expert_choice_select(aff_t, cap):
    bx, n_e, t = aff_t.shape
    n_tb = t // OFFSET_BLOCK
    return pl.pallas_call(
        functools.partial(_select_kernel, cap=cap, n_tok=t),
        grid=(bx,),
        in_specs=[pl.BlockSpec((1, n_e, t), lambda b: (b, 0, 0))],
        out_specs=[pl.BlockSpec((1, n_e, t), lambda b: (b, 0, 0)),
                   pl.BlockSpec((1, n_e, n_tb), lambda b: (b, 0, 0))],
        out_shape=[jax.ShapeDtypeStruct((bx, n_e, t), jnp.int32),
                   jax.ShapeDtypeStruct((bx, n_e, n_tb), jnp.int32)],
        compiler_params=_params("parallel"),
        name="expert_choice_select",
    )(aff_t)


GATE_PARTS = 3
TOKEN_LANE = GATE_PARTS * N_EXPERTS


def _slot_table_kernel(off_ref, pos_ref, aff_ref, tbl_ref, *, n_off, blocks_per_step):
    b, eg, kc = pl.program_id(0), pl.program_id(1), pl.program_id(2)
    epg = EXPERTS_PER_GATHER_STEP

    @pl.when(kc == 0)
    def _():
        tbl_ref[...] = jnp.zeros_like(tbl_ref)

    slot = lax.broadcasted_iota(jnp.int32, (GATHER_WINDOW, 1), 0)
    slot_2d = lax.broadcasted_iota(jnp.int32, (GATHER_WINDOW, GATHER_BLOCK), 0)
    lane = lax.broadcasted_iota(jnp.int32, (GATHER_BLOCK, LANES), 1)
    out_lane = lax.broadcasted_iota(jnp.int32, (GATHER_WINDOW, LANES), 1)
    local_token = lax.broadcasted_iota(jnp.int32, (GATHER_BLOCK, LANES), 0).astype(F32)
    place_r = lax.broadcasted_iota(jnp.int32, (N_EXPERTS, LANES), 0)
    place_c = lax.broadcasted_iota(jnp.int32, (N_EXPERTS, LANES), 1)
    def payload_of(kk):
        payload = jnp.where(lane == TOKEN_LANE, local_token, 0.0)
        rest = aff_ref[0, kk * GATHER_BLOCK:(kk + 1) * GATHER_BLOCK]
        for k in range(GATE_PARTS):
            part = rest.astype(BF16)
            payload = payload + _dot(part, (place_c == place_r + k * N_EXPERTS).astype(BF16))
            rest = rest - part.astype(F32)
        return payload.astype(BF16)

    payloads = [payload_of(kk) for kk in range(blocks_per_step)]
    pending = []
    for kk in range(blocks_per_step):
        kb = kc * blocks_per_step + kk
        first_token = jnp.where(out_lane == TOKEN_LANE, (kb * GATHER_BLOCK).astype(F32), 0.0)
        for ee in range(epg):
            off = off_ref[(b * N_EXPERTS + eg * epg + ee) * n_off + kb * (GATHER_BLOCK // OFFSET_BLOCK)]
            base = pl.multiple_of((off >> 4) << 4, WINDOW_ALIGN)
            onehot = (pos_ref[0, ee, 0, kk:kk + 1, :] - base == slot_2d).astype(BF16)
            pending.append((ee, off, base, _dot(onehot, payloads[kk]) + first_token))
    for ee, off, base, gathered in pending:
        win = pl.ds(base, GATHER_WINDOW)
        tbl_ref[0, ee, win, :] = jnp.where(slot >= off - base, gathered, tbl_ref[0, ee, win, :])


def slot_table(aff, pos, block_off, cap_pad):
    bx, t, _ = aff.shape
    n_tb = t // GATHER_BLOCK
    blocks_per_step = min(4, n_tb)
    n_steps = n_tb // blocks_per_step
    epg = EXPERTS_PER_GATHER_STEP
    tokens = blocks_per_step * GATHER_BLOCK
    pos5 = pos.reshape(bx, N_EXPERTS, n_steps, blocks_per_step, GATHER_BLOCK)
    grid_spec = pltpu.PrefetchScalarGridSpec(
        num_scalar_prefetch=1,
        grid=(bx, N_EXPERTS // epg, n_steps),
        in_specs=[
            pl.BlockSpec((1, epg, 1, blocks_per_step, GATHER_BLOCK), lambda b, g, k, s: (b, g, k, 0, 0)),
            pl.BlockSpec((1, tokens, N_EXPERTS), lambda b, g, k, s: (b, k, 0)),
        ],
        out_specs=pl.BlockSpec((1, epg, cap_pad, LANES), lambda b, g, k, s: (b, g, 0, 0)),
    )
    return pl.pallas_call(
        functools.partial(_slot_table_kernel, n_off=t // OFFSET_BLOCK, blocks_per_step=blocks_per_step),
        grid_spec=grid_spec,
        out_shape=jax.ShapeDtypeStruct((bx, N_EXPERTS, cap_pad, LANES), F32),
        compiler_params=_params("parallel", "parallel", "arbitrary"),
        name="slot_table",
    )(block_off.reshape(-1), pos5, aff)


GATHER_UNROLL = 16


def _row_gather_kernel(idx_ref, src_ref, xs_ref, tile_ref, *, cap, rows_per_token, chunk_stride):
    b, e = pl.program_id(0), pl.program_id(1)
    first = (b * N_EXPERTS + e) * cap

    def group(g, carry):
        for u in range(GATHER_UNROLL):
            s = g * GATHER_UNROLL + u
            row = pl.multiple_of(idx_ref[first + s], rows_per_token)
            tile_ref[pl.ds(s, rows_per_token, stride=chunk_stride), :] = src_ref[0, pl.ds(row, rows_per_token), :]
        return carry

    lax.fori_loop(0, cap // GATHER_UNROLL, group, 0)
    for j in range(rows_per_token):
        xs_ref[0, 0, :, j * LANES:(j + 1) * LANES] = tile_ref[j * chunk_stride:j * chunk_stride + cap, :].astype(BF16)


def gather_rows(h2_rows, token_row, cap, d):
    bx, n_rows, _ = h2_rows.shape
    rows_per_token = d // LANES
    chunk_stride = cap + 8
    grid_spec = pltpu.PrefetchScalarGridSpec(
        num_scalar_prefetch=1,
        grid=(bx, N_EXPERTS),
        in_specs=[pl.BlockSpec((1, n_rows, LANES), lambda b, e, s: (b, 0, 0), pipeline_mode=pl.Buffered(1))],
        out_specs=pl.BlockSpec((1, 1, cap, d), lambda b, e, s: (b, e, 0, 0)),
        scratch_shapes=[pltpu.VMEM((rows_per_token * chunk_stride, LANES), F32)],
    )
    return pl.pallas_call(
        functools.partial(_row_gather_kernel, cap=cap, rows_per_token=rows_per_token, chunk_stride=chunk_stride),
        grid_spec=grid_spec,
        out_shape=jax.ShapeDtypeStruct((bx, N_EXPERTS, cap, d), BF16),
        compiler_params=_params("parallel", "arbitrary"),
        name="gather_rows",
    )(token_row, h2_rows)


def _ffn_kernel(xs_ref, gs_ref, wg_ref, wu_ref, wd_ref, ye_ref, wg_bf, wu_bf, wd_bf, *, cap, row_tile):
    @pl.when(pl.program_id(1) == 0)
    def _():
        wg_bf[...] = wg_ref[0, 0].astype(BF16)
        wu_bf[...] = wu_ref[0, 0].astype(BF16)
        wd_bf[...] = wd_ref[0, 0].astype(BF16)

    n_b = xs_ref.shape[0]
    if n_b == 1:
        tiles = [[(0, r0, row_tile)] for r0 in range(0, cap, row_tile)]
    else:
        tiles = [[(bb, 0, cap) for bb in range(n_b)]]
    for tile in tiles:
        x = jnp.concatenate([xs_ref[bb, 0, r0:r0 + n] for bb, r0, n in tile], axis=0)
        terms = jnp.concatenate([gs_ref[bb, 0, r0:r0 + n, :] for bb, r0, n in tile], axis=0)
        lane = lax.broadcasted_iota(jnp.int32, terms.shape, 1)
        own = ((lane & (N_EXPERTS - 1)) == pl.program_id(0)) & (lane < TOKEN_LANE)
        terms = jnp.where(own, terms, 0.0)
        gate = _dot(x, wg_bf[...])
        up = _dot(x, wu_bf[...])
        hid = (gate * jax.nn.sigmoid(gate) * up).astype(BF16)
        g = jnp.sum(terms, axis=1, keepdims=True)
        ye = _dot(hid, wd_bf[...]) * g
        rows_per_slot = ye.shape[1] // LANES
        row = 0
        for bb, r0, n in tile:
            for j in range(rows_per_slot):
                ye_ref[bb, 0, pl.ds(r0 * rows_per_slot + j, n, stride=rows_per_slot), :] = (
                    ye[row:row + n, j * LANES:(j + 1) * LANES])
            row += n


def expert_ffn(xs, gs, wg, wu, wd, layer):
    bx, n_e, cap, d = xs.shape
    ff = wg.shape[-1]
    slot_rows = cap * (d // LANES)
    max_rows = 512
    row_tile = min(max_rows, cap)
    n_b = bx if bx * cap <= max_rows else 1
    return pl.pallas_call(
        functools.partial(_ffn_kernel, cap=cap, row_tile=row_tile),
        grid=(n_e, bx // n_b),
        in_specs=[
            pl.BlockSpec((n_b, 1, cap, d), lambda e, b: (b, e, 0, 0)),
            pl.BlockSpec((n_b, 1, cap, LANES), lambda e, b: (b, e, 0, 0)),
            pl.BlockSpec((1, 1, d, ff), lambda e, b: (layer, e, 0, 0)),
            pl.BlockSpec((1, 1, d, ff), lambda e, b: (layer, e, 0, 0)),
            pl.BlockSpec((1, 1, ff, d), lambda e, b: (layer, e, 0, 0)),
        ],
        out_specs=pl.BlockSpec((n_b, 1, slot_rows, LANES), lambda e, b: (b, e, 0, 0)),
        out_shape=jax.ShapeDtypeStruct((bx, n_e, slot_rows, LANES), F32),
        scratch_shapes=[pltpu.VMEM((d, ff), BF16), pltpu.VMEM((d, ff), BF16), pltpu.VMEM((ff, d), BF16)],
        compiler_params=_params("parallel", "arbitrary"),
        name="expert_ffn",
    )(xs, gs, wg, wu, wd)


SCATTER_UNROLL = 16


def _scatter_norm_kernel(idx_ref, ye_ref, x1_ref, g2_ref, lng_ref, lnb_ref, out_ref, acc_ref, *,
                         cap, rows_per_token, alpha):
    b, step = pl.program_id(0), pl.program_id(1)

    @pl.when(step == 0)
    def _():
        acc_ref[...] = jnp.zeros_like(acc_ref)

    @pl.when(step < N_EXPERTS)
    def _():
        first = (b * N_EXPERTS + step) * cap

        def group(g, carry):
            rows, sums = [], []
            for u in range(SCATTER_UNROLL):
                s = g * SCATTER_UNROLL + u
                row = pl.multiple_of(idx_ref[first + s], rows_per_token)
                src = pl.multiple_of(s * rows_per_token, rows_per_token)
                rows.append(row)
                sums.append(acc_ref[pl.ds(row, rows_per_token), :] + ye_ref[0, 0, pl.ds(src, rows_per_token), :])
            for row, total in zip(rows, sums):
                acc_ref[pl.ds(row, rows_per_token), :] = total
            return carry

        lax.fori_loop(0, cap // SCATTER_UNROLL, group, 0)

    @pl.when(step >= N_EXPERTS)
    def _():
        tm = x1_ref.shape[1]
        tile_row = pl.multiple_of((step - N_EXPERTS) * (tm * rows_per_token), rows_per_token)
        moe = jnp.concatenate([acc_ref[pl.ds(tile_row + j, tm, stride=rows_per_token), :]
                               for j in range(rows_per_token)], axis=1)
        y = alpha * x1_ref[0] + g2_ref[0] * moe
        out_ref[0] = _layer_norm(y) * lng_ref[...] + lnb_ref[...]


def scatter_and_norm(ye_rows, token_row, x1, g2, ln_g, ln_b, alpha, cap):
    bx, t, d = x1.shape
    n_e, slot_rows = ye_rows.shape[1], ye_rows.shape[2]
    rows_per_token = d // LANES
    tm = min(512, t)
    tile = lambda b, i, s: (b, jnp.maximum(i - n_e, 0), 0)
    const = lambda b, i, s: (0, 0)
    grid_spec = pltpu.PrefetchScalarGridSpec(
        num_scalar_prefetch=1,
        grid=(bx, n_e + t // tm),
        in_specs=[
            pl.BlockSpec((1, 1, slot_rows, LANES), lambda b, i, s: (b, jnp.minimum(i, n_e - 1), 0, 0)),
            pl.BlockSpec((1, tm, d), tile),
            pl.BlockSpec((1, 1, d), lambda b, i, s: (b, 0, 0)),
            pl.BlockSpec((1, d), const), pl.BlockSpec((1, d), const),
        ],
        out_specs=pl.BlockSpec((1, tm, d), tile),
        scratch_shapes=[pltpu.VMEM((t * rows_per_token, LANES), F32)],
    )
    return pl.pallas_call(
        functools.partial(_scatter_norm_kernel, cap=cap, rows_per_token=rows_per_token, alpha=alpha),
        grid_spec=grid_spec,
        out_shape=jax.ShapeDtypeStruct((bx, t, d), F32),
        compiler_params=_params("parallel", "arbitrary"),
        name="scatter_and_norm",
    )(token_row, ye_rows, x1, g2, ln_g.reshape(1, d), ln_b.reshape(1, d))


def _split_bf16(w):
    hi = w.astype(BF16)
    return hi, (w - hi.astype(F32)).astype(BF16)


def _mixer_tail(oa, ob, oc, x, mod, lw, alpha):
    g1, sh2, sc2, g2 = mod
    t = x.shape[1]
    cap = EC_CAPACITY * t // N_EXPERTS
    cap_pad = cap + SLOT_PAD
    x1, h2_rows, aff_t, aff = out_projection(oa, ob, oc, x, g1, sh2, sc2, lw["w_out"], lw["ln1_g"], lw["ln1_b"],
                                        lw["wr_hi"], lw["wr_lo"], alpha)
    pos, block_off = expert_choice_select(aff_t, cap)
    table = slot_table(aff, pos, block_off, cap_pad)
    token_row = (table[:, :, :cap, TOKEN_LANE].astype(jnp.int32) * (x.shape[2] // LANES)).reshape(-1)
    xs = gather_rows(h2_rows, token_row, cap, x.shape[2])
    ye_rows = expert_ffn(xs, table, lw["w_gate"], lw["w_up"], lw["w_down"], lw["layer"])
    return scatter_and_norm(ye_rows, token_row, x1, g2, lw["ln2_g"], lw["ln2_b"], alpha, cap)


def kernel(x, c, ctx, c_ctx, w_mod, b_mod, w_in, a_sink, nat_bias, conv_w, conv_b, conv_ln_g, conv_ln_b,
           w_out, ln1_g, ln1_b, w_router, w_gate, w_up, w_down, ln2_g, ln2_b):
    bsz, n_lat, d = x.shape
    depth = w_mod.shape[0]
    alpha = (2 * depth) ** 0.25
    cos_t, sin_t = rope_tables(n_lat)

    cond = jnp.concatenate([c, c_ctx[None, :], jnp.zeros((8 - bsz - 1, d), F32)], axis=0)
    mods = adaln_all(cond, w_mod, b_mod)

    for l in range(depth):
        last = l == depth - 1
        wr_hi, wr_lo = _split_bf16(w_router[l].T)
        lw = dict(w_out=w_out[l].astype(BF16), ln1_g=ln1_g[l], ln1_b=ln1_b[l], wr_hi=wr_hi, wr_lo=wr_lo,
                  w_gate=w_gate, w_up=w_up, w_down=w_down, layer=l,
                  ln2_g=ln2_g[l], ln2_b=ln2_b[l])
        w_in_l = w_in[l].astype(BF16)
        lat = [mods[l, :bsz, k * d:(k + 1) * d][:, None, :] for k in range(N_MOD)]
        cm = [jnp.broadcast_to(mods[l, bsz, k * d:(k + 1) * d][None, None, :], (bsz, 1, d))
              for k in range(N_MOD)]
        conv_args = (conv_w[l], conv_b[l], conv_ln_g[l], conv_ln_b[l])

        qa_c, ka_c, va_c, qb_c, kb_c, vb_c, hc_c = in_projection(ctx, cm[0], cm[1], w_in_l, cos_t, sin_t, rope=False)
        if not last:
            oa_c, ob_c = context_attention(qa_c, qb_c, ka_c, va_c, kb_c, vb_c, a_sink[l])
            oc_c = conformer_conv(hc_c, *conv_args)
            ctx_new = _mixer_tail(oa_c, ob_c, oc_c, ctx, (cm[2], cm[3], cm[4], cm[5]), lw, alpha)

        qa, ka, va, qb, kb, vb, hc = in_projection(x, lat[0], lat[1], w_in_l, cos_t, sin_t, rope=True)
        oa = window_attention(qa, ka, va, ka_c, va_c, a_sink[l])
        ob = neighbourhood_attention(qb, kb, vb, kb_c, vb_c, nat_bias[l])
        oc = conformer_conv(hc, *conv_args)
        x = _mixer_tail(oa, ob, oc, x, (lat[2], lat[3], lat[4], lat[5]), lw, alpha)
        if not last:
            ctx = ctx_new
    return x
```

```python
import functools

import numpy as np
import jax
import jax.numpy as jnp
from jax import lax
from jax.experimental import pallas as pl
from jax.experimental.pallas import tpu as pltpu

HEAD_DIM = 64
GRID_W = 64
A_Q_HEADS = 8
A_KV_HEADS = 2
A_GROUP = A_Q_HEADS // A_KV_HEADS
A_WINDOW = 128
B_HEADS = 4
NA_ROWS = 8
NA_COLS = 16
C_CHANNELS = 256
C_CONV_WIDTH = 31
A_Q_W = A_Q_HEADS * HEAD_DIM
A_KV_W = A_KV_HEADS * HEAD_DIM
B_W = B_HEADS * HEAD_DIM
OFF_AK = A_Q_W
OFF_AV = OFF_AK + A_KV_W
OFF_BQ = OFF_AV + A_KV_W
OFF_BK = OFF_BQ + B_W
OFF_BV = OFF_BK + B_W
OFF_C = OFF_BV + B_W
IN_WIDTH = OFF_C + 2 * C_CHANNELS
ROPE_WIDTH = A_Q_W + A_KV_W
N_EXPERTS = 16
EC_CAPACITY = 2
ROPE_BASE = 10000.0
LN_EPS = 1e-6
N_MOD = 6
NEG_INF = -1e30
QK_SCALE = HEAD_DIM ** -0.5

LANES = 128
WINDOW_ALIGN = 16
MXU_DEPTH = 256
OFFSET_BLOCK = 128
GATHER_BLOCK = MXU_DEPTH
GATHER_WINDOW = GATHER_BLOCK + WINDOW_ALIGN
SLOT_PAD = 3 * LANES
EXPERTS_PER_GATHER_STEP = 16
VMEM_LIMIT = 56 * 1024 * 1024

F32 = jnp.float32
BF16 = jnp.bfloat16


def _dot(a, b):
    return jnp.dot(a, b, preferred_element_type=F32)


def _dot_t(a, b):
    return lax.dot_general(a, b, (((1,), (1,)), ((), ())), preferred_element_type=F32)


def _layer_norm(x):
    mu = jnp.mean(x, axis=-1, keepdims=True)
    xc = x - mu
    var = jnp.mean(xc * xc, axis=-1, keepdims=True)
    return xc * lax.rsqrt(var + LN_EPS)


def _params(*sem):
    return pltpu.CompilerParams(dimension_semantics=sem, vmem_limit_bytes=VMEM_LIMIT)


def _mod_kernel(cond_ref, w_ref, b_ref, out_ref):
    cnd = cond_ref[...]
    act = cnd * jax.nn.sigmoid(cnd)
    out_ref[0] = jnp.dot(act, w_ref[0], preferred_element_type=F32,
                         precision=lax.Precision.HIGHEST) + b_ref[0]


def adaln_all(cond, w_mod, b_mod):
    n_layers, d, width = w_mod.shape
    rows = cond.shape[0]
    tn = 1536
    return pl.pallas_call(
        _mod_kernel,
        grid=(n_layers, width // tn),
        in_specs=[
            pl.BlockSpec((rows, d), lambda l, j: (0, 0)),
            pl.BlockSpec((1, d, tn), lambda l, j: (l, 0, j)),
            pl.BlockSpec((1, 1, tn), lambda l, j: (l, 0, j)),
        ],
        out_specs=pl.BlockSpec((1, rows, tn), lambda l, j: (l, 0, j)),
        out_shape=jax.ShapeDtypeStruct((n_layers, rows, width), F32),
        compiler_params=_params("parallel", "parallel"),
        name="adaln",
    )(cond, w_mod, b_mod.reshape(n_layers, 1, width))


def _inproj_kernel(x_ref, sh_ref, sc_ref, w_ref, cos_ref, sin_ref,
                   qa_ref, ka_ref, va_ref, qb_ref, kb_ref, vb_ref, hc_ref, *, rope):
    x = x_ref[0]
    h = _layer_norm(x) * (1.0 + sc_ref[0]) + sh_ref[0]
    u = _dot(h.astype(BF16), w_ref[...])

    def rotated(col):
        xq = u[:, col:col + LANES]
        if not rope:
            return xq
        lane = lax.broadcasted_iota(jnp.int32, xq.shape, 1)
        first = (lane & (HEAD_DIM // 2 - 1)) < (HEAD_DIM // 4)
        partner = jnp.where(first, pltpu.roll(xq, LANES - HEAD_DIM // 4, 1),
                            pltpu.roll(xq, HEAD_DIM // 4, 1))
        return xq * cos_ref[...] + partner * sin_ref[...]

    rot = [rotated(col) for col in range(0, ROPE_WIDTH, LANES)]
    n_q = A_Q_W // LANES
    qa_ref[0] = (jnp.concatenate(rot[:n_q], axis=1) * QK_SCALE).astype(BF16)
    ka_ref[0] = jnp.concatenate(rot[n_q:], axis=1).astype(BF16)
    va_ref[0] = u[:, OFF_AV:OFF_BQ].astype(BF16)
    qb_ref[0] = (u[:, OFF_BQ:OFF_BK] * QK_SCALE).astype(BF16)
    kb_ref[0] = u[:, OFF_BK:OFF_BV].astype(BF16)
    vb_ref[0] = u[:, OFF_BV:OFF_C].astype(BF16)
    a = u[:, OFF_C:OFF_C + C_CHANNELS]
    gate = u[:, OFF_C + C_CHANNELS:]
    hc_ref[0] = a * jax.nn.sigmoid(gate)


def in_projection(x, shift, scale, w_in_bf16, cos_t, sin_t, *, rope):
    bx, t, d = x.shape
    tm = min(1024, t)
    widths = (A_Q_W, A_KV_W, A_KV_W, B_W, B_W, B_W, C_CHANNELS)
    dtypes = (BF16,) * 6 + (F32,)
    tok = lambda b, i: (b, i, 0)
    per_b = lambda b, i: (b, 0, 0)
    return pl.pallas_call(
        functools.partial(_inproj_kernel, rope=rope),
        grid=(bx, t // tm),
        in_specs=[
            pl.BlockSpec((1, tm, d), tok),
            pl.BlockSpec((1, 1, d), per_b),
            pl.BlockSpec((1, 1, d), per_b),
            pl.BlockSpec((d, IN_WIDTH), lambda b, i: (0, 0)),
            pl.BlockSpec((tm, LANES), lambda b, i: (i, 0)),
            pl.BlockSpec((tm, LANES), lambda b, i: (i, 0)),
        ],
        out_specs=[pl.BlockSpec((1, tm, w), tok) for w in widths],
        out_shape=[jax.ShapeDtypeStruct((bx, t, w), dt) for w, dt in zip(widths, dtypes)],
        compiler_params=_params("parallel", "parallel"),
        name="in_projection",
    )(x, shift, scale, w_in_bf16, cos_t, sin_t)


def rope_tables(n_tokens):
    t = jnp.arange(n_tokens, dtype=jnp.int32)
    row = (t // GRID_W).astype(F32)[:, None]
    col = (t % GRID_W).astype(F32)[:, None]
    n_freq = HEAD_DIM // 4
    inv_freq = ROPE_BASE ** (-jnp.arange(n_freq, dtype=F32) / n_freq)
    ang_r = row * inv_freq
    ang_c = col * inv_freq
    cos_h = jnp.concatenate([jnp.cos(ang_r), jnp.cos(ang_r), jnp.cos(ang_c), jnp.cos(ang_c)], axis=1)
    sin_h = jnp.concatenate([-jnp.sin(ang_r), jnp.sin(ang_r), -jnp.sin(ang_c), jnp.sin(ang_c)], axis=1)
    reps = LANES // HEAD_DIM
    return jnp.tile(cos_h, (1, reps)), jnp.tile(sin_h, (1, reps))


def _with_ones(v):
    return jnp.concatenate([v, jnp.ones_like(v)], axis=1)


def _attend(score_parts, values, sink=None):
    m = score_parts[0].max(axis=-1, keepdims=True)
    for s in score_parts[1:]:
        m = jnp.maximum(m, s.max(axis=-1, keepdims=True))
    if sink is not None:
        m = jnp.maximum(m, sink)
    acc = None
    for s, v in zip(score_parts, values):
        term = _dot(jnp.exp((s - m).astype(BF16)), v)
        acc = term if acc is None else acc + term
    den = acc[:, HEAD_DIM:HEAD_DIM + 1]
    if sink is not None:
        den = den + jnp.exp(sink - m)
    return acc[:, :HEAD_DIM] / den


def _attn_a_kernel(sink_ref, q_ref, kp_ref, kc_ref, kn_ref, vp_ref, vc_ref, vn_ref,
                   kctx_ref, vctx_ref, out_ref, *, n_lat, tq):
    i = pl.program_id(1)
    k_win = jnp.concatenate([kp_ref[0], kc_ref[0], kn_ref[0]], axis=0)
    v_win = jnp.concatenate([vp_ref[0], vc_ref[0], vn_ref[0]], axis=0)
    kctx = kctx_ref[0]
    vctx = vctx_ref[0]
    sub = A_WINDOW
    span = 3 * A_WINDOW
    rows = A_GROUP * sub
    q_onehot = ((lax.broadcasted_iota(jnp.int32, (rows, sub), 0) & (sub - 1))
                == lax.broadcasted_iota(jnp.int32, (rows, sub), 1)).astype(BF16)
    key_i = lax.broadcasted_iota(jnp.int32, (span, sub), 0)
    qry_i = lax.broadcasted_iota(jnp.int32, (span, sub), 1)
    rel = key_i - A_WINDOW - qry_i
    in_band = (rel <= A_WINDOW) & (rel >= -A_WINDOW)
    group_of_row = lax.broadcasted_iota(jnp.int32, (rows, 1), 0) >> (sub.bit_length() - 1)
    sinks = []
    for hk in range(A_KV_HEADS):
        sink = jnp.zeros((rows, 1), F32)
        for g in range(A_GROUP):
            sink = jnp.where(group_of_row == g, sink_ref[hk * A_GROUP + g], sink)
        sinks.append(sink)
    vctx_ext = [_with_ones(vctx[:, hk * HEAD_DIM:(hk + 1) * HEAD_DIM]) for hk in range(A_KV_HEADS)]
    v_ext = [_with_ones(v_win[:, hk * HEAD_DIM:(hk + 1) * HEAD_DIM]) for hk in range(A_KV_HEADS)]
    def mask_columns(j):
        kpos = i * tq + j * sub - A_WINDOW + key_i
        valid = in_band & (kpos >= 0) & (kpos < n_lat)
        return jnp.where(valid, 0.0, NEG_INF).astype(BF16)

    masks = [mask_columns(j) for j in range(tq // sub)]

    def scores(j, hk):
        sl = slice(hk * HEAD_DIM, (hk + 1) * HEAD_DIM)
        q_rows = q_ref[0, j * sub:(j + 1) * sub]
        q = jnp.concatenate([q_rows[:, h * HEAD_DIM:(h + 1) * HEAD_DIM]
                             for h in range(hk * A_GROUP, (hk + 1) * A_GROUP)], axis=0)
        q_aug = jnp.concatenate([q_onehot, q], axis=1)
        k_aug = jnp.concatenate([masks[j], k_win[j * sub:j * sub + span, sl]], axis=1)
        return [_dot_t(q_aug, k_aug), _dot_t(q, kctx[:, sl])]

    units = [(j, hk) for j in range(tq // sub) for hk in range(A_KV_HEADS)]
    all_scores = [scores(j, hk) for j, hk in units]
    for (j, hk), s in zip(units, all_scores):
        v_sub = v_ext[hk][j * sub:j * sub + span]
        o = _attend(s, [v_sub, vctx_ext[hk]], sink=sinks[hk]).astype(BF16)
        for g in range(A_GROUP):
            h = hk * A_GROUP + g
            out_ref[0, j * sub:(j + 1) * sub, h * HEAD_DIM:(h + 1) * HEAD_DIM] = o[g * sub:(g + 1) * sub]


def window_attention(qa, ka, va, kc_a, vc_a, sink):
    bsz, n_lat, _ = qa.shape
    n_ctx = kc_a.shape[1]
    tq = min(1024, n_lat)
    w = A_WINDOW
    per = tq // w
    last = n_lat // w - 1
    prev = lambda b, i, s: (b, jnp.maximum(i * per - 1, 0), 0)
    cur = lambda b, i, s: (b, i, 0)
    nxt = lambda b, i, s: (b, jnp.minimum((i + 1) * per, last), 0)
    ctx = lambda b, i, s: (b, 0, 0)
    kv_specs = [pl.BlockSpec((1, w, A_KV_W), prev), pl.BlockSpec((1, tq, A_KV_W), cur),
                pl.BlockSpec((1, w, A_KV_W), nxt)]
    grid_spec = pltpu.PrefetchScalarGridSpec(
        num_scalar_prefetch=1,
        grid=(bsz, n_lat // tq),
        in_specs=[pl.BlockSpec((1, tq, A_Q_W), cur)] + kv_specs + kv_specs + [
            pl.BlockSpec((1, n_ctx, A_KV_W), ctx), pl.BlockSpec((1, n_ctx, A_KV_W), ctx)],
        out_specs=pl.BlockSpec((1, tq, A_Q_W), cur),
    )
    return pl.pallas_call(
        functools.partial(_attn_a_kernel, n_lat=n_lat, tq=tq),
        grid_spec=grid_spec,
        out_shape=jax.ShapeDtypeStruct((bsz, n_lat, A_Q_W), BF16),
        compiler_params=_params("parallel", "parallel"),
        name="window_attention",
    )(sink, qa, ka, ka, ka, va, va, va, kc_a, vc_a)


NB_Q_ROWS = 4
NB_BLOCKS_PER_STEP = 4


def _attn_b_kernel(q_ref, *refs):
    n_kv = NB_BLOCKS_PER_STEP + 2
    k_refs, v_refs = refs[:n_kv], refs[n_kv:2 * n_kv]
    kctx_ref, vctx_ref = refs[2 * n_kv:2 * n_kv + 2]
    bias_refs = refs[2 * n_kv + 2:-1]
    out_ref = refs[-1]
    tb = k_refs[0].shape[1]
    k_all = jnp.concatenate([r[0] for r in k_refs], axis=0)
    v_all = jnp.concatenate([r[0] for r in v_refs], axis=0)
    kctx = kctx_ref[0]
    vctx = vctx_ref[0]
    heads = [slice(h * HEAD_DIM, (h + 1) * HEAD_DIM) for h in range(B_HEADS)]
    units = [(sb, h) for sb in range(NB_BLOCKS_PER_STEP) for h in range(B_HEADS)]

    def scores(sb, h):
        q = q_ref[0, sb * tb:(sb + 1) * tb, heads[h]]
        return [_dot_t(q, k_all[sb * tb:(sb + 3) * tb, heads[h]]) + bias_refs[sb][0, h],
                _dot_t(q, kctx[:, heads[h]])]

    all_scores = [scores(sb, h) for sb, h in units]
    v_ext = [_with_ones(v_all[:, sl]) for sl in heads]
    vctx_ext = [_with_ones(vctx[:, sl]) for sl in heads]
    for (sb, h), s in zip(units, all_scores):
        o = _attend(s, [v_ext[h][sb * tb:(sb + 3) * tb], vctx_ext[h]])
        out_ref[0, sb * tb:(sb + 1) * tb, heads[h]] = o.astype(BF16)


def neighbourhood_bias(rel_bias, n_lat):
    rows = n_lat // GRID_W
    kr_n = min(NA_ROWS, rows)
    n_blocks = rows // NB_Q_ROWS
    n_heads, n_dr, n_dc = rel_bias.shape
    cols = np.arange(GRID_W)
    c_start = np.clip(cols - NA_COLS // 2, 0, GRID_W - NA_COLS)
    col_ok = (cols[None, :] >= c_start[:, None]) & (cols[None, :] < c_start[:, None] + NA_COLS)
    dc = np.clip(cols[None, :] - cols[:, None], -(NA_COLS - 1), NA_COLS - 1) + NA_COLS - 1
    pick_dc = (dc.reshape(-1)[None, :] == np.arange(n_dc)[:, None]).astype(np.float32)
    toeplitz = jnp.dot(rel_bias.reshape(n_heads * n_dr, n_dc), pick_dc, precision=lax.Precision.HIGHEST)
    toeplitz = jnp.where(col_ok.reshape(-1), toeplitz, NEG_INF).reshape(n_heads, n_dr, GRID_W, GRID_W)
    q_rl = np.arange(NB_Q_ROWS)
    k_rl = np.arange(3 * NB_Q_ROWS)
    row_ok, dr = [], []
    for j in sorted({0, min(1, n_blocks - 1), n_blocks - 1}):
        r = NB_Q_ROWS * j + q_rl
        kr = NB_Q_ROWS * (j - 1) + k_rl
        r_start = np.clip(r - kr_n // 2, 0, rows - kr_n)
        ok = (kr[None, :] >= r_start[:, None]) & (kr[None, :] < r_start[:, None] + kr_n)
        row_ok.append(ok & (kr[None, :] >= 0) & (kr[None, :] < rows))
        dr.append(np.clip(kr[None, :] - r[:, None] + NA_ROWS - 1, 0, n_dr - 1))
    row_ok = np.stack(row_ok)
    dr = np.stack(dr)
    tiles = jnp.stack([toeplitz[:, int(i)] for i in dr.reshape(-1)], axis=1)
    tiles = tiles.reshape((n_heads,) + dr.shape + (GRID_W, GRID_W))
    tiles = jnp.where(row_ok[None, :, :, :, None, None], tiles, NEG_INF)
    table = tiles.transpose(1, 0, 2, 4, 3, 5).reshape(
        dr.shape[0], n_heads, NB_Q_ROWS * GRID_W, 3 * NB_Q_ROWS * GRID_W)
    return table, n_blocks


def neighbourhood_attention(qb, kb, vb, kc_b, vc_b, rel_bias):
    bsz, n_lat, _ = qb.shape
    n_ctx = kc_b.shape[1]
    table, n_blocks = neighbourhood_bias(rel_bias, n_lat)
    assert n_blocks % NB_BLOCKS_PER_STEP == 0
    n_var = table.shape[0]
    tb = NB_Q_ROWS * GRID_W
    last = n_blocks - 1
    cur = lambda b, j: (b, j, 0)
    ctx = lambda b, j: (b, 0, 0)

    def key_block(off):
        return lambda b, j: (b, jnp.clip(NB_BLOCKS_PER_STEP * j + off, 0, last), 0)

    def variant(sb):
        def index(b, j):
            g = NB_BLOCKS_PER_STEP * j + sb
            return (jnp.where(g == 0, 0, jnp.where(g == last, n_var - 1, min(1, n_var - 1))), 0, 0, 0)
        return index

    kv_specs = [pl.BlockSpec((1, tb, B_W), key_block(off)) for off in range(-1, NB_BLOCKS_PER_STEP + 1)]
    return pl.pallas_call(
        _attn_b_kernel,
        grid=(bsz, n_blocks // NB_BLOCKS_PER_STEP),
        in_specs=[pl.BlockSpec((1, NB_BLOCKS_PER_STEP * tb, B_W), cur)] + kv_specs + kv_specs + [
            pl.BlockSpec((1, n_ctx, B_W), ctx), pl.BlockSpec((1, n_ctx, B_W), ctx)] + [
            pl.BlockSpec((1, B_HEADS, tb, 3 * tb), variant(sb)) for sb in range(NB_BLOCKS_PER_STEP)],
        out_specs=pl.BlockSpec((1, NB_BLOCKS_PER_STEP * tb, B_W), cur),
        out_shape=jax.ShapeDtypeStruct((bsz, n_lat, B_W), BF16),
        compiler_params=_params("parallel", "parallel"),
        name="neighbourhood_attention",
    )(qb, *([kb] * len(kv_specs)), *([vb] * len(kv_specs)), kc_b, vc_b, *([table] * NB_BLOCKS_PER_STEP))


def _ctx_attn_kernel(sink_ref, qa_ref, qb_ref, ka_ref, va_ref, kb_ref, vb_ref, oa_ref, ob_ref):
    qa, qb = qa_ref[0], qb_ref[0]
    ka, va, kb, vb = ka_ref[0], va_ref[0], kb_ref[0], vb_ref[0]
    for hq in range(A_Q_HEADS):
        sl = slice(hq * HEAD_DIM, (hq + 1) * HEAD_DIM)
        hk = hq // A_GROUP
        kv = slice(hk * HEAD_DIM, (hk + 1) * HEAD_DIM)
        q = qa[:, sl]
        o = _attend([_dot_t(q, ka[:, kv])], [_with_ones(va[:, kv])], sink=sink_ref[hq])
        oa_ref[0, :, sl] = o.astype(BF16)
    for h in range(B_HEADS):
        sl = slice(h * HEAD_DIM, (h + 1) * HEAD_DIM)
        q = qb[:, sl]
        o = _attend([_dot_t(q, kb[:, sl])], [_with_ones(vb[:, sl])])
        ob_ref[0, :, sl] = o.astype(BF16)


def context_attention(qa, qb, ka, va, kb, vb, sink):
    bsz, n_ctx, _ = qa.shape
    blk = lambda w: pl.BlockSpec((1, n_ctx, w), lambda b, s: (b, 0, 0))
    grid_spec = pltpu.PrefetchScalarGridSpec(
        num_scalar_prefetch=1,
        grid=(bsz,),
        in_specs=[blk(A_Q_W), blk(B_W), blk(A_KV_W), blk(A_KV_W), blk(B_W), blk(B_W)],
        out_specs=[blk(A_Q_W), blk(B_W)],
    )
    return pl.pallas_call(
        _ctx_attn_kernel,
        grid_spec=grid_spec,
        out_shape=[jax.ShapeDtypeStruct((bsz, n_ctx, A_Q_W), BF16),
                   jax.ShapeDtypeStruct((bsz, n_ctx, B_W), BF16)],
        compiler_params=_params("parallel"),
        name="context_attention",
    )(sink, qa, qb, ka, va, kb, vb)


CONV_HALO = 16
F32_SUBLANES = 8
CONV_SHIFT_SPAN = (CONV_HALO + C_CONV_WIDTH // 2) // F32_SUBLANES * F32_SUBLANES


def _conv_kernel(prev_ref, cur_ref, next_ref, w_ref, b_ref, g_ref, beta_ref, out_ref, shifted_ref, *, ts):
    i = pl.program_id(1)
    n_i = pl.num_programs(1)
    ext = jnp.concatenate([jnp.where(i > 0, prev_ref[0], 0.0), cur_ref[0],
                           jnp.where(i < n_i - 1, next_ref[0], 0.0)], axis=0)
    for r in range(F32_SUBLANES):
        shifted_ref[r] = ext[r:r + ts + CONV_SHIFT_SPAN]
    acc = jnp.zeros((ts, C_CHANNELS), F32)
    for k in range(C_CONV_WIDTH):
        start = CONV_HALO - C_CONV_WIDTH // 2 + k
        aligned = start - start % F32_SUBLANES
        acc = acc + shifted_ref[start % F32_SUBLANES, aligned:aligned + ts] * w_ref[k:k + 1]
    y = _layer_norm(acc + b_ref[...]) * g_ref[...] + beta_ref[...]
    out_ref[0] = (y * jax.nn.sigmoid(y)).astype(BF16)


def conformer_conv(hc, conv_w, conv_b, ln_g, ln_b):
    bx, t, ch = hc.shape
    ts = min(512, t)
    per = ts // CONV_HALO
    last = t // CONV_HALO - 1
    row = lambda v: v.reshape(1, ch)
    const = lambda b, i: (0, 0)
    return pl.pallas_call(
        functools.partial(_conv_kernel, ts=ts),
        grid=(bx, t // ts),
        in_specs=[
            pl.BlockSpec((1, CONV_HALO, ch), lambda b, i: (b, jnp.maximum(i * per - 1, 0), 0)),
            pl.BlockSpec((1, ts, ch), lambda b, i: (b, i, 0)),
            pl.BlockSpec((1, CONV_HALO, ch), lambda b, i: (b, jnp.minimum((i + 1) * per, last), 0)),
            pl.BlockSpec((C_CONV_WIDTH, ch), const),
            pl.BlockSpec((1, ch), const), pl.BlockSpec((1, ch), const), pl.BlockSpec((1, ch), const),
        ],
        out_specs=pl.BlockSpec((1, ts, ch), lambda b, i: (b, i, 0)),
        out_shape=jax.ShapeDtypeStruct((bx, t, ch), BF16),
        scratch_shapes=[pltpu.VMEM((F32_SUBLANES, ts + CONV_SHIFT_SPAN, ch), F32)],
        compiler_params=_params("parallel", "parallel"),
        name="conformer_conv",
    )(hc, hc, hc, conv_w, row(conv_b), row(ln_g), row(ln_b))


OUTPROJ_CHUNK_ROWS = 128


def _outproj_kernel(oa_ref, ob_ref, oc_ref, x_ref, g1_ref, sh_ref, sc_ref, w_ref, lng_ref, lnb_ref,
                    wr_hi_ref, wr_lo_ref, x1_ref, h2_ref, afft_ref, aff_ref, *, alpha):
    tm, d = x_ref.shape[1], x_ref.shape[2]
    rc = min(OUTPROJ_CHUNK_ROWS, tm)
    chunks = [slice(k * rc, (k + 1) * rc) for k in range(tm // rc)]
    outs = [(_dot(oa_ref[0, r], w_ref[0:A_Q_W])
             + _dot(ob_ref[0, r], w_ref[A_Q_W:A_Q_W + B_W])
             + _dot(oc_ref[0, r], w_ref[A_Q_W + B_W:])) for r in chunks]
    w_hi, w_lo = wr_hi_ref[...], wr_lo_ref[...]
    eye = (lax.broadcasted_iota(jnp.int32, (N_EXPERTS, N_EXPERTS), 0)
           == lax.broadcasted_iota(jnp.int32, (N_EXPERTS, N_EXPERTS), 1)).astype(BF16)
    rows_per_token = d // LANES
    ys = [_layer_norm(alpha * x_ref[0, r] + g1_ref[0] * o) * lng_ref[...] + lnb_ref[...]
          for r, o in zip(chunks, outs)]
    for r, y in zip(chunks, ys):
        x1_ref[0, r] = y
    h2s = [_layer_norm(y) * (1.0 + sc_ref[0]) + sh_ref[0] for y in ys]
    his = [h2.astype(BF16) for h2 in h2s]
    los = [(h2 - hi.astype(F32)).astype(BF16) for h2, hi in zip(h2s, his)]
    all_logits = [_dot_t(hi, w_hi) + _dot_t(lo, w_hi) + _dot_t(hi, w_lo) for hi, lo in zip(his, los)]
    for k, h2 in enumerate(h2s):
        for j in range(rows_per_token):
            h2_ref[0, pl.ds(k * rc * rows_per_token + j, rc, stride=rows_per_token), :] = (
                h2[:, j * LANES:(j + 1) * LANES])
    for r, logits in zip(chunks, all_logits):
        e_n = jnp.exp(logits - logits.max(axis=1, keepdims=True))
        aff = e_n / e_n.sum(axis=1, keepdims=True)
        aff_ref[0, r] = aff
        aff_t, rest = None, aff
        for _ in range(GATE_PARTS):
            part = rest.astype(BF16)
            term = _dot_t(eye, part)
            aff_t = term if aff_t is None else aff_t + term
            rest = rest - part.astype(F32)
        afft_ref[0, :, r] = aff_t


def out_projection(oa, ob, oc, x, g1, sh2, sc2, w_out_bf16, ln_g, ln_b, wr_hi, wr_lo, alpha):
    bx, t, d = x.shape
    tm = min(1024, t)
    tok = lambda b, i: (b, i, 0)
    per_b = lambda b, i: (b, 0, 0)
    const = lambda b, i: (0, 0)
    vec = pl.BlockSpec((1, d), const)
    return pl.pallas_call(
        functools.partial(_outproj_kernel, alpha=alpha),
        grid=(bx, t // tm),
        in_specs=[
            pl.BlockSpec((1, tm, A_Q_W), tok), pl.BlockSpec((1, tm, B_W), tok),
            pl.BlockSpec((1, tm, C_CHANNELS), tok), pl.BlockSpec((1, tm, d), tok),
            pl.BlockSpec((1, 1, d), per_b), pl.BlockSpec((1, 1, d), per_b), pl.BlockSpec((1, 1, d), per_b),
            pl.BlockSpec(w_out_bf16.shape, const), vec, vec,
            pl.BlockSpec((N_EXPERTS, d), const), pl.BlockSpec((N_EXPERTS, d), const),
        ],
        out_specs=[pl.BlockSpec((1, tm, d), tok), pl.BlockSpec((1, tm * (d // LANES), LANES), tok),
                   pl.BlockSpec((1, N_EXPERTS, tm), lambda b, i: (b, 0, i)),
                   pl.BlockSpec((1, tm, N_EXPERTS), tok)],
        out_shape=[jax.ShapeDtypeStruct((bx, t, d), F32),
                   jax.ShapeDtypeStruct((bx, t * (d // LANES), LANES), F32),
                   jax.ShapeDtypeStruct((bx, N_EXPERTS, t), F32),
                   jax.ShapeDtypeStruct((bx, t, N_EXPERTS), F32)],
        compiler_params=_params("parallel", "parallel"),
        name="out_projection",
    )(oa, ob, oc, x, g1, sh2, sc2, w_out_bf16, ln_g.reshape(1, d), ln_b.reshape(1, d), wr_hi, wr_lo)


def _select_kernel(afft_ref, pos_ref, off_ref, *, cap, n_tok):
    aff = afft_ref[0]

    def bit_step(j, bits):
        cand = bits | (jnp.int32(1) << (30 - j))
        cnt = jnp.sum((aff >= pltpu.bitcast(cand, F32)).astype(jnp.int32), axis=1, keepdims=True)
        return jnp.where(cnt >= cap, cand, bits)

    thr = pltpu.bitcast(lax.fori_loop(0, 31, bit_step, jnp.zeros((N_EXPERTS, 1), jnp.int32)), F32)
    above = (aff > thr).astype(F32)
    tied = (aff == thr).astype(F32)
    need = cap - jnp.sum(above, axis=1, keepdims=True)

    blk = MXU_DEPTH
    blocks = [slice(k * blk, (k + 1) * blk) for k in range(n_tok // blk)]
    r_i = lax.broadcasted_iota(jnp.int32, (blk, blk), 0)
    c_i = lax.broadcasted_iota(jnp.int32, (blk, blk), 1)
    strict_upper = (r_i < c_i).astype(BF16)

    def running(block_sums):
        run, total = [], jnp.zeros((N_EXPERTS, 1), F32)
        for s in block_sums:
            run.append(total)
            total = total + s
        return run

    tied_b = [tied[:, sl] for sl in blocks]
    tied_rank = [_dot(t.astype(BF16), strict_upper) for t in tied_b]
    tied_before = running([t.sum(axis=1, keepdims=True) for t in tied_b])
    sel_b = [above[:, sl] + t * ((before + rank) < need).astype(F32)
             for sl, t, before, rank in zip(blocks, tied_b, tied_before, tied_rank)]
    sel_rank = [_dot(s.astype(BF16), strict_upper) for s in sel_b]
    half_sums = [[s[:, h * OFFSET_BLOCK:(h + 1) * OFFSET_BLOCK].sum(axis=1, keepdims=True)
                  for h in range(blk // OFFSET_BLOCK)] for s in sel_b]
    offs = running([h for hs in half_sums for h in hs])
    per_blk = blk // OFFSET_BLOCK
    for k, (sl, s, rank) in enumerate(zip(blocks, sel_b, sel_rank)):
        pos_ref[0, :, sl] = jnp.where(s > 0.5, (offs[k * per_blk] + rank).astype(jnp.int32), -1)
    off_ref[0] = jnp.concatenate(offs, axis=1).astype(jnp.int32)


def expert_choice_select(aff_t, cap):
    bx, n_e, t = aff_t.shape
    n_tb = t // OFFSET_BLOCK
    return pl.pallas_call(
        functools.partial(_select_kernel, cap=cap, n_tok=t),
        grid=(bx,),
        in_specs=[pl.BlockSpec((1, n_e, t), lambda b: (b, 0, 0))],
        out_specs=[pl.BlockSpec((1, n_e, t), lambda b: (b, 0, 0)),
                   pl.BlockSpec((1, n_e, n_tb), lambda b: (b, 0, 0))],
        out_shape=[jax.ShapeDtypeStruct((bx, n_e, t), jnp.int32),
                   jax.ShapeDtypeStruct((bx, n_e, n_tb), jnp.int32)],
        compiler_params=_params("parallel"),
        name="expert_choice_select",
    )(aff_t)


GATE_PARTS = 3
TOKEN_LANE = GATE_PARTS * N_EXPERTS


def _slot_table_kernel(off_ref, pos_ref, aff_ref, tbl_ref, *, n_off, blocks_per_step):
    b, eg, kc = pl.program_id(0), pl.program_id(1), pl.program_id(2)
    epg = EXPERTS_PER_GATHER_STEP

    @pl.when(kc == 0)
    def _():
        tbl_ref[...] = jnp.zeros_like(tbl_ref)

    slot = lax.broadcasted_iota(jnp.int32, (GATHER_WINDOW, 1), 0)
    slot_2d = lax.broadcasted_iota(jnp.int32, (GATHER_WINDOW, GATHER_BLOCK), 0)
    lane = lax.broadcasted_iota(jnp.int32, (GATHER_BLOCK, LANES), 1)
    out_lane = lax.broadcasted_iota(jnp.int32, (GATHER_WINDOW, LANES), 1)
    local_token = lax.broadcasted_iota(jnp.int32, (GATHER_BLOCK, LANES), 0).astype(F32)
    place_r = lax.broadcasted_iota(jnp.int32, (N_EXPERTS, LANES), 0)
    place_c = lax.broadcasted_iota(jnp.int32, (N_EXPERTS, LANES), 1)
    def payload_of(kk):
        payload = jnp.where(lane == TOKEN_LANE, local_token, 0.0)
        rest = aff_ref[0, kk * GATHER_BLOCK:(kk + 1) * GATHER_BLOCK]
        for k in range(GATE_PARTS):
            part = rest.astype(BF16)
            payload = payload + _dot(part, (place_c == place_r + k * N_EXPERTS).astype(BF16))
            rest = rest - part.astype(F32)
        return payload.astype(BF16)

    payloads = [payload_of(kk) for kk in range(blocks_per_step)]
    pending = []
    for kk in range(blocks_per_step):
        kb = kc * blocks_per_step + kk
        first_token = jnp.where(out_lane == TOKEN_LANE, (kb * GATHER_BLOCK).astype(F32), 0.0)
        for ee in range(epg):
            off = off_ref[(b * N_EXPERTS + eg * epg + ee) * n_off + kb * (GATHER_BLOCK // OFFSET_BLOCK)]
            base = pl.multiple_of((off >> 4) << 4, WINDOW_ALIGN)
            pos_row = pos_ref[0, ee:ee + 1, kk * GATHER_BLOCK:(kk + 1) * GATHER_BLOCK]
            onehot = (pos_row - base == slot_2d).astype(BF16)
            pending.append((ee, off, base, _dot(onehot, payloads[kk]) + first_token))
    for ee, off, base, gathered in pending:
        win = pl.ds(base, GATHER_WINDOW)
        tbl_ref[0, ee, win, :] = jnp.where(slot >= off - base, gathered, tbl_ref[0, ee, win, :])


def slot_table(aff, pos, block_off, cap_pad):
    bx, t, _ = aff.shape
    n_tb = t // GATHER_BLOCK
    blocks_per_step = min(4, n_tb)
    n_steps = n_tb // blocks_per_step
    epg = EXPERTS_PER_GATHER_STEP
    tokens = blocks_per_step * GATHER_BLOCK
    grid_spec = pltpu.PrefetchScalarGridSpec(
        num_scalar_prefetch=1,
        grid=(bx, N_EXPERTS // epg, n_steps),
        in_specs=[
            pl.BlockSpec((1, epg, tokens), lambda b, g, k, s: (b, g, k)),
            pl.BlockSpec((1, tokens, N_EXPERTS), lambda b, g, k, s: (b, k, 0)),
        ],
        out_specs=pl.BlockSpec((1, epg, cap_pad, LANES), lambda b, g, k, s: (b, g, 0, 0)),
    )
    return pl.pallas_call(
        functools.partial(_slot_table_kernel, n_off=t // OFFSET_BLOCK, blocks_per_step=blocks_per_step),
        grid_spec=grid_spec,
        out_shape=jax.ShapeDtypeStruct((bx, N_EXPERTS, cap_pad, LANES), F32),
        compiler_params=_params("parallel", "parallel", "arbitrary"),
        name="slot_table",
    )(block_off.reshape(-1), pos, aff)


GATHER_UNROLL = 16


def _row_gather_kernel(idx_ref, src_ref, xs_ref, tile_ref, *, cap, rows_per_token, chunk_stride):
    b, e = pl.program_id(0), pl.program_id(1)
    first = (b * N_EXPERTS + e) * cap

    def group(g, carry):
        for u in range(GATHER_UNROLL):
            s = g * GATHER_UNROLL + u
            row = pl.multiple_of(idx_ref[first + s], rows_per_token)
            tile_ref[pl.ds(s, rows_per_token, stride=chunk_stride), :] = src_ref[0, pl.ds(row, rows_per_token), :]
        return carry

    lax.fori_loop(0, cap // GATHER_UNROLL, group, 0)
    for j in range(rows_per_token):
        xs_ref[0, 0, :, j * LANES:(j + 1) * LANES] = tile_ref[j * chunk_stride:j * chunk_stride + cap, :].astype(BF16)


def gather_rows(h2_rows, token_row, cap, d):
    bx, n_rows, _ = h2_rows.shape
    rows_per_token = d // LANES
    chunk_stride = cap + 8
    grid_spec = pltpu.PrefetchScalarGridSpec(
        num_scalar_prefetch=1,
        grid=(bx, N_EXPERTS),
        in_specs=[pl.BlockSpec((1, n_rows, LANES), lambda b, e, s: (b, 0, 0), pipeline_mode=pl.Buffered(1))],
        out_specs=pl.BlockSpec((1, 1, cap, d), lambda b, e, s: (b, e, 0, 0)),
        scratch_shapes=[pltpu.VMEM((rows_per_token * chunk_stride, LANES), F32)],
    )
    return pl.pallas_call(
        functools.partial(_row_gather_kernel, cap=cap, rows_per_token=rows_per_token, chunk_stride=chunk_stride),
        grid_spec=grid_spec,
        out_shape=jax.ShapeDtypeStruct((bx, N_EXPERTS, cap, d), BF16),
        compiler_params=_params("parallel", "arbitrary"),
        name="gather_rows",
    )(token_row, h2_rows)


def _ffn_kernel(xs_ref, gs_ref, wg_ref, wu_ref, wd_ref, ye_ref, wg_bf, wu_bf, wd_bf, *, cap, row_tile):
    @pl.when(pl.program_id(1) == 0)
    def _():
        wg_bf[...] = wg_ref[0, 0].astype(BF16)
        wu_bf[...] = wu_ref[0, 0].astype(BF16)
        wd_bf[...] = wd_ref[0, 0].astype(BF16)

    n_b = xs_ref.shape[0]
    if n_b == 1:
        tiles = [[(0, r0, row_tile)] for r0 in range(0, cap, row_tile)]
    else:
        tiles = [[(bb, 0, cap) for bb in range(n_b)]]
    for tile in tiles:
        x = jnp.concatenate([xs_ref[bb, 0, r0:r0 + n] for bb, r0, n in tile], axis=0)
        terms = jnp.concatenate([gs_ref[bb, 0, r0:r0 + n, :] for bb, r0, n in tile], axis=0)
        lane = lax.broadcasted_iota(jnp.int32, terms.shape, 1)
        own = ((lane & (N_EXPERTS - 1)) == pl.program_id(0)) & (lane < TOKEN_LANE)
        terms = jnp.where(own, terms, 0.0)
        gate = _dot(x, wg_bf[...])
        up = _dot(x, wu_bf[...])
        hid = (gate * jax.nn.sigmoid(gate) * up).astype(BF16)
        g = jnp.sum(terms, axis=1, keepdims=True)
        ye = _dot(hid, wd_bf[...]) * g
        rows_per_slot = ye.shape[1] // LANES
        row = 0
        for bb, r0, n in tile:
            for j in range(rows_per_slot):
                ye_ref[bb, 0, pl.ds(r0 * rows_per_slot + j, n, stride=rows_per_slot), :] = (
                    ye[row:row + n, j * LANES:(j + 1) * LANES])
            row += n


def expert_ffn(xs, gs, wg, wu, wd, layer):
    bx, n_e, cap, d = xs.shape
    ff = wg.shape[-1]
    slot_rows = cap * (d // LANES)
    max_rows = 512
    row_tile = min(max_rows, cap)
    n_b = bx if bx * cap <= max_rows else 1
    return pl.pallas_call(
        functools.partial(_ffn_kernel, cap=cap, row_tile=row_tile),
        grid=(n_e, bx // n_b),
        in_specs=[
            pl.BlockSpec((n_b, 1, cap, d), lambda e, b: (b, e, 0, 0)),
            pl.BlockSpec((n_b, 1, cap, LANES), lambda e, b: (b, e, 0, 0)),
            pl.BlockSpec((1, 1, d, ff), lambda e, b: (layer, e, 0, 0)),
            pl.BlockSpec((1, 1, d, ff), lambda e, b: (layer, e, 0, 0)),
            pl.BlockSpec((1, 1, ff, d), lambda e, b: (layer, e, 0, 0)),
        ],
        out_specs=pl.BlockSpec((n_b, 1, slot_rows, LANES), lambda e, b: (b, e, 0, 0)),
        out_shape=jax.ShapeDtypeStruct((bx, n_e, slot_rows, LANES), F32),
        scratch_shapes=[pltpu.VMEM((d, ff), BF16), pltpu.VMEM((d, ff), BF16), pltpu.VMEM((ff, d), BF16)],
        compiler_params=_params("parallel", "arbitrary"),
        name="expert_ffn",
    )(xs, gs, wg, wu, wd)


SCATTER_UNROLL = 16


def _scatter_norm_kernel(idx_ref, ye_ref, x1_ref, g2_ref, lng_ref, lnb_ref, out_ref, acc_ref, *,
                         cap, rows_per_token, alpha):
    b, step = pl.program_id(0), pl.program_id(1)

    @pl.when(step == 0)
    def _():
        acc_ref[...] = jnp.zeros_like(acc_ref)

    @pl.when(step < N_EXPERTS)
    def _():
        first = (b * N_EXPERTS + step) * cap

        def group(g, carry):
            rows, sums = [], []
            for u in range(SCATTER_UNROLL):
                s = g * SCATTER_UNROLL + u
                row = pl.multiple_of(idx_ref[first + s], rows_per_token)
                src = pl.multiple_of(s * rows_per_token, rows_per_token)
                rows.append(row)
                sums.append(acc_ref[pl.ds(row, rows_per_token), :] + ye_ref[0, 0, pl.ds(src, rows_per_token), :])
            for row, total in zip(rows, sums):
                acc_ref[pl.ds(row, rows_per_token), :] = total
            return carry

        lax.fori_loop(0, cap // SCATTER_UNROLL, group, 0)

    @pl.when(step >= N_EXPERTS)
    def _():
        tm = x1_ref.shape[1]
        tile_row = pl.multiple_of((step - N_EXPERTS) * (tm * rows_per_token), rows_per_token)
        moe = jnp.concatenate([acc_ref[pl.ds(tile_row + j, tm, stride=rows_per_token), :]
                               for j in range(rows_per_token)], axis=1)
        y = alpha * x1_ref[0] + g2_ref[0] * moe
        out_ref[0] = _layer_norm(y) * lng_ref[...] + lnb_ref[...]


def scatter_and_norm(ye_rows, token_row, x1, g2, ln_g, ln_b, alpha, cap):
    bx, t, d = x1.shape
    n_e, slot_rows = ye_rows.shape[1], ye_rows.shape[2]
    rows_per_token = d // LANES
    tm = min(512, t)
    tile = lambda b, i, s: (b, jnp.maximum(i - n_e, 0), 0)
    const = lambda b, i, s: (0, 0)
    grid_spec = pltpu.PrefetchScalarGridSpec(
        num_scalar_prefetch=1,
        grid=(bx, n_e + t // tm),
        in_specs=[
            pl.BlockSpec((1, 1, slot_rows, LANES), lambda b, i, s: (b, jnp.minimum(i, n_e - 1), 0, 0)),
            pl.BlockSpec((1, tm, d), tile),
            pl.BlockSpec((1, 1, d), lambda b, i, s: (b, 0, 0)),
            pl.BlockSpec((1, d), const), pl.BlockSpec((1, d), const),
        ],
        out_specs=pl.BlockSpec((1, tm, d), tile),
        scratch_shapes=[pltpu.VMEM((t * rows_per_token, LANES), F32)],
    )
    return pl.pallas_call(
        functools.partial(_scatter_norm_kernel, cap=cap, rows_per_token=rows_per_token, alpha=alpha),
        grid_spec=grid_spec,
        out_shape=jax.ShapeDtypeStruct((bx, t, d), F32),
        compiler_params=_params("parallel", "arbitrary"),
        name="scatter_and_norm",
    )(token_row, ye_rows, x1, g2, ln_g.reshape(1, d), ln_b.reshape(1, d))


def _split_bf16(w):
    hi = w.astype(BF16)
    return hi, (w - hi.astype(F32)).astype(BF16)


def _mixer_tail(oa, ob, oc, x, mod, lw, alpha):
    g1, sh2, sc2, g2 = mod
    t = x.shape[1]
    cap = EC_CAPACITY * t // N_EXPERTS
    cap_pad = cap + SLOT_PAD
    x1, h2_rows, aff_t, aff = out_projection(oa, ob, oc, x, g1, sh2, sc2, lw["w_out"], lw["ln1_g"], lw["ln1_b"],
                                        lw["wr_hi"], lw["wr_lo"], alpha)
    pos, block_off = expert_choice_select(aff_t, cap)
    table = slot_table(aff, pos, block_off, cap_pad)
    token_row = (table[:, :, :cap, TOKEN_LANE].astype(jnp.int32) * (x.shape[2] // LANES)).reshape(-1)
    xs = gather_rows(h2_rows, token_row, cap, x.shape[2])
    ye_rows = expert_ffn(xs, table, lw["w_gate"], lw["w_up"], lw["w_down"], lw["layer"])
    return scatter_and_norm(ye_rows, token_row, x1, g2, lw["ln2_g"], lw["ln2_b"], alpha, cap)


def kernel(x, c, ctx, c_ctx, w_mod, b_mod, w_in, a_sink, nat_bias, conv_w, conv_b, conv_ln_g, conv_ln_b,
           w_out, ln1_g, ln1_b, w_router, w_gate, w_up, w_down, ln2_g, ln2_b):
    bsz, n_lat, d = x.shape
    depth = w_mod.shape[0]
    alpha = (2 * depth) ** 0.25
    cos_t, sin_t = rope_tables(n_lat)

    cond = jnp.concatenate([c, c_ctx[None, :], jnp.zeros((8 - bsz - 1, d), F32)], axis=0)
    mods = adaln_all(cond, w_mod, b_mod)

    for l in range(depth):
        last = l == depth - 1
        wr_hi, wr_lo = _split_bf16(w_router[l].T)
        lw = dict(w_out=w_out[l].astype(BF16), ln1_g=ln1_g[l], ln1_b=ln1_b[l], wr_hi=wr_hi, wr_lo=wr_lo,
                  w_gate=w_gate, w_up=w_up, w_down=w_down, layer=l,
                  ln2_g=ln2_g[l], ln2_b=ln2_b[l])
        w_in_l = w_in[l].astype(BF16)
        lat = [mods[l, :bsz, k * d:(k + 1) * d][:, None, :] for k in range(N_MOD)]
        cm = [jnp.broadcast_to(mods[l, bsz, k * d:(k + 1) * d][None, None, :], (bsz, 1, d))
              for k in range(N_MOD)]
        conv_args = (conv_w[l], conv_b[l], conv_ln_g[l], conv_ln_b[l])

        qa_c, ka_c, va_c, qb_c, kb_c, vb_c, hc_c = in_projection(ctx, cm[0], cm[1], w_in_l, cos_t, sin_t, rope=False)
        if not last:
            oa_c, ob_c = context_attention(qa_c, qb_c, ka_c, va_c, kb_c, vb_c, a_sink[l])
            oc_c = conformer_conv(hc_c, *conv_args)
            ctx_new = _mixer_tail(oa_c, ob_c, oc_c, ctx, (cm[2], cm[3], cm[4], cm[5]), lw, alpha)

        qa, ka, va, qb, kb, vb, hc = in_projection(x, lat[0], lat[1], w_in_l, cos_t, sin_t, rope=True)
        oa = window_attention(qa, ka, va, ka_c, va_c, a_sink[l])
        ob = neighbourhood_attention(qb, kb, vb, kb_c, vb_c, nat_bias[l])
        oc = conformer_conv(hc, *conv_args)
        x = _mixer_tail(oa, ob, oc, x, (lat[2], lat[3], lat[4], lat[5]), lw, alpha)
        if not last:
            ctx = ctx_new
    return x
```

```python
import functools

import numpy as np
import jax
import jax.numpy as jnp
from jax import lax
from jax.experimental import pallas as pl
from jax.experimental.pallas import tpu as pltpu

HEAD_DIM = 64
GRID_W = 64
A_Q_HEADS = 8
A_KV_HEADS = 2
A_GROUP = A_Q_HEADS // A_KV_HEADS
A_WINDOW = 128
B_HEADS = 4
NA_ROWS = 8
NA_COLS = 16
C_CHANNELS = 256
C_CONV_WIDTH = 31
A_Q_W = A_Q_HEADS * HEAD_DIM
A_KV_W = A_KV_HEADS * HEAD_DIM
B_W = B_HEADS * HEAD_DIM
OFF_AK = A_Q_W
OFF_AV = OFF_AK + A_KV_W
OFF_BQ = OFF_AV + A_KV_W
OFF_BK = OFF_BQ + B_W
OFF_BV = OFF_BK + B_W
OFF_C = OFF_BV + B_W
IN_WIDTH = OFF_C + 2 * C_CHANNELS
ROPE_WIDTH = A_Q_W + A_KV_W
N_EXPERTS = 16
EC_CAPACITY = 2
ROPE_BASE = 10000.0
LN_EPS = 1e-6
N_MOD = 6
NEG_INF = -1e30
QK_SCALE = HEAD_DIM ** -0.5

LANES = 128
WINDOW_ALIGN = 16
MXU_DEPTH = 256
OFFSET_BLOCK = 128
GATHER_BLOCK = MXU_DEPTH
GATHER_WINDOW = GATHER_BLOCK + WINDOW_ALIGN
SLOT_PAD = 3 * LANES
EXPERTS_PER_GATHER_STEP = 16
VMEM_LIMIT = 56 * 1024 * 1024

F32 = jnp.float32
BF16 = jnp.bfloat16


def _dot(a, b):
    return jnp.dot(a, b, preferred_element_type=F32)


def _dot_t(a, b):
    return lax.dot_general(a, b, (((1,), (1,)), ((), ())), preferred_element_type=F32)


def _layer_norm(x):
    mu = jnp.mean(x, axis=-1, keepdims=True)
    xc = x - mu
    var = jnp.mean(xc * xc, axis=-1, keepdims=True)
    return xc * lax.rsqrt(var + LN_EPS)


def _params(*sem):
    return pltpu.CompilerParams(dimension_semantics=sem, vmem_limit_bytes=VMEM_LIMIT)


def _mod_kernel(cond_ref, w_ref, b_ref, out_ref):
    cnd = cond_ref[...]
    act = cnd * jax.nn.sigmoid(cnd)
    out_ref[0] = jnp.dot(act, w_ref[0], preferred_element_type=F32,
                         precision=lax.Precision.HIGHEST) + b_ref[0]


def adaln_all(cond, w_mod, b_mod):
    n_layers, d, width = w_mod.shape
    rows = cond.shape[0]
    tn = 1536
    return pl.pallas_call(
        _mod_kernel,
        grid=(n_layers, width // tn),
        in_specs=[
            pl.BlockSpec((rows, d), lambda l, j: (0, 0)),
            pl.BlockSpec((1, d, tn), lambda l, j: (l, 0, j)),
            pl.BlockSpec((1, 1, tn), lambda l, j: (l, 0, j)),
        ],
        out_specs=pl.BlockSpec((1, rows, tn), lambda l, j: (l, 0, j)),
        out_shape=jax.ShapeDtypeStruct((n_layers, rows, width), F32),
        compiler_params=_params("parallel", "parallel"),
        name="adaln",
    )(cond, w_mod, b_mod.reshape(n_layers, 1, width))


def _inproj_kernel(x_ref, sh_ref, sc_ref, w_ref, cos_ref, sin_ref,
                   qa_ref, ka_ref, va_ref, qb_ref, kb_ref, vb_ref, hc_ref, *, rope):
    x = x_ref[0]
    h = _layer_norm(x) * (1.0 + sc_ref[0]) + sh_ref[0]
    u = _dot(h.astype(BF16), w_ref[...])

    def rotated(col):
        xq = u[:, col:col + LANES]
        if not rope:
            return xq
        lane = lax.broadcasted_iota(jnp.int32, xq.shape, 1)
        first = (lane & (HEAD_DIM // 2 - 1)) < (HEAD_DIM // 4)
        partner = jnp.where(first, pltpu.roll(xq, LANES - HEAD_DIM // 4, 1),
                            pltpu.roll(xq, HEAD_DIM // 4, 1))
        return xq * cos_ref[...] + partner * sin_ref[...]

    rot = [rotated(col) for col in range(0, ROPE_WIDTH, LANES)]
    n_q = A_Q_W // LANES
    qa_ref[0] = (jnp.concatenate(rot[:n_q], axis=1) * QK_SCALE).astype(BF16)
    ka_ref[0] = jnp.concatenate(rot[n_q:], axis=1).astype(BF16)
    va_ref[0] = u[:, OFF_AV:OFF_BQ].astype(BF16)
    qb_ref[0] = (u[:, OFF_BQ:OFF_BK] * QK_SCALE).astype(BF16)
    kb_ref[0] = u[:, OFF_BK:OFF_BV].astype(BF16)
    vb_ref[0] = u[:, OFF_BV:OFF_C].astype(BF16)
    a = u[:, OFF_C:OFF_C + C_CHANNELS]
    gate = u[:, OFF_C + C_CHANNELS:]
    hc_ref[0] = a * jax.nn.sigmoid(gate)


def in_projection(x, shift, scale, w_in_bf16, cos_t, sin_t, *, rope):
    bx, t, d = x.shape
    tm = min(1024, t)
    widths = (A_Q_W, A_KV_W, A_KV_W, B_W, B_W, B_W, C_CHANNELS)
    dtypes = (BF16,) * 6 + (F32,)
    tok = lambda b, i: (b, i, 0)
    per_b = lambda b, i: (b, 0, 0)
    return pl.pallas_call(
        functools.partial(_inproj_kernel, rope=rope),
        grid=(bx, t // tm),
        in_specs=[
            pl.BlockSpec((1, tm, d), tok),
            pl.BlockSpec((1, 1, d), per_b),
            pl.BlockSpec((1, 1, d), per_b),
            pl.BlockSpec((d, IN_WIDTH), lambda b, i: (0, 0)),
            pl.BlockSpec((tm, LANES), lambda b, i: (i, 0)),
            pl.BlockSpec((tm, LANES), lambda b, i: (i, 0)),
        ],
        out_specs=[pl.BlockSpec((1, tm, w), tok) for w in widths],
        out_shape=[jax.ShapeDtypeStruct((bx, t, w), dt) for w, dt in zip(widths, dtypes)],
        compiler_params=_params("parallel", "parallel"),
        name="in_projection",
    )(x, shift, scale, w_in_bf16, cos_t, sin_t)


def rope_tables(n_tokens):
    t = jnp.arange(n_tokens, dtype=jnp.int32)
    row = (t // GRID_W).astype(F32)[:, None]
    col = (t % GRID_W).astype(F32)[:, None]
    n_freq = HEAD_DIM // 4
    inv_freq = ROPE_BASE ** (-jnp.arange(n_freq, dtype=F32) / n_freq)
    ang_r = row * inv_freq
    ang_c = col * inv_freq
    cos_h = jnp.concatenate([jnp.cos(ang_r), jnp.cos(ang_r), jnp.cos(ang_c), jnp.cos(ang_c)], axis=1)
    sin_h = jnp.concatenate([-jnp.sin(ang_r), jnp.sin(ang_r), -jnp.sin(ang_c), jnp.sin(ang_c)], axis=1)
    reps = LANES // HEAD_DIM
    return jnp.tile(cos_h, (1, reps)), jnp.tile(sin_h, (1, reps))


def _with_ones(v):
    return jnp.concatenate([v, jnp.ones_like(v)], axis=1)


def _attend(score_parts, values, sink=None):
    m = score_parts[0].max(axis=-1, keepdims=True)
    for s in score_parts[1:]:
        m = jnp.maximum(m, s.max(axis=-1, keepdims=True))
    if sink is not None:
        m = jnp.maximum(m, sink)
    acc = None
    for s, v in zip(score_parts, values):
        term = _dot(jnp.exp((s - m).astype(BF16)), v)
        acc = term if acc is None else acc + term
    den = acc[:, HEAD_DIM:HEAD_DIM + 1]
    if sink is not None:
        den = den + jnp.exp(sink - m)
    return acc[:, :HEAD_DIM] / den


def _attn_a_kernel(sink_ref, q_ref, kp_ref, kc_ref, kn_ref, vp_ref, vc_ref, vn_ref,
                   kctx_ref, vctx_ref, out_ref, *, n_lat, tq):
    i = pl.program_id(1)
    k_win = jnp.concatenate([kp_ref[0], kc_ref[0], kn_ref[0]], axis=0)
    v_win = jnp.concatenate([vp_ref[0], vc_ref[0], vn_ref[0]], axis=0)
    kctx = kctx_ref[0]
    vctx = vctx_ref[0]
    sub = A_WINDOW
    span = 3 * A_WINDOW
    rows = A_GROUP * sub
    q_onehot = ((lax.broadcasted_iota(jnp.int32, (rows, sub), 0) & (sub - 1))
                == lax.broadcasted_iota(jnp.int32, (rows, sub), 1)).astype(BF16)
    key_i = lax.broadcasted_iota(jnp.int32, (span, sub), 0)
    qry_i = lax.broadcasted_iota(jnp.int32, (span, sub), 1)
    rel = key_i - A_WINDOW - qry_i
    in_band = (rel <= A_WINDOW) & (rel >= -A_WINDOW)
    group_of_row = lax.broadcasted_iota(jnp.int32, (rows, 1), 0) >> (sub.bit_length() - 1)
    sinks = []
    for hk in range(A_KV_HEADS):
        sink = jnp.zeros((rows, 1), F32)
        for g in range(A_GROUP):
            sink = jnp.where(group_of_row == g, sink_ref[hk * A_GROUP + g], sink)
        sinks.append(sink)
    vctx_ext = [_with_ones(vctx[:, hk * HEAD_DIM:(hk + 1) * HEAD_DIM]) for hk in range(A_KV_HEADS)]
    v_ext = [_with_ones(v_win[:, hk * HEAD_DIM:(hk + 1) * HEAD_DIM]) for hk in range(A_KV_HEADS)]
    def mask_columns(j):
        kpos = i * tq + j * sub - A_WINDOW + key_i
        valid = in_band & (kpos >= 0) & (kpos < n_lat)
        return jnp.where(valid, 0.0, NEG_INF).astype(BF16)

    masks = [mask_columns(j) for j in range(tq // sub)]

    def scores(j, hk):
        sl = slice(hk * HEAD_DIM, (hk + 1) * HEAD_DIM)
        q_rows = q_ref[0, j * sub:(j + 1) * sub]
        q = jnp.concatenate([q_rows[:, h * HEAD_DIM:(h + 1) * HEAD_DIM]
                             for h in range(hk * A_GROUP, (hk + 1) * A_GROUP)], axis=0)
        q_aug = jnp.concatenate([q_onehot, q], axis=1)
        k_aug = jnp.concatenate([masks[j], k_win[j * sub:j * sub + span, sl]], axis=1)
        return [_dot_t(q_aug, k_aug), _dot_t(q, kctx[:, sl])]

    units = [(j, hk) for j in range(tq // sub) for hk in range(A_KV_HEADS)]
    all_scores = [scores(j, hk) for j, hk in units]
    for (j, hk), s in zip(units, all_scores):
        v_sub = v_ext[hk][j * sub:j * sub + span]
        o = _attend(s, [v_sub, vctx_ext[hk]], sink=sinks[hk]).astype(BF16)
        for g in range(A_GROUP):
            h = hk * A_GROUP + g
            out_ref[0, j * sub:(j + 1) * sub, h * HEAD_DIM:(h + 1) * HEAD_DIM] = o[g * sub:(g + 1) * sub]


def window_attention(qa, ka, va, kc_a, vc_a, sink):
    bsz, n_lat, _ = qa.shape
    n_ctx = kc_a.shape[1]
    tq = min(1024, n_lat)
    w = A_WINDOW
    per = tq // w
    last = n_lat // w - 1
    prev = lambda b, i, s: (b, jnp.maximum(i * per - 1, 0), 0)
    cur = lambda b, i, s: (b, i, 0)
    nxt = lambda b, i, s: (b, jnp.minimum((i + 1) * per, last), 0)
    ctx = lambda b, i, s: (b, 0, 0)
    kv_specs = [pl.BlockSpec((1, w, A_KV_W), prev), pl.BlockSpec((1, tq, A_KV_W), cur),
                pl.BlockSpec((1, w, A_KV_W), nxt)]
    grid_spec = pltpu.PrefetchScalarGridSpec(
        num_scalar_prefetch=1,
        grid=(bsz, n_lat // tq),
        in_specs=[pl.BlockSpec((1, tq, A_Q_W), cur)] + kv_specs + kv_specs + [
            pl.BlockSpec((1, n_ctx, A_KV_W), ctx), pl.BlockSpec((1, n_ctx, A_KV_W), ctx)],
        out_specs=pl.BlockSpec((1, tq, A_Q_W), cur),
    )
    return pl.pallas_call(
        functools.partial(_attn_a_kernel, n_lat=n_lat, tq=tq),
        grid_spec=grid_spec,
        out_shape=jax.ShapeDtypeStruct((bsz, n_lat, A_Q_W), BF16),
        compiler_params=_params("parallel", "parallel"),
        name="window_attention",
    )(sink, qa, ka, ka, ka, va, va, va, kc_a, vc_a)


NB_Q_ROWS = 4
NB_BLOCKS_PER_STEP = 4


def _attn_b_kernel(q_ref, *refs):
    n_kv = NB_BLOCKS_PER_STEP + 2
    k_refs, v_refs = refs[:n_kv], refs[n_kv:2 * n_kv]
    kctx_ref, vctx_ref = refs[2 * n_kv:2 * n_kv + 2]
    bias_refs = refs[2 * n_kv + 2:-1]
    out_ref = refs[-1]
    tb = k_refs[0].shape[1]
    k_all = jnp.concatenate([r[0] for r in k_refs], axis=0)
    v_all = jnp.concatenate([r[0] for r in v_refs], axis=0)
    kctx = kctx_ref[0]
    vctx = vctx_ref[0]
    heads = [slice(h * HEAD_DIM, (h + 1) * HEAD_DIM) for h in range(B_HEADS)]
    units = [(sb, h) for sb in range(NB_BLOCKS_PER_STEP) for h in range(B_HEADS)]

    def scores(sb, h):
        q = q_ref[0, sb * tb:(sb + 1) * tb, heads[h]]
        return [_dot_t(q, k_all[sb * tb:(sb + 3) * tb, heads[h]]) + bias_refs[sb][0, h],
                _dot_t(q, kctx[:, heads[h]])]

    all_scores = [scores(sb, h) for sb, h in units]
    v_ext = [_with_ones(v_all[:, sl]) for sl in heads]
    vctx_ext = [_with_ones(vctx[:, sl]) for sl in heads]
    for (sb, h), s in zip(units, all_scores):
        o = _attend(s, [v_ext[h][sb * tb:(sb + 3) * tb], vctx_ext[h]])
        out_ref[0, sb * tb:(sb + 1) * tb, heads[h]] = o.astype(BF16)


def neighbourhood_bias(rel_bias, n_lat):
    rows = n_lat // GRID_W
    kr_n = min(NA_ROWS, rows)
    n_blocks = rows // NB_Q_ROWS
    n_heads, n_dr, n_dc = rel_bias.shape
    cols = np.arange(GRID_W)
    c_start = np.clip(cols - NA_COLS // 2, 0, GRID_W - NA_COLS)
    col_ok = (cols[None, :] >= c_start[:, None]) & (cols[None, :] < c_start[:, None] + NA_COLS)
    dc = np.clip(cols[None, :] - cols[:, None], -(NA_COLS - 1), NA_COLS - 1) + NA_COLS - 1
    pick_dc = (dc.reshape(-1)[None, :] == np.arange(n_dc)[:, None]).astype(np.float32)
    toeplitz = jnp.dot(rel_bias.reshape(n_heads * n_dr, n_dc), pick_dc, precision=lax.Precision.HIGHEST)
    toeplitz = jnp.where(col_ok.reshape(-1), toeplitz, NEG_INF).reshape(n_heads, n_dr, GRID_W, GRID_W)
    q_rl = np.arange(NB_Q_ROWS)
    k_rl = np.arange(3 * NB_Q_ROWS)
    row_ok, dr = [], []
    for j in sorted({0, min(1, n_blocks - 1), n_blocks - 1}):
        r = NB_Q_ROWS * j + q_rl
        kr = NB_Q_ROWS * (j - 1) + k_rl
        r_start = np.clip(r - kr_n // 2, 0, rows - kr_n)
        ok = (kr[None, :] >= r_start[:, None]) & (kr[None, :] < r_start[:, None] + kr_n)
        row_ok.append(ok & (kr[None, :] >= 0) & (kr[None, :] < rows))
        dr.append(np.clip(kr[None, :] - r[:, None] + NA_ROWS - 1, 0, n_dr - 1))
    row_ok = np.stack(row_ok)
    dr = np.stack(dr)
    n_var = dr.shape[0]
    tq, tk = NB_Q_ROWS * GRID_W, 3 * NB_Q_ROWS * GRID_W
    table = pl.pallas_call(
        functools.partial(_bias_table_kernel, dr=dr, row_ok=row_ok),
        grid=(n_heads,),
        in_specs=[pl.BlockSpec((1, n_dr, GRID_W, GRID_W), lambda h: (h, 0, 0, 0))],
        out_specs=pl.BlockSpec((n_var, 1, tq, tk), lambda h: (0, h, 0, 0)),
        out_shape=jax.ShapeDtypeStruct((n_var, n_heads, tq, tk), F32),
        compiler_params=_params("parallel"),
        name="neighbourhood_bias_table",
    )(toeplitz)
    return table, n_blocks


def _bias_table_kernel(toe_ref, out_ref, *, dr, row_ok):
    n_var, n_q, n_k = dr.shape
    for v in range(n_var):
        for rl in range(n_q):
            for krl in range(n_k):
                tile = toe_ref[0, int(dr[v, rl, krl])] if row_ok[v, rl, krl] else jnp.full(
                    (GRID_W, GRID_W), NEG_INF, F32)
                out_ref[v, 0, rl * GRID_W:(rl + 1) * GRID_W, krl * GRID_W:(krl + 1) * GRID_W] = tile


def neighbourhood_attention(qb, kb, vb, kc_b, vc_b, rel_bias):
    bsz, n_lat, _ = qb.shape
    n_ctx = kc_b.shape[1]
    table, n_blocks = neighbourhood_bias(rel_bias, n_lat)
    assert n_blocks % NB_BLOCKS_PER_STEP == 0
    n_var = table.shape[0]
    tb = NB_Q_ROWS * GRID_W
    last = n_blocks - 1
    cur = lambda b, j: (b, j, 0)
    ctx = lambda b, j: (b, 0, 0)

    def key_block(off):
        return lambda b, j: (b, jnp.clip(NB_BLOCKS_PER_STEP * j + off, 0, last), 0)

    def variant(sb):
        def index(b, j):
            g = NB_BLOCKS_PER_STEP * j + sb
            return (jnp.where(g == 0, 0, jnp.where(g == last, n_var - 1, min(1, n_var - 1))), 0, 0, 0)
        return index

    kv_specs = [pl.BlockSpec((1, tb, B_W), key_block(off)) for off in range(-1, NB_BLOCKS_PER_STEP + 1)]
    return pl.pallas_call(
        _attn_b_kernel,
        grid=(bsz, n_blocks // NB_BLOCKS_PER_STEP),
        in_specs=[pl.BlockSpec((1, NB_BLOCKS_PER_STEP * tb, B_W), cur)] + kv_specs + kv_specs + [
            pl.BlockSpec((1, n_ctx, B_W), ctx), pl.BlockSpec((1, n_ctx, B_W), ctx)] + [
            pl.BlockSpec((1, B_HEADS, tb, 3 * tb), variant(sb)) for sb in range(NB_BLOCKS_PER_STEP)],
        out_specs=pl.BlockSpec((1, NB_BLOCKS_PER_STEP * tb, B_W), cur),
        out_shape=jax.ShapeDtypeStruct((bsz, n_lat, B_W), BF16),
        compiler_params=_params("parallel", "parallel"),
        name="neighbourhood_attention",
    )(qb, *([kb] * len(kv_specs)), *([vb] * len(kv_specs)), kc_b, vc_b, *([table] * NB_BLOCKS_PER_STEP))


def _ctx_attn_kernel(sink_ref, qa_ref, qb_ref, ka_ref, va_ref, kb_ref, vb_ref, oa_ref, ob_ref):
    qa, qb = qa_ref[0], qb_ref[0]
    ka, va, kb, vb = ka_ref[0], va_ref[0], kb_ref[0], vb_ref[0]
    for hq in range(A_Q_HEADS):
        sl = slice(hq * HEAD_DIM, (hq + 1) * HEAD_DIM)
        hk = hq // A_GROUP
        kv = slice(hk * HEAD_DIM, (hk + 1) * HEAD_DIM)
        q = qa[:, sl]
        o = _attend([_dot_t(q, ka[:, kv])], [_with_ones(va[:, kv])], sink=sink_ref[hq])
        oa_ref[0, :, sl] = o.astype(BF16)
    for h in range(B_HEADS):
        sl = slice(h * HEAD_DIM, (h + 1) * HEAD_DIM)
        q = qb[:, sl]
        o = _attend([_dot_t(q, kb[:, sl])], [_with_ones(vb[:, sl])])
        ob_ref[0, :, sl] = o.astype(BF16)


def context_attention(qa, qb, ka, va, kb, vb, sink):
    bsz, n_ctx, _ = qa.shape
    blk = lambda w: pl.BlockSpec((1, n_ctx, w), lambda b, s: (b, 0, 0))
    grid_spec = pltpu.PrefetchScalarGridSpec(
        num_scalar_prefetch=1,
        grid=(bsz,),
        in_specs=[blk(A_Q_W), blk(B_W), blk(A_KV_W), blk(A_KV_W), blk(B_W), blk(B_W)],
        out_specs=[blk(A_Q_W), blk(B_W)],
    )
    return pl.pallas_call(
        _ctx_attn_kernel,
        grid_spec=grid_spec,
        out_shape=[jax.ShapeDtypeStruct((bsz, n_ctx, A_Q_W), BF16),
                   jax.ShapeDtypeStruct((bsz, n_ctx, B_W), BF16)],
        compiler_params=_params("parallel"),
        name="context_attention",
    )(sink, qa, qb, ka, va, kb, vb)


CONV_HALO = 16
F32_SUBLANES = 8
CONV_SHIFT_SPAN = (CONV_HALO + C_CONV_WIDTH // 2) // F32_SUBLANES * F32_SUBLANES


def _conv_kernel(prev_ref, cur_ref, next_ref, w_ref, b_ref, g_ref, beta_ref, out_ref, shifted_ref, *, ts):
    i = pl.program_id(1)
    n_i = pl.num_programs(1)
    ext = jnp.concatenate([jnp.where(i > 0, prev_ref[0], 0.0), cur_ref[0],
                           jnp.where(i < n_i - 1, next_ref[0], 0.0)], axis=0)
    for r in range(F32_SUBLANES):
        shifted_ref[r] = ext[r:r + ts + CONV_SHIFT_SPAN]
    acc = jnp.zeros((ts, C_CHANNELS), F32)
    for k in range(C_CONV_WIDTH):
        start = CONV_HALO - C_CONV_WIDTH // 2 + k
        aligned = start - start % F32_SUBLANES
        acc = acc + shifted_ref[start % F32_SUBLANES, aligned:aligned + ts] * w_ref[k:k + 1]
    y = _layer_norm(acc + b_ref[...]) * g_ref[...] + beta_ref[...]
    out_ref[0] = (y * jax.nn.sigmoid(y)).astype(BF16)


def conformer_conv(hc, conv_w, conv_b, ln_g, ln_b):
    bx, t, ch = hc.shape
    ts = min(512, t)
    per = ts // CONV_HALO
    last = t // CONV_HALO - 1
    row = lambda v: v.reshape(1, ch)
    const = lambda b, i: (0, 0)
    return pl.pallas_call(
        functools.partial(_conv_kernel, ts=ts),
        grid=(bx, t // ts),
        in_specs=[
            pl.BlockSpec((1, CONV_HALO, ch), lambda b, i: (b, jnp.maximum(i * per - 1, 0), 0)),
            pl.BlockSpec((1, ts, ch), lambda b, i: (b, i, 0)),
            pl.BlockSpec((1, CONV_HALO, ch), lambda b, i: (b, jnp.minimum((i + 1) * per, last), 0)),
            pl.BlockSpec((C_CONV_WIDTH, ch), const),
            pl.BlockSpec((1, ch), const), pl.BlockSpec((1, ch), const), pl.BlockSpec((1, ch), const),
        ],
        out_specs=pl.BlockSpec((1, ts, ch), lambda b, i: (b, i, 0)),
        out_shape=jax.ShapeDtypeStruct((bx, t, ch), BF16),
        scratch_shapes=[pltpu.VMEM((F32_SUBLANES, ts + CONV_SHIFT_SPAN, ch), F32)],
        compiler_params=_params("parallel", "parallel"),
        name="conformer_conv",
    )(hc, hc, hc, conv_w, row(conv_b), row(ln_g), row(ln_b))


OUTPROJ_CHUNK_ROWS = 128


def _outproj_kernel(oa_ref, ob_ref, oc_ref, x_ref, g1_ref, sh_ref, sc_ref, w_ref, lng_ref, lnb_ref,
                    wr_hi_ref, wr_lo_ref, x1_ref, h2_ref, afft_ref, aff_ref, *, alpha):
    tm, d = x_ref.shape[1], x_ref.shape[2]
    rc = min(OUTPROJ_CHUNK_ROWS, tm)
    chunks = [slice(k * rc, (k + 1) * rc) for k in range(tm // rc)]
    outs = [(_dot(oa_ref[0, r], w_ref[0:A_Q_W])
             + _dot(ob_ref[0, r], w_ref[A_Q_W:A_Q_W + B_W])
             + _dot(oc_ref[0, r], w_ref[A_Q_W + B_W:])) for r in chunks]
    w_hi, w_lo = wr_hi_ref[...], wr_lo_ref[...]
    eye = (lax.broadcasted_iota(jnp.int32, (N_EXPERTS, N_EXPERTS), 0)
           == lax.broadcasted_iota(jnp.int32, (N_EXPERTS, N_EXPERTS), 1)).astype(BF16)
    rows_per_token = d // LANES
    ys = [_layer_norm(alpha * x_ref[0, r] + g1_ref[0] * o) * lng_ref[...] + lnb_ref[...]
          for r, o in zip(chunks, outs)]
    for r, y in zip(chunks, ys):
        x1_ref[0, r] = y
    h2s = [_layer_norm(y) * (1.0 + sc_ref[0]) + sh_ref[0] for y in ys]
    his = [h2.astype(BF16) for h2 in h2s]
    los = [(h2 - hi.astype(F32)).astype(BF16) for h2, hi in zip(h2s, his)]
    all_logits = [_dot_t(hi, w_hi) + _dot_t(lo, w_hi) + _dot_t(hi, w_lo) for hi, lo in zip(his, los)]
    for k, h2 in enumerate(h2s):
        for j in range(rows_per_token):
            h2_ref[0, pl.ds(k * rc * rows_per_token + j, rc, stride=rows_per_token), :] = (
                h2[:, j * LANES:(j + 1) * LANES])
    for r, logits in zip(chunks, all_logits):
        e_n = jnp.exp(logits - logits.max(axis=1, keepdims=True))
        aff = e_n / e_n.sum(axis=1, keepdims=True)
        aff_ref[0, r] = aff
        aff_t, rest = None, aff
        for _ in range(GATE_PARTS):
            part = rest.astype(BF16)
            term = _dot_t(eye, part)
            aff_t = term if aff_t is None else aff_t + term
            rest = rest - part.astype(F32)
        afft_ref[0, :, r] = aff_t


def out_projection(oa, ob, oc, x, g1, sh2, sc2, w_out_bf16, ln_g, ln_b, wr_hi, wr_lo, alpha):
    bx, t, d = x.shape
    tm = min(1024, t)
    tok = lambda b, i: (b, i, 0)
    per_b = lambda b, i: (b, 0, 0)
    const = lambda b, i: (0, 0)
    vec = pl.BlockSpec((1, d), const)
    return pl.pallas_call(
        functools.partial(_outproj_kernel, alpha=alpha),
        grid=(bx, t // tm),
        in_specs=[
            pl.BlockSpec((1, tm, A_Q_W), tok), pl.BlockSpec((1, tm, B_W), tok),
            pl.BlockSpec((1, tm, C_CHANNELS), tok), pl.BlockSpec((1, tm, d), tok),
            pl.BlockSpec((1, 1, d), per_b), pl.BlockSpec((1, 1, d), per_b), pl.BlockSpec((1, 1, d), per_b),
            pl.BlockSpec(w_out_bf16.shape, const), vec, vec,
            pl.BlockSpec((N_EXPERTS, d), const), pl.BlockSpec((N_EXPERTS, d), const),
        ],
        out_specs=[pl.BlockSpec((1, tm, d), tok), pl.BlockSpec((1, tm * (d // LANES), LANES), tok),
                   pl.BlockSpec((1, N_EXPERTS, tm), lambda b, i: (b, 0, i)),
                   pl.BlockSpec((1, tm, N_EXPERTS), tok)],
        out_shape=[jax.ShapeDtypeStruct((bx, t, d), F32),
                   jax.ShapeDtypeStruct((bx, t * (d // LANES), LANES), F32),
                   jax.ShapeDtypeStruct((bx, N_EXPERTS, t), F32),
                   jax.ShapeDtypeStruct((bx, t, N_EXPERTS), F32)],
        compiler_params=_params("parallel", "parallel"),
        name="out_projection",
    )(oa, ob, oc, x, g1, sh2, sc2, w_out_bf16, ln_g.reshape(1, d), ln_b.reshape(1, d), wr_hi, wr_lo)


def _select_kernel(afft_ref, pos_ref, off_ref, *, cap, n_tok):
    aff = afft_ref[0]

    def bit_step(j, bits):
        cand = bits | (jnp.int32(1) << (30 - j))
        cnt = jnp.sum((aff >= pltpu.bitcast(cand, F32)).astype(jnp.int32), axis=1, keepdims=True)
        return jnp.where(cnt >= cap, cand, bits)

    thr = pltpu.bitcast(lax.fori_loop(0, 31, bit_step, jnp.zeros((N_EXPERTS, 1), jnp.int32)), F32)
    above = (aff > thr).astype(F32)
    tied = (aff == thr).astype(F32)
    need = cap - jnp.sum(above, axis=1, keepdims=True)

    blk = MXU_DEPTH
    blocks = [slice(k * blk, (k + 1) * blk) for k in range(n_tok // blk)]
    r_i = lax.broadcasted_iota(jnp.int32, (blk, blk), 0)
    c_i = lax.broadcasted_iota(jnp.int32, (blk, blk), 1)
    strict_upper = (r_i < c_i).astype(BF16)

    def running(block_sums):
        run, total = [], jnp.zeros((N_EXPERTS, 1), F32)
        for s in block_sums:
            run.append(total)
            total = total + s
        return run

    tied_b = [tied[:, sl] for sl in blocks]
    tied_rank = [_dot(t.astype(BF16), strict_upper) for t in tied_b]
    tied_before = running([t.sum(axis=1, keepdims=True) for t in tied_b])
    sel_b = [above[:, sl] + t * ((before + rank) < need).astype(F32)
             for sl, t, before, rank in zip(blocks, tied_b, tied_before, tied_rank)]
    sel_rank = [_dot(s.astype(BF16), strict_upper) for s in sel_b]
    half_sums = [[s[:, h * OFFSET_BLOCK:(h + 1) * OFFSET_BLOCK].sum(axis=1, keepdims=True)
                  for h in range(blk // OFFSET_BLOCK)] for s in sel_b]
    offs = running([h for hs in half_sums for h in hs])
    per_blk = blk // OFFSET_BLOCK
    for k, (sl, s, rank) in enumerate(zip(blocks, sel_b, sel_rank)):
        pos_ref[0, :, sl] = jnp.where(s > 0.5, (offs[k * per_blk] + rank).astype(jnp.int32), -1)
    off_ref[0] = jnp.concatenate(offs, axis=1).astype(jnp.int32)


def expert_choice_select(aff_t, cap):
    bx, n_e, t = aff_t.shape
    n_tb = t // OFFSET_BLOCK
    return pl.pallas_call(
        functools.partial(_select_kernel, cap=cap, n_tok=t),
        grid=(bx,),
        in_specs=[pl.BlockSpec((1, n_e, t), lambda b: (b, 0, 0))],
        out_specs=[pl.BlockSpec((1, n_e, t), lambda b: (b, 0, 0)),
                   pl.BlockSpec((1, n_e, n_tb), lambda b: (b, 0, 0))],
        out_shape=[jax.ShapeDtypeStruct((bx, n_e, t), jnp.int32),
                   jax.ShapeDtypeStruct((bx, n_e, n_tb), jnp.int32)],
        compiler_params=_params("parallel"),
        name="expert_choice_select",
    )(aff_t)


GATE_PARTS = 3
TOKEN_LANE = GATE_PARTS * N_EXPERTS


def _slot_table_kernel(off_ref, pos_ref, aff_ref, tbl_ref, *, n_off, blocks_per_step):
    b, eg, kc = pl.program_id(0), pl.program_id(1), pl.program_id(2)
    epg = EXPERTS_PER_GATHER_STEP

    @pl.when(kc == 0)
    def _():
        tbl_ref[...] = jnp.zeros_like(tbl_ref)

    slot = lax.broadcasted_iota(jnp.int32, (GATHER_WINDOW, 1), 0)
    slot_2d = lax.broadcasted_iota(jnp.int32, (GATHER_WINDOW, GATHER_BLOCK), 0)
    lane = lax.broadcasted_iota(jnp.int32, (GATHER_BLOCK, LANES), 1)
    out_lane = lax.broadcasted_iota(jnp.int32, (GATHER_WINDOW, LANES), 1)
    local_token = lax.broadcasted_iota(jnp.int32, (GATHER_BLOCK, LANES), 0).astype(F32)
    place_r = lax.broadcasted_iota(jnp.int32, (N_EXPERTS, LANES), 0)
    place_c = lax.broadcasted_iota(jnp.int32, (N_EXPERTS, LANES), 1)
    def payload_of(kk):
        payload = jnp.where(lane == TOKEN_LANE, local_token, 0.0)
        rest = aff_ref[0, kk * GATHER_BLOCK:(kk + 1) * GATHER_BLOCK]
        for k in range(GATE_PARTS):
            part = rest.astype(BF16)
            payload = payload + _dot(part, (place_c == place_r + k * N_EXPERTS).astype(BF16))
            rest = rest - part.astype(F32)
        return payload.astype(BF16)

    payloads = [payload_of(kk) for kk in range(blocks_per_step)]
    pending = []
    for kk in range(blocks_per_step):
        kb = kc * blocks_per_step + kk
        first_token = jnp.where(out_lane == TOKEN_LANE, (kb * GATHER_BLOCK).astype(F32), 0.0)
        for ee in range(epg):
            off = off_ref[(b * N_EXPERTS + eg * epg + ee) * n_off + kb * (GATHER_BLOCK // OFFSET_BLOCK)]
            base = pl.multiple_of((off >> 4) << 4, WINDOW_ALIGN)
            pos_row = pos_ref[0, ee:ee + 1, kk * GATHER_BLOCK:(kk + 1) * GATHER_BLOCK]
            onehot = (pos_row - base == slot_2d).astype(BF16)
            pending.append((ee, off, base, _dot(onehot, payloads[kk]) + first_token))
    for ee, off, base, gathered in pending:
        win = pl.ds(base, GATHER_WINDOW)
        tbl_ref[0, ee, win, :] = jnp.where(slot >= off - base, gathered, tbl_ref[0, ee, win, :])


def slot_table(aff, pos, block_off, cap_pad):
    bx, t, _ = aff.shape
    n_tb = t // GATHER_BLOCK
    blocks_per_step = min(4, n_tb)
    n_steps = n_tb // blocks_per_step
    epg = EXPERTS_PER_GATHER_STEP
    tokens = blocks_per_step * GATHER_BLOCK
    grid_spec = pltpu.PrefetchScalarGridSpec(
        num_scalar_prefetch=1,
        grid=(bx, N_EXPERTS // epg, n_steps),
        in_specs=[
            pl.BlockSpec((1, epg, tokens), lambda b, g, k, s: (b, g, k)),
            pl.BlockSpec((1, tokens, N_EXPERTS), lambda b, g, k, s: (b, k, 0)),
        ],
        out_specs=pl.BlockSpec((1, epg, cap_pad, LANES), lambda b, g, k, s: (b, g, 0, 0)),
    )
    return pl.pallas_call(
        functools.partial(_slot_table_kernel, n_off=t // OFFSET_BLOCK, blocks_per_step=blocks_per_step),
        grid_spec=grid_spec,
        out_shape=jax.ShapeDtypeStruct((bx, N_EXPERTS, cap_pad, LANES), F32),
        compiler_params=_params("parallel", "parallel", "arbitrary"),
        name="slot_table",
    )(block_off.reshape(-1), pos, aff)


GATHER_UNROLL = 16


def _row_gather_kernel(idx_ref, src_ref, xs_ref, tile_ref, *, cap, rows_per_token, chunk_stride):
    b, e = pl.program_id(0), pl.program_id(1)
    first = (b * N_EXPERTS + e) * cap

    def group(g, carry):
        for u in range(GATHER_UNROLL):
            s = g * GATHER_UNROLL + u
            row = pl.multiple_of(idx_ref[first + s], rows_per_token)
            tile_ref[pl.ds(s, rows_per_token, stride=chunk_stride), :] = src_ref[0, pl.ds(row, rows_per_token), :]
        return carry

    lax.fori_loop(0, cap // GATHER_UNROLL, group, 0)
    for j in range(rows_per_token):
        xs_ref[0, 0, :, j * LANES:(j + 1) * LANES] = tile_ref[j * chunk_stride:j * chunk_stride + cap, :].astype(BF16)


def gather_rows(h2_rows, token_row, cap, d):
    bx, n_rows, _ = h2_rows.shape
    rows_per_token = d // LANES
    chunk_stride = cap + 8
    grid_spec = pltpu.PrefetchScalarGridSpec(
        num_scalar_prefetch=1,
        grid=(bx, N_EXPERTS),
        in_specs=[pl.BlockSpec((1, n_rows, LANES), lambda b, e, s: (b, 0, 0), pipeline_mode=pl.Buffered(1))],
        out_specs=pl.BlockSpec((1, 1, cap, d), lambda b, e, s: (b, e, 0, 0)),
        scratch_shapes=[pltpu.VMEM((rows_per_token * chunk_stride, LANES), F32)],
    )
    return pl.pallas_call(
        functools.partial(_row_gather_kernel, cap=cap, rows_per_token=rows_per_token, chunk_stride=chunk_stride),
        grid_spec=grid_spec,
        out_shape=jax.ShapeDtypeStruct((bx, N_EXPERTS, cap, d), BF16),
        compiler_params=_params("parallel", "arbitrary"),
        name="gather_rows",
    )(token_row, h2_rows)


def _ffn_kernel(xs_ref, gs_ref, wg_ref, wu_ref, wd_ref, ye_ref, wg_bf, wu_bf, wd_bf, *, cap, row_tile):
    @pl.when(pl.program_id(1) == 0)
    def _():
        wg_bf[...] = wg_ref[0, 0].astype(BF16)
        wu_bf[...] = wu_ref[0, 0].astype(BF16)
        wd_bf[...] = wd_ref[0, 0].astype(BF16)

    n_b = xs_ref.shape[0]
    if n_b == 1:
        tiles = [[(0, r0, row_tile)] for r0 in range(0, cap, row_tile)]
    else:
        tiles = [[(bb, 0, cap) for bb in range(n_b)]]
    for tile in tiles:
        x = jnp.concatenate([xs_ref[bb, 0, r0:r0 + n] for bb, r0, n in tile], axis=0)
        terms = jnp.concatenate([gs_ref[bb, 0, r0:r0 + n, :] for bb, r0, n in tile], axis=0)
        lane = lax.broadcasted_iota(jnp.int32, terms.shape, 1)
        own = ((lane & (N_EXPERTS - 1)) == pl.program_id(0)) & (lane < TOKEN_LANE)
        terms = jnp.where(own, terms, 0.0)
        gate = _dot(x, wg_bf[...])
        up = _dot(x, wu_bf[...])
        hid = (gate * jax.nn.sigmoid(gate) * up).astype(BF16)
        g = jnp.sum(terms, axis=1, keepdims=True)
        ye = _dot(hid, wd_bf[...]) * g
        rows_per_slot = ye.shape[1] // LANES
        row = 0
        for bb, r0, n in tile:
            for j in range(rows_per_slot):
                ye_ref[bb, 0, pl.ds(r0 * rows_per_slot + j, n, stride=rows_per_slot), :] = (
                    ye[row:row + n, j * LANES:(j + 1) * LANES])
            row += n


def expert_ffn(xs, gs, wg, wu, wd, layer):
    bx, n_e, cap, d = xs.shape
    ff = wg.shape[-1]
    slot_rows = cap * (d // LANES)
    max_rows = 512
    row_tile = min(max_rows, cap)
    n_b = bx if bx * cap <= max_rows else 1
    return pl.pallas_call(
        functools.partial(_ffn_kernel, cap=cap, row_tile=row_tile),
        grid=(n_e, bx // n_b),
        in_specs=[
            pl.BlockSpec((n_b, 1, cap, d), lambda e, b: (b, e, 0, 0)),
            pl.BlockSpec((n_b, 1, cap, LANES), lambda e, b: (b, e, 0, 0)),
            pl.BlockSpec((1, 1, d, ff), lambda e, b: (layer, e, 0, 0)),
            pl.BlockSpec((1, 1, d, ff), lambda e, b: (layer, e, 0, 0)),
            pl.BlockSpec((1, 1, ff, d), lambda e, b: (layer, e, 0, 0)),
        ],
        out_specs=pl.BlockSpec((n_b, 1, slot_rows, LANES), lambda e, b: (b, e, 0, 0)),
        out_shape=jax.ShapeDtypeStruct((bx, n_e, slot_rows, LANES), F32),
        scratch_shapes=[pltpu.VMEM((d, ff), BF16), pltpu.VMEM((d, ff), BF16), pltpu.VMEM((ff, d), BF16)],
        compiler_params=_params("parallel", "arbitrary"),
        name="expert_ffn",
    )(xs, gs, wg, wu, wd)


SCATTER_UNROLL = 16


def _scatter_norm_kernel(idx_ref, ye_ref, x1_ref, g2_ref, lng_ref, lnb_ref, out_ref, acc_ref, *,
                         cap, rows_per_token, alpha):
    b, step = pl.program_id(0), pl.program_id(1)

    @pl.when(step == 0)
    def _():
        acc_ref[...] = jnp.zeros_like(acc_ref)

    @pl.when(step < N_EXPERTS)
    def _():
        first = (b * N_EXPERTS + step) * cap

        def group(g, carry):
            rows, sums = [], []
            for u in range(SCATTER_UNROLL):
                s = g * SCATTER_UNROLL + u
                row = pl.multiple_of(idx_ref[first + s], rows_per_token)
                src = pl.multiple_of(s * rows_per_token, rows_per_token)
                rows.append(row)
                sums.append(acc_ref[pl.ds(row, rows_per_token), :] + ye_ref[0, 0, pl.ds(src, rows_per_token), :])
            for row, total in zip(rows, sums):
                acc_ref[pl.ds(row, rows_per_token), :] = total
            return carry

        lax.fori_loop(0, cap // SCATTER_UNROLL, group, 0)

    @pl.when(step >= N_EXPERTS)
    def _():
        tm = x1_ref.shape[1]
        tile_row = pl.multiple_of((step - N_EXPERTS) * (tm * rows_per_token), rows_per_token)
        moe = jnp.concatenate([acc_ref[pl.ds(tile_row + j, tm, stride=rows_per_token), :]
                               for j in range(rows_per_token)], axis=1)
        y = alpha * x1_ref[0] + g2_ref[0] * moe
        out_ref[0] = _layer_norm(y) * lng_ref[...] + lnb_ref[...]


def scatter_and_norm(ye_rows, token_row, x1, g2, ln_g, ln_b, alpha, cap):
    bx, t, d = x1.shape
    n_e, slot_rows = ye_rows.shape[1], ye_rows.shape[2]
    rows_per_token = d // LANES
    tm = min(512, t)
    tile = lambda b, i, s: (b, jnp.maximum(i - n_e, 0), 0)
    const = lambda b, i, s: (0, 0)
    grid_spec = pltpu.PrefetchScalarGridSpec(
        num_scalar_prefetch=1,
        grid=(bx, n_e + t // tm),
        in_specs=[
            pl.BlockSpec((1, 1, slot_rows, LANES), lambda b, i, s: (b, jnp.minimum(i, n_e - 1), 0, 0)),
            pl.BlockSpec((1, tm, d), tile),
            pl.BlockSpec((1, 1, d), lambda b, i, s: (b, 0, 0)),
            pl.BlockSpec((1, d), const), pl.BlockSpec((1, d), const),
        ],
        out_specs=pl.BlockSpec((1, tm, d), tile),
        scratch_shapes=[pltpu.VMEM((t * rows_per_token, LANES), F32)],
    )
    return pl.pallas_call(
        functools.partial(_scatter_norm_kernel, cap=cap, rows_per_token=rows_per_token, alpha=alpha),
        grid_spec=grid_spec,
        out_shape=jax.ShapeDtypeStruct((bx, t, d), F32),
        compiler_params=_params("parallel", "arbitrary"),
        name="scatter_and_norm",
    )(token_row, ye_rows, x1, g2, ln_g.reshape(1, d), ln_b.reshape(1, d))


def _split_bf16(w):
    hi = w.astype(BF16)
    return hi, (w - hi.astype(F32)).astype(BF16)


def _mixer_tail(oa, ob, oc, x, mod, lw, alpha):
    g1, sh2, sc2, g2 = mod
    t = x.shape[1]
    cap = EC_CAPACITY * t // N_EXPERTS
    cap_pad = cap + SLOT_PAD
    x1, h2_rows, aff_t, aff = out_projection(oa, ob, oc, x, g1, sh2, sc2, lw["w_out"], lw["ln1_g"], lw["ln1_b"],
                                        lw["wr_hi"], lw["wr_lo"], alpha)
    pos, block_off = expert_choice_select(aff_t, cap)
    table = slot_table(aff, pos, block_off, cap_pad)
    token_row = (table[:, :, :cap, TOKEN_LANE].astype(jnp.int32) * (x.shape[2] // LANES)).reshape(-1)
    xs = gather_rows(h2_rows, token_row, cap, x.shape[2])
    ye_rows = expert_ffn(xs, table, lw["w_gate"], lw["w_up"], lw["w_down"], lw["layer"])
    return scatter_and_norm(ye_rows, token_row, x1, g2, lw["ln2_g"], lw["ln2_b"], alpha, cap)


def kernel(x, c, ctx, c_ctx, w_mod, b_mod, w_in, a_sink, nat_bias, conv_w, conv_b, conv_ln_g, conv_ln_b,
           w_out, ln1_g, ln1_b, w_router, w_gate, w_up, w_down, ln2_g, ln2_b):
    bsz, n_lat, d = x.shape
    depth = w_mod.shape[0]
    alpha = (2 * depth) ** 0.25
    cos_t, sin_t = rope_tables(n_lat)

    cond = jnp.concatenate([c, c_ctx[None, :], jnp.zeros((8 - bsz - 1, d), F32)], axis=0)
    mods = adaln_all(cond, w_mod, b_mod)

    for l in range(depth):
        last = l == depth - 1
        wr_hi, wr_lo = _split_bf16(w_router[l].T)
        lw = dict(w_out=w_out[l].astype(BF16), ln1_g=ln1_g[l], ln1_b=ln1_b[l], wr_hi=wr_hi, wr_lo=wr_lo,
                  w_gate=w_gate, w_up=w_up, w_down=w_down, layer=l,
                  ln2_g=ln2_g[l], ln2_b=ln2_b[l])
        w_in_l = w_in[l].astype(BF16)
        lat = [mods[l, :bsz, k * d:(k + 1) * d][:, None, :] for k in range(N_MOD)]
        cm = [jnp.broadcast_to(mods[l, bsz, k * d:(k + 1) * d][None, None, :], (bsz, 1, d))
              for k in range(N_MOD)]
        conv_args = (conv_w[l], conv_b[l], conv_ln_g[l], conv_ln_b[l])

        qa_c, ka_c, va_c, qb_c, kb_c, vb_c, hc_c = in_projection(ctx, cm[0], cm[1], w_in_l, cos_t, sin_t, rope=False)
        if not last:
            oa_c, ob_c = context_attention(qa_c, qb_c, ka_c, va_c, kb_c, vb_c, a_sink[l])
            oc_c = conformer_conv(hc_c, *conv_args)
            ctx_new = _mixer_tail(oa_c, ob_c, oc_c, ctx, (cm[2], cm[3], cm[4], cm[5]), lw, alpha)

        qa, ka, va, qb, kb, vb, hc = in_projection(x, lat[0], lat[1], w_in_l, cos_t, sin_t, rope=True)
        oa = window_attention(qa, ka, va, ka_c, va_c, a_sink[l])
        ob = neighbourhood_attention(qb, kb, vb, kb_c, vb_c, nat_bias[l])
        oc = conformer_conv(hc, *conv_args)
        x = _mixer_tail(oa, ob, oc, x, (lat[2], lat[3], lat[4], lat[5]), lw, alpha)
        if not last:
            ctx = ctx_new
    return x
```

```python
import functools

import numpy as np
import jax
import jax.numpy as jnp
from jax import lax
from jax.experimental import pallas as pl
from jax.experimental.pallas import tpu as pltpu

HEAD_DIM = 64
GRID_W = 64
A_Q_HEADS = 8
A_KV_HEADS = 2
A_GROUP = A_Q_HEADS // A_KV_HEADS
A_WINDOW = 128
B_HEADS = 4
NA_ROWS = 8
NA_COLS = 16
C_CHANNELS = 256
C_CONV_WIDTH = 31
A_Q_W = A_Q_HEADS * HEAD_DIM
A_KV_W = A_KV_HEADS * HEAD_DIM
B_W = B_HEADS * HEAD_DIM
OFF_AK = A_Q_W
OFF_AV = OFF_AK + A_KV_W
OFF_BQ = OFF_AV + A_KV_W
OFF_BK = OFF_BQ + B_W
OFF_BV = OFF_BK + B_W
OFF_C = OFF_BV + B_W
IN_WIDTH = OFF_C + 2 * C_CHANNELS
ROPE_WIDTH = A_Q_W + A_KV_W
N_EXPERTS = 16
EC_CAPACITY = 2
ROPE_BASE = 10000.0
LN_EPS = 1e-6
N_MOD = 6
NEG_INF = -1e30
QK_SCALE = HEAD_DIM ** -0.5

LANES = 128
WINDOW_ALIGN = 16
MXU_DEPTH = 256
OFFSET_BLOCK = 128
GATHER_BLOCK = MXU_DEPTH
GATHER_WINDOW = GATHER_BLOCK + WINDOW_ALIGN
SLOT_PAD = 3 * LANES
EXPERTS_PER_GATHER_STEP = 16
F32_SUBLANES = 8
TOKEN_TILE = 1024
SMALL_TILE = 512
MOD_COLUMN_TILE = 1536
VMEM_LIMIT = 56 * 1024 * 1024

F32 = jnp.float32
BF16 = jnp.bfloat16


def _dot(a, b):
    return jnp.dot(a, b, preferred_element_type=F32)


def _dot_t(a, b):
    return lax.dot_general(a, b, (((1,), (1,)), ((), ())), preferred_element_type=F32)


def _layer_norm(x):
    mu = jnp.mean(x, axis=-1, keepdims=True)
    xc = x - mu
    var = jnp.mean(xc * xc, axis=-1, keepdims=True)
    return xc * lax.rsqrt(var + LN_EPS)


def _params(*sem):
    return pltpu.CompilerParams(dimension_semantics=sem, vmem_limit_bytes=VMEM_LIMIT)


def _mod_kernel(cond_ref, w_ref, b_ref, out_ref):
    cnd = cond_ref[...]
    act = cnd * jax.nn.sigmoid(cnd)
    out_ref[0] = jnp.dot(act, w_ref[0], preferred_element_type=F32,
                         precision=lax.Precision.HIGHEST) + b_ref[0]


def adaln_all(cond, w_mod, b_mod):
    n_layers, d, width = w_mod.shape
    rows = cond.shape[0]
    tn = MOD_COLUMN_TILE
    return pl.pallas_call(
        _mod_kernel,
        grid=(n_layers, width // tn),
        in_specs=[
            pl.BlockSpec((rows, d), lambda l, j: (0, 0)),
            pl.BlockSpec((1, d, tn), lambda l, j: (l, 0, j)),
            pl.BlockSpec((1, 1, tn), lambda l, j: (l, 0, j)),
        ],
        out_specs=pl.BlockSpec((1, rows, tn), lambda l, j: (l, 0, j)),
        out_shape=jax.ShapeDtypeStruct((n_layers, rows, width), F32),
        compiler_params=_params("parallel", "parallel"),
        name="adaln",
    )(cond, w_mod, b_mod.reshape(n_layers, 1, width))


def _inproj_kernel(x_ref, sh_ref, sc_ref, w_ref, cos_ref, sin_ref,
                   qa_ref, ka_ref, va_ref, qb_ref, kb_ref, vb_ref, hc_ref, *, rope):
    x = x_ref[0]
    h = _layer_norm(x) * (1.0 + sc_ref[0]) + sh_ref[0]
    u = _dot(h.astype(BF16), w_ref[...])

    def rotated(col):
        xq = u[:, col:col + LANES]
        if not rope:
            return xq
        lane = lax.broadcasted_iota(jnp.int32, xq.shape, 1)
        first = (lane & (HEAD_DIM // 2 - 1)) < (HEAD_DIM // 4)
        partner = jnp.where(first, pltpu.roll(xq, LANES - HEAD_DIM // 4, 1),
                            pltpu.roll(xq, HEAD_DIM // 4, 1))
        return xq * cos_ref[...] + partner * sin_ref[...]

    rot = [rotated(col) for col in range(0, ROPE_WIDTH, LANES)]
    n_q = A_Q_W // LANES
    qa_ref[0] = (jnp.concatenate(rot[:n_q], axis=1) * QK_SCALE).astype(BF16)
    ka_ref[0] = jnp.concatenate(rot[n_q:], axis=1).astype(BF16)
    va_ref[0] = u[:, OFF_AV:OFF_BQ].astype(BF16)
    qb_ref[0] = (u[:, OFF_BQ:OFF_BK] * QK_SCALE).astype(BF16)
    kb_ref[0] = u[:, OFF_BK:OFF_BV].astype(BF16)
    vb_ref[0] = u[:, OFF_BV:OFF_C].astype(BF16)
    a = u[:, OFF_C:OFF_C + C_CHANNELS]
    gate = u[:, OFF_C + C_CHANNELS:]
    hc_ref[0] = a * jax.nn.sigmoid(gate)


def in_projection(x, shift, scale, w_in_bf16, cos_t, sin_t, *, rope):
    bx, t, d = x.shape
    tm = min(TOKEN_TILE, t)
    widths = (A_Q_W, A_KV_W, A_KV_W, B_W, B_W, B_W, C_CHANNELS)
    dtypes = (BF16,) * 6 + (F32,)
    tok = lambda b, i: (b, i, 0)
    per_b = lambda b, i: (b, 0, 0)
    return pl.pallas_call(
        functools.partial(_inproj_kernel, rope=rope),
        grid=(bx, t // tm),
        in_specs=[
            pl.BlockSpec((1, tm, d), tok),
            pl.BlockSpec((1, 1, d), per_b),
            pl.BlockSpec((1, 1, d), per_b),
            pl.BlockSpec((d, IN_WIDTH), lambda b, i: (0, 0)),
            pl.BlockSpec((tm, LANES), lambda b, i: (i, 0)),
            pl.BlockSpec((tm, LANES), lambda b, i: (i, 0)),
        ],
        out_specs=[pl.BlockSpec((1, tm, w), tok) for w in widths],
        out_shape=[jax.ShapeDtypeStruct((bx, t, w), dt) for w, dt in zip(widths, dtypes)],
        compiler_params=_params("parallel", "parallel"),
        name="in_projection",
    )(x, shift, scale, w_in_bf16, cos_t, sin_t)


def rope_tables(n_tokens):
    t = jnp.arange(n_tokens, dtype=jnp.int32)
    row = (t // GRID_W).astype(F32)[:, None]
    col = (t % GRID_W).astype(F32)[:, None]
    n_freq = HEAD_DIM // 4
    inv_freq = ROPE_BASE ** (-jnp.arange(n_freq, dtype=F32) / n_freq)
    ang_r = row * inv_freq
    ang_c = col * inv_freq
    cos_h = jnp.concatenate([jnp.cos(ang_r), jnp.cos(ang_r), jnp.cos(ang_c), jnp.cos(ang_c)], axis=1)
    sin_h = jnp.concatenate([-jnp.sin(ang_r), jnp.sin(ang_r), -jnp.sin(ang_c), jnp.sin(ang_c)], axis=1)
    reps = LANES // HEAD_DIM
    return jnp.tile(cos_h, (1, reps)), jnp.tile(sin_h, (1, reps))


def _with_ones(v):
    return jnp.concatenate([v, jnp.ones_like(v)], axis=1)


def _attend(score_parts, values, sink=None):
    m = score_parts[0].max(axis=-1, keepdims=True)
    for s in score_parts[1:]:
        m = jnp.maximum(m, s.max(axis=-1, keepdims=True))
    if sink is not None:
        m = jnp.maximum(m, sink)
    acc = None
    for s, v in zip(score_parts, values):
        term = _dot(jnp.exp((s - m).astype(BF16)), v)
        acc = term if acc is None else acc + term
    den = acc[:, HEAD_DIM:HEAD_DIM + 1]
    if sink is not None:
        den = den + jnp.exp(sink - m)
    return acc[:, :HEAD_DIM] / den


def _attn_a_kernel(sink_ref, q_ref, kp_ref, kc_ref, kn_ref, vp_ref, vc_ref, vn_ref,
                   kctx_ref, vctx_ref, out_ref, *, n_lat, tq):
    i = pl.program_id(1)
    k_win = jnp.concatenate([kp_ref[0], kc_ref[0], kn_ref[0]], axis=0)
    v_win = jnp.concatenate([vp_ref[0], vc_ref[0], vn_ref[0]], axis=0)
    kctx = kctx_ref[0]
    vctx = vctx_ref[0]
    sub = A_WINDOW
    span = 3 * A_WINDOW
    rows = A_GROUP * sub
    q_onehot = ((lax.broadcasted_iota(jnp.int32, (rows, sub), 0) & (sub - 1))
                == lax.broadcasted_iota(jnp.int32, (rows, sub), 1)).astype(BF16)
    key_i = lax.broadcasted_iota(jnp.int32, (span, sub), 0)
    qry_i = lax.broadcasted_iota(jnp.int32, (span, sub), 1)
    rel = key_i - A_WINDOW - qry_i
    in_band = (rel <= A_WINDOW) & (rel >= -A_WINDOW)
    group_of_row = lax.broadcasted_iota(jnp.int32, (rows, 1), 0) >> (sub.bit_length() - 1)
    sinks = []
    for hk in range(A_KV_HEADS):
        sink = jnp.zeros((rows, 1), F32)
        for g in range(A_GROUP):
            sink = jnp.where(group_of_row == g, sink_ref[hk * A_GROUP + g], sink)
        sinks.append(sink)
    vctx_ext = [_with_ones(vctx[:, hk * HEAD_DIM:(hk + 1) * HEAD_DIM]) for hk in range(A_KV_HEADS)]
    v_ext = [_with_ones(v_win[:, hk * HEAD_DIM:(hk + 1) * HEAD_DIM]) for hk in range(A_KV_HEADS)]
    def mask_columns(j):
        kpos = i * tq + j * sub - A_WINDOW + key_i
        valid = in_band & (kpos >= 0) & (kpos < n_lat)
        return jnp.where(valid, 0.0, NEG_INF).astype(BF16)

    masks = [mask_columns(j) for j in range(tq // sub)]

    def scores(j, hk):
        sl = slice(hk * HEAD_DIM, (hk + 1) * HEAD_DIM)
        q_rows = q_ref[0, j * sub:(j + 1) * sub]
        q = jnp.concatenate([q_rows[:, h * HEAD_DIM:(h + 1) * HEAD_DIM]
                             for h in range(hk * A_GROUP, (hk + 1) * A_GROUP)], axis=0)
        q_aug = jnp.concatenate([q_onehot, q], axis=1)
        k_aug = jnp.concatenate([masks[j], k_win[j * sub:j * sub + span, sl]], axis=1)
        return [_dot_t(q_aug, k_aug), _dot_t(q, kctx[:, sl])]

    units = [(j, hk) for j in range(tq // sub) for hk in range(A_KV_HEADS)]
    all_scores = [scores(j, hk) for j, hk in units]
    for (j, hk), s in zip(units, all_scores):
        v_sub = v_ext[hk][j * sub:j * sub + span]
        o = _attend(s, [v_sub, vctx_ext[hk]], sink=sinks[hk]).astype(BF16)
        for g in range(A_GROUP):
            h = hk * A_GROUP + g
            out_ref[0, j * sub:(j + 1) * sub, h * HEAD_DIM:(h + 1) * HEAD_DIM] = o[g * sub:(g + 1) * sub]


def window_attention(qa, ka, va, kc_a, vc_a, sink):
    bsz, n_lat, _ = qa.shape
    n_ctx = kc_a.shape[1]
    tq = min(TOKEN_TILE, n_lat)
    w = A_WINDOW
    per = tq // w
    last = n_lat // w - 1
    prev = lambda b, i, s: (b, jnp.maximum(i * per - 1, 0), 0)
    cur = lambda b, i, s: (b, i, 0)
    nxt = lambda b, i, s: (b, jnp.minimum((i + 1) * per, last), 0)
    ctx = lambda b, i, s: (b, 0, 0)
    kv_specs = [pl.BlockSpec((1, w, A_KV_W), prev), pl.BlockSpec((1, tq, A_KV_W), cur),
                pl.BlockSpec((1, w, A_KV_W), nxt)]
    grid_spec = pltpu.PrefetchScalarGridSpec(
        num_scalar_prefetch=1,
        grid=(bsz, n_lat // tq),
        in_specs=[pl.BlockSpec((1, tq, A_Q_W), cur)] + kv_specs + kv_specs + [
            pl.BlockSpec((1, n_ctx, A_KV_W), ctx), pl.BlockSpec((1, n_ctx, A_KV_W), ctx)],
        out_specs=pl.BlockSpec((1, tq, A_Q_W), cur),
    )
    return pl.pallas_call(
        functools.partial(_attn_a_kernel, n_lat=n_lat, tq=tq),
        grid_spec=grid_spec,
        out_shape=jax.ShapeDtypeStruct((bsz, n_lat, A_Q_W), BF16),
        compiler_params=_params("parallel", "parallel"),
        name="window_attention",
    )(sink, qa, ka, ka, ka, va, va, va, kc_a, vc_a)


NB_Q_ROWS = 4
NB_BLOCKS_PER_STEP = 4


def _attn_b_kernel(q_ref, *refs):
    n_kv = NB_BLOCKS_PER_STEP + 2
    k_refs, v_refs = refs[:n_kv], refs[n_kv:2 * n_kv]
    kctx_ref, vctx_ref = refs[2 * n_kv:2 * n_kv + 2]
    bias_refs = refs[2 * n_kv + 2:-1]
    out_ref = refs[-1]
    tb = k_refs[0].shape[1]
    k_all = jnp.concatenate([r[0] for r in k_refs], axis=0)
    v_all = jnp.concatenate([r[0] for r in v_refs], axis=0)
    kctx = kctx_ref[0]
    vctx = vctx_ref[0]
    heads = [slice(h * HEAD_DIM, (h + 1) * HEAD_DIM) for h in range(B_HEADS)]
    units = [(sb, h) for sb in range(NB_BLOCKS_PER_STEP) for h in range(B_HEADS)]

    def scores(sb, h):
        q = q_ref[0, sb * tb:(sb + 1) * tb, heads[h]]
        return [_dot_t(q, k_all[sb * tb:(sb + 3) * tb, heads[h]]) + bias_refs[sb][0, h],
                _dot_t(q, kctx[:, heads[h]])]

    all_scores = [scores(sb, h) for sb, h in units]
    v_ext = [_with_ones(v_all[:, sl]) for sl in heads]
    vctx_ext = [_with_ones(vctx[:, sl]) for sl in heads]
    for (sb, h), s in zip(units, all_scores):
        o = _attend(s, [v_ext[h][sb * tb:(sb + 3) * tb], vctx_ext[h]])
        out_ref[0, sb * tb:(sb + 1) * tb, heads[h]] = o.astype(BF16)


def neighbourhood_bias(rel_bias, n_lat):
    rows = n_lat // GRID_W
    kr_n = min(NA_ROWS, rows)
    n_blocks = rows // NB_Q_ROWS
    n_heads, n_dr, n_dc = rel_bias.shape
    cols = np.arange(GRID_W)
    c_start = np.clip(cols - NA_COLS // 2, 0, GRID_W - NA_COLS)
    col_ok = (cols[None, :] >= c_start[:, None]) & (cols[None, :] < c_start[:, None] + NA_COLS)
    dc = np.clip(cols[None, :] - cols[:, None], -(NA_COLS - 1), NA_COLS - 1) + NA_COLS - 1
    pick_dc = (dc.reshape(-1)[None, :] == np.arange(n_dc)[:, None]).astype(np.float32)
    toeplitz = jnp.dot(rel_bias.reshape(n_heads * n_dr, n_dc), pick_dc, precision=lax.Precision.HIGHEST)
    toeplitz = jnp.where(col_ok.reshape(-1), toeplitz, NEG_INF).reshape(n_heads, n_dr, GRID_W, GRID_W)
    q_rl = np.arange(NB_Q_ROWS)
    k_rl = np.arange(3 * NB_Q_ROWS)
    row_ok, dr = [], []
    for j in sorted({0, min(1, n_blocks - 1), n_blocks - 1}):
        r = NB_Q_ROWS * j + q_rl
        kr = NB_Q_ROWS * (j - 1) + k_rl
        r_start = np.clip(r - kr_n // 2, 0, rows - kr_n)
        ok = (kr[None, :] >= r_start[:, None]) & (kr[None, :] < r_start[:, None] + kr_n)
        row_ok.append(ok & (kr[None, :] >= 0) & (kr[None, :] < rows))
        dr.append(np.clip(kr[None, :] - r[:, None] + NA_ROWS - 1, 0, n_dr - 1))
    row_ok = np.stack(row_ok)
    dr = np.stack(dr)
    n_var = dr.shape[0]
    tq, tk = NB_Q_ROWS * GRID_W, 3 * NB_Q_ROWS * GRID_W
    table = pl.pallas_call(
        functools.partial(_bias_table_kernel, dr=dr, row_ok=row_ok),
        grid=(n_heads,),
        in_specs=[pl.BlockSpec((1, n_dr, GRID_W, GRID_W), lambda h: (h, 0, 0, 0))],
        out_specs=pl.BlockSpec((n_var, 1, tq, tk), lambda h: (0, h, 0, 0)),
        out_shape=jax.ShapeDtypeStruct((n_var, n_heads, tq, tk), F32),
        compiler_params=_params("parallel"),
        name="neighbourhood_bias_table",
    )(toeplitz)
    return table, n_blocks


def _bias_table_kernel(toe_ref, out_ref, *, dr, row_ok):
    n_var, n_q, n_k = dr.shape
    for v in range(n_var):
        for rl in range(n_q):
            for krl in range(n_k):
                tile = toe_ref[0, int(dr[v, rl, krl])] if row_ok[v, rl, krl] else jnp.full(
                    (GRID_W, GRID_W), NEG_INF, F32)
                out_ref[v, 0, rl * GRID_W:(rl + 1) * GRID_W, krl * GRID_W:(krl + 1) * GRID_W] = tile


def neighbourhood_attention(qb, kb, vb, kc_b, vc_b, rel_bias):
    bsz, n_lat, _ = qb.shape
    n_ctx = kc_b.shape[1]
    table, n_blocks = neighbourhood_bias(rel_bias, n_lat)
    assert n_blocks % NB_BLOCKS_PER_STEP == 0
    n_var = table.shape[0]
    tb = NB_Q_ROWS * GRID_W
    last = n_blocks - 1
    cur = lambda b, j: (b, j, 0)
    ctx = lambda b, j: (b, 0, 0)

    def key_block(off):
        return lambda b, j: (b, jnp.clip(NB_BLOCKS_PER_STEP * j + off, 0, last), 0)

    def variant(sb):
        def index(b, j):
            g = NB_BLOCKS_PER_STEP * j + sb
            return (jnp.where(g == 0, 0, jnp.where(g == last, n_var - 1, min(1, n_var - 1))), 0, 0, 0)
        return index

    kv_specs = [pl.BlockSpec((1, tb, B_W), key_block(off)) for off in range(-1, NB_BLOCKS_PER_STEP + 1)]
    return pl.pallas_call(
        _attn_b_kernel,
        grid=(bsz, n_blocks // NB_BLOCKS_PER_STEP),
        in_specs=[pl.BlockSpec((1, NB_BLOCKS_PER_STEP * tb, B_W), cur)] + kv_specs + kv_specs + [
            pl.BlockSpec((1, n_ctx, B_W), ctx), pl.BlockSpec((1, n_ctx, B_W), ctx)] + [
            pl.BlockSpec((1, B_HEADS, tb, 3 * tb), variant(sb)) for sb in range(NB_BLOCKS_PER_STEP)],
        out_specs=pl.BlockSpec((1, NB_BLOCKS_PER_STEP * tb, B_W), cur),
        out_shape=jax.ShapeDtypeStruct((bsz, n_lat, B_W), BF16),
        compiler_params=_params("parallel", "parallel"),
        name="neighbourhood_attention",
    )(qb, *([kb] * len(kv_specs)), *([vb] * len(kv_specs)), kc_b, vc_b, *([table] * NB_BLOCKS_PER_STEP))


def _ctx_attn_kernel(sink_ref, qa_ref, qb_ref, ka_ref, va_ref, kb_ref, vb_ref, oa_ref, ob_ref):
    qa, qb = qa_ref[0], qb_ref[0]
    ka, va, kb, vb = ka_ref[0], va_ref[0], kb_ref[0], vb_ref[0]
    for hq in range(A_Q_HEADS):
        sl = slice(hq * HEAD_DIM, (hq + 1) * HEAD_DIM)
        hk = hq // A_GROUP
        kv = slice(hk * HEAD_DIM, (hk + 1) * HEAD_DIM)
        q = qa[:, sl]
        o = _attend([_dot_t(q, ka[:, kv])], [_with_ones(va[:, kv])], sink=sink_ref[hq])
        oa_ref[0, :, sl] = o.astype(BF16)
    for h in range(B_HEADS):
        sl = slice(h * HEAD_DIM, (h + 1) * HEAD_DIM)
        q = qb[:, sl]
        o = _attend([_dot_t(q, kb[:, sl])], [_with_ones(vb[:, sl])])
        ob_ref[0, :, sl] = o.astype(BF16)


def context_attention(qa, qb, ka, va, kb, vb, sink):
    bsz, n_ctx, _ = qa.shape
    blk = lambda w: pl.BlockSpec((1, n_ctx, w), lambda b, s: (b, 0, 0))
    grid_spec = pltpu.PrefetchScalarGridSpec(
        num_scalar_prefetch=1,
        grid=(bsz,),
        in_specs=[blk(A_Q_W), blk(B_W), blk(A_KV_W), blk(A_KV_W), blk(B_W), blk(B_W)],
        out_specs=[blk(A_Q_W), blk(B_W)],
    )
    return pl.pallas_call(
        _ctx_attn_kernel,
        grid_spec=grid_spec,
        out_shape=[jax.ShapeDtypeStruct((bsz, n_ctx, A_Q_W), BF16),
                   jax.ShapeDtypeStruct((bsz, n_ctx, B_W), BF16)],
        compiler_params=_params("parallel"),
        name="context_attention",
    )(sink, qa, qb, ka, va, kb, vb)


CONV_HALO = 16
CONV_SHIFT_SPAN = (CONV_HALO + C_CONV_WIDTH // 2) // F32_SUBLANES * F32_SUBLANES


def _conv_kernel(prev_ref, cur_ref, next_ref, w_ref, b_ref, g_ref, beta_ref, out_ref, shifted_ref, *, ts):
    i = pl.program_id(1)
    n_i = pl.num_programs(1)
    ext = jnp.concatenate([jnp.where(i > 0, prev_ref[0], 0.0), cur_ref[0],
                           jnp.where(i < n_i - 1, next_ref[0], 0.0)], axis=0)
    for r in range(F32_SUBLANES):
        shifted_ref[r] = ext[r:r + ts + CONV_SHIFT_SPAN]
    acc = jnp.zeros((ts, C_CHANNELS), F32)
    for k in range(C_CONV_WIDTH):
        start = CONV_HALO - C_CONV_WIDTH // 2 + k
        aligned = start - start % F32_SUBLANES
        acc = acc + shifted_ref[start % F32_SUBLANES, aligned:aligned + ts] * w_ref[k:k + 1]
    y = _layer_norm(acc + b_ref[...]) * g_ref[...] + beta_ref[...]
    out_ref[0] = (y * jax.nn.sigmoid(y)).astype(BF16)


def conformer_conv(hc, conv_w, conv_b, ln_g, ln_b):
    bx, t, ch = hc.shape
    ts = min(SMALL_TILE, t)
    per = ts // CONV_HALO
    last = t // CONV_HALO - 1
    row = lambda v: v.reshape(1, ch)
    const = lambda b, i: (0, 0)
    return pl.pallas_call(
        functools.partial(_conv_kernel, ts=ts),
        grid=(bx, t // ts),
        in_specs=[
            pl.BlockSpec((1, CONV_HALO, ch), lambda b, i: (b, jnp.maximum(i * per - 1, 0), 0)),
            pl.BlockSpec((1, ts, ch), lambda b, i: (b, i, 0)),
            pl.BlockSpec((1, CONV_HALO, ch), lambda b, i: (b, jnp.minimum((i + 1) * per, last), 0)),
            pl.BlockSpec((C_CONV_WIDTH, ch), const),
            pl.BlockSpec((1, ch), const), pl.BlockSpec((1, ch), const), pl.BlockSpec((1, ch), const),
        ],
        out_specs=pl.BlockSpec((1, ts, ch), lambda b, i: (b, i, 0)),
        out_shape=jax.ShapeDtypeStruct((bx, t, ch), BF16),
        scratch_shapes=[pltpu.VMEM((F32_SUBLANES, ts + CONV_SHIFT_SPAN, ch), F32)],
        compiler_params=_params("parallel", "parallel"),
        name="conformer_conv",
    )(hc, hc, hc, conv_w, row(conv_b), row(ln_g), row(ln_b))


OUTPROJ_CHUNK_ROWS = 128


def _outproj_kernel(oa_ref, ob_ref, oc_ref, x_ref, g1_ref, sh_ref, sc_ref, w_ref, lng_ref, lnb_ref,
                    wr_hi_ref, wr_lo_ref, x1_ref, h2_ref, afft_ref, aff_ref, *, alpha):
    tm, d = x_ref.shape[1], x_ref.shape[2]
    rc = min(OUTPROJ_CHUNK_ROWS, tm)
    chunks = [slice(k * rc, (k + 1) * rc) for k in range(tm // rc)]
    outs = [(_dot(oa_ref[0, r], w_ref[0:A_Q_W])
             + _dot(ob_ref[0, r], w_ref[A_Q_W:A_Q_W + B_W])
             + _dot(oc_ref[0, r], w_ref[A_Q_W + B_W:])) for r in chunks]
    w_hi, w_lo = wr_hi_ref[...], wr_lo_ref[...]
    eye = (lax.broadcasted_iota(jnp.int32, (N_EXPERTS, N_EXPERTS), 0)
           == lax.broadcasted_iota(jnp.int32, (N_EXPERTS, N_EXPERTS), 1)).astype(BF16)
    rows_per_token = d // LANES
    ys = [_layer_norm(alpha * x_ref[0, r] + g1_ref[0] * o) * lng_ref[...] + lnb_ref[...]
          for r, o in zip(chunks, outs)]
    for r, y in zip(chunks, ys):
        x1_ref[0, r] = y
    h2s = [_layer_norm(y) * (1.0 + sc_ref[0]) + sh_ref[0] for y in ys]
    his = [h2.astype(BF16) for h2 in h2s]
    los = [(h2 - hi.astype(F32)).astype(BF16) for h2, hi in zip(h2s, his)]
    all_logits = [_dot_t(hi, w_hi) + _dot_t(lo, w_hi) + _dot_t(hi, w_lo) for hi, lo in zip(his, los)]
    for k, h2 in enumerate(h2s):
        for j in range(rows_per_token):
            h2_ref[0, pl.ds(k * rc * rows_per_token + j, rc, stride=rows_per_token), :] = (
                h2[:, j * LANES:(j + 1) * LANES])
    for r, logits in zip(chunks, all_logits):
        e_n = jnp.exp(logits - logits.max(axis=1, keepdims=True))
        aff = e_n / e_n.sum(axis=1, keepdims=True)
        aff_ref[0, r] = aff
        aff_t, rest = None, aff
        for _ in range(GATE_PARTS):
            part = rest.astype(BF16)
            term = _dot_t(eye, part)
            aff_t = term if aff_t is None else aff_t + term
            rest = rest - part.astype(F32)
        afft_ref[0, :, r] = aff_t


def out_projection(oa, ob, oc, x, g1, sh2, sc2, w_out_bf16, ln_g, ln_b, wr_hi, wr_lo, alpha):
    bx, t, d = x.shape
    tm = min(TOKEN_TILE, t)
    tok = lambda b, i: (b, i, 0)
    per_b = lambda b, i: (b, 0, 0)
    const = lambda b, i: (0, 0)
    vec = pl.BlockSpec((1, d), const)
    return pl.pallas_call(
        functools.partial(_outproj_kernel, alpha=alpha),
        grid=(bx, t // tm),
        in_specs=[
            pl.BlockSpec((1, tm, A_Q_W), tok), pl.BlockSpec((1, tm, B_W), tok),
            pl.BlockSpec((1, tm, C_CHANNELS), tok), pl.BlockSpec((1, tm, d), tok),
            pl.BlockSpec((1, 1, d), per_b), pl.BlockSpec((1, 1, d), per_b), pl.BlockSpec((1, 1, d), per_b),
            pl.BlockSpec(w_out_bf16.shape, const), vec, vec,
            pl.BlockSpec((N_EXPERTS, d), const), pl.BlockSpec((N_EXPERTS, d), const),
        ],
        out_specs=[pl.BlockSpec((1, tm, d), tok), pl.BlockSpec((1, tm * (d // LANES), LANES), tok),
                   pl.BlockSpec((1, N_EXPERTS, tm), lambda b, i: (b, 0, i)),
                   pl.BlockSpec((1, tm, N_EXPERTS), tok)],
        out_shape=[jax.ShapeDtypeStruct((bx, t, d), F32),
                   jax.ShapeDtypeStruct((bx, t * (d // LANES), LANES), F32),
                   jax.ShapeDtypeStruct((bx, N_EXPERTS, t), F32),
                   jax.ShapeDtypeStruct((bx, t, N_EXPERTS), F32)],
        compiler_params=_params("parallel", "parallel"),
        name="out_projection",
    )(oa, ob, oc, x, g1, sh2, sc2, w_out_bf16, ln_g.reshape(1, d), ln_b.reshape(1, d), wr_hi, wr_lo)


def _select_kernel(afft_ref, pos_ref, off_ref, *, cap, n_tok):
    aff = afft_ref[0]

    def bit_step(j, bits):
        cand = bits | (jnp.int32(1) << (30 - j))
        cnt = jnp.sum((aff >= pltpu.bitcast(cand, F32)).astype(jnp.int32), axis=1, keepdims=True)
        return jnp.where(cnt >= cap, cand, bits)

    thr = pltpu.bitcast(lax.fori_loop(0, 31, bit_step, jnp.zeros((N_EXPERTS, 1), jnp.int32)), F32)
    above = (aff > thr).astype(F32)
    tied = (aff == thr).astype(F32)
    need = cap - jnp.sum(above, axis=1, keepdims=True)

    blk = MXU_DEPTH
    blocks = [slice(k * blk, (k + 1) * blk) for k in range(n_tok // blk)]
    r_i = lax.broadcasted_iota(jnp.int32, (blk, blk), 0)
    c_i = lax.broadcasted_iota(jnp.int32, (blk, blk), 1)
    strict_upper = (r_i < c_i).astype(BF16)

    def running(block_sums):
        run, total = [], jnp.zeros((N_EXPERTS, 1), F32)
        for s in block_sums:
            run.append(total)
            total = total + s
        return run

    tied_b = [tied[:, sl] for sl in blocks]
    tied_rank = [_dot(t.astype(BF16), strict_upper) for t in tied_b]
    tied_before = running([t.sum(axis=1, keepdims=True) for t in tied_b])
    sel_b = [above[:, sl] + t * ((before + rank) < need).astype(F32)
             for sl, t, before, rank in zip(blocks, tied_b, tied_before, tied_rank)]
    sel_rank = [_dot(s.astype(BF16), strict_upper) for s in sel_b]
    half_sums = [[s[:, h * OFFSET_BLOCK:(h + 1) * OFFSET_BLOCK].sum(axis=1, keepdims=True)
                  for h in range(blk // OFFSET_BLOCK)] for s in sel_b]
    offs = running([h for hs in half_sums for h in hs])
    per_blk = blk // OFFSET_BLOCK
    for k, (sl, s, rank) in enumerate(zip(blocks, sel_b, sel_rank)):
        pos_ref[0, :, sl] = jnp.where(s > 0.5, (offs[k * per_blk] + rank).astype(jnp.int32), -1)
    off_ref[0] = jnp.concatenate(offs, axis=1).astype(jnp.int32)


def expert_choice_select(aff_t, cap):
    bx, n_e, t = aff_t.shape
    n_tb = t // OFFSET_BLOCK
    return pl.pallas_call(
        functools.partial(_select_kernel, cap=cap, n_tok=t),
        grid=(bx,),
        in_specs=[pl.BlockSpec((1, n_e, t), lambda b: (b, 0, 0))],
        out_specs=[pl.BlockSpec((1, n_e, t), lambda b: (b, 0, 0)),
                   pl.BlockSpec((1, n_e, n_tb), lambda b: (b, 0, 0))],
        out_shape=[jax.ShapeDtypeStruct((bx, n_e, t), jnp.int32),
                   jax.ShapeDtypeStruct((bx, n_e, n_tb), jnp.int32)],
        compiler_params=_params("parallel"),
        name="expert_choice_select",
    )(aff_t)


GATE_PARTS = 3
TOKEN_LANE = GATE_PARTS * N_EXPERTS


def _slot_table_kernel(off_ref, pos_ref, aff_ref, tbl_ref, *, n_off, blocks_per_step):
    b, eg, kc = pl.program_id(0), pl.program_id(1), pl.program_id(2)
    epg = EXPERTS_PER_GATHER_STEP

    @pl.when(kc == 0)
    def _():
        tbl_ref[...] = jnp.zeros_like(tbl_ref)

    slot = lax.broadcasted_iota(jnp.int32, (GATHER_WINDOW, 1), 0)
    slot_2d = lax.broadcasted_iota(jnp.int32, (GATHER_WINDOW, GATHER_BLOCK), 0)
    lane = lax.broadcasted_iota(jnp.int32, (GATHER_BLOCK, LANES), 1)
    out_lane = lax.broadcasted_iota(jnp.int32, (GATHER_WINDOW, LANES), 1)
    local_token = lax.broadcasted_iota(jnp.int32, (GATHER_BLOCK, LANES), 0).astype(F32)
    place_r = lax.broadcasted_iota(jnp.int32, (N_EXPERTS, LANES), 0)
    place_c = lax.broadcasted_iota(jnp.int32, (N_EXPERTS, LANES), 1)
    def payload_of(kk):
        payload = jnp.where(lane == TOKEN_LANE, local_token, 0.0)
        rest = aff_ref[0, kk * GATHER_BLOCK:(kk + 1) * GATHER_BLOCK]
        for k in range(GATE_PARTS):
            part = rest.astype(BF16)
            payload = payload + _dot(part, (place_c == place_r + k * N_EXPERTS).astype(BF16))
            rest = rest - part.astype(F32)
        return payload.astype(BF16)

    payloads = [payload_of(kk) for kk in range(blocks_per_step)]
    pending = []
    for kk in range(blocks_per_step):
        kb = kc * blocks_per_step + kk
        first_token = jnp.where(out_lane == TOKEN_LANE, (kb * GATHER_BLOCK).astype(F32), 0.0)
        for ee in range(epg):
            off = off_ref[(b * N_EXPERTS + eg * epg + ee) * n_off + kb * (GATHER_BLOCK // OFFSET_BLOCK)]
            base = pl.multiple_of(off - (off & (WINDOW_ALIGN - 1)), WINDOW_ALIGN)
            pos_row = pos_ref[0, ee:ee + 1, kk * GATHER_BLOCK:(kk + 1) * GATHER_BLOCK]
            onehot = (pos_row - base == slot_2d).astype(BF16)
            pending.append((ee, off, base, _dot(onehot, payloads[kk]) + first_token))
    for ee, off, base, gathered in pending:
        win = pl.ds(base, GATHER_WINDOW)
        tbl_ref[0, ee, win, :] = jnp.where(slot >= off - base, gathered, tbl_ref[0, ee, win, :])


def slot_table(aff, pos, block_off, cap_pad):
    bx, t, _ = aff.shape
    n_tb = t // GATHER_BLOCK
    blocks_per_step = min(4, n_tb)
    n_steps = n_tb // blocks_per_step
    epg = EXPERTS_PER_GATHER_STEP
    tokens = blocks_per_step * GATHER_BLOCK
    grid_spec = pltpu.PrefetchScalarGridSpec(
        num_scalar_prefetch=1,
        grid=(bx, N_EXPERTS // epg, n_steps),
        in_specs=[
            pl.BlockSpec((1, epg, tokens), lambda b, g, k, s: (b, g, k)),
            pl.BlockSpec((1, tokens, N_EXPERTS), lambda b, g, k, s: (b, k, 0)),
        ],
        out_specs=pl.BlockSpec((1, epg, cap_pad, LANES), lambda b, g, k, s: (b, g, 0, 0)),
    )
    return pl.pallas_call(
        functools.partial(_slot_table_kernel, n_off=t // OFFSET_BLOCK, blocks_per_step=blocks_per_step),
        grid_spec=grid_spec,
        out_shape=jax.ShapeDtypeStruct((bx, N_EXPERTS, cap_pad, LANES), F32),
        compiler_params=_params("parallel", "parallel", "arbitrary"),
        name="slot_table",
    )(block_off.reshape(-1), pos, aff)


GATHER_UNROLL = 16


def _row_gather_kernel(idx_ref, src_ref, xs_ref, tile_ref, *, cap, rows_per_token, chunk_stride):
    b, e = pl.program_id(0), pl.program_id(1)
    first = (b * N_EXPERTS + e) * cap

    def group(g, carry):
        for u in range(GATHER_UNROLL):
            s = g * GATHER_UNROLL + u
            row = pl.multiple_of(idx_ref[first + s], rows_per_token)
            tile_ref[pl.ds(s, rows_per_token, stride=chunk_stride), :] = src_ref[0, pl.ds(row, rows_per_token), :]
        return carry

    lax.fori_loop(0, cap // GATHER_UNROLL, group, 0)
    for j in range(rows_per_token):
        xs_ref[0, 0, :, j * LANES:(j + 1) * LANES] = tile_ref[j * chunk_stride:j * chunk_stride + cap, :].astype(BF16)


def gather_rows(h2_rows, token_row, cap, d):
    bx, n_rows, _ = h2_rows.shape
    rows_per_token = d // LANES
    chunk_stride = cap + F32_SUBLANES
    grid_spec = pltpu.PrefetchScalarGridSpec(
        num_scalar_prefetch=1,
        grid=(bx, N_EXPERTS),
        in_specs=[pl.BlockSpec((1, n_rows, LANES), lambda b, e, s: (b, 0, 0), pipeline_mode=pl.Buffered(1))],
        out_specs=pl.BlockSpec((1, 1, cap, d), lambda b, e, s: (b, e, 0, 0)),
        scratch_shapes=[pltpu.VMEM((rows_per_token * chunk_stride, LANES), F32)],
    )
    return pl.pallas_call(
        functools.partial(_row_gather_kernel, cap=cap, rows_per_token=rows_per_token, chunk_stride=chunk_stride),
        grid_spec=grid_spec,
        out_shape=jax.ShapeDtypeStruct((bx, N_EXPERTS, cap, d), BF16),
        compiler_params=_params("parallel", "arbitrary"),
        name="gather_rows",
    )(token_row, h2_rows)


def _ffn_kernel(xs_ref, gs_ref, wg_ref, wu_ref, wd_ref, ye_ref, wg_bf, wu_bf, wd_bf, *, cap, row_tile):
    @pl.when(pl.program_id(1) == 0)
    def _():
        wg_bf[...] = wg_ref[0, 0].astype(BF16)
        wu_bf[...] = wu_ref[0, 0].astype(BF16)
        wd_bf[...] = wd_ref[0, 0].astype(BF16)

    n_b = xs_ref.shape[0]
    if n_b == 1:
        tiles = [[(0, r0, row_tile)] for r0 in range(0, cap, row_tile)]
    else:
        tiles = [[(bb, 0, cap) for bb in range(n_b)]]
    for tile in tiles:
        x = jnp.concatenate([xs_ref[bb, 0, r0:r0 + n] for bb, r0, n in tile], axis=0)
        terms = jnp.concatenate([gs_ref[bb, 0, r0:r0 + n, :] for bb, r0, n in tile], axis=0)
        lane = lax.broadcasted_iota(jnp.int32, terms.shape, 1)
        own = ((lane & (N_EXPERTS - 1)) == pl.program_id(0)) & (lane < TOKEN_LANE)
        terms = jnp.where(own, terms, 0.0)
        gate = _dot(x, wg_bf[...])
        up = _dot(x, wu_bf[...])
        hid = (gate * jax.nn.sigmoid(gate) * up).astype(BF16)
        g = jnp.sum(terms, axis=1, keepdims=True)
        ye = _dot(hid, wd_bf[...]) * g
        rows_per_slot = ye.shape[1] // LANES
        row = 0
        for bb, r0, n in tile:
            for j in range(rows_per_slot):
                ye_ref[bb, 0, pl.ds(r0 * rows_per_slot + j, n, stride=rows_per_slot), :] = (
                    ye[row:row + n, j * LANES:(j + 1) * LANES])
            row += n


def expert_ffn(xs, gs, wg, wu, wd, layer):
    bx, n_e, cap, d = xs.shape
    ff = wg.shape[-1]
    slot_rows = cap * (d // LANES)
    max_rows = SMALL_TILE
    row_tile = min(max_rows, cap)
    n_b = bx if bx * cap <= max_rows else 1
    return pl.pallas_call(
        functools.partial(_ffn_kernel, cap=cap, row_tile=row_tile),
        grid=(n_e, bx // n_b),
        in_specs=[
            pl.BlockSpec((n_b, 1, cap, d), lambda e, b: (b, e, 0, 0)),
            pl.BlockSpec((n_b, 1, cap, LANES), lambda e, b: (b, e, 0, 0)),
            pl.BlockSpec((1, 1, d, ff), lambda e, b: (layer, e, 0, 0)),
            pl.BlockSpec((1, 1, d, ff), lambda e, b: (layer, e, 0, 0)),
            pl.BlockSpec((1, 1, ff, d), lambda e, b: (layer, e, 0, 0)),
        ],
        out_specs=pl.BlockSpec((n_b, 1, slot_rows, LANES), lambda e, b: (b, e, 0, 0)),
        out_shape=jax.ShapeDtypeStruct((bx, n_e, slot_rows, LANES), F32),
        scratch_shapes=[pltpu.VMEM((d, ff), BF16), pltpu.VMEM((d, ff), BF16), pltpu.VMEM((ff, d), BF16)],
        compiler_params=_params("parallel", "arbitrary"),
        name="expert_ffn",
    )(xs, gs, wg, wu, wd)


SCATTER_UNROLL = 16


def _scatter_norm_kernel(idx_ref, ye_ref, x1_ref, g2_ref, lng_ref, lnb_ref, out_ref, acc_ref, *,
                         cap, rows_per_token, alpha):
    b, step = pl.program_id(0), pl.program_id(1)

    @pl.when(step == 0)
    def _():
        acc_ref[...] = jnp.zeros_like(acc_ref)

    @pl.when(step < N_EXPERTS)
    def _():
        first = (b * N_EXPERTS + step) * cap

        def group(g, carry):
            rows, sums = [], []
            for u in range(SCATTER_UNROLL):
                s = g * SCATTER_UNROLL + u
                row = pl.multiple_of(idx_ref[first + s], rows_per_token)
                src = pl.multiple_of(s * rows_per_token, rows_per_token)
                rows.append(row)
                sums.append(acc_ref[pl.ds(row, rows_per_token), :] + ye_ref[0, 0, pl.ds(src, rows_per_token), :])
            for row, total in zip(rows, sums):
                acc_ref[pl.ds(row, rows_per_token), :] = total
            return carry

        lax.fori_loop(0, cap // SCATTER_UNROLL, group, 0)

    @pl.when(step >= N_EXPERTS)
    def _():
        tm = x1_ref.shape[1]
        tile_row = pl.multiple_of((step - N_EXPERTS) * (tm * rows_per_token), rows_per_token)
        moe = jnp.concatenate([acc_ref[pl.ds(tile_row + j, tm, stride=rows_per_token), :]
                               for j in range(rows_per_token)], axis=1)
        y = alpha * x1_ref[0] + g2_ref[0] * moe
        out_ref[0] = _layer_norm(y) * lng_ref[...] + lnb_ref[...]


def scatter_and_norm(ye_rows, token_row, x1, g2, ln_g, ln_b, alpha, cap):
    bx, t, d = x1.shape
    n_e, slot_rows = ye_rows.shape[1], ye_rows.shape[2]
    rows_per_token = d // LANES
    tm = min(SMALL_TILE, t)
    tile = lambda b, i, s: (b, jnp.maximum(i - n_e, 0), 0)
    const = lambda b, i, s: (0, 0)
    grid_spec = pltpu.PrefetchScalarGridSpec(
        num_scalar_prefetch=1,
        grid=(bx, n_e + t // tm),
        in_specs=[
            pl.BlockSpec((1, 1, slot_rows, LANES), lambda b, i, s: (b, jnp.minimum(i, n_e - 1), 0, 0)),
            pl.BlockSpec((1, tm, d), tile),
            pl.BlockSpec((1, 1, d), lambda b, i, s: (b, 0, 0)),
            pl.BlockSpec((1, d), const), pl.BlockSpec((1, d), const),
        ],
        out_specs=pl.BlockSpec((1, tm, d), tile),
        scratch_shapes=[pltpu.VMEM((t * rows_per_token, LANES), F32)],
    )
    return pl.pallas_call(
        functools.partial(_scatter_norm_kernel, cap=cap, rows_per_token=rows_per_token, alpha=alpha),
        grid_spec=grid_spec,
        out_shape=jax.ShapeDtypeStruct((bx, t, d), F32),
        compiler_params=_params("parallel", "arbitrary"),
        name="scatter_and_norm",
    )(token_row, ye_rows, x1, g2, ln_g.reshape(1, d), ln_b.reshape(1, d))


def _split_bf16(w):
    hi = w.astype(BF16)
    return hi, (w - hi.astype(F32)).astype(BF16)


def _mixer_tail(oa, ob, oc, x, mod, lw, alpha):
    g1, sh2, sc2, g2 = mod
    t = x.shape[1]
    cap = EC_CAPACITY * t // N_EXPERTS
    cap_pad = cap + SLOT_PAD
    x1, h2_rows, aff_t, aff = out_projection(oa, ob, oc, x, g1, sh2, sc2, lw["w_out"], lw["ln1_g"], lw["ln1_b"],
                                        lw["wr_hi"], lw["wr_lo"], alpha)
    pos, block_off = expert_choice_select(aff_t, cap)
    table = slot_table(aff, pos, block_off, cap_pad)
    token_row = (table[:, :, :cap, TOKEN_LANE].astype(jnp.int32) * (x.shape[2] // LANES)).reshape(-1)
    xs = gather_rows(h2_rows, token_row, cap, x.shape[2])
    ye_rows = expert_ffn(xs, table, lw["w_gate"], lw["w_up"], lw["w_down"], lw["layer"])
    return scatter_and_norm(ye_rows, token_row, x1, g2, lw["ln2_g"], lw["ln2_b"], alpha, cap)


def kernel(x, c, ctx, c_ctx, w_mod, b_mod, w_in, a_sink, nat_bias, conv_w, conv_b, conv_ln_g, conv_ln_b,
           w_out, ln1_g, ln1_b, w_router, w_gate, w_up, w_down, ln2_g, ln2_b):
    bsz, n_lat, d = x.shape
    depth = w_mod.shape[0]
    alpha = (2 * depth) ** 0.25
    cos_t, sin_t = rope_tables(n_lat)

    cond = jnp.concatenate([c, c_ctx[None, :], jnp.zeros((F32_SUBLANES - bsz - 1, d), F32)], axis=0)
    mods = adaln_all(cond, w_mod, b_mod)

    for l in range(depth):
        last = l == depth - 1
        wr_hi, wr_lo = _split_bf16(w_router[l].T)
        lw = dict(w_out=w_out[l].astype(BF16), ln1_g=ln1_g[l], ln1_b=ln1_b[l], wr_hi=wr_hi, wr_lo=wr_lo,
                  w_gate=w_gate, w_up=w_up, w_down=w_down, layer=l,
                  ln2_g=ln2_g[l], ln2_b=ln2_b[l])
        w_in_l = w_in[l].astype(BF16)
        lat = [mods[l, :bsz, k * d:(k + 1) * d][:, None, :] for k in range(N_MOD)]
        cm = [jnp.broadcast_to(mods[l, bsz, k * d:(k + 1) * d][None, None, :], (bsz, 1, d))
              for k in range(N_MOD)]
        conv_args = (conv_w[l], conv_b[l], conv_ln_g[l], conv_ln_b[l])

        qa_c, ka_c, va_c, qb_c, kb_c, vb_c, hc_c = in_projection(ctx, cm[0], cm[1], w_in_l, cos_t, sin_t, rope=False)
        if not last:
            oa_c, ob_c = context_attention(qa_c, qb_c, ka_c, va_c, kb_c, vb_c, a_sink[l])
            oc_c = conformer_conv(hc_c, *conv_args)
            ctx_new = _mixer_tail(oa_c, ob_c, oc_c, ctx, (cm[2], cm[3], cm[4], cm[5]), lw, alpha)

        qa, ka, va, qb, kb, vb, hc = in_projection(x, lat[0], lat[1], w_in_l, cos_t, sin_t, rope=True)
        oa = window_attention(qa, ka, va, ka_c, va_c, a_sink[l])
        ob = neighbourhood_attention(qb, kb, vb, kb_c, vb_c, nat_bias[l])
        oc = conformer_conv(hc, *conv_args)
        x = _mixer_tail(oa, ob, oc, x, (lat[2], lat[3], lat[4], lat[5]), lw, alpha)
        if not last:
            ctx = ctx_new
    return x
```

```python
import functools

import numpy as np
import jax
import jax.numpy as jnp
from jax import lax
from jax.experimental import pallas as pl
from jax.experimental.pallas import tpu as pltpu

HEAD_DIM = 64
GRID_W = 64
A_Q_HEADS = 8
A_KV_HEADS = 2
A_GROUP = A_Q_HEADS // A_KV_HEADS
A_WINDOW = 128
B_HEADS = 4
NA_ROWS = 8
NA_COLS = 16
C_CHANNELS = 256
C_CONV_WIDTH = 31
A_Q_W = A_Q_HEADS * HEAD_DIM
A_KV_W = A_KV_HEADS * HEAD_DIM
B_W = B_HEADS * HEAD_DIM
OFF_AK = A_Q_W
OFF_AV = OFF_AK + A_KV_W
OFF_BQ = OFF_AV + A_KV_W
OFF_BK = OFF_BQ + B_W
OFF_BV = OFF_BK + B_W
OFF_C = OFF_BV + B_W
IN_WIDTH = OFF_C + 2 * C_CHANNELS
ROPE_WIDTH = A_Q_W + A_KV_W
N_EXPERTS = 16
EC_CAPACITY = 2
ROPE_BASE = 10000.0
LN_EPS = 1e-6
N_MOD = 6
NEG_INF = -1e30
QK_SCALE = HEAD_DIM ** -0.5

LANES = 128
WINDOW_ALIGN = 16
MXU_DEPTH = 256
OFFSET_BLOCK = 128
GATHER_BLOCK = MXU_DEPTH
GATHER_WINDOW = GATHER_BLOCK + WINDOW_ALIGN
SLOT_PAD = 3 * LANES
EXPERTS_PER_GATHER_STEP = 16
TABLE_BLOCKS_PER_STEP = 8
F32_SUBLANES = 8
TOKEN_TILE = 1024
SMALL_TILE = 512
MOD_COLUMN_TILE = 1536
VMEM_LIMIT = 56 * 1024 * 1024

F32 = jnp.float32
BF16 = jnp.bfloat16


def _dot(a, b):
    return jnp.dot(a, b, preferred_element_type=F32)


def _dot_t(a, b):
    return lax.dot_general(a, b, (((1,), (1,)), ((), ())), preferred_element_type=F32)


def _layer_norm(x):
    mu = jnp.mean(x, axis=-1, keepdims=True)
    xc = x - mu
    var = jnp.mean(xc * xc, axis=-1, keepdims=True)
    return xc * lax.rsqrt(var + LN_EPS)


def _params(*sem):
    return pltpu.CompilerParams(dimension_semantics=sem, vmem_limit_bytes=VMEM_LIMIT)


def _mod_kernel(cond_ref, w_ref, b_ref, out_ref):
    cnd = cond_ref[...]
    act = cnd * jax.nn.sigmoid(cnd)
    out_ref[0] = jnp.dot(act, w_ref[0], preferred_element_type=F32,
                         precision=lax.Precision.HIGHEST) + b_ref[0]


def adaln_all(cond, w_mod, b_mod):
    n_layers, d, width = w_mod.shape
    rows = cond.shape[0]
    tn = MOD_COLUMN_TILE
    return pl.pallas_call(
        _mod_kernel,
        grid=(n_layers, width // tn),
        in_specs=[
            pl.BlockSpec((rows, d), lambda l, j: (0, 0)),
            pl.BlockSpec((1, d, tn), lambda l, j: (l, 0, j)),
            pl.BlockSpec((1, 1, tn), lambda l, j: (l, 0, j)),
        ],
        out_specs=pl.BlockSpec((1, rows, tn), lambda l, j: (l, 0, j)),
        out_shape=jax.ShapeDtypeStruct((n_layers, rows, width), F32),
        compiler_params=_params("parallel", "parallel"),
        name="adaln",
    )(cond, w_mod, b_mod.reshape(n_layers, 1, width))


def _inproj_kernel(x_ref, sh_ref, sc_ref, w_ref, cos_ref, sin_ref,
                   qa_ref, ka_ref, va_ref, qb_ref, kb_ref, vb_ref, hc_ref, *, rope):
    x = x_ref[0]
    h = _layer_norm(x) * (1.0 + sc_ref[0]) + sh_ref[0]
    u = _dot(h.astype(BF16), w_ref[...])

    def rotated(col):
        xq = u[:, col:col + LANES]
        if not rope:
            return xq
        lane = lax.broadcasted_iota(jnp.int32, xq.shape, 1)
        first = (lane & (HEAD_DIM // 2 - 1)) < (HEAD_DIM // 4)
        partner = jnp.where(first, pltpu.roll(xq, LANES - HEAD_DIM // 4, 1),
                            pltpu.roll(xq, HEAD_DIM // 4, 1))
        return xq * cos_ref[...] + partner * sin_ref[...]

    rot = [rotated(col) for col in range(0, ROPE_WIDTH, LANES)]
    n_q = A_Q_W // LANES
    qa_ref[0] = (jnp.concatenate(rot[:n_q], axis=1) * QK_SCALE).astype(BF16)
    ka_ref[0] = jnp.concatenate(rot[n_q:], axis=1).astype(BF16)
    va_ref[0] = u[:, OFF_AV:OFF_BQ].astype(BF16)
    qb_ref[0] = (u[:, OFF_BQ:OFF_BK] * QK_SCALE).astype(BF16)
    kb_ref[0] = u[:, OFF_BK:OFF_BV].astype(BF16)
    vb_ref[0] = u[:, OFF_BV:OFF_C].astype(BF16)
    a = u[:, OFF_C:OFF_C + C_CHANNELS]
    gate = u[:, OFF_C + C_CHANNELS:]
    hc_ref[0] = a * jax.nn.sigmoid(gate)


def in_projection(x, shift, scale, w_in_bf16, cos_t, sin_t, *, rope):
    bx, t, d = x.shape
    tm = min(TOKEN_TILE, t)
    widths = (A_Q_W, A_KV_W, A_KV_W, B_W, B_W, B_W, C_CHANNELS)
    dtypes = (BF16,) * 6 + (F32,)
    tok = lambda b, i: (b, i, 0)
    per_b = lambda b, i: (b, 0, 0)
    return pl.pallas_call(
        functools.partial(_inproj_kernel, rope=rope),
        grid=(bx, t // tm),
        in_specs=[
            pl.BlockSpec((1, tm, d), tok),
            pl.BlockSpec((1, 1, d), per_b),
            pl.BlockSpec((1, 1, d), per_b),
            pl.BlockSpec((d, IN_WIDTH), lambda b, i: (0, 0)),
            pl.BlockSpec((tm, LANES), lambda b, i: (i, 0)),
            pl.BlockSpec((tm, LANES), lambda b, i: (i, 0)),
        ],
        out_specs=[pl.BlockSpec((1, tm, w), tok) for w in widths],
        out_shape=[jax.ShapeDtypeStruct((bx, t, w), dt) for w, dt in zip(widths, dtypes)],
        compiler_params=_params("parallel", "parallel"),
        name="in_projection",
    )(x, shift, scale, w_in_bf16, cos_t, sin_t)


def rope_tables(n_tokens):
    t = jnp.arange(n_tokens, dtype=jnp.int32)
    row = (t // GRID_W).astype(F32)[:, None]
    col = (t % GRID_W).astype(F32)[:, None]
    n_freq = HEAD_DIM // 4
    inv_freq = ROPE_BASE ** (-jnp.arange(n_freq, dtype=F32) / n_freq)
    ang_r = row * inv_freq
    ang_c = col * inv_freq
    cos_h = jnp.concatenate([jnp.cos(ang_r), jnp.cos(ang_r), jnp.cos(ang_c), jnp.cos(ang_c)], axis=1)
    sin_h = jnp.concatenate([-jnp.sin(ang_r), jnp.sin(ang_r), -jnp.sin(ang_c), jnp.sin(ang_c)], axis=1)
    reps = LANES // HEAD_DIM
    return jnp.tile(cos_h, (1, reps)), jnp.tile(sin_h, (1, reps))


def _with_ones(v):
    return jnp.concatenate([v, jnp.ones_like(v)], axis=1)


def _attend(score_parts, values, sink=None):
    m = score_parts[0].max(axis=-1, keepdims=True)
    for s in score_parts[1:]:
        m = jnp.maximum(m, s.max(axis=-1, keepdims=True))
    if sink is not None:
        m = jnp.maximum(m, sink)
    acc = None
    for s, v in zip(score_parts, values):
        term = _dot(jnp.exp((s - m).astype(BF16)), v)
        acc = term if acc is None else acc + term
    den = acc[:, HEAD_DIM:HEAD_DIM + 1]
    if sink is not None:
        den = den + jnp.exp(sink - m)
    return acc[:, :HEAD_DIM] / den


def _attn_a_kernel(sink_ref, q_ref, kp_ref, kc_ref, kn_ref, vp_ref, vc_ref, vn_ref,
                   kctx_ref, vctx_ref, out_ref, *, n_lat, tq):
    i = pl.program_id(1)
    k_win = jnp.concatenate([kp_ref[0], kc_ref[0], kn_ref[0]], axis=0)
    v_win = jnp.concatenate([vp_ref[0], vc_ref[0], vn_ref[0]], axis=0)
    kctx = kctx_ref[0]
    vctx = vctx_ref[0]
    sub = A_WINDOW
    span = 3 * A_WINDOW
    rows = A_GROUP * sub
    q_onehot = ((lax.broadcasted_iota(jnp.int32, (rows, sub), 0) & (sub - 1))
                == lax.broadcasted_iota(jnp.int32, (rows, sub), 1)).astype(BF16)
    key_i = lax.broadcasted_iota(jnp.int32, (span, sub), 0)
    qry_i = lax.broadcasted_iota(jnp.int32, (span, sub), 1)
    rel = key_i - A_WINDOW - qry_i
    in_band = (rel <= A_WINDOW) & (rel >= -A_WINDOW)
    group_of_row = lax.broadcasted_iota(jnp.int32, (rows, 1), 0) >> (sub.bit_length() - 1)
    sinks = []
    for hk in range(A_KV_HEADS):
        sink = jnp.zeros((rows, 1), F32)
        for g in range(A_GROUP):
            sink = jnp.where(group_of_row == g, sink_ref[hk * A_GROUP + g], sink)
        sinks.append(sink)
    vctx_ext = [_with_ones(vctx[:, hk * HEAD_DIM:(hk + 1) * HEAD_DIM]) for hk in range(A_KV_HEADS)]
    v_ext = [_with_ones(v_win[:, hk * HEAD_DIM:(hk + 1) * HEAD_DIM]) for hk in range(A_KV_HEADS)]
    def mask_columns(j):
        kpos = i * tq + j * sub - A_WINDOW + key_i
        valid = in_band & (kpos >= 0) & (kpos < n_lat)
        return jnp.where(valid, 0.0, NEG_INF).astype(BF16)

    masks = [mask_columns(j) for j in range(tq // sub)]

    def scores(j, hk):
        sl = slice(hk * HEAD_DIM, (hk + 1) * HEAD_DIM)
        q_rows = q_ref[0, j * sub:(j + 1) * sub]
        q = jnp.concatenate([q_rows[:, h * HEAD_DIM:(h + 1) * HEAD_DIM]
                             for h in range(hk * A_GROUP, (hk + 1) * A_GROUP)], axis=0)
        q_aug = jnp.concatenate([q_onehot, q], axis=1)
        k_aug = jnp.concatenate([masks[j], k_win[j * sub:j * sub + span, sl]], axis=1)
        return [_dot_t(q_aug, k_aug), _dot_t(q, kctx[:, sl])]

    units = [(j, hk) for j in range(tq // sub) for hk in range(A_KV_HEADS)]
    all_scores = [scores(j, hk) for j, hk in units]
    for (j, hk), s in zip(units, all_scores):
        v_sub = v_ext[hk][j * sub:j * sub + span]
        o = _attend(s, [v_sub, vctx_ext[hk]], sink=sinks[hk]).astype(BF16)
        for g in range(A_GROUP):
            h = hk * A_GROUP + g
            out_ref[0, j * sub:(j + 1) * sub, h * HEAD_DIM:(h + 1) * HEAD_DIM] = o[g * sub:(g + 1) * sub]


def window_attention(qa, ka, va, kc_a, vc_a, sink):
    bsz, n_lat, _ = qa.shape
    n_ctx = kc_a.shape[1]
    tq = min(TOKEN_TILE, n_lat)
    w = A_WINDOW
    per = tq // w
    last = n_lat // w - 1
    prev = lambda b, i, s: (b, jnp.maximum(i * per - 1, 0), 0)
    cur = lambda b, i, s: (b, i, 0)
    nxt = lambda b, i, s: (b, jnp.minimum((i + 1) * per, last), 0)
    ctx = lambda b, i, s: (b, 0, 0)
    kv_specs = [pl.BlockSpec((1, w, A_KV_W), prev), pl.BlockSpec((1, tq, A_KV_W), cur),
                pl.BlockSpec((1, w, A_KV_W), nxt)]
    grid_spec = pltpu.PrefetchScalarGridSpec(
        num_scalar_prefetch=1,
        grid=(bsz, n_lat // tq),
        in_specs=[pl.BlockSpec((1, tq, A_Q_W), cur)] + kv_specs + kv_specs + [
            pl.BlockSpec((1, n_ctx, A_KV_W), ctx), pl.BlockSpec((1, n_ctx, A_KV_W), ctx)],
        out_specs=pl.BlockSpec((1, tq, A_Q_W), cur),
    )
    return pl.pallas_call(
        functools.partial(_attn_a_kernel, n_lat=n_lat, tq=tq),
        grid_spec=grid_spec,
        out_shape=jax.ShapeDtypeStruct((bsz, n_lat, A_Q_W), BF16),
        compiler_params=_params("parallel", "parallel"),
        name="window_attention",
    )(sink, qa, ka, ka, ka, va, va, va, kc_a, vc_a)


NB_Q_ROWS = 4
NB_BLOCKS_PER_STEP = 4


def _attn_b_kernel(q_ref, *refs):
    n_kv = NB_BLOCKS_PER_STEP + 2
    k_refs, v_refs = refs[:n_kv], refs[n_kv:2 * n_kv]
    kctx_ref, vctx_ref = refs[2 * n_kv:2 * n_kv + 2]
    bias_refs = refs[2 * n_kv + 2:-1]
    out_ref = refs[-1]
    tb = k_refs[0].shape[1]
    k_all = jnp.concatenate([r[0] for r in k_refs], axis=0)
    v_all = jnp.concatenate([r[0] for r in v_refs], axis=0)
    kctx = kctx_ref[0]
    vctx = vctx_ref[0]
    heads = [slice(h * HEAD_DIM, (h + 1) * HEAD_DIM) for h in range(B_HEADS)]
    units = [(sb, h) for sb in range(NB_BLOCKS_PER_STEP) for h in range(B_HEADS)]

    def scores(sb, h):
        q = q_ref[0, sb * tb:(sb + 1) * tb, heads[h]]
        return [_dot_t(q, k_all[sb * tb:(sb + 3) * tb, heads[h]]) + bias_refs[sb][0, h],
                _dot_t(q, kctx[:, heads[h]])]

    all_scores = [scores(sb, h) for sb, h in units]
    v_ext = [_with_ones(v_all[:, sl]) for sl in heads]
    vctx_ext = [_with_ones(vctx[:, sl]) for sl in heads]
    for (sb, h), s in zip(units, all_scores):
        o = _attend(s, [v_ext[h][sb * tb:(sb + 3) * tb], vctx_ext[h]])
        out_ref[0, sb * tb:(sb + 1) * tb, heads[h]] = o.astype(BF16)


def neighbourhood_bias(rel_bias, n_lat):
    rows = n_lat // GRID_W
    kr_n = min(NA_ROWS, rows)
    n_blocks = rows // NB_Q_ROWS
    n_heads, n_dr, n_dc = rel_bias.shape
    cols = np.arange(GRID_W)
    c_start = np.clip(cols - NA_COLS // 2, 0, GRID_W - NA_COLS)
    col_ok = (cols[None, :] >= c_start[:, None]) & (cols[None, :] < c_start[:, None] + NA_COLS)
    dc = np.clip(cols[None, :] - cols[:, None], -(NA_COLS - 1), NA_COLS - 1) + NA_COLS - 1
    pick_dc = (dc.reshape(-1)[None, :] == np.arange(n_dc)[:, None]).astype(np.float32)
    toeplitz = jnp.dot(rel_bias.reshape(n_heads * n_dr, n_dc), pick_dc, precision=lax.Precision.HIGHEST)
    toeplitz = jnp.where(col_ok.reshape(-1), toeplitz, NEG_INF).reshape(n_heads, n_dr, GRID_W, GRID_W)
    q_rl = np.arange(NB_Q_ROWS)
    k_rl = np.arange(3 * NB_Q_ROWS)
    row_ok, dr = [], []
    for j in sorted({0, min(1, n_blocks - 1), n_blocks - 1}):
        r = NB_Q_ROWS * j + q_rl
        kr = NB_Q_ROWS * (j - 1) + k_rl
        r_start = np.clip(r - kr_n // 2, 0, rows - kr_n)
        ok = (kr[None, :] >= r_start[:, None]) & (kr[None, :] < r_start[:, None] + kr_n)
        row_ok.append(ok & (kr[None, :] >= 0) & (kr[None, :] < rows))
        dr.append(np.clip(kr[None, :] - r[:, None] + NA_ROWS - 1, 0, n_dr - 1))
    row_ok = np.stack(row_ok)
    dr = np.stack(dr)
    n_var = dr.shape[0]
    tq, tk = NB_Q_ROWS * GRID_W, 3 * NB_Q_ROWS * GRID_W
    table = pl.pallas_call(
        functools.partial(_bias_table_kernel, dr=dr, row_ok=row_ok),
        grid=(n_heads,),
        in_specs=[pl.BlockSpec((1, n_dr, GRID_W, GRID_W), lambda h: (h, 0, 0, 0))],
        out_specs=pl.BlockSpec((n_var, 1, tq, tk), lambda h: (0, h, 0, 0)),
        out_shape=jax.ShapeDtypeStruct((n_var, n_heads, tq, tk), F32),
        compiler_params=_params("parallel"),
        name="neighbourhood_bias_table",
    )(toeplitz)
    return table, n_blocks


def _bias_table_kernel(toe_ref, out_ref, *, dr, row_ok):
    n_var, n_q, n_k = dr.shape
    for v in range(n_var):
        for rl in range(n_q):
            for krl in range(n_k):
                tile = toe_ref[0, int(dr[v, rl, krl])] if row_ok[v, rl, krl] else jnp.full(
                    (GRID_W, GRID_W), NEG_INF, F32)
                out_ref[v, 0, rl * GRID_W:(rl + 1) * GRID_W, krl * GRID_W:(krl + 1) * GRID_W] = tile


def neighbourhood_attention(qb, kb, vb, kc_b, vc_b, rel_bias):
    bsz, n_lat, _ = qb.shape
    n_ctx = kc_b.shape[1]
    table, n_blocks = neighbourhood_bias(rel_bias, n_lat)
    assert n_blocks % NB_BLOCKS_PER_STEP == 0
    n_var = table.shape[0]
    tb = NB_Q_ROWS * GRID_W
    last = n_blocks - 1
    cur = lambda b, j: (b, j, 0)
    ctx = lambda b, j: (b, 0, 0)

    def key_block(off):
        return lambda b, j: (b, jnp.clip(NB_BLOCKS_PER_STEP * j + off, 0, last), 0)

    def variant(sb):
        def index(b, j):
            g = NB_BLOCKS_PER_STEP * j + sb
            return (jnp.where(g == 0, 0, jnp.where(g == last, n_var - 1, min(1, n_var - 1))), 0, 0, 0)
        return index

    kv_specs = [pl.BlockSpec((1, tb, B_W), key_block(off)) for off in range(-1, NB_BLOCKS_PER_STEP + 1)]
    return pl.pallas_call(
        _attn_b_kernel,
        grid=(bsz, n_blocks // NB_BLOCKS_PER_STEP),
        in_specs=[pl.BlockSpec((1, NB_BLOCKS_PER_STEP * tb, B_W), cur)] + kv_specs + kv_specs + [
            pl.BlockSpec((1, n_ctx, B_W), ctx), pl.BlockSpec((1, n_ctx, B_W), ctx)] + [
            pl.BlockSpec((1, B_HEADS, tb, 3 * tb), variant(sb)) for sb in range(NB_BLOCKS_PER_STEP)],
        out_specs=pl.BlockSpec((1, NB_BLOCKS_PER_STEP * tb, B_W), cur),
        out_shape=jax.ShapeDtypeStruct((bsz, n_lat, B_W), BF16),
        compiler_params=_params("parallel", "parallel"),
        name="neighbourhood_attention",
    )(qb, *([kb] * len(kv_specs)), *([vb] * len(kv_specs)), kc_b, vc_b, *([table] * NB_BLOCKS_PER_STEP))


def _ctx_attn_kernel(sink_ref, qa_ref, qb_ref, ka_ref, va_ref, kb_ref, vb_ref, oa_ref, ob_ref):
    qa, qb = qa_ref[0], qb_ref[0]
    ka, va, kb, vb = ka_ref[0], va_ref[0], kb_ref[0], vb_ref[0]
    for hq in range(A_Q_HEADS):
        sl = slice(hq * HEAD_DIM, (hq + 1) * HEAD_DIM)
        hk = hq // A_GROUP
        kv = slice(hk * HEAD_DIM, (hk + 1) * HEAD_DIM)
        q = qa[:, sl]
        o = _attend([_dot_t(q, ka[:, kv])], [_with_ones(va[:, kv])], sink=sink_ref[hq])
        oa_ref[0, :, sl] = o.astype(BF16)
    for h in range(B_HEADS):
        sl = slice(h * HEAD_DIM, (h + 1) * HEAD_DIM)
        q = qb[:, sl]
        o = _attend([_dot_t(q, kb[:, sl])], [_with_ones(vb[:, sl])])
        ob_ref[0, :, sl] = o.astype(BF16)


def context_attention(qa, qb, ka, va, kb, vb, sink):
    bsz, n_ctx, _ = qa.shape
    blk = lambda w: pl.BlockSpec((1, n_ctx, w), lambda b, s: (b, 0, 0))
    grid_spec = pltpu.PrefetchScalarGridSpec(
        num_scalar_prefetch=1,
        grid=(bsz,),
        in_specs=[blk(A_Q_W), blk(B_W), blk(A_KV_W), blk(A_KV_W), blk(B_W), blk(B_W)],
        out_specs=[blk(A_Q_W), blk(B_W)],
    )
    return pl.pallas_call(
        _ctx_attn_kernel,
        grid_spec=grid_spec,
        out_shape=[jax.ShapeDtypeStruct((bsz, n_ctx, A_Q_W), BF16),
                   jax.ShapeDtypeStruct((bsz, n_ctx, B_W), BF16)],
        compiler_params=_params("parallel"),
        name="context_attention",
    )(sink, qa, qb, ka, va, kb, vb)


CONV_HALO = 16
CONV_SHIFT_SPAN = (CONV_HALO + C_CONV_WIDTH // 2) // F32_SUBLANES * F32_SUBLANES


def _conv_kernel(prev_ref, cur_ref, next_ref, w_ref, b_ref, g_ref, beta_ref, out_ref, shifted_ref, *, ts):
    i = pl.program_id(1)
    n_i = pl.num_programs(1)
    ext = jnp.concatenate([jnp.where(i > 0, prev_ref[0], 0.0), cur_ref[0],
                           jnp.where(i < n_i - 1, next_ref[0], 0.0)], axis=0)
    for r in range(F32_SUBLANES):
        shifted_ref[r] = ext[r:r + ts + CONV_SHIFT_SPAN]
    acc = jnp.zeros((ts, C_CHANNELS), F32)
    for k in range(C_CONV_WIDTH):
        start = CONV_HALO - C_CONV_WIDTH // 2 + k
        aligned = start - start % F32_SUBLANES
        acc = acc + shifted_ref[start % F32_SUBLANES, aligned:aligned + ts] * w_ref[k:k + 1]
    y = _layer_norm(acc + b_ref[...]) * g_ref[...] + beta_ref[...]
    out_ref[0] = (y * jax.nn.sigmoid(y)).astype(BF16)


def conformer_conv(hc, conv_w, conv_b, ln_g, ln_b):
    bx, t, ch = hc.shape
    ts = min(SMALL_TILE, t)
    per = ts // CONV_HALO
    last = t // CONV_HALO - 1
    row = lambda v: v.reshape(1, ch)
    const = lambda b, i: (0, 0)
    return pl.pallas_call(
        functools.partial(_conv_kernel, ts=ts),
        grid=(bx, t // ts),
        in_specs=[
            pl.BlockSpec((1, CONV_HALO, ch), lambda b, i: (b, jnp.maximum(i * per - 1, 0), 0)),
            pl.BlockSpec((1, ts, ch), lambda b, i: (b, i, 0)),
            pl.BlockSpec((1, CONV_HALO, ch), lambda b, i: (b, jnp.minimum((i + 1) * per, last), 0)),
            pl.BlockSpec((C_CONV_WIDTH, ch), const),
            pl.BlockSpec((1, ch), const), pl.BlockSpec((1, ch), const), pl.BlockSpec((1, ch), const),
        ],
        out_specs=pl.BlockSpec((1, ts, ch), lambda b, i: (b, i, 0)),
        out_shape=jax.ShapeDtypeStruct((bx, t, ch), BF16),
        scratch_shapes=[pltpu.VMEM((F32_SUBLANES, ts + CONV_SHIFT_SPAN, ch), F32)],
        compiler_params=_params("parallel", "parallel"),
        name="conformer_conv",
    )(hc, hc, hc, conv_w, row(conv_b), row(ln_g), row(ln_b))


OUTPROJ_CHUNK_ROWS = 128


def _outproj_kernel(oa_ref, ob_ref, oc_ref, x_ref, g1_ref, sh_ref, sc_ref, w_ref, lng_ref, lnb_ref,
                    wr_hi_ref, wr_lo_ref, x1_ref, h2_ref, afft_ref, aff_ref, *, alpha):
    tm, d = x_ref.shape[1], x_ref.shape[2]
    rc = min(OUTPROJ_CHUNK_ROWS, tm)
    chunks = [slice(k * rc, (k + 1) * rc) for k in range(tm // rc)]
    outs = [(_dot(oa_ref[0, r], w_ref[0:A_Q_W])
             + _dot(ob_ref[0, r], w_ref[A_Q_W:A_Q_W + B_W])
             + _dot(oc_ref[0, r], w_ref[A_Q_W + B_W:])) for r in chunks]
    w_hi, w_lo = wr_hi_ref[...], wr_lo_ref[...]
    eye = (lax.broadcasted_iota(jnp.int32, (N_EXPERTS, N_EXPERTS), 0)
           == lax.broadcasted_iota(jnp.int32, (N_EXPERTS, N_EXPERTS), 1)).astype(BF16)
    rows_per_token = d // LANES
    ys = [_layer_norm(alpha * x_ref[0, r] + g1_ref[0] * o) * lng_ref[...] + lnb_ref[...]
          for r, o in zip(chunks, outs)]
    for r, y in zip(chunks, ys):
        x1_ref[0, r] = y
    h2s = [_layer_norm(y) * (1.0 + sc_ref[0]) + sh_ref[0] for y in ys]
    his = [h2.astype(BF16) for h2 in h2s]
    los = [(h2 - hi.astype(F32)).astype(BF16) for h2, hi in zip(h2s, his)]
    all_logits = [_dot_t(hi, w_hi) + _dot_t(lo, w_hi) + _dot_t(hi, w_lo) for hi, lo in zip(his, los)]
    for k, h2 in enumerate(h2s):
        for j in range(rows_per_token):
            h2_ref[0, pl.ds(k * rc * rows_per_token + j, rc, stride=rows_per_token), :] = (
                h2[:, j * LANES:(j + 1) * LANES])
    for r, logits in zip(chunks, all_logits):
        e_n = jnp.exp(logits - logits.max(axis=1, keepdims=True))
        aff = e_n / e_n.sum(axis=1, keepdims=True)
        aff_ref[0, r] = aff
        aff_t, rest = None, aff
        for _ in range(GATE_PARTS):
            part = rest.astype(BF16)
            term = _dot_t(eye, part)
            aff_t = term if aff_t is None else aff_t + term
            rest = rest - part.astype(F32)
        afft_ref[0, :, r] = aff_t


def out_projection(oa, ob, oc, x, g1, sh2, sc2, w_out_bf16, ln_g, ln_b, wr_hi, wr_lo, alpha):
    bx, t, d = x.shape
    tm = min(TOKEN_TILE, t)
    tok = lambda b, i: (b, i, 0)
    per_b = lambda b, i: (b, 0, 0)
    const = lambda b, i: (0, 0)
    vec = pl.BlockSpec((1, d), const)
    return pl.pallas_call(
        functools.partial(_outproj_kernel, alpha=alpha),
        grid=(bx, t // tm),
        in_specs=[
            pl.BlockSpec((1, tm, A_Q_W), tok), pl.BlockSpec((1, tm, B_W), tok),
            pl.BlockSpec((1, tm, C_CHANNELS), tok), pl.BlockSpec((1, tm, d), tok),
            pl.BlockSpec((1, 1, d), per_b), pl.BlockSpec((1, 1, d), per_b), pl.BlockSpec((1, 1, d), per_b),
            pl.BlockSpec(w_out_bf16.shape, const), vec, vec,
            pl.BlockSpec((N_EXPERTS, d), const), pl.BlockSpec((N_EXPERTS, d), const),
        ],
        out_specs=[pl.BlockSpec((1, tm, d), tok), pl.BlockSpec((1, tm * (d // LANES), LANES), tok),
                   pl.BlockSpec((1, N_EXPERTS, tm), lambda b, i: (b, 0, i)),
                   pl.BlockSpec((1, tm, N_EXPERTS), tok)],
        out_shape=[jax.ShapeDtypeStruct((bx, t, d), F32),
                   jax.ShapeDtypeStruct((bx, t * (d // LANES), LANES), F32),
                   jax.ShapeDtypeStruct((bx, N_EXPERTS, t), F32),
                   jax.ShapeDtypeStruct((bx, t, N_EXPERTS), F32)],
        compiler_params=_params("parallel", "parallel"),
        name="out_projection",
    )(oa, ob, oc, x, g1, sh2, sc2, w_out_bf16, ln_g.reshape(1, d), ln_b.reshape(1, d), wr_hi, wr_lo)


def _select_kernel(afft_ref, pos_ref, off_ref, *, cap, n_tok):
    aff = afft_ref[0]

    def bit_step(j, bits):
        cand = bits | (jnp.int32(1) << (30 - j))
        cnt = jnp.sum((aff >= pltpu.bitcast(cand, F32)).astype(jnp.int32), axis=1, keepdims=True)
        return jnp.where(cnt >= cap, cand, bits)

    thr = pltpu.bitcast(lax.fori_loop(0, 31, bit_step, jnp.zeros((N_EXPERTS, 1), jnp.int32)), F32)
    above = (aff > thr).astype(F32)
    tied = (aff == thr).astype(F32)
    need = cap - jnp.sum(above, axis=1, keepdims=True)

    blk = MXU_DEPTH
    blocks = [slice(k * blk, (k + 1) * blk) for k in range(n_tok // blk)]
    r_i = lax.broadcasted_iota(jnp.int32, (blk, blk), 0)
    c_i = lax.broadcasted_iota(jnp.int32, (blk, blk), 1)
    strict_upper = (r_i < c_i).astype(BF16)

    def running(block_sums):
        run, total = [], jnp.zeros((N_EXPERTS, 1), F32)
        for s in block_sums:
            run.append(total)
            total = total + s
        return run

    tied_b = [tied[:, sl] for sl in blocks]
    tied_rank = [_dot(t.astype(BF16), strict_upper) for t in tied_b]
    tied_before = running([t.sum(axis=1, keepdims=True) for t in tied_b])
    sel_b = [above[:, sl] + t * ((before + rank) < need).astype(F32)
             for sl, t, before, rank in zip(blocks, tied_b, tied_before, tied_rank)]
    sel_rank = [_dot(s.astype(BF16), strict_upper) for s in sel_b]
    half_sums = [[s[:, h * OFFSET_BLOCK:(h + 1) * OFFSET_BLOCK].sum(axis=1, keepdims=True)
                  for h in range(blk // OFFSET_BLOCK)] for s in sel_b]
    offs = running([h for hs in half_sums for h in hs])
    per_blk = blk // OFFSET_BLOCK
    for k, (sl, s, rank) in enumerate(zip(blocks, sel_b, sel_rank)):
        pos_ref[0, :, sl] = jnp.where(s > 0.5, (offs[k * per_blk] + rank).astype(jnp.int32), -1)
    off_ref[0] = jnp.concatenate(offs, axis=1).astype(jnp.int32)


def expert_choice_select(aff_t, cap):
    bx, n_e, t = aff_t.shape
    n_tb = t // OFFSET_BLOCK
    return pl.pallas_call(
        functools.partial(_select_kernel, cap=cap, n_tok=t),
        grid=(bx,),
        in_specs=[pl.BlockSpec((1, n_e, t), lambda b: (b, 0, 0))],
        out_specs=[pl.BlockSpec((1, n_e, t), lambda b: (b, 0, 0)),
                   pl.BlockSpec((1, n_e, n_tb), lambda b: (b, 0, 0))],
        out_shape=[jax.ShapeDtypeStruct((bx, n_e, t), jnp.int32),
                   jax.ShapeDtypeStruct((bx, n_e, n_tb), jnp.int32)],
        compiler_params=_params("parallel"),
        name="expert_choice_select",
    )(aff_t)


GATE_PARTS = 3
TOKEN_LANE = GATE_PARTS * N_EXPERTS


def _slot_table_kernel(off_ref, pos_ref, aff_ref, tbl_ref, *, n_off, blocks_per_step):
    b, eg, kc = pl.program_id(0), pl.program_id(1), pl.program_id(2)
    epg = EXPERTS_PER_GATHER_STEP

    @pl.when(kc == 0)
    def _():
        tbl_ref[...] = jnp.zeros_like(tbl_ref)

    slot = lax.broadcasted_iota(jnp.int32, (GATHER_WINDOW, 1), 0)
    slot_2d = lax.broadcasted_iota(jnp.int32, (GATHER_WINDOW, GATHER_BLOCK), 0)
    lane = lax.broadcasted_iota(jnp.int32, (GATHER_BLOCK, LANES), 1)
    out_lane = lax.broadcasted_iota(jnp.int32, (GATHER_WINDOW, LANES), 1)
    local_token = lax.broadcasted_iota(jnp.int32, (GATHER_BLOCK, LANES), 0).astype(F32)
    place_r = lax.broadcasted_iota(jnp.int32, (N_EXPERTS, LANES), 0)
    place_c = lax.broadcasted_iota(jnp.int32, (N_EXPERTS, LANES), 1)
    def payload_of(kk):
        payload = jnp.where(lane == TOKEN_LANE, local_token, 0.0)
        rest = aff_ref[0, kk * GATHER_BLOCK:(kk + 1) * GATHER_BLOCK]
        for k in range(GATE_PARTS):
            part = rest.astype(BF16)
            payload = payload + _dot(part, (place_c == place_r + k * N_EXPERTS).astype(BF16))
            rest = rest - part.astype(F32)
        return payload.astype(BF16)

    payloads = [payload_of(kk) for kk in range(blocks_per_step)]
    pending = []
    for kk in range(blocks_per_step):
        kb = kc * blocks_per_step + kk
        first_token = jnp.where(out_lane == TOKEN_LANE, (kb * GATHER_BLOCK).astype(F32), 0.0)
        for ee in range(epg):
            off = off_ref[(b * N_EXPERTS + eg * epg + ee) * n_off + kb * (GATHER_BLOCK // OFFSET_BLOCK)]
            base = pl.multiple_of(off - (off & (WINDOW_ALIGN - 1)), WINDOW_ALIGN)
            pos_row = pos_ref[0, ee:ee + 1, kk * GATHER_BLOCK:(kk + 1) * GATHER_BLOCK]
            onehot = (pos_row - base == slot_2d).astype(BF16)
            pending.append((ee, off, base, _dot(onehot, payloads[kk]) + first_token))
    for ee, off, base, gathered in pending:
        win = pl.ds(base, GATHER_WINDOW)
        tbl_ref[0, ee, win, :] = jnp.where(slot >= off - base, gathered, tbl_ref[0, ee, win, :])


def slot_table(aff, pos, block_off, cap_pad):
    bx, t, _ = aff.shape
    n_tb = t // GATHER_BLOCK
    blocks_per_step = min(TABLE_BLOCKS_PER_STEP, n_tb)
    n_steps = n_tb // blocks_per_step
    epg = EXPERTS_PER_GATHER_STEP
    tokens = blocks_per_step * GATHER_BLOCK
    grid_spec = pltpu.PrefetchScalarGridSpec(
        num_scalar_prefetch=1,
        grid=(bx, N_EXPERTS // epg, n_steps),
        in_specs=[
            pl.BlockSpec((1, epg, tokens), lambda b, g, k, s: (b, g, k)),
            pl.BlockSpec((1, tokens, N_EXPERTS), lambda b, g, k, s: (b, k, 0)),
        ],
        out_specs=pl.BlockSpec((1, epg, cap_pad, LANES), lambda b, g, k, s: (b, g, 0, 0)),
    )
    return pl.pallas_call(
        functools.partial(_slot_table_kernel, n_off=t // OFFSET_BLOCK, blocks_per_step=blocks_per_step),
        grid_spec=grid_spec,
        out_shape=jax.ShapeDtypeStruct((bx, N_EXPERTS, cap_pad, LANES), F32),
        compiler_params=_params("parallel", "parallel", "arbitrary"),
        name="slot_table",
    )(block_off.reshape(-1), pos, aff)


GATHER_UNROLL = 32


def _row_gather_kernel(idx_ref, src_ref, xs_ref, tile_ref, *, cap, rows_per_token, chunk_stride):
    b, e = pl.program_id(0), pl.program_id(1)
    first = (b * N_EXPERTS + e) * cap

    def group(g, carry):
        for u in range(GATHER_UNROLL):
            s = g * GATHER_UNROLL + u
            row = pl.multiple_of(idx_ref[first + s], rows_per_token)
            tile_ref[pl.ds(s, rows_per_token, stride=chunk_stride), :] = src_ref[0, pl.ds(row, rows_per_token), :]
        return carry

    lax.fori_loop(0, cap // GATHER_UNROLL, group, 0)
    for j in range(rows_per_token):
        xs_ref[0, 0, :, j * LANES:(j + 1) * LANES] = tile_ref[j * chunk_stride:j * chunk_stride + cap, :].astype(BF16)


def gather_rows(h2_rows, token_row, cap, d):
    bx, n_rows, _ = h2_rows.shape
    rows_per_token = d // LANES
    chunk_stride = cap + F32_SUBLANES
    grid_spec = pltpu.PrefetchScalarGridSpec(
        num_scalar_prefetch=1,
        grid=(bx, N_EXPERTS),
        in_specs=[pl.BlockSpec((1, n_rows, LANES), lambda b, e, s: (b, 0, 0), pipeline_mode=pl.Buffered(1))],
        out_specs=pl.BlockSpec((1, 1, cap, d), lambda b, e, s: (b, e, 0, 0)),
        scratch_shapes=[pltpu.VMEM((rows_per_token * chunk_stride, LANES), F32)],
    )
    return pl.pallas_call(
        functools.partial(_row_gather_kernel, cap=cap, rows_per_token=rows_per_token, chunk_stride=chunk_stride),
        grid_spec=grid_spec,
        out_shape=jax.ShapeDtypeStruct((bx, N_EXPERTS, cap, d), BF16),
        compiler_params=_params("parallel", "arbitrary"),
        name="gather_rows",
    )(token_row, h2_rows)


def _ffn_kernel(xs_ref, gs_ref, wg_ref, wu_ref, wd_ref, ye_ref, wg_bf, wu_bf, wd_bf, *, cap, row_tile):
    @pl.when(pl.program_id(1) == 0)
    def _():
        wg_bf[...] = wg_ref[0, 0].astype(BF16)
        wu_bf[...] = wu_ref[0, 0].astype(BF16)
        wd_bf[...] = wd_ref[0, 0].astype(BF16)

    n_b = xs_ref.shape[0]
    if n_b == 1:
        tiles = [[(0, r0, row_tile)] for r0 in range(0, cap, row_tile)]
    else:
        tiles = [[(bb, 0, cap) for bb in range(n_b)]]
    for tile in tiles:
        x = jnp.concatenate([xs_ref[bb, 0, r0:r0 + n] for bb, r0, n in tile], axis=0)
        terms = jnp.concatenate([gs_ref[bb, 0, r0:r0 + n, :] for bb, r0, n in tile], axis=0)
        lane = lax.broadcasted_iota(jnp.int32, terms.shape, 1)
        own = ((lane & (N_EXPERTS - 1)) == pl.program_id(0)) & (lane < TOKEN_LANE)
        terms = jnp.where(own, terms, 0.0)
        gate = _dot(x, wg_bf[...])
        up = _dot(x, wu_bf[...])
        hid = (gate * jax.nn.sigmoid(gate) * up).astype(BF16)
        g = jnp.sum(terms, axis=1, keepdims=True)
        ye = _dot(hid, wd_bf[...]) * g
        rows_per_slot = ye.shape[1] // LANES
        row = 0
        for bb, r0, n in tile:
            for j in range(rows_per_slot):
                ye_ref[bb, 0, pl.ds(r0 * rows_per_slot + j, n, stride=rows_per_slot), :] = (
                    ye[row:row + n, j * LANES:(j + 1) * LANES])
            row += n


def expert_ffn(xs, gs, wg, wu, wd, layer):
    bx, n_e, cap, d = xs.shape
    ff = wg.shape[-1]
    slot_rows = cap * (d // LANES)
    max_rows = SMALL_TILE
    row_tile = min(max_rows, cap)
    n_b = bx if bx * cap <= max_rows else 1
    return pl.pallas_call(
        functools.partial(_ffn_kernel, cap=cap, row_tile=row_tile),
        grid=(n_e, bx // n_b),
        in_specs=[
            pl.BlockSpec((n_b, 1, cap, d), lambda e, b: (b, e, 0, 0)),
            pl.BlockSpec((n_b, 1, cap, LANES), lambda e, b: (b, e, 0, 0)),
            pl.BlockSpec((1, 1, d, ff), lambda e, b: (layer, e, 0, 0)),
            pl.BlockSpec((1, 1, d, ff), lambda e, b: (layer, e, 0, 0)),
            pl.BlockSpec((1, 1, ff, d), lambda e, b: (layer, e, 0, 0)),
        ],
        out_specs=pl.BlockSpec((n_b, 1, slot_rows, LANES), lambda e, b: (b, e, 0, 0)),
        out_shape=jax.ShapeDtypeStruct((bx, n_e, slot_rows, LANES), F32),
        scratch_shapes=[pltpu.VMEM((d, ff), BF16), pltpu.VMEM((d, ff), BF16), pltpu.VMEM((ff, d), BF16)],
        compiler_params=_params("parallel", "arbitrary"),
        name="expert_ffn",
    )(xs, gs, wg, wu, wd)


SCATTER_UNROLL = 16


def _scatter_norm_kernel(idx_ref, ye_ref, x1_ref, g2_ref, lng_ref, lnb_ref, out_ref, acc_ref, *,
                         cap, rows_per_token, alpha):
    b, step = pl.program_id(0), pl.program_id(1)

    @pl.when(step == 0)
    def _():
        acc_ref[...] = jnp.zeros_like(acc_ref)

    @pl.when(step < N_EXPERTS)
    def _():
        first = (b * N_EXPERTS + step) * cap

        def group(g, carry):
            rows, sums = [], []
            for u in range(SCATTER_UNROLL):
                s = g * SCATTER_UNROLL + u
                row = pl.multiple_of(idx_ref[first + s], rows_per_token)
                src = pl.multiple_of(s * rows_per_token, rows_per_token)
                rows.append(row)
                sums.append(acc_ref[pl.ds(row, rows_per_token), :] + ye_ref[0, 0, pl.ds(src, rows_per_token), :])
            for row, total in zip(rows, sums):
                acc_ref[pl.ds(row, rows_per_token), :] = total
            return carry

        lax.fori_loop(0, cap // SCATTER_UNROLL, group, 0)

    @pl.when(step >= N_EXPERTS)
    def _():
        tm = x1_ref.shape[1]
        tile_row = pl.multiple_of((step - N_EXPERTS) * (tm * rows_per_token), rows_per_token)
        moe = jnp.concatenate([acc_ref[pl.ds(tile_row + j, tm, stride=rows_per_token), :]
                               for j in range(rows_per_token)], axis=1)
        y = alpha * x1_ref[0] + g2_ref[0] * moe
        out_ref[0] = _layer_norm(y) * lng_ref[...] + lnb_ref[...]


def scatter_and_norm(ye_rows, token_row, x1, g2, ln_g, ln_b, alpha, cap):
    bx, t, d = x1.shape
    n_e, slot_rows = ye_rows.shape[1], ye_rows.shape[2]
    rows_per_token = d // LANES
    tm = min(SMALL_TILE, t)
    tile = lambda b, i, s: (b, jnp.maximum(i - n_e, 0), 0)
    const = lambda b, i, s: (0, 0)
    grid_spec = pltpu.PrefetchScalarGridSpec(
        num_scalar_prefetch=1,
        grid=(bx, n_e + t // tm),
        in_specs=[
            pl.BlockSpec((1, 1, slot_rows, LANES), lambda b, i, s: (b, jnp.minimum(i, n_e - 1), 0, 0)),
            pl.BlockSpec((1, tm, d), tile),
            pl.BlockSpec((1, 1, d), lambda b, i, s: (b, 0, 0)),
            pl.BlockSpec((1, d), const), pl.BlockSpec((1, d), const),
        ],
        out_specs=pl.BlockSpec((1, tm, d), tile),
        scratch_shapes=[pltpu.VMEM((t * rows_per_token, LANES), F32)],
    )
    return pl.pallas_call(
        functools.partial(_scatter_norm_kernel, cap=cap, rows_per_token=rows_per_token, alpha=alpha),
        grid_spec=grid_spec,
        out_shape=jax.ShapeDtypeStruct((bx, t, d), F32),
        compiler_params=_params("parallel", "arbitrary"),
        name="scatter_and_norm",
    )(token_row, ye_rows, x1, g2, ln_g.reshape(1, d), ln_b.reshape(1, d))


def _split_bf16(w):
    hi = w.astype(BF16)
    return hi, (w - hi.astype(F32)).astype(BF16)


def _mixer_tail(oa, ob, oc, x, mod, lw, alpha):
    g1, sh2, sc2, g2 = mod
    t = x.shape[1]
    cap = EC_CAPACITY * t // N_EXPERTS
    cap_pad = cap + SLOT_PAD
    x1, h2_rows, aff_t, aff = out_projection(oa, ob, oc, x, g1, sh2, sc2, lw["w_out"], lw["ln1_g"], lw["ln1_b"],
                                        lw["wr_hi"], lw["wr_lo"], alpha)
    pos, block_off = expert_choice_select(aff_t, cap)
    table = slot_table(aff, pos, block_off, cap_pad)
    token_row = (table[:, :, :cap, TOKEN_LANE].astype(jnp.int32) * (x.shape[2] // LANES)).reshape(-1)
    xs = gather_rows(h2_rows, token_row, cap, x.shape[2])
    ye_rows = expert_ffn(xs, table, lw["w_gate"], lw["w_up"], lw["w_down"], lw["layer"])
    return scatter_and_norm(ye_rows, token_row, x1, g2, lw["ln2_g"], lw["ln2_b"], alpha, cap)


def kernel(x, c, ctx, c_ctx, w_mod, b_mod, w_in, a_sink, nat_bias, conv_w, conv_b, conv_ln_g, conv_ln_b,
           w_out, ln1_g, ln1_b, w_router, w_gate, w_up, w_down, ln2_g, ln2_b):
    bsz, n_lat, d = x.shape
    depth = w_mod.shape[0]
    alpha = (2 * depth) ** 0.25
    cos_t, sin_t = rope_tables(n_lat)

    cond = jnp.concatenate([c, c_ctx[None, :], jnp.zeros((F32_SUBLANES - bsz - 1, d), F32)], axis=0)
    mods = adaln_all(cond, w_mod, b_mod)

    for l in range(depth):
        last = l == depth - 1
        wr_hi, wr_lo = _split_bf16(w_router[l].T)
        lw = dict(w_out=w_out[l].astype(BF16), ln1_g=ln1_g[l], ln1_b=ln1_b[l], wr_hi=wr_hi, wr_lo=wr_lo,
                  w_gate=w_gate, w_up=w_up, w_down=w_down, layer=l,
                  ln2_g=ln2_g[l], ln2_b=ln2_b[l])
        w_in_l = w_in[l].astype(BF16)
        lat = [mods[l, :bsz, k * d:(k + 1) * d][:, None, :] for k in range(N_MOD)]
        cm = [jnp.broadcast_to(mods[l, bsz, k * d:(k + 1) * d][None, None, :], (bsz, 1, d))
              for k in range(N_MOD)]
        conv_args = (conv_w[l], conv_b[l], conv_ln_g[l], conv_ln_b[l])

        qa_c, ka_c, va_c, qb_c, kb_c, vb_c, hc_c = in_projection(ctx, cm[0], cm[1], w_in_l, cos_t, sin_t, rope=False)
        if not last:
            oa_c, ob_c = context_attention(qa_c, qb_c, ka_c, va_c, kb_c, vb_c, a_sink[l])
            oc_c = conformer_conv(hc_c, *conv_args)
            ctx_new = _mixer_tail(oa_c, ob_c, oc_c, ctx, (cm[2], cm[3], cm[4], cm[5]), lw, alpha)

        qa, ka, va, qb, kb, vb, hc = in_projection(x, lat[0], lat[1], w_in_l, cos_t, sin_t, rope=True)
        oa = window_attention(qa, ka, va, ka_c, va_c, a_sink[l])
        ob = neighbourhood_attention(qb, kb, vb, kb_c, vb_c, nat_bias[l])
        oc = conformer_conv(hc, *conv_args)
        x = _mixer_tail(oa, ob, oc, x, (lat[2], lat[3], lat[4], lat[5]), lw, alpha)
        if not last:
            ctx = ctx_new
    return x
```
